```python
import math
import jax, jax.numpy as jnp
from jax import lax
import numpy as np


D_MODEL = 1024
BATCH = 8
SEQ = 2048
DEPTH = 4

MEM_LEN = 256
N_EVEN = (DEPTH + 1) // 2
N_ODD = DEPTH // 2
EPS = 1e-6
NEG = -1e30
Q_BLOCK = 128

HY_WIDTH = D_MODEL // 2
HY_ORDER = 2
HY_EMB = 33
HY_BANDS = (HY_EMB - 1) // 2
HY_FILT_HIDDEN = 64
HY_CONV = 3
HY_DECAY_TARGET = 1e-2
HY_FAST_PCT = 0.3
HY_SLOW_PCT = 1.5
HY_FILT_STD = 0.02
DIFF_WIDTH = D_MODEL // 2
DIFF_HEADS = 4
DIFF_HEAD_DIM = DIFF_WIDTH // DIFF_HEADS // 2
E_IN = 3 * HY_WIDTH + 3 * DIFF_WIDTH
E_MIX = HY_WIDTH + DIFF_WIDTH

WIN = 128
GQA_HEADS = 16
GQA_KV = 4
GQA_HD = 64
GQA_GROUP = GQA_HEADS // GQA_KV
O_Q = GQA_HEADS * GQA_HD
O_KV = GQA_KV * GQA_HD
O_IN = O_Q + 2 * O_KV

X_HEADS = 4
X_HD = 128
X_W = X_HEADS * X_HD

N_GROUPS = 4
EXP_PER_GROUP = 8
N_EXPERTS = N_GROUPS * EXP_PER_GROUP
TOP_K = 2
EXP_HIDDEN = 512
MOE_BLOCK = 128

kernel_name = 'hybrid_hyena_diffattn_swa_hmoe_encoder'

F32 = jnp.float32


def rmsnorm(x, g):
    x32 = x.astype(F32)
    y = x32 * lax.rsqrt(jnp.mean(x32 * x32, axis=-1, keepdims=True) + EPS)
    return (y * g.astype(F32)).astype(x.dtype)


def alibi_slopes(n):
    return 2.0 ** (-8.0 * jnp.arange(1, n + 1, dtype=F32) / n)


def short_conv3(u, w, b):
    up = jnp.pad(u, ((0, 0), (1, 1), (0, 0)))
    return up[:, :-2] * w[0] + up[:, 1:-1] * w[1] + up[:, 2:] * w[2] + b


def hyena_filters(L, w1, b1, w2, b2, w3, freq):
    pos = jnp.arange(L, dtype=F32)
    t = jnp.linspace(0.0, 1.0, L, dtype=F32)[:, None]
    bands = jnp.linspace(1e-4, HY_BANDS - 1, HY_BANDS, dtype=F32)[None]
    ang = bands * (2.0 * math.pi * pos[:, None] / L)
    z = jnp.concatenate([t, jnp.cos(ang), -jnp.sin(ang)], axis=-1)
    f = freq.astype(F32)
    a = jnp.sin(f * (z @ w1.astype(F32) + b1.astype(F32)))
    a = jnp.sin(f * (a @ w2.astype(F32) + b2.astype(F32)))
    h = (a @ w3.astype(F32)).reshape(L, HY_ORDER, 2, HY_WIDTH)
    max_decay = math.log(HY_DECAY_TARGET) / HY_FAST_PCT
    min_decay = math.log(HY_DECAY_TARGET) / HY_SLOW_PCT
    deltas = jnp.linspace(min_decay, max_decay, HY_WIDTH, dtype=F32)
    decay = jnp.exp(-t * jnp.abs(deltas))
    h = h * decay[:, None, None, :]
    k = jnp.concatenate([h[:, :, 0], jnp.zeros((1, HY_ORDER, HY_WIDTH), F32), h[:0:-1, :, 1]], axis=0)
    return jnp.fft.rfft(k, n=2 * L, axis=0)


def fft_conv(z, kf, bias):
    L = z.shape[1]
    z32 = z.astype(F32)
    zf = jnp.fft.rfft(z32, n=2 * L, axis=1)
    y = jnp.fft.irfft(zf * kf[None], n=2 * L, axis=1)[:, :L]
    return (y + z32 * bias.astype(F32)).astype(z.dtype)


def hyena_mixer(u, conv_w, conv_b, w1, b1, w2, b2, w3, freq, hy_bias):
    L = u.shape[1]
    u = short_conv3(u, conv_w, conv_b)
    v, g1, g2 = jnp.split(u, 3, axis=-1)
    kf = hyena_filters(L, w1, b1, w2, b2, w3, freq)
    z = g1 * fft_conv(v, kf[:, 0], hy_bias[0])
    z = g2 * fft_conv(z, kf[:, 1], hy_bias[1])
    return z


def diff_attention(q, k, v, lam, subln, layer_idx):
    B, S, _ = q.shape
    H, d = DIFF_HEADS, DIFF_HEAD_DIM
    nb = S // Q_BLOCK
    lam_init = 0.8 - 0.6 * math.exp(-0.3 * layer_idx)
    l32 = lam.astype(F32)
    lam_full = jnp.exp(jnp.sum(l32[0] * l32[1])) - jnp.exp(jnp.sum(l32[2] * l32[3])) + lam_init
    qb_all = q.reshape(B, nb, Q_BLOCK, 2 * H, d).transpose(1, 0, 2, 3, 4)
    k = k.reshape(B, S, 2 * H, d)
    v = v.reshape(B, S, H, 2 * d)
    slopes = jnp.repeat(alibi_slopes(H), 2)
    kpos = jnp.arange(S)
    scale = d ** -0.5

    def block(args):
        qb, j = args
        qpos = j * Q_BLOCK + jnp.arange(Q_BLOCK)
        dist = jnp.abs(qpos[:, None] - kpos[None]).astype(F32)
        s = jnp.einsum('bqhd,bkhd->bhqk', qb, k).astype(F32) * scale - slopes[:, None, None] * dist
        p = jax.nn.softmax(s, axis=-1).reshape(B, H, 2, Q_BLOCK, S)
        a = p[:, :, 0] - lam_full * p[:, :, 1]
        return jnp.einsum('bhqk,bkhe->bqhe', a.astype(v.dtype), v)

    o = lax.map(block, (qb_all, jnp.arange(nb)))
    o = o.transpose(1, 0, 2, 3, 4).reshape(B, S, H, 2 * d)
    o = rmsnorm(o, subln) * (1.0 - lam_init)
    return o.reshape(B, S, H * 2 * d)


def window_gqa(q, k, v, sink):
    B, S, _ = q.shape
    nb = S // Q_BLOCK
    span = Q_BLOCK + 2 * WIN
    qb_all = q.reshape(B, nb, Q_BLOCK, GQA_KV, GQA_GROUP, GQA_HD).transpose(1, 0, 2, 3, 4, 5)
    pad = ((0, 0), (WIN, WIN), (0, 0), (0, 0))
    kp = jnp.pad(k.reshape(B, S, GQA_KV, GQA_HD), pad)
    vp = jnp.pad(v.reshape(B, S, GQA_KV, GQA_HD), pad)
    slopes = alibi_slopes(GQA_HEADS).reshape(GQA_KV, GQA_GROUP)
    sink32 = sink.astype(F32).reshape(GQA_KV, GQA_GROUP)
    scale = GQA_HD ** -0.5

    def block(args):
        qb, j = args
        start = j * Q_BLOCK
        kb = lax.dynamic_slice_in_dim(kp, start, span, axis=1)
        vb = lax.dynamic_slice_in_dim(vp, start, span, axis=1)
        qpos = start + jnp.arange(Q_BLOCK)
        kpos = start - WIN + jnp.arange(span)
        rel = jnp.abs(qpos[:, None] - kpos[None])
        valid = (rel <= WIN) & (kpos >= 0)[None] & (kpos < S)[None]
        s = jnp.einsum('bqkgd,bckd->bkgqc', qb, kb).astype(F32) * scale - slopes[..., None, None] * rel.astype(F32)
        s = jnp.where(valid, s, NEG)
        sk = jnp.broadcast_to(sink32[None, :, :, None, None], s.shape[:-1] + (1,))
        p = jax.nn.softmax(jnp.concatenate([s, sk], axis=-1), axis=-1)[..., :-1]
        return jnp.einsum('bkgqc,bckd->bqkgd', p.astype(vb.dtype), vb)

    o = lax.map(block, (qb_all, jnp.arange(nb)))
    return o.transpose(1, 0, 2, 3, 4, 5).reshape(B, S, O_Q)


def cross_attention(h, memn, wq, wkv, wo):
    B, S, _ = h.shape
    M = memn.shape[1]
    q = (h @ wq).reshape(B, S, X_HEADS, X_HD)
    kv = (memn @ wkv).reshape(B, M, 2, X_HEADS, X_HD)
    s = jnp.einsum('bqhd,bkhd->bhqk', q, kv[:, :, 0]).astype(F32) * (X_HD ** -0.5)
    p = jax.nn.softmax(s, axis=-1)
    o = jnp.einsum('bhqk,bkhd->bqhd', p.astype(kv.dtype), kv[:, :, 1]).reshape(B, S, X_W)
    return o @ wo


def hier_moe(h, w_grp, b_grp, w_exp, b_exp, w_gate, w_up, w_down):
    B, S, D = h.shape
    N = B * S
    xf = h.reshape(N, D)
    g_logits = (xf @ w_grp + b_grp).astype(F32)
    g_prob = jax.nn.softmax(g_logits, axis=-1)
    g_sel = jnp.argmax(g_logits, axis=-1).astype(jnp.int32)
    g_gate = jnp.take_along_axis(g_prob, g_sel[:, None], axis=-1)
    e_logits = (xf @ w_exp + b_exp).astype(F32).reshape(N, N_GROUPS, EXP_PER_GROUP)
    e_logits = jnp.take_along_axis(e_logits, g_sel[:, None, None], axis=1)[:, 0]
    top_v, top_i = lax.top_k(e_logits, TOP_K)
    weights = g_gate * jax.nn.softmax(top_v, axis=-1)
    eid = (g_sel[:, None] * EXP_PER_GROUP + top_i).reshape(-1)
    tok = jnp.repeat(jnp.arange(N, dtype=jnp.int32), TOP_K)
    wt = weights.reshape(-1)
    P = N * TOP_K
    order = jnp.argsort(eid)
    s_eid, s_tok, s_wt = eid[order], tok[order], wt[order]
    counts = jnp.bincount(eid, length=N_EXPERTS)
    starts = jnp.cumsum(counts) - counts
    padded = (counts + MOE_BLOCK - 1) // MOE_BLOCK * MOE_BLOCK
    p_ends = jnp.cumsum(padded)
    p_starts = p_ends - padded
    dest = p_starts[s_eid] + (jnp.arange(P) - starts[s_eid])
    cap = P + N_EXPERTS * MOE_BLOCK
    nblk = cap // MOE_BLOCK
    buf_tok = jnp.full((cap,), N, jnp.int32).at[dest].set(s_tok)
    buf_wt = jnp.zeros((cap,), F32).at[dest].set(s_wt)
    blk_e = jnp.minimum(jnp.searchsorted(p_ends, jnp.arange(nblk) * MOE_BLOCK, side='right'), N_EXPERTS - 1)
    xpad = jnp.concatenate([xf, jnp.zeros((1, D), xf.dtype)], axis=0)
    xb = xpad[buf_tok].reshape(nblk, MOE_BLOCK, D)

    def expert_block(args):
        xblk, e = args
        a = xblk @ w_gate[e]
        b = xblk @ w_up[e]
        return (jax.nn.silu(a) * b) @ w_down[e]

    yb = lax.map(expert_block, (xb, blk_e)).reshape(cap, D)
    y = jnp.zeros((N + 1, D), F32).at[buf_tok].add(yb.astype(F32) * buf_wt[:, None])[:N]
    return y.astype(h.dtype).reshape(B, S, D)


def setup_inputs(seed: int = 0) -> dict:
    key = jax.random.key(seed)
    ks = iter(jax.random.split(key, 48))

    def nrm(shape, scale):
        return jax.random.normal(next(ks), shape, F32) * scale

    def gain(shape):
        return 1.0 + nrm(shape, 0.02)

    D = D_MODEL
    return {
        'x': nrm((BATCH, SEQ, D), 1.0),
        'mem': nrm((BATCH, MEM_LEN, D), 1.0),
        'e_norm': gain((N_EVEN, D)),
        'e_w_in': nrm((N_EVEN, D, E_IN), D ** -0.5),
        'e_conv_w': nrm((N_EVEN, HY_CONV, 3 * HY_WIDTH), HY_CONV ** -0.5),
        'e_conv_b': nrm((N_EVEN, 3 * HY_WIDTH), 0.02),
        'e_filt_w1': nrm((N_EVEN, HY_EMB, HY_FILT_HIDDEN), HY_EMB ** -0.5),
        'e_filt_b1': nrm((N_EVEN, HY_FILT_HIDDEN), 0.02),
        'e_filt_w2': nrm((N_EVEN, HY_FILT_HIDDEN, HY_FILT_HIDDEN), HY_FILT_HIDDEN ** -0.5),
        'e_filt_b2': nrm((N_EVEN, HY_FILT_HIDDEN), 0.02),
        'e_filt_w3': nrm((N_EVEN, HY_FILT_HIDDEN, HY_ORDER * 2 * HY_WIDTH), HY_FILT_STD),
        'e_filt_freq': gain((N_EVEN, HY_FILT_HIDDEN)),
        'e_hy_bias': nrm((N_EVEN, HY_ORDER, HY_WIDTH), 1.0),
        'e_lam': nrm((N_EVEN, 4, DIFF_HEAD_DIM), 0.1),
        'e_subln': gain((N_EVEN, 2 * DIFF_HEAD_DIM)),
        'e_w_out': nrm((N_EVEN, E_MIX, D), E_MIX ** -0.5),
        'o_norm': gain((N_ODD, D)),
        'o_w_in': nrm((N_ODD, D, O_IN), D ** -0.5),
        'o_sink': nrm((N_ODD, GQA_HEADS), 0.5),
        'o_w_out': nrm((N_ODD, O_Q, D), O_Q ** -0.5),
        'c_norm': gain((DEPTH, D)),
        'c_wq': nrm((DEPTH, D, X_W), D ** -0.5),
        'c_wkv': nrm((DEPTH, D, 2 * X_W), D ** -0.5),
        'c_wo': nrm((DEPTH, X_W, D), X_W ** -0.5),
        'f_norm': gain((DEPTH, D)),
        'f_w_grp': nrm((DEPTH, D, N_GROUPS), D ** -0.5),
        'f_b_grp': nrm((DEPTH, N_GROUPS), 0.01),
        'f_w_exp': nrm((DEPTH, D, N_EXPERTS), D ** -0.5),
        'f_b_exp': nrm((DEPTH, N_EXPERTS), 0.01),
        'f_w_gate': nrm((DEPTH, N_EXPERTS, D, EXP_HIDDEN), D ** -0.5),
        'f_w_up': nrm((DEPTH, N_EXPERTS, D, EXP_HIDDEN), D ** -0.5),
        'f_w_down': nrm((DEPTH, N_EXPERTS, EXP_HIDDEN, D), EXP_HIDDEN ** -0.5),
        'g_mem': gain((D,)),
        'g_final': gain((D,)),
    }


def reference(x, mem, e_norm, e_w_in, e_conv_w, e_conv_b, e_filt_w1, e_filt_b1, e_filt_w2, e_filt_b2,
              e_filt_w3, e_filt_freq, e_hy_bias, e_lam, e_subln, e_w_out, o_norm, o_w_in, o_sink, o_w_out,
              c_norm, c_wq, c_wkv, c_wo, f_norm, f_w_grp, f_b_grp, f_w_exp, f_b_exp, f_w_gate, f_w_up,
              f_w_down, g_mem, g_final):
    memn = rmsnorm(mem, g_mem)
    for i in range(DEPTH):
        j = i // 2
        if i % 2 == 0:
            h = rmsnorm(x, e_norm[j])
            proj = h @ e_w_in[j]
            hy_u, q, k, v = jnp.split(proj, [3 * HY_WIDTH, 3 * HY_WIDTH + DIFF_WIDTH,
                                             3 * HY_WIDTH + 2 * DIFF_WIDTH], axis=-1)
            y_hy = hyena_mixer(hy_u, e_conv_w[j], e_conv_b[j], e_filt_w1[j], e_filt_b1[j], e_filt_w2[j],
                               e_filt_b2[j], e_filt_w3[j], e_filt_freq[j], e_hy_bias[j])
            y_df = diff_attention(q, k, v, e_lam[j], e_subln[j], i)
            x = x + jnp.concatenate([y_hy, y_df], axis=-1) @ e_w_out[j]
        else:
            h = rmsnorm(x, o_norm[j])
            proj = h @ o_w_in[j]
            q, k, v = jnp.split(proj, [O_Q, O_Q + O_KV], axis=-1)
            x = x + window_gqa(q, k, v, o_sink[j]) @ o_w_out[j]
        h = rmsnorm(x, c_norm[i])
        x = x + cross_attention(h, memn, c_wq[i], c_wkv[i], c_wo[i])
        h = rmsnorm(x, f_norm[i])
        x = x + hier_moe(h, f_w_grp[i], f_b_grp[i], f_w_exp[i], f_b_exp[i], f_w_gate[i], f_w_up[i], f_w_down[i])
    return rmsnorm(x, g_final)
```

```python
import functools
import math

import jax
import jax.numpy as jnp
from jax import lax
from jax.experimental import pallas as pl
from jax.experimental.pallas import tpu as pltpu

F32 = jnp.float32
BF16 = jnp.bfloat16
I32 = jnp.int32
U32 = jnp.uint32

D_MODEL = 1024
DEPTH = 4
EPS = 1e-6
NEG = -1e30
HY_WIDTH = 512
HY_BANDS = 16
HY_FILT_HIDDEN = 64
HY_DECAY_TARGET = 1e-2
HY_FAST_PCT = 0.3
HY_SLOW_PCT = 1.5
DIFF_HEADS = 4
DIFF_HEAD_DIM = 64
WIN = 128
GQA_HEADS = 16
GQA_KV = 4
GQA_HD = 64
X_HEADS = 4
X_HD = 128
X_W = X_HEADS * X_HD
N_GROUPS = 4
EXP_PER_GROUP = 8
N_EXPERTS = N_GROUPS * EXP_PER_GROUP
EXP_HIDDEN = 512

LANES = 128
VMEM_LIMIT = 56 * 1024 * 1024
MOE_BLOCK = 256


def _cparams(sem):
    return pltpu.CompilerParams(dimension_semantics=sem, vmem_limit_bytes=VMEM_LIMIT)


def _rms(x, g):
    ms = jnp.mean(x * x, axis=-1, keepdims=True)
    return x * lax.rsqrt(ms + EPS) * g


def _dot(a, b):
    return jnp.dot(a, b, preferred_element_type=F32)


def _dot_nt(a, b):
    return lax.dot_general(a, b, (((1,), (1,)), ((), ())), preferred_element_type=F32)


def _norm_matmul_kernel(x_ref, g_ref, w_ref, o_ref):
    h = _rms(x_ref[...], g_ref[...]).astype(BF16)
    o_ref[...] = _dot(h, w_ref[...]).astype(o_ref.dtype)


def norm_matmul(x2d, gamma, w_bf16, tm, out_dtype=F32):
    n, d = x2d.shape
    f = w_bf16.shape[1]
    return pl.pallas_call(
        _norm_matmul_kernel,
        grid=(n // tm,),
        in_specs=[pl.BlockSpec((tm, d), lambda i: (i, 0)),
                  pl.BlockSpec((1, d), lambda i: (0, 0)),
                  pl.BlockSpec((d, f), lambda i: (0, 0))],
        out_specs=pl.BlockSpec((tm, f), lambda i: (i, 0)),
        out_shape=jax.ShapeDtypeStruct((n, f), out_dtype),
        compiler_params=_cparams(("parallel",)),
        name="norm_matmul",
    )(x2d, gamma.reshape(1, d), w_bf16)


def _matmul_residual_kernel(*refs, n_lhs):
    x_ref = refs[0]
    o_ref = refs[-1]
    acc = x_ref[...]
    for a_ref, w_ref in zip(refs[1:1 + n_lhs], refs[1 + n_lhs:1 + 2 * n_lhs]):
        acc = acc + _dot(a_ref[...].astype(BF16), w_ref[...])
    o_ref[...] = acc


def matmul_residual(x2d, lhs_list, w_list, tm):
    n, d = x2d.shape
    in_specs = [pl.BlockSpec((tm, d), lambda i: (i, 0))]
    in_specs += [pl.BlockSpec((tm, a.shape[1]), lambda i: (i, 0)) for a in lhs_list]
    in_specs += [pl.BlockSpec(w.shape, lambda i: (0, 0)) for w in w_list]
    return pl.pallas_call(
        functools.partial(_matmul_residual_kernel, n_lhs=len(lhs_list)),
        grid=(n // tm,),
        in_specs=in_specs,
        out_specs=pl.BlockSpec((tm, d), lambda i: (i, 0)),
        out_shape=jax.ShapeDtypeStruct((n, d), F32),
        compiler_params=_cparams(("parallel",)),
        name="matmul_residual",
    )(x2d, *lhs_list, *w_list)


def _conv3_kernel(u_ref, w_ref, b_ref, o_ref):
    u = u_ref[0]
    seq = u.shape[0]
    row = lax.broadcasted_iota(I32, u.shape, 0)
    prev = jnp.where(row == 0, 0.0, pltpu.roll(u, 1, 0))
    nxt = jnp.where(row == seq - 1, 0.0, pltpu.roll(u, seq - 1, 0))
    w = w_ref[...]
    o_ref[0] = prev * w[0:1] + u * w[1:2] + nxt * w[2:3] + b_ref[...]


def conv3(proj3d, conv_w, conv_b):
    b, seq, _ = proj3d.shape
    c = 3 * HY_WIDTH
    tc = HY_WIDTH
    return pl.pallas_call(
        _conv3_kernel,
        grid=(b, c // tc),
        in_specs=[pl.BlockSpec((1, seq, tc), lambda i, j: (i, 0, j)),
                  pl.BlockSpec((3, tc), lambda i, j: (0, j)),
                  pl.BlockSpec((1, tc), lambda i, j: (0, j))],
        out_specs=pl.BlockSpec((1, seq, tc), lambda i, j: (i, 0, j)),
        out_shape=jax.ShapeDtypeStruct((b, seq, c), F32),
        compiler_params=_cparams(("parallel", "parallel")),
        name="hyena_conv3",
    )(proj3d, conv_w, conv_b.reshape(1, c))


def _hy_filter_kernel(t_ref, bands_ref, w1t_ref, w1c_ref, w1s_ref, b1_ref, w2_ref, b2_ref, w3_ref,
                      freq_ref, delta_ref, hs_ref, hd_ref, *, seq, tl):
    hi = lax.Precision.HIGHEST
    i = pl.program_id(0)
    t = t_ref[...]
    pos = (i * tl + lax.broadcasted_iota(I32, (tl, 1), 0)).astype(F32)
    ang = bands_ref[...] * (2.0 * math.pi * pos / seq)
    f = freq_ref[...]
    pre = (t * w1t_ref[...]
           + jnp.dot(jnp.cos(ang), w1c_ref[...], precision=hi, preferred_element_type=F32)
           - jnp.dot(jnp.sin(ang), w1s_ref[...], precision=hi, preferred_element_type=F32)
           + b1_ref[...])
    a = jnp.sin(f * pre)
    a = jnp.sin(f * (jnp.dot(a, w2_ref[...], precision=hi, preferred_element_type=F32) + b2_ref[...]))
    h = jnp.dot(a, w3_ref[...], precision=hi, preferred_element_type=F32)
    decay = jnp.exp(-t * jnp.abs(delta_ref[...]))
    w = HY_WIDTH
    for o in range(2):
        fwd = h[:, o * 2 * w:o * 2 * w + w] * decay
        bwd = jnp.where(pos == 0.0, 0.0, h[:, o * 2 * w + w:(o + 1) * 2 * w] * decay)
        hs_ref[:, o * w:(o + 1) * w] = fwd + bwd
        hd_ref[:, o * w:(o + 1) * w] = bwd - fwd


def hyena_filters_time(seq, w1, b1, w2, b2, w3, freq):
    tl = 512
    hid = LANES
    pad_h = hid - HY_FILT_HIDDEN
    t = jnp.linspace(0.0, 1.0, seq, dtype=F32)[:, None]
    bands = jnp.pad(jnp.linspace(1e-4, HY_BANDS - 1, HY_BANDS, dtype=F32)[None], ((0, 0), (0, LANES - HY_BANDS)))
    w1p = jnp.pad(w1, ((0, 0), (0, pad_h)))
    w1t = w1p[0:1]
    w1c = jnp.pad(w1p[1:1 + HY_BANDS], ((0, LANES - HY_BANDS), (0, 0)))
    w1s = jnp.pad(w1p[1 + HY_BANDS:], ((0, LANES - HY_BANDS), (0, 0)))
    b1p = jnp.pad(b1, (0, pad_h)).reshape(1, hid)
    w2p = jnp.pad(w2, ((0, pad_h), (0, pad_h)))
    b2p = jnp.pad(b2, (0, pad_h)).reshape(1, hid)
    w3p = jnp.pad(w3, ((0, pad_h), (0, 0)))
    freqp = jnp.pad(freq, (0, pad_h)).reshape(1, hid)
    max_decay = math.log(HY_DECAY_TARGET) / HY_FAST_PCT
    min_decay = math.log(HY_DECAY_TARGET) / HY_SLOW_PCT
    deltas = jnp.linspace(min_decay, max_decay, HY_WIDTH, dtype=F32)[None]
    fw = w3.shape[1]
    full = lambda shape: pl.BlockSpec(shape, lambda i: (0, 0))
    return pl.pallas_call(
        functools.partial(_hy_filter_kernel, seq=seq, tl=tl),
        grid=(seq // tl,),
        in_specs=[pl.BlockSpec((tl, 1), lambda i: (i, 0)), full((1, LANES)), full((1, hid)),
                  full((LANES, hid)), full((LANES, hid)), full((1, hid)), full((hid, hid)), full((1, hid)),
                  full((hid, fw)), full((1, hid)), full((1, HY_WIDTH))],
        out_specs=[pl.BlockSpec((tl, 2 * HY_WIDTH), lambda i: (i, 0)),
                   pl.BlockSpec((tl, 2 * HY_WIDTH), lambda i: (i, 0))],
        out_shape=[jax.ShapeDtypeStruct((seq, 2 * HY_WIDTH), F32)] * 2,
        compiler_params=_cparams(("parallel",)),
        name="hyena_filter_mlp",
    )(t, bands, w1t, w1c, w1s, b1p, w2p, b2p, w3p, freqp, deltas)


def dft_tables(seq):
    n2 = 4 * seq
    sub = 64
    f = jnp.arange(seq, dtype=I32)[:, None]
    odd = 2 * f + 1
    s1 = jnp.arange(seq // sub, dtype=I32)[None]
    s0 = jnp.arange(sub, dtype=I32)[None]
    ang_p = ((odd * s1) % (n2 // sub)).astype(F32) * (2.0 * math.pi * sub / n2)
    ang_q = ((odd * s0) % n2).astype(F32) * (2.0 * math.pi / n2)
    pc, ps, qc, qs = jnp.cos(ang_p), jnp.sin(ang_p), jnp.cos(ang_q), jnp.sin(ang_q)
    c = (pc[:, :, None] * qc[:, None, :] - ps[:, :, None] * qs[:, None, :]).reshape(seq, seq)
    s = (ps[:, :, None] * qc[:, None, :] + pc[:, :, None] * qs[:, None, :]).reshape(seq, seq)
    return c.astype(BF16), s.astype(BF16), c.T.astype(BF16), s.T.astype(BF16)


def _spectrum_kernel(c_ref, s_ref, hs_ref, hd_ref, kre_ref, kim_ref):
    kre_ref[...] = _dot(c_ref[...], hs_ref[...].astype(BF16))
    kim_ref[...] = _dot(s_ref[...], hd_ref[...].astype(BF16))


def filter_spectrum(cmat, smat, hs, hd):
    seq, cols = hs.shape
    tf, tn = 512, 512
    return pl.pallas_call(
        _spectrum_kernel,
        grid=(seq // tf, cols // tn),
        in_specs=[pl.BlockSpec((tf, seq), lambda i, j: (i, 0)),
                  pl.BlockSpec((tf, seq), lambda i, j: (i, 0)),
                  pl.BlockSpec((seq, tn), lambda i, j: (0, j)),
                  pl.BlockSpec((seq, tn), lambda i, j: (0, j))],
        out_specs=[pl.BlockSpec((tf, tn), lambda i, j: (i, j))] * 2,
        out_shape=[jax.ShapeDtypeStruct((seq, cols), F32)] * 2,
        compiler_params=_cparams(("parallel", "parallel")),
        name="hyena_filter_spectrum",
    )(cmat, smat, hs, hd)


def _hy_fwd_kernel(z_ref, c_ref, s_ref, kre_ref, kim_ref, yre_ref, yim_ref):
    z = z_ref[0].astype(BF16)
    zc = _dot(c_ref[...], z)
    zs = _dot(s_ref[...], z)
    kre = kre_ref[...]
    kim = kim_ref[...]
    yre_ref[0] = (zc * kre + zs * kim).astype(BF16)
    yim_ref[0] = (zc * kim - zs * kre).astype(BF16)


def hyena_fwd(z3d, zcol, cmat, smat, kre, kim, order):
    b, seq, _ = z3d.shape
    w = HY_WIDTH
    tf = 512
    return pl.pallas_call(
        _hy_fwd_kernel,
        grid=(b, seq // tf),
        in_specs=[pl.BlockSpec((1, seq, w), lambda i, j: (i, 0, zcol)),
                  pl.BlockSpec((tf, seq), lambda i, j: (j, 0)),
                  pl.BlockSpec((tf, seq), lambda i, j: (j, 0)),
                  pl.BlockSpec((tf, w), lambda i, j: (j, order)),
                  pl.BlockSpec((tf, w), lambda i, j: (j, order))],
        out_specs=[pl.BlockSpec((1, tf, w), lambda i, j: (i, j, 0))] * 2,
        out_shape=[jax.ShapeDtypeStruct((b, seq, w), BF16)] * 2,
        compiler_params=_cparams(("parallel", "parallel")),
        name="hyena_dft_fwd",
    )(z3d, cmat, smat, kre, kim)


def _hy_inv_kernel(ct_ref, st_ref, yre_ref, yim_ref, z_ref, g_ref, bias_ref, o_ref, *, scale):
    y = (_dot(ct_ref[...], yre_ref[0]) - _dot(st_ref[...], yim_ref[0])) * scale
    o_ref[0] = g_ref[0] * (y + z_ref[0] * bias_ref[...])


def hyena_inv(ctm, stm, yre, yim, z3d, zcol, g3d, gcol, bias_row):
    b, seq, w = yre.shape
    tt = 512
    return pl.pallas_call(
        functools.partial(_hy_inv_kernel, scale=1.0 / seq),
        grid=(b, seq // tt),
        in_specs=[pl.BlockSpec((tt, seq), lambda i, j: (j, 0)),
                  pl.BlockSpec((tt, seq), lambda i, j: (j, 0)),
                  pl.BlockSpec((1, seq, w), lambda i, j: (i, 0, 0)),
                  pl.BlockSpec((1, seq, w), lambda i, j: (i, 0, 0)),
                  pl.BlockSpec((1, tt, w), lambda i, j: (i, j, zcol)),
                  pl.BlockSpec((1, tt, w), lambda i, j: (i, j, gcol)),
                  pl.BlockSpec((1, w), lambda i, j: (0, 0))],
        out_specs=pl.BlockSpec((1, tt, w), lambda i, j: (i, j, 0)),
        out_shape=jax.ShapeDtypeStruct((b, seq, w), F32),
        compiler_params=_cparams(("parallel", "parallel")),
        name="hyena_dft_inv",
    )(ctm, stm, yre, yim, z3d, g3d, bias_row)


def hyena_mixer(proj3d, tables, conv_w, conv_b, w1, b1, w2, b2, w3, freq, hy_bias):
    seq = proj3d.shape[1]
    cmat, smat, ctm, stm = tables
    u = conv3(proj3d, conv_w, conv_b)
    hs, hd = hyena_filters_time(seq, w1, b1, w2, b2, w3, freq)
    kre, kim = filter_spectrum(cmat, smat, hs, hd)
    yre, yim = hyena_fwd(u, 0, cmat, smat, kre, kim, 0)
    z1 = hyena_inv(ctm, stm, yre, yim, u, 0, u, 1, hy_bias[0:1])
    yre, yim = hyena_fwd(z1, 0, cmat, smat, kre, kim, 1)
    return hyena_inv(ctm, stm, yre, yim, z1, 0, u, 2, hy_bias[1:2])


def _diff_attn_kernel(slope_ref, q_ref, k_ref, v_ref, lam_ref, sub_ref, o_ref, *, tq, lam_init):
    h = pl.program_id(1)
    qi = pl.program_id(2)
    q = q_ref[0] * (DIFF_HEAD_DIM ** -0.5)
    k = k_ref[0].astype(BF16)
    v = v_ref[0].astype(BF16)
    seq = k.shape[0]
    lane = lax.broadcasted_iota(I32, (1, 2 * DIFF_HEAD_DIM), 1)
    qpos = qi * tq + lax.broadcasted_iota(I32, (tq, seq), 0)
    kpos = lax.broadcasted_iota(I32, (tq, seq), 1)
    bias = slope_ref[h] * jnp.abs(qpos - kpos).astype(F32)

    def probs(m):
        keep = (lane < DIFF_HEAD_DIM) if m == 0 else (lane >= DIFF_HEAD_DIM)
        qm = jnp.where(keep, q, 0.0).astype(BF16)
        s = _dot_nt(qm, k) - bias
        e = jnp.exp(s - jnp.max(s, axis=-1, keepdims=True))
        return e / jnp.sum(e, axis=-1, keepdims=True)

    l = lam_ref[...]
    lam_full = (jnp.exp(jnp.sum(l[0:1] * l[1:2], axis=-1, keepdims=True))
                - jnp.exp(jnp.sum(l[2:3] * l[3:4], axis=-1, keepdims=True)) + lam_init)
    a = probs(0) - lam_full * probs(1)
    o = _dot(a.astype(BF16), v)
    o_ref[0] = _rms(o, sub_ref[...]) * (1.0 - lam_init)


def diff_attention(proj3d, lam, subln, layer_idx):
    b, seq, _ = proj3d.shape
    tq = 256
    hw = 2 * DIFF_HEAD_DIM
    qb, kb, vb = 3 * HY_WIDTH // hw, (3 * HY_WIDTH + 512) // hw, (3 * HY_WIDTH + 1024) // hw
    lam_init = 0.8 - 0.6 * math.exp(-0.3 * layer_idx)
    slopes = 2.0 ** (-8.0 * jnp.arange(1, DIFF_HEADS + 1, dtype=F32) / DIFF_HEADS)
    return pl.pallas_call(
        functools.partial(_diff_attn_kernel, tq=tq, lam_init=lam_init),
        grid=(b, DIFF_HEADS, seq // tq),
        in_specs=[pl.BlockSpec(memory_space=pltpu.SMEM),
                  pl.BlockSpec((1, tq, hw), lambda i, h, j: (i, j, qb + h)),
                  pl.BlockSpec((1, seq, hw), lambda i, h, j: (i, 0, kb + h)),
                  pl.BlockSpec((1, seq, hw), lambda i, h, j: (i, 0, vb + h)),
                  pl.BlockSpec((4, DIFF_HEAD_DIM), lambda i, h, j: (0, 0)),
                  pl.BlockSpec((1, hw), lambda i, h, j: (0, 0))],
        out_specs=pl.BlockSpec((1, tq, hw), lambda i, h, j: (i, j, h)),
        out_shape=jax.ShapeDtypeStruct((b, seq, DIFF_HEADS * hw), F32),
        compiler_params=_cparams(("parallel", "parallel", "parallel")),
        name="diff_attention",
    )(slopes, proj3d, proj3d, proj3d, lam, subln.reshape(1, hw))


def _win_attn_kernel(slope_ref, sink_ref, q_ref, k_ref, v_ref, o_ref, *, tq, span):
    qi = pl.program_id(1)
    seq = k_ref.shape[1]
    q0 = qi * tq
    kstart = pl.multiple_of(jnp.clip(q0 - WIN, 0, seq - span), LANES)
    kwin = k_ref[0, pl.ds(kstart, span), :]
    vwin = v_ref[0, pl.ds(kstart, span), :]
    group = GQA_HEADS // GQA_KV
    rows = group * tq
    lane = lax.broadcasted_iota(I32, (1, LANES), 1)
    low = lane < GQA_HD
    row = lax.broadcasted_iota(I32, (rows, span), 0)
    qpos = q0 + (row % tq)
    kpos = kstart + lax.broadcasted_iota(I32, (rows, span), 1)
    rel = jnp.abs(qpos - kpos)
    valid = rel <= WIN
    relf = rel.astype(F32)
    rsel = lax.broadcasted_iota(I32, (rows, 1), 0) // tq
    scale = GQA_HD ** -0.5
    for kv in range(GQA_KV):
        pair = kv // 2
        mine = low if kv % 2 == 0 else jnp.logical_not(low)
        kp = jnp.where(mine, kwin[:, pair * LANES:(pair + 1) * LANES], 0.0)
        vp = jnp.where(mine, vwin[:, pair * LANES:(pair + 1) * LANES], 0.0)
        kd = (kp + pltpu.roll(kp, GQA_HD, 1)).astype(BF16)
        vd = (vp + pltpu.roll(vp, GQA_HD, 1)).astype(BF16)
        parts = []
        for g in range(group):
            head = kv * group + g
            qp = q_ref[0, :, (head // 2) * LANES:(head // 2 + 1) * LANES] * scale
            parts.append(jnp.where(low if head % 2 == 0 else jnp.logical_not(low), qp, 0.0))
        qs = jnp.concatenate(parts, axis=0).astype(BF16)
        slope = jnp.zeros((rows, 1), F32)
        sink = jnp.zeros((rows, 1), F32)
        for g in range(group):
            slope = jnp.where(rsel == g, slope_ref[kv * group + g], slope)
            sink = jnp.where(rsel == g, sink_ref[kv * group + g], sink)
        s = _dot_nt(qs, kd) - slope * relf
        s = jnp.where(valid, s, NEG)
        m = jnp.maximum(jnp.max(s, axis=-1, keepdims=True), sink)
        e = jnp.exp(s - m)
        p = e / (jnp.sum(e, axis=-1, keepdims=True) + jnp.exp(sink - m))
        o = _dot(p.astype(BF16), vd)
        for j in range(group // 2):
            blk = jnp.where(low, o[(2 * j) * tq:(2 * j + 1) * tq], o[(2 * j + 1) * tq:(2 * j + 2) * tq])
            col = (kv * group // 2 + j) * LANES
            o_ref[0, :, col:col + LANES] = blk


def window_gqa(proj3d, sink):
    b, seq, _ = proj3d.shape
    tq = 128
    span = tq + 2 * WIN
    oq = GQA_HEADS * GQA_HD
    okv = GQA_KV * GQA_HD
    slopes = 2.0 ** (-8.0 * jnp.arange(1, GQA_HEADS + 1, dtype=F32) / GQA_HEADS)
    return pl.pallas_call(
        functools.partial(_win_attn_kernel, tq=tq, span=span),
        grid=(b, seq // tq),
        in_specs=[pl.BlockSpec(memory_space=pltpu.SMEM),
                  pl.BlockSpec(memory_space=pltpu.SMEM),
                  pl.BlockSpec((1, tq, oq), lambda i, j: (i, j, 0)),
                  pl.BlockSpec((1, seq, okv), lambda i, j: (i, 0, oq // okv)),
                  pl.BlockSpec((1, seq, okv), lambda i, j: (i, 0, oq // okv + 1))],
        out_specs=pl.BlockSpec((1, tq, oq), lambda i, j: (i, j, 0)),
        out_shape=jax.ShapeDtypeStruct((b, seq, oq), F32),
        compiler_params=_cparams(("parallel", "parallel")),
        name="window_gqa",
    )(slopes, sink.astype(F32), proj3d, proj3d, proj3d)


def _cross_block_kernel(x_ref, g_ref, wq_ref, kv_ref, wo_ref, o_ref):
    x = x_ref[0]
    h = _rms(x, g_ref[...]).astype(BF16)
    q = _dot(h, wq_ref[...])
    kv = kv_ref[0]
    scale = X_HD ** -0.5
    outs = []
    for hd in range(X_HEADS):
        qh = q[:, hd * X_HD:(hd + 1) * X_HD].astype(BF16)
        kh = kv[:, hd * X_HD:(hd + 1) * X_HD]
        vh = kv[:, X_W + hd * X_HD:X_W + (hd + 1) * X_HD]
        s = _dot_nt(qh, kh) * scale
        e = jnp.exp(s - jnp.max(s, axis=-1, keepdims=True))
        p = e / jnp.sum(e, axis=-1, keepdims=True)
        outs.append(_dot(p.astype(BF16), vh))
    o = jnp.concatenate(outs, axis=-1).astype(BF16)
    o_ref[0] = x + _dot(o, wo_ref[...])


def cross_block(x3d, gamma, wq_bf16, kv_bf16, wo_bf16):
    b, seq, d = x3d.shape
    m = kv_bf16.shape[1]
    tm = 512
    return pl.pallas_call(
        _cross_block_kernel,
        grid=(b, seq // tm),
        in_specs=[pl.BlockSpec((1, tm, d), lambda i, j: (i, j, 0)),
                  pl.BlockSpec((1, d), lambda i, j: (0, 0)),
                  pl.BlockSpec((d, X_W), lambda i, j: (0, 0)),
                  pl.BlockSpec((1, m, 2 * X_W), lambda i, j: (i, 0, 0)),
                  pl.BlockSpec((X_W, d), lambda i, j: (0, 0))],
        out_specs=pl.BlockSpec((1, tm, d), lambda i, j: (i, j, 0)),
        out_shape=jax.ShapeDtypeStruct((b, seq, d), F32),
        compiler_params=_cparams(("parallel", "parallel")),
        name="cross_attention_block",
    )(x3d, gamma.reshape(1, d), wq_bf16, kv_bf16, wo_bf16)


def _split_bf16(x):
    hi = x.astype(BF16)
    lo = (x - hi.astype(F32)).astype(BF16)
    return hi, lo


def _pack_halves(h):
    c = h.shape[1] // 2
    left = lax.bitcast_convert_type(h[:, :c].astype(BF16).astype(F32), U32)
    right = lax.bitcast_convert_type(h[:, c:].astype(BF16).astype(F32), U32)
    return left | (right >> 16)


def _unpack_halves(p):
    left = lax.bitcast_convert_type(p & jnp.uint32(0xFFFF0000), F32)
    right = lax.bitcast_convert_type(p << 16, F32)
    return jnp.concatenate([left, right], axis=-1).astype(BF16)


def _router_kernel(x_ref, g_ref, whi_ref, wlo_ref, b_ref, hp_ref, eid_ref, wt_ref, rank_ref, cnt_ref,
                   base_ref, *, tm):
    i = pl.program_id(0)

    @pl.when(i == 0)
    def _():
        base_ref[...] = jnp.zeros_like(base_ref)

    h = _rms(x_ref[...], g_ref[...])
    hp_ref[...] = _pack_halves(h)
    hhi, hlo = _split_bf16(h)
    whi = whi_ref[...]
    logits = _dot(hhi, whi) + _dot(hlo, whi) + _dot(hhi, wlo_ref[...]) + b_ref[...]
    lane = lax.broadcasted_iota(I32, logits.shape, 1)
    big = jnp.int32(LANES)
    ninf = -jnp.inf

    gl = jnp.where(lane < N_GROUPS, logits, ninf)
    gmax = jnp.max(gl, axis=-1, keepdims=True)
    gsel = jnp.min(jnp.where(gl == gmax, lane, big), axis=-1, keepdims=True)
    ggate = 1.0 / jnp.sum(jnp.exp(gl - gmax), axis=-1, keepdims=True)

    lo_lane = N_GROUPS + gsel * EXP_PER_GROUP
    el = jnp.where((lane >= lo_lane) & (lane < lo_lane + EXP_PER_GROUP), logits, ninf)
    v1 = jnp.max(el, axis=-1, keepdims=True)
    i1 = jnp.min(jnp.where(el == v1, lane, big), axis=-1, keepdims=True)
    el2 = jnp.where(lane == i1, ninf, el)
    v2 = jnp.max(el2, axis=-1, keepdims=True)
    i2 = jnp.min(jnp.where(el2 == v2, lane, big), axis=-1, keepdims=True)
    e2 = jnp.exp(v2 - v1)
    w1 = ggate / (1.0 + e2)
    w2 = ggate * e2 / (1.0 + e2)

    one1 = lane == i1
    one2 = lane == i2
    onehot = (one1 | one2).astype(F32)
    r = lax.broadcasted_iota(I32, (tm, tm), 0)
    c = lax.broadcasted_iota(I32, (tm, tm), 1)
    tri = (c < r).astype(BF16)
    before = _dot(tri, onehot.astype(BF16)) + base_ref[...]
    rank1 = jnp.sum(jnp.where(one1, before, 0.0), axis=-1, keepdims=True)
    rank2 = jnp.sum(jnp.where(one2, before, 0.0), axis=-1, keepdims=True)
    total = base_ref[...] + jnp.sum(onehot, axis=0, keepdims=True)
    base_ref[...] = total
    cnt_ref[...] = total

    col = lax.broadcasted_iota(I32, (tm, 2), 1)
    eid_ref[...] = jnp.where(col == 0, i1, i2) - N_GROUPS
    wt_ref[...] = jnp.where(col == 0, w1, w2)
    rank_ref[...] = jnp.where(col == 0, rank1, rank2).astype(I32)


def moe_router(x2d, gamma, w_grp, b_grp, w_exp, b_exp):
    n, d = x2d.shape
    tm = 512
    wcat = jnp.pad(jnp.concatenate([w_grp, w_exp], axis=1), ((0, 0), (0, LANES - N_GROUPS - N_EXPERTS)))
    bcat = jnp.pad(jnp.concatenate([b_grp, b_exp]), (0, LANES - N_GROUPS - N_EXPERTS)).reshape(1, LANES)
    whi = wcat.astype(BF16)
    wlo = (wcat - whi.astype(F32)).astype(BF16)
    return pl.pallas_call(
        functools.partial(_router_kernel, tm=tm),
        grid=(n // tm,),
        in_specs=[pl.BlockSpec((tm, d), lambda i: (i, 0)),
                  pl.BlockSpec((1, d), lambda i: (0, 0)),
                  pl.BlockSpec((d, LANES), lambda i: (0, 0)),
                  pl.BlockSpec((d, LANES), lambda i: (0, 0)),
                  pl.BlockSpec((1, LANES), lambda i: (0, 0))],
        out_specs=[pl.BlockSpec((tm, d // 2), lambda i: (i, 0)),
                   pl.BlockSpec((tm, 2), lambda i: (i, 0)),
                   pl.BlockSpec((tm, 2), lambda i: (i, 0)),
                   pl.BlockSpec((tm, 2), lambda i: (i, 0)),
                   pl.BlockSpec((1, LANES), lambda i: (0, 0))],
        out_shape=[jax.ShapeDtypeStruct((n, d // 2), U32),
                   jax.ShapeDtypeStruct((n, 2), I32),
                   jax.ShapeDtypeStruct((n, 2), F32),
                   jax.ShapeDtypeStruct((n, 2), I32),
                   jax.ShapeDtypeStruct((1, LANES), F32)],
        scratch_shapes=[pltpu.VMEM((1, LANES), F32)],
        compiler_params=_cparams(("arbitrary",)),
        name="moe_router",
    )(x2d, gamma.reshape(1, d), whi, wlo, bcat)


def _row_copy(src_ref, src_row, dst_ref, dst_row, sem):
    return pltpu.make_async_copy(src_ref.at[pl.ds(src_row, 1)], dst_ref.at[pl.ds(dst_row, 1)], sem)


def _dispatch_kernel(dest_ref, hp_ref, xs_in_ref, xs_ref, sem, *, tm):
    del xs_in_ref

    def issue(r, carry):
        _row_copy(hp_ref, r, xs_ref, dest_ref[2 * r], sem).start()
        _row_copy(hp_ref, r, xs_ref, dest_ref[2 * r + 1], sem).start()
        return carry

    lax.fori_loop(0, tm, issue, 0)

    def drain(r, carry):
        _row_copy(hp_ref, r, xs_ref, dest_ref[2 * r], sem).wait()
        _row_copy(hp_ref, r, xs_ref, dest_ref[2 * r + 1], sem).wait()
        return carry

    lax.fori_loop(0, tm, drain, 0)


def moe_dispatch(hp, dest_flat, cap):
    n, c = hp.shape
    tm = 512
    zeros = jnp.zeros((cap, c), U32)
    return pl.pallas_call(
        functools.partial(_dispatch_kernel, tm=tm),
        grid=(n // tm,),
        in_specs=[pl.BlockSpec((2 * tm,), lambda i: (i,), memory_space=pltpu.SMEM),
                  pl.BlockSpec((tm, c), lambda i: (i, 0)),
                  pl.BlockSpec(memory_space=pl.ANY)],
        out_specs=pl.BlockSpec(memory_space=pl.ANY),
        out_shape=jax.ShapeDtypeStruct((cap, c), U32),
        scratch_shapes=[pltpu.SemaphoreType.DMA(())],
        input_output_aliases={2: 0},
        compiler_params=_cparams(("arbitrary",)),
        name="moe_dispatch",
    )(dest_flat, hp, zeros)


def _expert_kernel(be_ref, nused_ref, xs_ref, wg_ref, wu_ref, wd_ref, y_ref, wg_s, wu_s, wd_s):
    b = pl.program_id(0)
    prev = be_ref[jnp.maximum(b - 1, 0)]
    active = b < nused_ref[0]

    @pl.when(active & ((b == 0) | (be_ref[b] != prev)))
    def _():
        wg_s[...] = wg_ref[0].astype(BF16)
        wu_s[...] = wu_ref[0].astype(BF16)
        wd_s[...] = wd_ref[0].astype(BF16)

    @pl.when(active)
    def _():
        x = _unpack_halves(xs_ref[...])
        a = _dot(x, wg_s[...])
        u = _dot(x, wu_s[...])
        hmid = (a / (1.0 + jnp.exp(-a)) * u).astype(BF16)
        y_ref[...] = _dot(hmid, wd_s[...])

    @pl.when(jnp.logical_not(active))
    def _():
        y_ref[...] = jnp.zeros_like(y_ref)


def moe_experts(xs, blk_e, nused, w_gate, w_up, w_down):
    cap, c = xs.shape
    d = 2 * c
    hid = w_gate.shape[2]
    nblk = cap // MOE_BLOCK
    grid_spec = pltpu.PrefetchScalarGridSpec(
        num_scalar_prefetch=2,
        grid=(nblk,),
        in_specs=[pl.BlockSpec((MOE_BLOCK, c), lambda b, be, nu: (b, 0)),
                  pl.BlockSpec((1, d, hid), lambda b, be, nu: (be[b], 0, 0)),
                  pl.BlockSpec((1, d, hid), lambda b, be, nu: (be[b], 0, 0)),
                  pl.BlockSpec((1, hid, d), lambda b, be, nu: (be[b], 0, 0))],
        out_specs=pl.BlockSpec((MOE_BLOCK, d), lambda b, be, nu: (b, 0)),
        scratch_shapes=[pltpu.VMEM((d, hid), BF16), pltpu.VMEM((d, hid), BF16), pltpu.VMEM((hid, d), BF16)],
    )
    return pl.pallas_call(
        _expert_kernel,
        grid_spec=grid_spec,
        out_shape=jax.ShapeDtypeStruct((cap, d), F32),
        compiler_params=_cparams(("arbitrary",)),
        name="moe_experts",
    )(blk_e, nused, xs, w_gate, w_up, w_down)


def _combine_kernel(dest_ref, x_ref, wt_ref, g_ref, yb_ref, o_ref, buf, sem, *, tm, final_norm):
    def issue(r, carry):
        _row_copy(yb_ref, dest_ref[2 * r], buf.at[0], r, sem).start()
        _row_copy(yb_ref, dest_ref[2 * r + 1], buf.at[1], r, sem).start()
        return carry

    lax.fori_loop(0, tm, issue, 0)

    def drain(r, carry):
        _row_copy(yb_ref, dest_ref[2 * r], buf.at[0], r, sem).wait()
        _row_copy(yb_ref, dest_ref[2 * r + 1], buf.at[1], r, sem).wait()
        return carry

    lax.fori_loop(0, tm, drain, 0)
    wt = wt_ref[...]
    y = x_ref[...] + (buf[0] * wt[:, 0:1] + buf[1] * wt[:, 1:2])
    o_ref[...] = _rms(y, g_ref[...]) if final_norm else y


def moe_combine(x2d, yb, dest_flat, wt, g_final, final_norm):
    n, d = x2d.shape
    tm = 256
    return pl.pallas_call(
        functools.partial(_combine_kernel, tm=tm, final_norm=final_norm),
        grid=(n // tm,),
        in_specs=[pl.BlockSpec((2 * tm,), lambda i: (i,), memory_space=pltpu.SMEM),
                  pl.BlockSpec((tm, d), lambda i: (i, 0)),
                  pl.BlockSpec((tm, 2), lambda i: (i, 0)),
                  pl.BlockSpec((1, d), lambda i: (0, 0)),
                  pl.BlockSpec(memory_space=pl.ANY)],
        out_specs=pl.BlockSpec((tm, d), lambda i: (i, 0)),
        out_shape=jax.ShapeDtypeStruct((n, d), F32),
        scratch_shapes=[pltpu.VMEM((2, tm, d), F32), pltpu.SemaphoreType.DMA(())],
        compiler_params=_cparams(("arbitrary",)),
        name="moe_combine",
    )(dest_flat, x2d, wt, g_final.reshape(1, d), yb)


def hier_moe_block(x2d, gamma, w_grp, b_grp, w_exp, b_exp, w_gate, w_up, w_down, g_final, final_norm):
    n = x2d.shape[0]
    cap = 2 * n + N_EXPERTS * MOE_BLOCK
    hp, eid, wt, rank, cnt = moe_router(x2d, gamma, w_grp, b_grp, w_exp, b_exp)
    counts = cnt[0, N_GROUPS:N_GROUPS + N_EXPERTS].astype(I32)
    padded = (counts + MOE_BLOCK - 1) // MOE_BLOCK * MOE_BLOCK
    p_ends = jnp.cumsum(padded)
    p_starts = p_ends - padded
    dest = (p_starts[eid] + rank).reshape(-1)
    nblk = cap // MOE_BLOCK
    blk_e = jnp.minimum(jnp.searchsorted(p_ends, jnp.arange(nblk, dtype=I32) * MOE_BLOCK, side='right'),
                        N_EXPERTS - 1).astype(I32)
    nused = (p_ends[-1:] // MOE_BLOCK).astype(I32)
    xs = moe_dispatch(hp, dest, cap)
    yb = moe_experts(xs, blk_e, nused, w_gate, w_up, w_down)
    return moe_combine(x2d, yb, dest, wt, g_final, final_norm)


def kernel(x, mem, e_norm, e_w_in, e_conv_w, e_conv_b, e_filt_w1, e_filt_b1, e_filt_w2, e_filt_b2, e_filt_w3, e_filt_freq, e_hy_bias, e_lam, e_subln, e_w_out, o_norm, o_w_in, o_sink, o_w_out, c_norm, c_wq, c_wkv, c_wo, f_norm, f_w_grp, f_b_grp, f_w_exp, f_b_exp, f_w_gate, f_w_up, f_w_down, g_mem, g_final):
    b, seq, d = x.shape
    n = b * seq
    m = mem.shape[1]
    tables = dft_tables(seq)
    x2 = x.reshape(n, d)
    mem2 = mem.reshape(b * m, d)
    for i in range(DEPTH):
        j = i // 2
        if i % 2 == 0:
            proj = norm_matmul(x2, e_norm[j], e_w_in[j].astype(BF16), 512).reshape(b, seq, -1)
            y_hy = hyena_mixer(proj, tables, e_conv_w[j], e_conv_b[j], e_filt_w1[j], e_filt_b1[j], e_filt_w2[j],
                               e_filt_b2[j], e_filt_w3[j], e_filt_freq[j], e_hy_bias[j])
            y_df = diff_attention(proj, e_lam[j], e_subln[j], i)
            w_out = e_w_out[j].astype(BF16)
            x2 = matmul_residual(x2, [y_hy.reshape(n, -1), y_df.reshape(n, -1)],
                                 [w_out[:HY_WIDTH], w_out[HY_WIDTH:]], 512)
        else:
            proj = norm_matmul(x2, o_norm[j], o_w_in[j].astype(BF16), 512).reshape(b, seq, -1)
            att = window_gqa(proj, o_sink[j])
            x2 = matmul_residual(x2, [att.reshape(n, -1)], [o_w_out[j].astype(BF16)], 512)
        kv = norm_matmul(mem2, g_mem, c_wkv[i].astype(BF16), 512, out_dtype=BF16).reshape(b, m, -1)
        x2 = cross_block(x2.reshape(b, seq, d), c_norm[i], c_wq[i].astype(BF16), kv,
                         c_wo[i].astype(BF16)).reshape(n, d)
        x2 = hier_moe_block(x2, f_norm[i], f_w_grp[i], f_b_grp[i], f_w_exp[i], f_b_exp[i], f_w_gate[i],
                            f_w_up[i], f_w_down[i], g_final, i == DEPTH - 1)
    return x2.reshape(b, seq, d)
```

```python
import functools
import math

import jax
import jax.numpy as jnp
from jax import lax
from jax.experimental import pallas as pl
from jax.experimental.pallas import tpu as pltpu

F32 = jnp.float32
BF16 = jnp.bfloat16
I32 = jnp.int32
U32 = jnp.uint32

D_MODEL = 1024
DEPTH = 4
EPS = 1e-6
NEG = -1e30
HY_WIDTH = 512
HY_BANDS = 16
HY_FILT_HIDDEN = 64
HY_DECAY_TARGET = 1e-2
HY_FAST_PCT = 0.3
HY_SLOW_PCT = 1.5
DIFF_HEADS = 4
DIFF_HEAD_DIM = 64
WIN = 128
GQA_HEADS = 16
GQA_KV = 4
GQA_HD = 64
X_HEADS = 4
X_HD = 128
X_W = X_HEADS * X_HD
N_GROUPS = 4
EXP_PER_GROUP = 8
N_EXPERTS = N_GROUPS * EXP_PER_GROUP
EXP_HIDDEN = 512

LOG2E = 1.4426950408889634
LANES = 128
VMEM_LIMIT = 56 * 1024 * 1024
MOE_BLOCK = 256
ROW_DMA_UNROLL = 8


def _cparams(sem):
    return pltpu.CompilerParams(dimension_semantics=sem, vmem_limit_bytes=VMEM_LIMIT)


def _rms(x, g):
    ms = jnp.mean(x * x, axis=-1, keepdims=True)
    return x * lax.rsqrt(ms + EPS) * g


def _dot(a, b):
    return jnp.dot(a, b, preferred_element_type=F32)


def _dot_nt(a, b):
    return lax.dot_general(a, b, (((1,), (1,)), ((), ())), preferred_element_type=F32)


def _norm_matmul_kernel(x_ref, g_ref, w_ref, o_ref):
    h = _rms(x_ref[...], g_ref[...]).astype(BF16)
    o_ref[...] = _dot(h, w_ref[...]).astype(o_ref.dtype)


def norm_matmul(x2d, gamma, w_bf16, tm, out_dtype=F32):
    n, d = x2d.shape
    f = w_bf16.shape[1]
    return pl.pallas_call(
        _norm_matmul_kernel,
        grid=(n // tm,),
        in_specs=[pl.BlockSpec((tm, d), lambda i: (i, 0)),
                  pl.BlockSpec((1, d), lambda i: (0, 0)),
                  pl.BlockSpec((d, f), lambda i: (0, 0))],
        out_specs=pl.BlockSpec((tm, f), lambda i: (i, 0)),
        out_shape=jax.ShapeDtypeStruct((n, f), out_dtype),
        compiler_params=_cparams(("parallel",)),
        name="norm_matmul",
    )(x2d, gamma.reshape(1, d), w_bf16)


def _matmul_residual_kernel(*refs, n_lhs):
    x_ref = refs[0]
    o_ref = refs[-1]
    acc = x_ref[...]
    for a_ref, w_ref in zip(refs[1:1 + n_lhs], refs[1 + n_lhs:1 + 2 * n_lhs]):
        acc = acc + _dot(a_ref[...].astype(BF16), w_ref[...])
    o_ref[...] = acc


def matmul_residual(x2d, lhs_list, w_list, tm):
    n, d = x2d.shape
    in_specs = [pl.BlockSpec((tm, d), lambda i: (i, 0))]
    in_specs += [pl.BlockSpec((tm, a.shape[1]), lambda i: (i, 0)) for a in lhs_list]
    in_specs += [pl.BlockSpec(w.shape, lambda i: (0, 0)) for w in w_list]
    return pl.pallas_call(
        functools.partial(_matmul_residual_kernel, n_lhs=len(lhs_list)),
        grid=(n // tm,),
        in_specs=in_specs,
        out_specs=pl.BlockSpec((tm, d), lambda i: (i, 0)),
        out_shape=jax.ShapeDtypeStruct((n, d), F32),
        compiler_params=_cparams(("parallel",)),
        name="matmul_residual",
    )(x2d, *lhs_list, *w_list)


def _conv3_kernel(u_ref, w_ref, b_ref, o_ref):
    u = u_ref[0]
    seq = u.shape[0]
    row = lax.broadcasted_iota(I32, u.shape, 0)
    prev = jnp.where(row == 0, 0.0, pltpu.roll(u, 1, 0))
    nxt = jnp.where(row == seq - 1, 0.0, pltpu.roll(u, seq - 1, 0))
    w = w_ref[...]
    o_ref[0] = prev * w[0:1] + u * w[1:2] + nxt * w[2:3] + b_ref[...]


def conv3(proj3d, conv_w, conv_b):
    b, seq, _ = proj3d.shape
    c = 3 * HY_WIDTH
    tc = HY_WIDTH
    return pl.pallas_call(
        _conv3_kernel,
        grid=(b, c // tc),
        in_specs=[pl.BlockSpec((1, seq, tc), lambda i, j: (i, 0, j)),
                  pl.BlockSpec((3, tc), lambda i, j: (0, j)),
                  pl.BlockSpec((1, tc), lambda i, j: (0, j))],
        out_specs=pl.BlockSpec((1, seq, tc), lambda i, j: (i, 0, j)),
        out_shape=jax.ShapeDtypeStruct((b, seq, c), F32),
        compiler_params=_cparams(("parallel", "parallel")),
        name="hyena_conv3",
    )(proj3d, conv_w, conv_b.reshape(1, c))


def _hy_filter_kernel(t_ref, bands_ref, w1t_ref, w1c_ref, w1s_ref, b1_ref, w2_ref, b2_ref, w3_ref,
                      freq_ref, delta_ref, hs_ref, hd_ref, *, seq, tl):
    hi = lax.Precision.HIGHEST
    i = pl.program_id(0)
    t = t_ref[...]
    pos = (i * tl + lax.broadcasted_iota(I32, (tl, 1), 0)).astype(F32)
    ang = bands_ref[...] * (2.0 * math.pi * pos / seq)
    f = freq_ref[...]
    pre = (t * w1t_ref[...]
           + jnp.dot(jnp.cos(ang), w1c_ref[...], precision=hi, preferred_element_type=F32)
           - jnp.dot(jnp.sin(ang), w1s_ref[...], precision=hi, preferred_element_type=F32)
           + b1_ref[...])
    a = jnp.sin(f * pre)
    a = jnp.sin(f * (jnp.dot(a, w2_ref[...], precision=hi, preferred_element_type=F32) + b2_ref[...]))
    h = jnp.dot(a, w3_ref[...], precision=hi, preferred_element_type=F32)
    decay = jnp.exp(-t * jnp.abs(delta_ref[...]))
    w = HY_WIDTH
    for o in range(2):
        fwd = h[:, o * 2 * w:o * 2 * w + w] * decay
        bwd = jnp.where(pos == 0.0, 0.0, h[:, o * 2 * w + w:(o + 1) * 2 * w] * decay)
        hs_ref[:, o * w:(o + 1) * w] = fwd + bwd
        hd_ref[:, o * w:(o + 1) * w] = bwd - fwd


def hyena_filters_time(seq, w1, b1, w2, b2, w3, freq):
    tl = 512
    hid = LANES
    pad_h = hid - HY_FILT_HIDDEN
    t = jnp.linspace(0.0, 1.0, seq, dtype=F32)[:, None]
    bands = jnp.pad(jnp.linspace(1e-4, HY_BANDS - 1, HY_BANDS, dtype=F32)[None], ((0, 0), (0, LANES - HY_BANDS)))
    w1p = jnp.pad(w1, ((0, 0), (0, pad_h)))
    w1t = w1p[0:1]
    w1c = jnp.pad(w1p[1:1 + HY_BANDS], ((0, LANES - HY_BANDS), (0, 0)))
    w1s = jnp.pad(w1p[1 + HY_BANDS:], ((0, LANES - HY_BANDS), (0, 0)))
    b1p = jnp.pad(b1, (0, pad_h)).reshape(1, hid)
    w2p = jnp.pad(w2, ((0, pad_h), (0, pad_h)))
    b2p = jnp.pad(b2, (0, pad_h)).reshape(1, hid)
    w3p = jnp.pad(w3, ((0, pad_h), (0, 0)))
    freqp = jnp.pad(freq, (0, pad_h)).reshape(1, hid)
    max_decay = math.log(HY_DECAY_TARGET) / HY_FAST_PCT
    min_decay = math.log(HY_DECAY_TARGET) / HY_SLOW_PCT
    deltas = jnp.linspace(min_decay, max_decay, HY_WIDTH, dtype=F32)[None]
    fw = w3.shape[1]
    full = lambda shape: pl.BlockSpec(shape, lambda i: (0, 0))
    return pl.pallas_call(
        functools.partial(_hy_filter_kernel, seq=seq, tl=tl),
        grid=(seq // tl,),
        in_specs=[pl.BlockSpec((tl, 1), lambda i: (i, 0)), full((1, LANES)), full((1, hid)),
                  full((LANES, hid)), full((LANES, hid)), full((1, hid)), full((hid, hid)), full((1, hid)),
                  full((hid, fw)), full((1, hid)), full((1, HY_WIDTH))],
        out_specs=[pl.BlockSpec((tl, 2 * HY_WIDTH), lambda i: (i, 0)),
                   pl.BlockSpec((tl, 2 * HY_WIDTH), lambda i: (i, 0))],
        out_shape=[jax.ShapeDtypeStruct((seq, 2 * HY_WIDTH), F32)] * 2,
        compiler_params=_cparams(("parallel",)),
        name="hyena_filter_mlp",
    )(t, bands, w1t, w1c, w1s, b1p, w2p, b2p, w3p, freqp, deltas)


def dft_tables(seq):
    n2 = 4 * seq
    sub = 64
    f = jnp.arange(seq, dtype=I32)[:, None]
    odd = 2 * f + 1
    s1 = jnp.arange(seq // sub, dtype=I32)[None]
    s0 = jnp.arange(sub, dtype=I32)[None]
    ang_p = ((odd * s1) % (n2 // sub)).astype(F32) * (2.0 * math.pi * sub / n2)
    ang_q = ((odd * s0) % n2).astype(F32) * (2.0 * math.pi / n2)
    pc, ps, qc, qs = jnp.cos(ang_p), jnp.sin(ang_p), jnp.cos(ang_q), jnp.sin(ang_q)
    c = (pc[:, :, None] * qc[:, None, :] - ps[:, :, None] * qs[:, None, :]).reshape(seq, seq)
    s = (ps[:, :, None] * qc[:, None, :] + pc[:, :, None] * qs[:, None, :]).reshape(seq, seq)
    return c.astype(BF16), s.astype(BF16), c.T.astype(BF16), s.T.astype(BF16)


def _spectrum_kernel(c_ref, s_ref, hs_ref, hd_ref, kre_ref, kim_ref):
    kre_ref[...] = _dot(c_ref[...], hs_ref[...].astype(BF16))
    kim_ref[...] = _dot(s_ref[...], hd_ref[...].astype(BF16))


def filter_spectrum(cmat, smat, hs, hd):
    seq, cols = hs.shape
    tf, tn = 512, 512
    return pl.pallas_call(
        _spectrum_kernel,
        grid=(seq // tf, cols // tn),
        in_specs=[pl.BlockSpec((tf, seq), lambda i, j: (i, 0)),
                  pl.BlockSpec((tf, seq), lambda i, j: (i, 0)),
                  pl.BlockSpec((seq, tn), lambda i, j: (0, j)),
                  pl.BlockSpec((seq, tn), lambda i, j: (0, j))],
        out_specs=[pl.BlockSpec((tf, tn), lambda i, j: (i, j))] * 2,
        out_shape=[jax.ShapeDtypeStruct((seq, cols), F32)] * 2,
        compiler_params=_cparams(("parallel", "parallel")),
        name="hyena_filter_spectrum",
    )(cmat, smat, hs, hd)


def _hy_fwd_kernel(z_ref, c_ref, s_ref, kre_ref, kim_ref, yre_ref, yim_ref):
    z = z_ref[0].astype(BF16)
    zc = _dot(c_ref[...], z)
    zs = _dot(s_ref[...], z)
    kre = kre_ref[...]
    kim = kim_ref[...]
    yre_ref[0] = (zc * kre + zs * kim).astype(BF16)
    yim_ref[0] = (zc * kim - zs * kre).astype(BF16)


def hyena_fwd(z3d, zcol, cmat, smat, kre, kim, order):
    b, seq, _ = z3d.shape
    w = HY_WIDTH
    tf = 512
    return pl.pallas_call(
        _hy_fwd_kernel,
        grid=(b, seq // tf),
        in_specs=[pl.BlockSpec((1, seq, w), lambda i, j: (i, 0, zcol)),
                  pl.BlockSpec((tf, seq), lambda i, j: (j, 0)),
                  pl.BlockSpec((tf, seq), lambda i, j: (j, 0)),
                  pl.BlockSpec((tf, w), lambda i, j: (j, order)),
                  pl.BlockSpec((tf, w), lambda i, j: (j, order))],
        out_specs=[pl.BlockSpec((1, tf, w), lambda i, j: (i, j, 0))] * 2,
        out_shape=[jax.ShapeDtypeStruct((b, seq, w), BF16)] * 2,
        compiler_params=_cparams(("parallel", "parallel")),
        name="hyena_dft_fwd",
    )(z3d, cmat, smat, kre, kim)


def _hy_inv_kernel(ct_ref, st_ref, yre_ref, yim_ref, z_ref, g_ref, bias_ref, o_ref, *, scale):
    y = (_dot(ct_ref[...], yre_ref[0]) - _dot(st_ref[...], yim_ref[0])) * scale
    o_ref[0] = g_ref[0] * (y + z_ref[0] * bias_ref[...])


def hyena_inv(ctm, stm, yre, yim, z3d, zcol, g3d, gcol, bias_row):
    b, seq, w = yre.shape
    tt = 512
    return pl.pallas_call(
        functools.partial(_hy_inv_kernel, scale=1.0 / seq),
        grid=(b, seq // tt),
        in_specs=[pl.BlockSpec((tt, seq), lambda i, j: (j, 0)),
                  pl.BlockSpec((tt, seq), lambda i, j: (j, 0)),
                  pl.BlockSpec((1, seq, w), lambda i, j: (i, 0, 0)),
                  pl.BlockSpec((1, seq, w), lambda i, j: (i, 0, 0)),
                  pl.BlockSpec((1, tt, w), lambda i, j: (i, j, zcol)),
                  pl.BlockSpec((1, tt, w), lambda i, j: (i, j, gcol)),
                  pl.BlockSpec((1, w), lambda i, j: (0, 0))],
        out_specs=pl.BlockSpec((1, tt, w), lambda i, j: (i, j, 0)),
        out_shape=jax.ShapeDtypeStruct((b, seq, w), F32),
        compiler_params=_cparams(("parallel", "parallel")),
        name="hyena_dft_inv",
    )(ctm, stm, yre, yim, z3d, g3d, bias_row)


def hyena_mixer(proj3d, tables, conv_w, conv_b, w1, b1, w2, b2, w3, freq, hy_bias):
    seq = proj3d.shape[1]
    cmat, smat, ctm, stm = tables
    u = conv3(proj3d, conv_w, conv_b)
    hs, hd = hyena_filters_time(seq, w1, b1, w2, b2, w3, freq)
    kre, kim = filter_spectrum(cmat, smat, hs, hd)
    yre, yim = hyena_fwd(u, 0, cmat, smat, kre, kim, 0)
    z1 = hyena_inv(ctm, stm, yre, yim, u, 0, u, 1, hy_bias[0:1])
    yre, yim = hyena_fwd(z1, 0, cmat, smat, kre, kim, 1)
    return hyena_inv(ctm, stm, yre, yim, z1, 0, u, 2, hy_bias[1:2])


def _diff_attn_kernel(slope_ref, q_ref, k_ref, v_ref, lam_ref, sub_ref, o_ref, *, tq, lam_init):
    h = pl.program_id(1)
    qi = pl.program_id(2)
    q = q_ref[0] * (DIFF_HEAD_DIM ** -0.5 * LOG2E)
    k = k_ref[0].astype(BF16)
    v = v_ref[0].astype(BF16)
    seq = k.shape[0]
    lane = lax.broadcasted_iota(I32, (1, 2 * DIFF_HEAD_DIM), 1)
    qpos = qi * tq + lax.broadcasted_iota(I32, (tq, seq), 0)
    kpos = lax.broadcasted_iota(I32, (tq, seq), 1)
    bias = slope_ref[h] * jnp.abs(qpos - kpos).astype(F32)

    def attend(m):
        keep = (lane < DIFF_HEAD_DIM) if m == 0 else (lane >= DIFF_HEAD_DIM)
        qm = jnp.where(keep, q, 0.0).astype(BF16)
        s = _dot_nt(qm, k) - bias
        e = jnp.exp2(s - jnp.max(s, axis=-1, keepdims=True))
        return _dot(e.astype(BF16), v) / jnp.sum(e, axis=-1, keepdims=True)

    l = lam_ref[...]
    lam_full = (jnp.exp(jnp.sum(l[0:1] * l[1:2], axis=-1, keepdims=True))
                - jnp.exp(jnp.sum(l[2:3] * l[3:4], axis=-1, keepdims=True)) + lam_init)
    o = attend(0) - lam_full * attend(1)
    o_ref[0] = _rms(o, sub_ref[...]) * (1.0 - lam_init)


def diff_attention(proj3d, lam, subln, layer_idx):
    b, seq, _ = proj3d.shape
    tq = 256
    hw = 2 * DIFF_HEAD_DIM
    qb, kb, vb = 3 * HY_WIDTH // hw, (3 * HY_WIDTH + 512) // hw, (3 * HY_WIDTH + 1024) // hw
    lam_init = 0.8 - 0.6 * math.exp(-0.3 * layer_idx)
    slopes = 2.0 ** (-8.0 * jnp.arange(1, DIFF_HEADS + 1, dtype=F32) / DIFF_HEADS) * LOG2E
    return pl.pallas_call(
        functools.partial(_diff_attn_kernel, tq=tq, lam_init=lam_init),
        grid=(b, DIFF_HEADS, seq // tq),
        in_specs=[pl.BlockSpec(memory_space=pltpu.SMEM),
                  pl.BlockSpec((1, tq, hw), lambda i, h, j: (i, j, qb + h)),
                  pl.BlockSpec((1, seq, hw), lambda i, h, j: (i, 0, kb + h)),
                  pl.BlockSpec((1, seq, hw), lambda i, h, j: (i, 0, vb + h)),
                  pl.BlockSpec((4, DIFF_HEAD_DIM), lambda i, h, j: (0, 0)),
                  pl.BlockSpec((1, hw), lambda i, h, j: (0, 0))],
        out_specs=pl.BlockSpec((1, tq, hw), lambda i, h, j: (i, j, h)),
        out_shape=jax.ShapeDtypeStruct((b, seq, DIFF_HEADS * hw), F32),
        compiler_params=_cparams(("parallel", "parallel", "parallel")),
        name="diff_attention",
    )(slopes, proj3d, proj3d, proj3d, lam, subln.reshape(1, hw))


def _win_attn_kernel(slope_ref, sink_ref, q_ref, k_ref, v_ref, o_ref, *, tq, span):
    qi = pl.program_id(1)
    seq = k_ref.shape[1]
    q0 = qi * tq
    kstart = pl.multiple_of(jnp.clip(q0 - WIN, 0, seq - span), LANES)
    kwin = k_ref[0, pl.ds(kstart, span), :]
    vwin = v_ref[0, pl.ds(kstart, span), :]
    group = GQA_HEADS // GQA_KV
    low = lax.broadcasted_iota(I32, (1, LANES), 1) < GQA_HD
    qpos = q0 + lax.broadcasted_iota(I32, (tq, span), 0)
    kpos = kstart + lax.broadcasted_iota(I32, (tq, span), 1)
    rel = jnp.abs(qpos - kpos)
    relf = rel.astype(F32)
    maskb = jnp.where(rel <= WIN, 0.0, NEG)
    qscale = GQA_HD ** -0.5 * LOG2E
    for p in range(GQA_KV // 2):
        kb = kwin[:, p * LANES:(p + 1) * LANES].astype(BF16)
        vb = vwin[:, p * LANES:(p + 1) * LANES].astype(BF16)
        outs = []
        for half in range(2):
            kv = 2 * p + half
            mine = low if half == 0 else jnp.logical_not(low)
            qs = jnp.concatenate(
                [jnp.where(mine, q_ref[0, :, (group * p + r) * LANES:(group * p + r + 1) * LANES] * qscale, 0.0)
                 for r in range(group)], axis=0).astype(BF16)
            s = _dot_nt(qs, kb)
            es, inv = [], []
            for r in range(group):
                head = kv * group + r
                sink = sink_ref[head]
                sr = s[r * tq:(r + 1) * tq] - slope_ref[head] * relf + maskb
                m = jnp.maximum(jnp.max(sr, axis=-1, keepdims=True), sink)
                e = jnp.exp2(sr - m)
                inv.append(1.0 / (jnp.sum(e, axis=-1, keepdims=True) + jnp.exp2(sink - m)))
                es.append(e.astype(BF16))
            o = _dot(jnp.concatenate(es, axis=0), vb)
            outs.append([o[r * tq:(r + 1) * tq] * inv[r] for r in range(group)])
        for r in range(group):
            col = (group * p + r) * LANES
            o_ref[0, :, col:col + LANES] = jnp.where(low, outs[0][r], outs[1][r])


def window_head_order():
    group = GQA_HEADS // GQA_KV
    order = []
    for p in range(GQA_KV // 2):
        for r in range(group):
            order += [(2 * p) * group + r, (2 * p + 1) * group + r]
    return order


def window_column_perm():
    cols = []
    for head in window_head_order():
        cols += list(range(head * GQA_HD, (head + 1) * GQA_HD))
    return jnp.asarray(cols, dtype=I32)


def window_gqa(proj3d, sink):
    b, seq, _ = proj3d.shape
    tq = 128
    span = tq + 2 * WIN
    oq = GQA_HEADS * GQA_HD
    okv = GQA_KV * GQA_HD
    slopes = 2.0 ** (-8.0 * jnp.arange(1, GQA_HEADS + 1, dtype=F32) / GQA_HEADS) * LOG2E
    return pl.pallas_call(
        functools.partial(_win_attn_kernel, tq=tq, span=span),
        grid=(b, seq // tq),
        in_specs=[pl.BlockSpec(memory_space=pltpu.SMEM),
                  pl.BlockSpec(memory_space=pltpu.SMEM),
                  pl.BlockSpec((1, tq, oq), lambda i, j: (i, j, 0)),
                  pl.BlockSpec((1, seq, okv), lambda i, j: (i, 0, oq // okv)),
                  pl.BlockSpec((1, seq, okv), lambda i, j: (i, 0, oq // okv + 1))],
        out_specs=pl.BlockSpec((1, tq, oq), lambda i, j: (i, j, 0)),
        out_shape=jax.ShapeDtypeStruct((b, seq, oq), F32),
        compiler_params=_cparams(("parallel", "parallel")),
        name="window_gqa",
    )(slopes, sink.astype(F32) * LOG2E, proj3d, proj3d, proj3d)


def _cross_block_kernel(x_ref, g_ref, wq_ref, kv_ref, wo_ref, o_ref):
    x = x_ref[0]
    h = _rms(x, g_ref[...]).astype(BF16)
    q = _dot(h, wq_ref[...])
    kv = kv_ref[0]
    scale = X_HD ** -0.5
    outs = []
    for hd in range(X_HEADS):
        qh = q[:, hd * X_HD:(hd + 1) * X_HD].astype(BF16)
        kh = kv[:, hd * X_HD:(hd + 1) * X_HD]
        vh = kv[:, X_W + hd * X_HD:X_W + (hd + 1) * X_HD]
        s = _dot_nt(qh, kh) * scale
        e = jnp.exp(s - jnp.max(s, axis=-1, keepdims=True))
        p = e / jnp.sum(e, axis=-1, keepdims=True)
        outs.append(_dot(p.astype(BF16), vh))
    o = jnp.concatenate(outs, axis=-1).astype(BF16)
    o_ref[0] = x + _dot(o, wo_ref[...])


def cross_block(x3d, gamma, wq_bf16, kv_bf16, wo_bf16):
    b, seq, d = x3d.shape
    m = kv_bf16.shape[1]
    tm = 512
    return pl.pallas_call(
        _cross_block_kernel,
        grid=(b, seq // tm),
        in_specs=[pl.BlockSpec((1, tm, d), lambda i, j: (i, j, 0)),
                  pl.BlockSpec((1, d), lambda i, j: (0, 0)),
                  pl.BlockSpec((d, X_W), lambda i, j: (0, 0)),
                  pl.BlockSpec((1, m, 2 * X_W), lambda i, j: (i, 0, 0)),
                  pl.BlockSpec((X_W, d), lambda i, j: (0, 0))],
        out_specs=pl.BlockSpec((1, tm, d), lambda i, j: (i, j, 0)),
        out_shape=jax.ShapeDtypeStruct((b, seq, d), F32),
        compiler_params=_cparams(("parallel", "parallel")),
        name="cross_attention_block",
    )(x3d, gamma.reshape(1, d), wq_bf16, kv_bf16, wo_bf16)


def _split_bf16(x):
    hi = x.astype(BF16)
    lo = (x - hi.astype(F32)).astype(BF16)
    return hi, lo


def _pack_halves(h):
    c = h.shape[1] // 2
    left = lax.bitcast_convert_type(h[:, :c].astype(BF16).astype(F32), U32)
    right = lax.bitcast_convert_type(h[:, c:].astype(BF16).astype(F32), U32)
    return left | (right >> 16)


def _unpack_halves(p):
    left = lax.bitcast_convert_type(p & jnp.uint32(0xFFFF0000), F32)
    right = lax.bitcast_convert_type(p << 16, F32)
    return jnp.concatenate([left, right], axis=-1).astype(BF16)


def _router_kernel(x_ref, g_ref, whi_ref, wlo_ref, b_ref, hp_ref, eid_ref, wt_ref, rank_ref, cnt_ref,
                   base_ref, *, tm):
    i = pl.program_id(0)

    @pl.when(i == 0)
    def _():
        base_ref[...] = jnp.zeros_like(base_ref)

    h = _rms(x_ref[...], g_ref[...])
    hp_ref[...] = _pack_halves(h)
    hhi, hlo = _split_bf16(h)
    whi = whi_ref[...]
    logits = _dot(hhi, whi) + _dot(hlo, whi) + _dot(hhi, wlo_ref[...]) + b_ref[...]
    lane = lax.broadcasted_iota(I32, logits.shape, 1)
    big = jnp.int32(LANES)
    ninf = -jnp.inf

    gl = jnp.where(lane < N_GROUPS, logits, ninf)
    gmax = jnp.max(gl, axis=-1, keepdims=True)
    gsel = jnp.min(jnp.where(gl == gmax, lane, big), axis=-1, keepdims=True)
    ggate = 1.0 / jnp.sum(jnp.exp(gl - gmax), axis=-1, keepdims=True)

    lo_lane = N_GROUPS + gsel * EXP_PER_GROUP
    el = jnp.where((lane >= lo_lane) & (lane < lo_lane + EXP_PER_GROUP), logits, ninf)
    v1 = jnp.max(el, axis=-1, keepdims=True)
    i1 = jnp.min(jnp.where(el == v1, lane, big), axis=-1, keepdims=True)
    el2 = jnp.where(lane == i1, ninf, el)
    v2 = jnp.max(el2, axis=-1, keepdims=True)
    i2 = jnp.min(jnp.where(el2 == v2, lane, big), axis=-1, keepdims=True)
    e2 = jnp.exp(v2 - v1)
    w1 = ggate / (1.0 + e2)
    w2 = ggate * e2 / (1.0 + e2)

    one1 = lane == i1
    one2 = lane == i2
    onehot = (one1 | one2).astype(F32)
    r = lax.broadcasted_iota(I32, (tm, tm), 0)
    c = lax.broadcasted_iota(I32, (tm, tm), 1)
    tri = (c < r).astype(BF16)
    before = _dot(tri, onehot.astype(BF16)) + base_ref[...]
    rank1 = jnp.sum(jnp.where(one1, before, 0.0), axis=-1, keepdims=True)
    rank2 = jnp.sum(jnp.where(one2, before, 0.0), axis=-1, keepdims=True)
    total = base_ref[...] + jnp.sum(onehot, axis=0, keepdims=True)
    base_ref[...] = total
    cnt_ref[...] = total

    col = lax.broadcasted_iota(I32, (tm, 2), 1)
    eid_ref[...] = jnp.where(col == 0, i1, i2) - N_GROUPS
    wt_ref[...] = jnp.where(col == 0, w1, w2)
    rank_ref[...] = jnp.where(col == 0, rank1, rank2).astype(I32)


def moe_router(x2d, gamma, w_grp, b_grp, w_exp, b_exp):
    n, d = x2d.shape
    tm = 512
    wcat = jnp.pad(jnp.concatenate([w_grp, w_exp], axis=1), ((0, 0), (0, LANES - N_GROUPS - N_EXPERTS)))
    bcat = jnp.pad(jnp.concatenate([b_grp, b_exp]), (0, LANES - N_GROUPS - N_EXPERTS)).reshape(1, LANES)
    whi = wcat.astype(BF16)
    wlo = (wcat - whi.astype(F32)).astype(BF16)
    return pl.pallas_call(
        functools.partial(_router_kernel, tm=tm),
        grid=(n // tm,),
        in_specs=[pl.BlockSpec((tm, d), lambda i: (i, 0)),
                  pl.BlockSpec((1, d), lambda i: (0, 0)),
                  pl.BlockSpec((d, LANES), lambda i: (0, 0)),
                  pl.BlockSpec((d, LANES), lambda i: (0, 0)),
                  pl.BlockSpec((1, LANES), lambda i: (0, 0))],
        out_specs=[pl.BlockSpec((tm, d // 2), lambda i: (i, 0)),
                   pl.BlockSpec((tm, 2), lambda i: (i, 0)),
                   pl.BlockSpec((tm, 2), lambda i: (i, 0)),
                   pl.BlockSpec((tm, 2), lambda i: (i, 0)),
                   pl.BlockSpec((1, LANES), lambda i: (0, 0))],
        out_shape=[jax.ShapeDtypeStruct((n, d // 2), U32),
                   jax.ShapeDtypeStruct((n, 2), I32),
                   jax.ShapeDtypeStruct((n, 2), F32),
                   jax.ShapeDtypeStruct((n, 2), I32),
                   jax.ShapeDtypeStruct((1, LANES), F32)],
        scratch_shapes=[pltpu.VMEM((1, LANES), F32)],
        compiler_params=_cparams(("arbitrary",)),
        name="moe_router",
    )(x2d, gamma.reshape(1, d), whi, wlo, bcat)


def _row_copy(src_ref, src_row, dst_ref, dst_row, sem):
    return pltpu.make_async_copy(src_ref.at[pl.ds(src_row, 1)], dst_ref.at[pl.ds(dst_row, 1)], sem)


def _dispatch_kernel(dest_ref, hp_ref, xs_in_ref, xs_ref, sem, *, tm):
    del xs_in_ref

    def issue(r, carry):
        _row_copy(hp_ref, r, xs_ref, dest_ref[2 * r], sem).start(priority=0)
        _row_copy(hp_ref, r, xs_ref, dest_ref[2 * r + 1], sem).start(priority=1)
        return carry

    lax.fori_loop(0, tm, issue, 0, unroll=ROW_DMA_UNROLL)
    for _ in range(2):
        pltpu.make_async_copy(hp_ref, xs_ref.at[pl.ds(0, tm)], sem).wait()


def moe_dispatch(hp, dest_flat, cap):
    n, c = hp.shape
    tm = 512
    zeros = jnp.zeros((cap, c), U32)
    return pl.pallas_call(
        functools.partial(_dispatch_kernel, tm=tm),
        grid=(n // tm,),
        in_specs=[pl.BlockSpec((2 * tm,), lambda i: (i,), memory_space=pltpu.SMEM),
                  pl.BlockSpec((tm, c), lambda i: (i, 0)),
                  pl.BlockSpec(memory_space=pl.ANY)],
        out_specs=pl.BlockSpec(memory_space=pl.ANY),
        out_shape=jax.ShapeDtypeStruct((cap, c), U32),
        scratch_shapes=[pltpu.SemaphoreType.DMA(())],
        input_output_aliases={2: 0},
        compiler_params=_cparams(("arbitrary",)),
        name="moe_dispatch",
    )(dest_flat, hp, zeros)


def _expert_kernel(be_ref, nused_ref, xs_ref, wg_ref, wu_ref, wd_ref, y_ref, wg_s, wu_s, wd_s):
    b = pl.program_id(0)
    prev = be_ref[jnp.maximum(b - 1, 0)]
    active = b < nused_ref[0]

    @pl.when(active & ((b == 0) | (be_ref[b] != prev)))
    def _():
        wg_s[...] = wg_ref[0].astype(BF16)
        wu_s[...] = wu_ref[0].astype(BF16)
        wd_s[...] = wd_ref[0].astype(BF16)

    @pl.when(active)
    def _():
        x = _unpack_halves(xs_ref[...])
        a = _dot(x, wg_s[...])
        u = _dot(x, wu_s[...])
        hmid = (a / (1.0 + jnp.exp(-a)) * u).astype(BF16)
        y_ref[...] = _dot(hmid, wd_s[...])

    @pl.when(jnp.logical_not(active))
    def _():
        y_ref[...] = jnp.zeros_like(y_ref)


def moe_experts(xs, blk_e, nused, w_gate, w_up, w_down):
    cap, c = xs.shape
    d = 2 * c
    hid = w_gate.shape[2]
    nblk = cap // MOE_BLOCK
    grid_spec = pltpu.PrefetchScalarGridSpec(
        num_scalar_prefetch=2,
        grid=(nblk,),
        in_specs=[pl.BlockSpec((MOE_BLOCK, c), lambda b, be, nu: (b, 0)),
                  pl.BlockSpec((1, d, hid), lambda b, be, nu: (be[b], 0, 0)),
                  pl.BlockSpec((1, d, hid), lambda b, be, nu: (be[b], 0, 0)),
                  pl.BlockSpec((1, hid, d), lambda b, be, nu: (be[b], 0, 0))],
        out_specs=pl.BlockSpec((MOE_BLOCK, d), lambda b, be, nu: (b, 0)),
        scratch_shapes=[pltpu.VMEM((d, hid), BF16), pltpu.VMEM((d, hid), BF16), pltpu.VMEM((hid, d), BF16)],
    )
    return pl.pallas_call(
        _expert_kernel,
        grid_spec=grid_spec,
        out_shape=jax.ShapeDtypeStruct((cap, d), F32),
        compiler_params=_cparams(("arbitrary",)),
        name="moe_experts",
    )(blk_e, nused, xs, w_gate, w_up, w_down)


def _combine_kernel(dest_ref, x_ref, wt_ref, g_ref, yb_ref, o_ref, buf, sem, *, tm, final_norm):
    def issue(r, carry):
        _row_copy(yb_ref, dest_ref[2 * r], buf.at[0], r, sem).start(priority=0)
        _row_copy(yb_ref, dest_ref[2 * r + 1], buf.at[1], r, sem).start(priority=1)
        return carry

    lax.fori_loop(0, tm, issue, 0, unroll=ROW_DMA_UNROLL)
    for k in range(2):
        pltpu.make_async_copy(yb_ref.at[pl.ds(0, tm)], buf.at[k], sem).wait()
    wt = wt_ref[...]
    y = x_ref[...] + (buf[0] * wt[:, 0:1] + buf[1] * wt[:, 1:2])
    o_ref[...] = _rms(y, g_ref[...]) if final_norm else y


def moe_combine(x2d, yb, dest_flat, wt, g_final, final_norm):
    n, d = x2d.shape
    tm = 256
    return pl.pallas_call(
        functools.partial(_combine_kernel, tm=tm, final_norm=final_norm),
        grid=(n // tm,),
        in_specs=[pl.BlockSpec((2 * tm,), lambda i: (i,), memory_space=pltpu.SMEM),
                  pl.BlockSpec((tm, d), lambda i: (i, 0)),
                  pl.BlockSpec((tm, 2), lambda i: (i, 0)),
                  pl.BlockSpec((1, d), lambda i: (0, 0)),
                  pl.BlockSpec(memory_space=pl.ANY)],
        out_specs=pl.BlockSpec((tm, d), lambda i: (i, 0)),
        out_shape=jax.ShapeDtypeStruct((n, d), F32),
        scratch_shapes=[pltpu.VMEM((2, tm, d), F32), pltpu.SemaphoreType.DMA(())],
        compiler_params=_cparams(("arbitrary",)),
        name="moe_combine",
    )(dest_flat, x2d, wt, g_final.reshape(1, d), yb)


def hier_moe_block(x2d, gamma, w_grp, b_grp, w_exp, b_exp, w_gate, w_up, w_down, layer, g_final, final_norm):
    n = x2d.shape[0]
    cap = 2 * n + N_EXPERTS * MOE_BLOCK
    hp, eid, wt, rank, cnt = moe_router(x2d, gamma, w_grp, b_grp, w_exp, b_exp)
    counts = cnt[0, N_GROUPS:N_GROUPS + N_EXPERTS].astype(I32)
    padded = (counts + MOE_BLOCK - 1) // MOE_BLOCK * MOE_BLOCK
    p_ends = jnp.cumsum(padded)
    p_starts = p_ends - padded
    experts = jnp.arange(N_EXPERTS, dtype=I32)
    dest = (jnp.sum(jnp.where(eid[..., None] == experts, p_starts, 0), axis=-1) + rank).reshape(-1)
    nblk = cap // MOE_BLOCK
    blk_start = jnp.arange(nblk, dtype=I32) * MOE_BLOCK
    blk_e = jnp.minimum(jnp.sum((p_ends[None, :] <= blk_start[:, None]).astype(I32), axis=1), N_EXPERTS - 1)
    nused = (p_ends[-1:] // MOE_BLOCK).astype(I32)
    xs = moe_dispatch(hp, dest, cap)
    yb = moe_experts(xs, blk_e + layer * N_EXPERTS, nused, w_gate, w_up, w_down)
    return moe_combine(x2d, yb, dest, wt, g_final, final_norm)


def kernel(x, mem, e_norm, e_w_in, e_conv_w, e_conv_b, e_filt_w1, e_filt_b1, e_filt_w2, e_filt_b2, e_filt_w3, e_filt_freq, e_hy_bias, e_lam, e_subln, e_w_out, o_norm, o_w_in, o_sink, o_w_out, c_norm, c_wq, c_wkv, c_wo, f_norm, f_w_grp, f_b_grp, f_w_exp, f_b_exp, f_w_gate, f_w_up, f_w_down, g_mem, g_final):
    b, seq, d = x.shape
    n = b * seq
    m = mem.shape[1]
    tables = dft_tables(seq)
    x2 = x.reshape(n, d)
    mem2 = mem.reshape(b * m, d)
    w_gate = f_w_gate.reshape(DEPTH * N_EXPERTS, d, EXP_HIDDEN)
    w_up = f_w_up.reshape(DEPTH * N_EXPERTS, d, EXP_HIDDEN)
    w_down = f_w_down.reshape(DEPTH * N_EXPERTS, EXP_HIDDEN, d)
    qperm = window_column_perm()
    for i in range(DEPTH):
        j = i // 2
        if i % 2 == 0:
            proj = norm_matmul(x2, e_norm[j], e_w_in[j].astype(BF16), 512).reshape(b, seq, -1)
            y_hy = hyena_mixer(proj, tables, e_conv_w[j], e_conv_b[j], e_filt_w1[j], e_filt_b1[j], e_filt_w2[j],
                               e_filt_b2[j], e_filt_w3[j], e_filt_freq[j], e_hy_bias[j])
            y_df = diff_attention(proj, e_lam[j], e_subln[j], i)
            w_out = e_w_out[j].astype(BF16)
            x2 = matmul_residual(x2, [y_hy.reshape(n, -1), y_df.reshape(n, -1)],
                                 [w_out[:HY_WIDTH], w_out[HY_WIDTH:]], 512)
        else:
            oq = GQA_HEADS * GQA_HD
            w_in = jnp.concatenate([o_w_in[j][:, :oq][:, qperm], o_w_in[j][:, oq:]], axis=1).astype(BF16)
            proj = norm_matmul(x2, o_norm[j], w_in, 512).reshape(b, seq, -1)
            att = window_gqa(proj, o_sink[j])
            x2 = matmul_residual(x2, [att.reshape(n, -1)], [o_w_out[j][qperm].astype(BF16)], 512)
        kv = norm_matmul(mem2, g_mem, c_wkv[i].astype(BF16), 512, out_dtype=BF16).reshape(b, m, -1)
        x2 = cross_block(x2.reshape(b, seq, d), c_norm[i], c_wq[i].astype(BF16), kv,
                         c_wo[i].astype(BF16)).reshape(n, d)
        x2 = hier_moe_block(x2, f_norm[i], f_w_grp[i], f_b_grp[i], f_w_exp[i], f_b_exp[i], w_gate, w_up, w_down,
                            i, g_final, i == DEPTH - 1)
    return x2.reshape(b, seq, d)
```

```python
import functools
import math

import jax
import jax.numpy as jnp
from jax import lax
from jax.experimental import pallas as pl
from jax.experimental.pallas import tpu as pltpu

F32 = jnp.float32
BF16 = jnp.bfloat16
I32 = jnp.int32
U32 = jnp.uint32

D_MODEL = 1024
DEPTH = 4
EPS = 1e-6
NEG = -1e30
HY_WIDTH = 512
HY_BANDS = 16
HY_FILT_HIDDEN = 64
HY_DECAY_TARGET = 1e-2
HY_FAST_PCT = 0.3
HY_SLOW_PCT = 1.5
DIFF_HEADS = 4
DIFF_HEAD_DIM = 64
WIN = 128
GQA_HEADS = 16
GQA_KV = 4
GQA_HD = 64
X_HEADS = 4
X_HD = 128
X_W = X_HEADS * X_HD
N_GROUPS = 4
EXP_PER_GROUP = 8
N_EXPERTS = N_GROUPS * EXP_PER_GROUP
EXP_HIDDEN = 512

LOG2E = 1.4426950408889634
LANES = 128
VMEM_LIMIT = 56 * 1024 * 1024
MOE_BLOCK = 256
DIFF_Q_BLOCK = 256
DIFF_KEY_CHUNK = 512
ROW_DMA_UNROLL = 8


def _cparams(sem):
    return pltpu.CompilerParams(dimension_semantics=sem, vmem_limit_bytes=VMEM_LIMIT)


def _rms(x, g):
    ms = jnp.mean(x * x, axis=-1, keepdims=True)
    return x * lax.rsqrt(ms + EPS) * g


def _dot(a, b):
    return jnp.dot(a, b, preferred_element_type=F32)


def _dot_nt(a, b):
    return lax.dot_general(a, b, (((1,), (1,)), ((), ())), preferred_element_type=F32)


def _norm_matmul_kernel(x_ref, g_ref, w_ref, *o_refs):
    h = _rms(x_ref[...], g_ref[...]).astype(BF16)
    col = 0
    for o_ref in o_refs:
        width = o_ref.shape[1]
        o_ref[...] = _dot(h, w_ref[:, col:col + width]).astype(o_ref.dtype)
        col += width


def norm_matmul(x2d, gamma, w_bf16, tm, outs):
    n, d = x2d.shape
    f = w_bf16.shape[1]
    assert sum(width for width, _ in outs) == f
    res = pl.pallas_call(
        _norm_matmul_kernel,
        grid=(n // tm,),
        in_specs=[pl.BlockSpec((tm, d), lambda i: (i, 0)),
                  pl.BlockSpec((1, d), lambda i: (0, 0)),
                  pl.BlockSpec((d, f), lambda i: (0, 0))],
        out_specs=[pl.BlockSpec((tm, width), lambda i: (i, 0)) for width, _ in outs],
        out_shape=[jax.ShapeDtypeStruct((n, width), dtype) for width, dtype in outs],
        compiler_params=_cparams(("parallel",)),
        name="norm_matmul",
    )(x2d, gamma.reshape(1, d), w_bf16)
    return res


def _matmul_residual_kernel(*refs, n_lhs):
    x_ref = refs[0]
    o_ref = refs[-1]
    acc = x_ref[...]
    for a_ref, w_ref in zip(refs[1:1 + n_lhs], refs[1 + n_lhs:1 + 2 * n_lhs]):
        acc = acc + _dot(a_ref[...].astype(BF16), w_ref[...])
    o_ref[...] = acc


def matmul_residual(x2d, lhs_list, w_list, tm):
    n, d = x2d.shape
    in_specs = [pl.BlockSpec((tm, d), lambda i: (i, 0))]
    in_specs += [pl.BlockSpec((tm, a.shape[1]), lambda i: (i, 0)) for a in lhs_list]
    in_specs += [pl.BlockSpec(w.shape, lambda i: (0, 0)) for w in w_list]
    return pl.pallas_call(
        functools.partial(_matmul_residual_kernel, n_lhs=len(lhs_list)),
        grid=(n // tm,),
        in_specs=in_specs,
        out_specs=pl.BlockSpec((tm, d), lambda i: (i, 0)),
        out_shape=jax.ShapeDtypeStruct((n, d), F32),
        compiler_params=_cparams(("parallel",)),
        name="matmul_residual",
    )(x2d, *lhs_list, *w_list)


def _conv3_kernel(u_ref, w_ref, b_ref, o_ref):
    u = u_ref[0]
    seq = u.shape[0]
    row = lax.broadcasted_iota(I32, u.shape, 0)
    prev = jnp.where(row == 0, 0.0, pltpu.roll(u, 1, 0))
    nxt = jnp.where(row == seq - 1, 0.0, pltpu.roll(u, seq - 1, 0))
    w = w_ref[...]
    o_ref[0] = prev * w[0:1] + u * w[1:2] + nxt * w[2:3] + b_ref[...]


def conv3(proj3d, conv_w, conv_b):
    b, seq, _ = proj3d.shape
    c = 3 * HY_WIDTH
    tc = HY_WIDTH
    return pl.pallas_call(
        _conv3_kernel,
        grid=(b, c // tc),
        in_specs=[pl.BlockSpec((1, seq, tc), lambda i, j: (i, 0, j)),
                  pl.BlockSpec((3, tc), lambda i, j: (0, j)),
                  pl.BlockSpec((1, tc), lambda i, j: (0, j))],
        out_specs=pl.BlockSpec((1, seq, tc), lambda i, j: (i, 0, j)),
        out_shape=jax.ShapeDtypeStruct((b, seq, c), F32),
        compiler_params=_cparams(("parallel", "parallel")),
        name="hyena_conv3",
    )(proj3d, conv_w, conv_b.reshape(1, c))


def _hy_filter_kernel(t_ref, bands_ref, w1t_ref, w1c_ref, w1s_ref, b1_ref, w2_ref, b2_ref, w3_ref,
                      freq_ref, delta_ref, hs_ref, hd_ref, *, seq, tl):
    hi = lax.Precision.HIGHEST
    i = pl.program_id(0)
    t = t_ref[...]
    pos = (i * tl + lax.broadcasted_iota(I32, (tl, 1), 0)).astype(F32)
    ang = bands_ref[...] * (2.0 * math.pi * pos / seq)
    f = freq_ref[...]
    pre = (t * w1t_ref[...]
           + jnp.dot(jnp.cos(ang), w1c_ref[...], precision=hi, preferred_element_type=F32)
           - jnp.dot(jnp.sin(ang), w1s_ref[...], precision=hi, preferred_element_type=F32)
           + b1_ref[...])
    a = jnp.sin(f * pre)
    a = jnp.sin(f * (jnp.dot(a, w2_ref[...], precision=hi, preferred_element_type=F32) + b2_ref[...]))
    h = jnp.dot(a, w3_ref[...], precision=hi, preferred_element_type=F32)
    decay = jnp.exp(-t * jnp.abs(delta_ref[...]))
    w = HY_WIDTH
    for o in range(2):
        fwd = h[:, o * 2 * w:o * 2 * w + w] * decay
        bwd = jnp.where(pos == 0.0, 0.0, h[:, o * 2 * w + w:(o + 1) * 2 * w] * decay)
        hs_ref[:, o * w:(o + 1) * w] = fwd + bwd
        hd_ref[:, o * w:(o + 1) * w] = bwd - fwd


def hyena_filters_time(seq, w1, b1, w2, b2, w3, freq):
    tl = 512
    hid = LANES
    pad_h = hid - HY_FILT_HIDDEN
    t = jnp.linspace(0.0, 1.0, seq, dtype=F32)[:, None]
    bands = jnp.pad(jnp.linspace(1e-4, HY_BANDS - 1, HY_BANDS, dtype=F32)[None], ((0, 0), (0, LANES - HY_BANDS)))
    w1p = jnp.pad(w1, ((0, 0), (0, pad_h)))
    w1t = w1p[0:1]
    w1c = jnp.pad(w1p[1:1 + HY_BANDS], ((0, LANES - HY_BANDS), (0, 0)))
    w1s = jnp.pad(w1p[1 + HY_BANDS:], ((0, LANES - HY_BANDS), (0, 0)))
    b1p = jnp.pad(b1, (0, pad_h)).reshape(1, hid)
    w2p = jnp.pad(w2, ((0, pad_h), (0, pad_h)))
    b2p = jnp.pad(b2, (0, pad_h)).reshape(1, hid)
    w3p = jnp.pad(w3, ((0, pad_h), (0, 0)))
    freqp = jnp.pad(freq, (0, pad_h)).reshape(1, hid)
    max_decay = math.log(HY_DECAY_TARGET) / HY_FAST_PCT
    min_decay = math.log(HY_DECAY_TARGET) / HY_SLOW_PCT
    deltas = jnp.linspace(min_decay, max_decay, HY_WIDTH, dtype=F32)[None]
    fw = w3.shape[1]
    full = lambda shape: pl.BlockSpec(shape, lambda i: (0, 0))
    return pl.pallas_call(
        functools.partial(_hy_filter_kernel, seq=seq, tl=tl),
        grid=(seq // tl,),
        in_specs=[pl.BlockSpec((tl, 1), lambda i: (i, 0)), full((1, LANES)), full((1, hid)),
                  full((LANES, hid)), full((LANES, hid)), full((1, hid)), full((hid, hid)), full((1, hid)),
                  full((hid, fw)), full((1, hid)), full((1, HY_WIDTH))],
        out_specs=[pl.BlockSpec((tl, 2 * HY_WIDTH), lambda i: (i, 0)),
                   pl.BlockSpec((tl, 2 * HY_WIDTH), lambda i: (i, 0))],
        out_shape=[jax.ShapeDtypeStruct((seq, 2 * HY_WIDTH), F32)] * 2,
        compiler_params=_cparams(("parallel",)),
        name="hyena_filter_mlp",
    )(t, bands, w1t, w1c, w1s, b1p, w2p, b2p, w3p, freqp, deltas)


def dft_tables(seq):
    n2 = 4 * seq
    sub = 64
    f = jnp.arange(seq, dtype=I32)[:, None]
    odd = 2 * f + 1
    s1 = jnp.arange(seq // sub, dtype=I32)[None]
    s0 = jnp.arange(sub, dtype=I32)[None]
    ang_p = ((odd * s1) % (n2 // sub)).astype(F32) * (2.0 * math.pi * sub / n2)
    ang_q = ((odd * s0) % n2).astype(F32) * (2.0 * math.pi / n2)
    pc, ps, qc, qs = jnp.cos(ang_p), jnp.sin(ang_p), jnp.cos(ang_q), jnp.sin(ang_q)
    c = (pc[:, :, None] * qc[:, None, :] - ps[:, :, None] * qs[:, None, :]).reshape(seq, seq)
    s = (ps[:, :, None] * qc[:, None, :] + pc[:, :, None] * qs[:, None, :]).reshape(seq, seq)
    return c.astype(BF16), s.astype(BF16), c.T.astype(BF16), s.T.astype(BF16)


def _spectrum_kernel(c_ref, s_ref, hs_ref, hd_ref, kre_ref, kim_ref):
    kre_ref[...] = _dot(c_ref[...], hs_ref[...].astype(BF16))
    kim_ref[...] = _dot(s_ref[...], hd_ref[...].astype(BF16))


def filter_spectrum(cmat, smat, hs, hd):
    seq, cols = hs.shape
    tf, tn = 512, 512
    return pl.pallas_call(
        _spectrum_kernel,
        grid=(seq // tf, cols // tn),
        in_specs=[pl.BlockSpec((tf, seq), lambda i, j: (i, 0)),
                  pl.BlockSpec((tf, seq), lambda i, j: (i, 0)),
                  pl.BlockSpec((seq, tn), lambda i, j: (0, j)),
                  pl.BlockSpec((seq, tn), lambda i, j: (0, j))],
        out_specs=[pl.BlockSpec((tf, tn), lambda i, j: (i, j))] * 2,
        out_shape=[jax.ShapeDtypeStruct((seq, cols), F32)] * 2,
        compiler_params=_cparams(("parallel", "parallel")),
        name="hyena_filter_spectrum",
    )(cmat, smat, hs, hd)


def _hy_fwd_kernel(z_ref, c_ref, s_ref, kre_ref, kim_ref, yre_ref, yim_ref):
    z = z_ref[0].astype(BF16)
    zc = _dot(c_ref[...], z)
    zs = _dot(s_ref[...], z)
    kre = kre_ref[...]
    kim = kim_ref[...]
    yre_ref[0] = (zc * kre + zs * kim).astype(BF16)
    yim_ref[0] = (zc * kim - zs * kre).astype(BF16)


def hyena_fwd(z3d, zcol, cmat, smat, kre, kim, order):
    b, seq, _ = z3d.shape
    w = HY_WIDTH
    tf = 512
    return pl.pallas_call(
        _hy_fwd_kernel,
        grid=(b, seq // tf),
        in_specs=[pl.BlockSpec((1, seq, w), lambda i, j: (i, 0, zcol)),
                  pl.BlockSpec((tf, seq), lambda i, j: (j, 0)),
                  pl.BlockSpec((tf, seq), lambda i, j: (j, 0)),
                  pl.BlockSpec((tf, w), lambda i, j: (j, order)),
                  pl.BlockSpec((tf, w), lambda i, j: (j, order))],
        out_specs=[pl.BlockSpec((1, tf, w), lambda i, j: (i, j, 0))] * 2,
        out_shape=[jax.ShapeDtypeStruct((b, seq, w), BF16)] * 2,
        compiler_params=_cparams(("parallel", "parallel")),
        name="hyena_dft_fwd",
    )(z3d, cmat, smat, kre, kim)


def _hy_inv_kernel(ct_ref, st_ref, yre_ref, yim_ref, z_ref, g_ref, bias_ref, o_ref, *, scale):
    y = (_dot(ct_ref[...], yre_ref[0]) - _dot(st_ref[...], yim_ref[0])) * scale
    o_ref[0] = g_ref[0] * (y + z_ref[0] * bias_ref[...])


def hyena_inv(ctm, stm, yre, yim, z3d, zcol, g3d, gcol, bias_row):
    b, seq, w = yre.shape
    tt = 512
    return pl.pallas_call(
        functools.partial(_hy_inv_kernel, scale=1.0 / seq),
        grid=(b, seq // tt),
        in_specs=[pl.BlockSpec((tt, seq), lambda i, j: (j, 0)),
                  pl.BlockSpec((tt, seq), lambda i, j: (j, 0)),
                  pl.BlockSpec((1, seq, w), lambda i, j: (i, 0, 0)),
                  pl.BlockSpec((1, seq, w), lambda i, j: (i, 0, 0)),
                  pl.BlockSpec((1, tt, w), lambda i, j: (i, j, zcol)),
                  pl.BlockSpec((1, tt, w), lambda i, j: (i, j, gcol)),
                  pl.BlockSpec((1, w), lambda i, j: (0, 0))],
        out_specs=pl.BlockSpec((1, tt, w), lambda i, j: (i, j, 0)),
        out_shape=jax.ShapeDtypeStruct((b, seq, w), F32),
        compiler_params=_cparams(("parallel", "parallel")),
        name="hyena_dft_inv",
    )(ctm, stm, yre, yim, z3d, g3d, bias_row)


def hyena_mixer(proj3d, tables, conv_w, conv_b, w1, b1, w2, b2, w3, freq, hy_bias):
    seq = proj3d.shape[1]
    cmat, smat, ctm, stm = tables
    u = conv3(proj3d, conv_w, conv_b)
    hs, hd = hyena_filters_time(seq, w1, b1, w2, b2, w3, freq)
    kre, kim = filter_spectrum(cmat, smat, hs, hd)
    yre, yim = hyena_fwd(u, 0, cmat, smat, kre, kim, 0)
    z1 = hyena_inv(ctm, stm, yre, yim, u, 0, u, 1, hy_bias[0:1])
    yre, yim = hyena_fwd(z1, 0, cmat, smat, kre, kim, 1)
    return hyena_inv(ctm, stm, yre, yim, z1, 0, u, 2, hy_bias[1:2])


def _diff_attn_kernel(slope_ref, q_ref, k_ref, v_ref, lam_ref, sub_ref, o_ref, bias_ref, *, tq, lam_init):
    h = pl.program_id(0)
    qi = pl.program_id(1)
    seq = k_ref.shape[1]

    @pl.when(pl.program_id(2) == 0)
    def _():
        qpos = qi * tq + lax.broadcasted_iota(I32, (tq, seq), 0)
        kpos = lax.broadcasted_iota(I32, (tq, seq), 1)
        bias_ref[...] = slope_ref[h] * jnp.abs(qpos - kpos).astype(F32)

    q = q_ref[0]
    k = k_ref[0]
    v = v_ref[0]
    lane = lax.broadcasted_iota(I32, (1, 2 * DIFF_HEAD_DIM), 1)
    ck = DIFF_KEY_CHUNK

    def attend(m):
        keep = (lane < DIFF_HEAD_DIM) if m == 0 else (lane >= DIFF_HEAD_DIM)
        qm = jnp.where(keep, q, jnp.zeros_like(q))
        s = [_dot_nt(qm, k[c:c + ck]) - bias_ref[:, c:c + ck] for c in range(0, seq, ck)]
        mx = functools.reduce(jnp.maximum, [jnp.max(sc, axis=-1, keepdims=True) for sc in s])
        acc = jnp.zeros((tq, 2 * DIFF_HEAD_DIM), F32)
        den = jnp.zeros((tq, 1), F32)
        for i, sc in enumerate(s):
            e = jnp.exp2(sc - mx)
            den = den + jnp.sum(e, axis=-1, keepdims=True)
            acc = acc + _dot(e.astype(BF16), v[i * ck:(i + 1) * ck])
        return acc / den

    l = lam_ref[...]
    lam_full = (jnp.exp(jnp.sum(l[0:1] * l[1:2], axis=-1, keepdims=True))
                - jnp.exp(jnp.sum(l[2:3] * l[3:4], axis=-1, keepdims=True)) + lam_init)
    o = attend(0) - lam_full * attend(1)
    o_ref[0] = _rms(o, sub_ref[...]) * (1.0 - lam_init)


def diff_attention(proj3d, lam, subln, layer_idx):
    b, seq, _ = proj3d.shape
    tq = DIFF_Q_BLOCK
    hw = 2 * DIFF_HEAD_DIM
    qb, kb, vb = 0, DIFF_HEADS, 2 * DIFF_HEADS
    lam_init = 0.8 - 0.6 * math.exp(-0.3 * layer_idx)
    slopes = 2.0 ** (-8.0 * jnp.arange(1, DIFF_HEADS + 1, dtype=F32) / DIFF_HEADS) * LOG2E
    return pl.pallas_call(
        functools.partial(_diff_attn_kernel, tq=tq, lam_init=lam_init),
        grid=(DIFF_HEADS, seq // tq, b),
        in_specs=[pl.BlockSpec(memory_space=pltpu.SMEM),
                  pl.BlockSpec((1, tq, hw), lambda h, j, i: (i, j, qb + h)),
                  pl.BlockSpec((1, seq, hw), lambda h, j, i: (i, 0, kb + h)),
                  pl.BlockSpec((1, seq, hw), lambda h, j, i: (i, 0, vb + h)),
                  pl.BlockSpec((4, DIFF_HEAD_DIM), lambda h, j, i: (0, 0)),
                  pl.BlockSpec((1, hw), lambda h, j, i: (0, 0))],
        out_specs=pl.BlockSpec((1, tq, hw), lambda h, j, i: (i, j, h)),
        out_shape=jax.ShapeDtypeStruct((b, seq, DIFF_HEADS * hw), F32),
        scratch_shapes=[pltpu.VMEM((tq, seq), F32)],
        compiler_params=_cparams(("parallel", "parallel", "arbitrary")),
        name="diff_attention",
    )(slopes, proj3d, proj3d, proj3d, lam, subln.reshape(1, hw))


def _win_attn_kernel(slope_ref, sink_ref, q_ref, kp_ref, kc_ref, kn_ref, vp_ref, vc_ref, vn_ref, o_ref, bias_ref,
                     *, tq, seq):
    qi = pl.program_id(0)
    span = 3 * tq
    group = GQA_HEADS // GQA_KV

    @pl.when(pl.program_id(1) == 0)
    def _():
        qpos = qi * tq + lax.broadcasted_iota(I32, (tq, span), 0)
        kpos = (qi - 1) * tq + lax.broadcasted_iota(I32, (tq, span), 1)
        rel = jnp.abs(qpos - kpos)
        relf = rel.astype(F32)
        masked = jnp.where((rel <= WIN) & (kpos >= 0) & (kpos < seq), 0.0, -NEG)
        for head in range(GQA_HEADS):
            bias_ref[head] = slope_ref[head] * relf + masked

    kwin = jnp.concatenate([kp_ref[0], kc_ref[0], kn_ref[0]], axis=0)
    vwin = jnp.concatenate([vp_ref[0], vc_ref[0], vn_ref[0]], axis=0)
    low = lax.broadcasted_iota(I32, (1, LANES), 1) < GQA_HD
    for p in range(GQA_KV // 2):
        kb = kwin[:, p * LANES:(p + 1) * LANES]
        vb = vwin[:, p * LANES:(p + 1) * LANES]
        outs = []
        for half in range(2):
            kv = 2 * p + half
            mine = low if half == 0 else jnp.logical_not(low)
            qblocks = [q_ref[0, :, (group * p + r) * LANES:(group * p + r + 1) * LANES] for r in range(group)]
            qs = jnp.concatenate([jnp.where(mine, qb, jnp.zeros_like(qb)) for qb in qblocks], axis=0)
            s = _dot_nt(qs, kb)
            es, inv = [], []
            for r in range(group):
                head = kv * group + r
                sink = sink_ref[head]
                sr = s[r * tq:(r + 1) * tq] - bias_ref[head]
                m = jnp.maximum(jnp.max(sr, axis=-1, keepdims=True), sink)
                e = jnp.exp2(sr - m)
                inv.append(1.0 / (jnp.sum(e, axis=-1, keepdims=True) + jnp.exp2(sink - m)))
                es.append(e.astype(BF16))
            o = _dot(jnp.concatenate(es, axis=0), vb)
            outs.append([o[r * tq:(r + 1) * tq] * inv[r] for r in range(group)])
        for r in range(group):
            col = (group * p + r) * LANES
            o_ref[0, :, col:col + LANES] = jnp.where(low, outs[0][r], outs[1][r])


def window_head_order():
    group = GQA_HEADS // GQA_KV
    order = []
    for p in range(GQA_KV // 2):
        for r in range(group):
            order += [(2 * p) * group + r, (2 * p + 1) * group + r]
    return order


def window_column_perm():
    cols = []
    for head in window_head_order():
        cols += list(range(head * GQA_HD, (head + 1) * GQA_HD))
    return jnp.asarray(cols, dtype=I32)


def window_gqa(proj3d, sink):
    b, seq, _ = proj3d.shape
    tq = WIN
    nq = seq // tq
    oq = GQA_HEADS * GQA_HD
    okv = GQA_KV * GQA_HD
    kcol, vcol = oq // okv, oq // okv + 1
    slopes = 2.0 ** (-8.0 * jnp.arange(1, GQA_HEADS + 1, dtype=F32) / GQA_HEADS) * LOG2E

    def neighbour(col, step):
        return pl.BlockSpec((1, tq, okv), lambda j, i: (i, jnp.clip(j + step, 0, nq - 1), col))

    return pl.pallas_call(
        functools.partial(_win_attn_kernel, tq=tq, seq=seq),
        grid=(nq, b),
        in_specs=[pl.BlockSpec(memory_space=pltpu.SMEM),
                  pl.BlockSpec(memory_space=pltpu.SMEM),
                  pl.BlockSpec((1, tq, oq), lambda j, i: (i, j, 0)),
                  neighbour(kcol, -1), neighbour(kcol, 0), neighbour(kcol, 1),
                  neighbour(vcol, -1), neighbour(vcol, 0), neighbour(vcol, 1)],
        out_specs=pl.BlockSpec((1, tq, oq), lambda j, i: (i, j, 0)),
        out_shape=jax.ShapeDtypeStruct((b, seq, oq), F32),
        scratch_shapes=[pltpu.VMEM((GQA_HEADS, tq, 3 * tq), F32)],
        compiler_params=_cparams(("parallel", "arbitrary")),
        name="window_gqa",
    )(slopes, sink.astype(F32) * LOG2E, proj3d, proj3d, proj3d, proj3d, proj3d, proj3d, proj3d)


def _cross_block_kernel(x_ref, g_ref, wq_ref, kv_ref, wo_ref, o_ref):
    x = x_ref[0]
    h = _rms(x, g_ref[...]).astype(BF16)
    q = _dot(h, wq_ref[...])
    kv = kv_ref[0]
    scale = X_HD ** -0.5
    outs = []
    for hd in range(X_HEADS):
        qh = q[:, hd * X_HD:(hd + 1) * X_HD].astype(BF16)
        kh = kv[:, hd * X_HD:(hd + 1) * X_HD]
        vh = kv[:, X_W + hd * X_HD:X_W + (hd + 1) * X_HD]
        s = _dot_nt(qh, kh) * scale
        e = jnp.exp(s - jnp.max(s, axis=-1, keepdims=True))
        p = e / jnp.sum(e, axis=-1, keepdims=True)
        outs.append(_dot(p.astype(BF16), vh))
    o = jnp.concatenate(outs, axis=-1).astype(BF16)
    o_ref[0] = x + _dot(o, wo_ref[...])


def cross_block(x3d, gamma, wq_bf16, kv_bf16, wo_bf16):
    b, seq, d = x3d.shape
    m = kv_bf16.shape[1]
    tm = 512
    return pl.pallas_call(
        _cross_block_kernel,
        grid=(b, seq // tm),
        in_specs=[pl.BlockSpec((1, tm, d), lambda i, j: (i, j, 0)),
                  pl.BlockSpec((1, d), lambda i, j: (0, 0)),
                  pl.BlockSpec((d, X_W), lambda i, j: (0, 0)),
                  pl.BlockSpec((1, m, 2 * X_W), lambda i, j: (i, 0, 0)),
                  pl.BlockSpec((X_W, d), lambda i, j: (0, 0))],
        out_specs=pl.BlockSpec((1, tm, d), lambda i, j: (i, j, 0)),
        out_shape=jax.ShapeDtypeStruct((b, seq, d), F32),
        compiler_params=_cparams(("parallel", "parallel")),
        name="cross_attention_block",
    )(x3d, gamma.reshape(1, d), wq_bf16, kv_bf16, wo_bf16)


def _split_bf16(x):
    hi = x.astype(BF16)
    lo = (x - hi.astype(F32)).astype(BF16)
    return hi, lo


def _pack_halves(h):
    c = h.shape[1] // 2
    left = lax.bitcast_convert_type(h[:, :c].astype(BF16).astype(F32), U32)
    right = lax.bitcast_convert_type(h[:, c:].astype(BF16).astype(F32), U32)
    return left | (right >> 16)


def _unpack_halves(p):
    left = lax.bitcast_convert_type(p & jnp.uint32(0xFFFF0000), F32)
    right = lax.bitcast_convert_type(p << 16, F32)
    return jnp.concatenate([left, right], axis=-1)


def _router_kernel(x_ref, g_ref, whi_ref, wlo_ref, b_ref, hp_ref, eid_ref, wt_ref, rank_ref, cnt_ref,
                   base_ref, *, tm):
    i = pl.program_id(0)

    @pl.when(i == 0)
    def _():
        base_ref[...] = jnp.zeros_like(base_ref)

    h = _rms(x_ref[...], g_ref[...])
    hp_ref[...] = _pack_halves(h)
    hhi, hlo = _split_bf16(h)
    whi = whi_ref[...]
    logits = _dot(hhi, whi) + _dot(hlo, whi) + _dot(hhi, wlo_ref[...]) + b_ref[...]
    lane = lax.broadcasted_iota(I32, logits.shape, 1)
    big = jnp.int32(LANES)
    ninf = -jnp.inf

    gl = jnp.where(lane < N_GROUPS, logits, ninf)
    gmax = jnp.max(gl, axis=-1, keepdims=True)
    gsel = jnp.min(jnp.where(gl == gmax, lane, big), axis=-1, keepdims=True)
    ggate = 1.0 / jnp.sum(jnp.exp(gl - gmax), axis=-1, keepdims=True)

    lo_lane = N_GROUPS + gsel * EXP_PER_GROUP
    el = jnp.where((lane >= lo_lane) & (lane < lo_lane + EXP_PER_GROUP), logits, ninf)
    v1 = jnp.max(el, axis=-1, keepdims=True)
    i1 = jnp.min(jnp.where(el == v1, lane, big), axis=-1, keepdims=True)
    el2 = jnp.where(lane == i1, ninf, el)
    v2 = jnp.max(el2, axis=-1, keepdims=True)
    i2 = jnp.min(jnp.where(el2 == v2, lane, big), axis=-1, keepdims=True)
    e2 = jnp.exp(v2 - v1)
    w1 = ggate / (1.0 + e2)
    w2 = ggate * e2 / (1.0 + e2)

    one1 = lane == i1
    one2 = lane == i2
    onehot = (one1 | one2).astype(F32)
    r = lax.broadcasted_iota(I32, (tm, tm), 0)
    c = lax.broadcasted_iota(I32, (tm, tm), 1)
    tri = (c < r).astype(BF16)
    before = _dot(tri, onehot.astype(BF16)) + base_ref[...]
    rank1 = jnp.sum(jnp.where(one1, before, 0.0), axis=-1, keepdims=True)
    rank2 = jnp.sum(jnp.where(one2, before, 0.0), axis=-1, keepdims=True)
    total = base_ref[...] + jnp.sum(onehot, axis=0, keepdims=True)
    base_ref[...] = total
    cnt_ref[...] = total

    col = lax.broadcasted_iota(I32, (tm, 2), 1)
    eid_ref[...] = jnp.where(col == 0, i1, i2) - N_GROUPS
    wt_ref[...] = jnp.where(col == 0, w1, w2)
    rank_ref[...] = jnp.where(col == 0, rank1, rank2).astype(I32)


def moe_router(x2d, gamma, w_grp, b_grp, w_exp, b_exp):
    n, d = x2d.shape
    tm = 512
    wcat = jnp.pad(jnp.concatenate([w_grp, w_exp], axis=1), ((0, 0), (0, LANES - N_GROUPS - N_EXPERTS)))
    bcat = jnp.pad(jnp.concatenate([b_grp, b_exp]), (0, LANES - N_GROUPS - N_EXPERTS)).reshape(1, LANES)
    whi = wcat.astype(BF16)
    wlo = (wcat - whi.astype(F32)).astype(BF16)
    return pl.pallas_call(
        functools.partial(_router_kernel, tm=tm),
        grid=(n // tm,),
        in_specs=[pl.BlockSpec((tm, d), lambda i: (i, 0)),
                  pl.BlockSpec((1, d), lambda i: (0, 0)),
                  pl.BlockSpec((d, LANES), lambda i: (0, 0)),
                  pl.BlockSpec((d, LANES), lambda i: (0, 0)),
                  pl.BlockSpec((1, LANES), lambda i: (0, 0))],
        out_specs=[pl.BlockSpec((tm, d // 2), lambda i: (i, 0)),
                   pl.BlockSpec((tm, 2), lambda i: (i, 0)),
                   pl.BlockSpec((tm, 2), lambda i: (i, 0)),
                   pl.BlockSpec((tm, 2), lambda i: (i, 0)),
                   pl.BlockSpec((1, LANES), lambda i: (0, 0))],
        out_shape=[jax.ShapeDtypeStruct((n, d // 2), U32),
                   jax.ShapeDtypeStruct((n, 2), I32),
                   jax.ShapeDtypeStruct((n, 2), F32),
                   jax.ShapeDtypeStruct((n, 2), I32),
                   jax.ShapeDtypeStruct((1, LANES), F32)],
        scratch_shapes=[pltpu.VMEM((1, LANES), F32)],
        compiler_params=_cparams(("arbitrary",)),
        name="moe_router",
    )(x2d, gamma.reshape(1, d), whi, wlo, bcat)


def _row_copy(src_ref, src_row, dst_ref, dst_row, sem):
    return pltpu.make_async_copy(src_ref.at[pl.ds(src_row, 1)], dst_ref.at[pl.ds(dst_row, 1)], sem)


def _dispatch_kernel(dest_ref, hp_ref, xs_in_ref, xs_ref, sem, *, tm):
    del xs_in_ref

    def issue(r, carry):
        _row_copy(hp_ref, r, xs_ref, dest_ref[2 * r], sem).start(priority=0)
        _row_copy(hp_ref, r, xs_ref, dest_ref[2 * r + 1], sem).start(priority=1)
        return carry

    lax.fori_loop(0, tm, issue, 0, unroll=ROW_DMA_UNROLL)
    for _ in range(2):
        pltpu.make_async_copy(hp_ref, xs_ref.at[pl.ds(0, tm)], sem).wait()


def moe_dispatch(hp, dest_flat, cap):
    n, c = hp.shape
    tm = 512
    zeros = jnp.zeros((cap, c), U32)
    return pl.pallas_call(
        functools.partial(_dispatch_kernel, tm=tm),
        grid=(n // tm,),
        in_specs=[pl.BlockSpec((2 * tm,), lambda i: (i,), memory_space=pltpu.SMEM),
                  pl.BlockSpec((tm, c), lambda i: (i, 0)),
                  pl.BlockSpec(memory_space=pl.ANY)],
        out_specs=pl.BlockSpec(memory_space=pl.ANY),
        out_shape=jax.ShapeDtypeStruct((cap, c), U32),
        scratch_shapes=[pltpu.SemaphoreType.DMA(())],
        input_output_aliases={2: 0},
        compiler_params=_cparams(("arbitrary",)),
        name="moe_dispatch",
    )(dest_flat, hp, zeros)


def _expert_kernel(be_ref, nused_ref, xs_ref, wg_ref, wu_ref, wd_ref, y_ref, wg_s, wu_s, wd_s):
    b = pl.program_id(0)
    prev = be_ref[jnp.maximum(b - 1, 0)]
    active = b < nused_ref[0]

    @pl.when(active & ((b == 0) | (be_ref[b] != prev)))
    def _():
        wg_s[...] = wg_ref[0].astype(BF16)
        wu_s[...] = wu_ref[0].astype(BF16)
        wd_s[...] = wd_ref[0].astype(BF16)

    @pl.when(active)
    def _():
        x = _unpack_halves(xs_ref[...]).astype(BF16)
        a = _dot(x, wg_s[...])
        u = _dot(x, wu_s[...])
        hmid = (a / (1.0 + jnp.exp(-a)) * u).astype(BF16)
        y_ref[...] = _pack_halves(_dot(hmid, wd_s[...]))

    @pl.when(jnp.logical_not(active))
    def _():
        y_ref[...] = jnp.zeros_like(y_ref)


def moe_experts(xs, blk_e, nused, w_gate, w_up, w_down):
    cap, c = xs.shape
    d = 2 * c
    hid = w_gate.shape[2]
    nblk = cap // MOE_BLOCK
    grid_spec = pltpu.PrefetchScalarGridSpec(
        num_scalar_prefetch=2,
        grid=(nblk,),
        in_specs=[pl.BlockSpec((MOE_BLOCK, c), lambda b, be, nu: (b, 0)),
                  pl.BlockSpec((1, d, hid), lambda b, be, nu: (be[b], 0, 0)),
                  pl.BlockSpec((1, d, hid), lambda b, be, nu: (be[b], 0, 0)),
                  pl.BlockSpec((1, hid, d), lambda b, be, nu: (be[b], 0, 0))],
        out_specs=pl.BlockSpec((MOE_BLOCK, c), lambda b, be, nu: (b, 0)),
        scratch_shapes=[pltpu.VMEM((d, hid), BF16), pltpu.VMEM((d, hid), BF16), pltpu.VMEM((hid, d), BF16)],
    )
    return pl.pallas_call(
        _expert_kernel,
        grid_spec=grid_spec,
        out_shape=jax.ShapeDtypeStruct((cap, c), U32),
        compiler_params=_cparams(("arbitrary",)),
        name="moe_experts",
    )(blk_e, nused, xs, w_gate, w_up, w_down)


def _combine_kernel(dest_ref, x_ref, wt_ref, g_ref, yb_ref, o_ref, buf, sem, *, tm, final_norm):
    def issue(r, carry):
        _row_copy(yb_ref, dest_ref[2 * r], buf.at[0], r, sem).start(priority=0)
        _row_copy(yb_ref, dest_ref[2 * r + 1], buf.at[1], r, sem).start(priority=1)
        return carry

    lax.fori_loop(0, tm, issue, 0, unroll=ROW_DMA_UNROLL)
    for k in range(2):
        pltpu.make_async_copy(yb_ref.at[pl.ds(0, tm)], buf.at[k], sem).wait()
    wt = wt_ref[...]
    y = x_ref[...] + (_unpack_halves(buf[0]) * wt[:, 0:1] + _unpack_halves(buf[1]) * wt[:, 1:2])
    o_ref[...] = _rms(y, g_ref[...]) if final_norm else y


def moe_combine(x2d, yb, dest_flat, wt, g_final, final_norm):
    n, d = x2d.shape
    c = yb.shape[1]
    tm = 512
    return pl.pallas_call(
        functools.partial(_combine_kernel, tm=tm, final_norm=final_norm),
        grid=(n // tm,),
        in_specs=[pl.BlockSpec((2 * tm,), lambda i: (i,), memory_space=pltpu.SMEM),
                  pl.BlockSpec((tm, d), lambda i: (i, 0)),
                  pl.BlockSpec((tm, 2), lambda i: (i, 0)),
                  pl.BlockSpec((1, d), lambda i: (0, 0)),
                  pl.BlockSpec(memory_space=pl.ANY)],
        out_specs=pl.BlockSpec((tm, d), lambda i: (i, 0)),
        out_shape=jax.ShapeDtypeStruct((n, d), F32),
        scratch_shapes=[pltpu.VMEM((2, tm, c), U32), pltpu.SemaphoreType.DMA(())],
        compiler_params=_cparams(("arbitrary",)),
        name="moe_combine",
    )(dest_flat, x2d, wt, g_final.reshape(1, d), yb)


def hier_moe_block(x2d, gamma, w_grp, b_grp, w_exp, b_exp, w_gate, w_up, w_down, layer, g_final, final_norm):
    n = x2d.shape[0]
    cap = 2 * n + N_EXPERTS * MOE_BLOCK
    hp, eid, wt, rank, cnt = moe_router(x2d, gamma, w_grp, b_grp, w_exp, b_exp)
    counts = cnt[0, N_GROUPS:N_GROUPS + N_EXPERTS].astype(I32)
    padded = (counts + MOE_BLOCK - 1) // MOE_BLOCK * MOE_BLOCK
    p_ends = jnp.cumsum(padded)
    p_starts = p_ends - padded
    experts = jnp.arange(N_EXPERTS, dtype=I32)
    dest = (jnp.sum(jnp.where(eid[..., None] == experts, p_starts, 0), axis=-1) + rank).reshape(-1)
    nblk = cap // MOE_BLOCK
    blk_start = jnp.arange(nblk, dtype=I32) * MOE_BLOCK
    blk_e = jnp.minimum(jnp.sum((p_ends[None, :] <= blk_start[:, None]).astype(I32), axis=1), N_EXPERTS - 1)
    nused = (p_ends[-1:] // MOE_BLOCK).astype(I32)
    xs = moe_dispatch(hp, dest, cap)
    yb = moe_experts(xs, blk_e + layer * N_EXPERTS, nused, w_gate, w_up, w_down)
    return moe_combine(x2d, yb, dest, wt, g_final, final_norm)


def kernel(x, mem, e_norm, e_w_in, e_conv_w, e_conv_b, e_filt_w1, e_filt_b1, e_filt_w2, e_filt_b2, e_filt_w3, e_filt_freq, e_hy_bias, e_lam, e_subln, e_w_out, o_norm, o_w_in, o_sink, o_w_out, c_norm, c_wq, c_wkv, c_wo, f_norm, f_w_grp, f_b_grp, f_w_exp, f_b_exp, f_w_gate, f_w_up, f_w_down, g_mem, g_final):
    b, seq, d = x.shape
    n = b * seq
    m = mem.shape[1]
    tables = dft_tables(seq)
    x2 = x.reshape(n, d)
    mem2 = mem.reshape(b * m, d)
    w_gate = f_w_gate.reshape(DEPTH * N_EXPERTS, d, EXP_HIDDEN)
    w_up = f_w_up.reshape(DEPTH * N_EXPERTS, d, EXP_HIDDEN)
    w_down = f_w_down.reshape(DEPTH * N_EXPERTS, EXP_HIDDEN, d)
    qperm = window_column_perm()
    for i in range(DEPTH):
        j = i // 2
        if i % 2 == 0:
            hyw, qw = 3 * HY_WIDTH, 2 * DIFF_HEADS * DIFF_HEAD_DIM
            w_in = jnp.concatenate([e_w_in[j][:, :hyw],
                                    e_w_in[j][:, hyw:hyw + qw] * (DIFF_HEAD_DIM ** -0.5 * LOG2E),
                                    e_w_in[j][:, hyw + qw:]], axis=1).astype(BF16)
            proj_h, proj_a = norm_matmul(x2, e_norm[j], w_in, 512, [(hyw, F32), (w_in.shape[1] - hyw, BF16)])
            y_hy = hyena_mixer(proj_h.reshape(b, seq, -1), tables, e_conv_w[j], e_conv_b[j], e_filt_w1[j],
                               e_filt_b1[j], e_filt_w2[j], e_filt_b2[j], e_filt_w3[j], e_filt_freq[j], e_hy_bias[j])
            y_df = diff_attention(proj_a.reshape(b, seq, -1), e_lam[j], e_subln[j], i)
            w_out = e_w_out[j].astype(BF16)
            x2 = matmul_residual(x2, [y_hy.reshape(n, -1), y_df.reshape(n, -1)],
                                 [w_out[:HY_WIDTH], w_out[HY_WIDTH:]], 512)
        else:
            oq = GQA_HEADS * GQA_HD
            w_in = jnp.concatenate([o_w_in[j][:, :oq][:, qperm] * (GQA_HD ** -0.5 * LOG2E), o_w_in[j][:, oq:]],
                                   axis=1).astype(BF16)
            (proj,) = norm_matmul(x2, o_norm[j], w_in, 512, [(w_in.shape[1], BF16)])
            att = window_gqa(proj.reshape(b, seq, -1), o_sink[j])
            x2 = matmul_residual(x2, [att.reshape(n, -1)], [o_w_out[j][qperm].astype(BF16)], 512)
        (kv,) = norm_matmul(mem2, g_mem, c_wkv[i].astype(BF16), 512, [(2 * X_W, BF16)])
        kv = kv.reshape(b, m, -1)
        x2 = cross_block(x2.reshape(b, seq, d), c_norm[i], c_wq[i].astype(BF16), kv,
                         c_wo[i].astype(BF16)).reshape(n, d)
        x2 = hier_moe_block(x2, f_norm[i], f_w_grp[i], f_b_grp[i], f_w_exp[i], f_b_exp[i], w_gate, w_up, w_down,
                            i, g_final, i == DEPTH - 1)
    return x2.reshape(b, seq, d)
```

```python
import functools
import math

import jax
import jax.numpy as jnp
from jax import lax
from jax.experimental import pallas as pl
from jax.experimental.pallas import tpu as pltpu

F32 = jnp.float32
BF16 = jnp.bfloat16
I32 = jnp.int32
U32 = jnp.uint32

D_MODEL = 1024
DEPTH = 4
EPS = 1e-6
NEG = -1e30
HY_WIDTH = 512
HY_BANDS = 16
HY_FILT_HIDDEN = 64
HY_DECAY_TARGET = 1e-2
HY_FAST_PCT = 0.3
HY_SLOW_PCT = 1.5
DIFF_HEADS = 4
DIFF_HEAD_DIM = 64
WIN = 128
GQA_HEADS = 16
GQA_KV = 4
GQA_HD = 64
X_HEADS = 4
X_HD = 128
X_W = X_HEADS * X_HD
N_GROUPS = 4
EXP_PER_GROUP = 8
N_EXPERTS = N_GROUPS * EXP_PER_GROUP
EXP_HIDDEN = 512

LOG2E = 1.4426950408889634
LANES = 128
VMEM_LIMIT = 56 * 1024 * 1024
MOE_BLOCK = 256
DIFF_Q_BLOCK = 256
DIFF_KEY_CHUNK = 512
ROW_DMA_UNROLL = 8


def _cparams(sem):
    return pltpu.CompilerParams(dimension_semantics=sem, vmem_limit_bytes=VMEM_LIMIT)


def _rms(x, g):
    ms = jnp.mean(x * x, axis=-1, keepdims=True)
    return x * lax.rsqrt(ms + EPS) * g


def _dot(a, b):
    return jnp.dot(a, b, preferred_element_type=F32)


def _dot_nt(a, b):
    return lax.dot_general(a, b, (((1,), (1,)), ((), ())), preferred_element_type=F32)


def _norm_matmul_kernel(x_ref, g_ref, w_ref, *o_refs):
    h = _rms(x_ref[...], g_ref[...]).astype(BF16)
    col = 0
    for o_ref in o_refs:
        width = o_ref.shape[1]
        o_ref[...] = _dot(h, w_ref[:, col:col + width]).astype(o_ref.dtype)
        col += width


def norm_matmul(x2d, gamma, w_bf16, tm, outs):
    n, d = x2d.shape
    f = w_bf16.shape[1]
    assert sum(width for width, _ in outs) == f
    res = pl.pallas_call(
        _norm_matmul_kernel,
        grid=(n // tm,),
        in_specs=[pl.BlockSpec((tm, d), lambda i: (i, 0)),
                  pl.BlockSpec((1, d), lambda i: (0, 0)),
                  pl.BlockSpec((d, f), lambda i: (0, 0))],
        out_specs=[pl.BlockSpec((tm, width), lambda i: (i, 0)) for width, _ in outs],
        out_shape=[jax.ShapeDtypeStruct((n, width), dtype) for width, dtype in outs],
        compiler_params=_cparams(("parallel",)),
        name="norm_matmul",
    )(x2d, gamma.reshape(1, d), w_bf16)
    return res


def _conv3_kernel(u_ref, w_ref, b_ref, o_ref):
    u = u_ref[0]
    seq = u.shape[0]
    row = lax.broadcasted_iota(I32, u.shape, 0)
    prev = jnp.where(row == 0, 0.0, pltpu.roll(u, 1, 0))
    nxt = jnp.where(row == seq - 1, 0.0, pltpu.roll(u, seq - 1, 0))
    w = w_ref[...]
    o_ref[0] = prev * w[0:1] + u * w[1:2] + nxt * w[2:3] + b_ref[...]


def conv3(proj3d, conv_w, conv_b):
    b, seq, _ = proj3d.shape
    c = 3 * HY_WIDTH
    tc = HY_WIDTH
    return pl.pallas_call(
        _conv3_kernel,
        grid=(b, c // tc),
        in_specs=[pl.BlockSpec((1, seq, tc), lambda i, j: (i, 0, j)),
                  pl.BlockSpec((3, tc), lambda i, j: (0, j)),
                  pl.BlockSpec((1, tc), lambda i, j: (0, j))],
        out_specs=pl.BlockSpec((1, seq, tc), lambda i, j: (i, 0, j)),
        out_shape=jax.ShapeDtypeStruct((b, seq, c), F32),
        compiler_params=_cparams(("parallel", "parallel")),
        name="hyena_conv3",
    )(proj3d, conv_w, conv_b.reshape(1, c))


def _hy_filter_kernel(t_ref, bands_ref, w1t_ref, w1c_ref, w1s_ref, b1_ref, w2_ref, b2_ref, w3_ref,
                      freq_ref, delta_ref, hf_ref, hb_ref, *, seq, tl):
    hi = lax.Precision.HIGHEST
    i = pl.program_id(0)
    t = t_ref[...]
    pos = (i * tl + lax.broadcasted_iota(I32, (tl, 1), 0)).astype(F32)
    ang = bands_ref[...] * (2.0 * math.pi * pos / seq)
    f = freq_ref[...]
    pre = (t * w1t_ref[...]
           + jnp.dot(jnp.cos(ang), w1c_ref[...], precision=hi, preferred_element_type=F32)
           - jnp.dot(jnp.sin(ang), w1s_ref[...], precision=hi, preferred_element_type=F32)
           + b1_ref[...])
    a = jnp.sin(f * pre)
    a = jnp.sin(f * (jnp.dot(a, w2_ref[...], precision=hi, preferred_element_type=F32) + b2_ref[...]))
    h = jnp.dot(a, w3_ref[...], precision=hi, preferred_element_type=F32)
    decay = jnp.exp(-t * jnp.abs(delta_ref[...]))
    w = HY_WIDTH
    for o in range(2):
        hf_ref[:, o * w:(o + 1) * w] = h[:, o * 2 * w:o * 2 * w + w] * decay
        hb_ref[:, o * w:(o + 1) * w] = jnp.where(pos == 0.0, 0.0, h[:, o * 2 * w + w:(o + 1) * 2 * w] * decay)


def hyena_filters_time(seq, w1, b1, w2, b2, w3, freq):
    tl = 512
    hid = LANES
    pad_h = hid - HY_FILT_HIDDEN
    t = jnp.linspace(0.0, 1.0, seq, dtype=F32)[:, None]
    bands = jnp.pad(jnp.linspace(1e-4, HY_BANDS - 1, HY_BANDS, dtype=F32)[None], ((0, 0), (0, LANES - HY_BANDS)))
    w1p = jnp.pad(w1, ((0, 0), (0, pad_h)))
    w1t = w1p[0:1]
    w1c = jnp.pad(w1p[1:1 + HY_BANDS], ((0, LANES - HY_BANDS), (0, 0)))
    w1s = jnp.pad(w1p[1 + HY_BANDS:], ((0, LANES - HY_BANDS), (0, 0)))
    b1p = jnp.pad(b1, (0, pad_h)).reshape(1, hid)
    w2p = jnp.pad(w2, ((0, pad_h), (0, pad_h)))
    b2p = jnp.pad(b2, (0, pad_h)).reshape(1, hid)
    w3p = jnp.pad(w3, ((0, pad_h), (0, 0)))
    freqp = jnp.pad(freq, (0, pad_h)).reshape(1, hid)
    max_decay = math.log(HY_DECAY_TARGET) / HY_FAST_PCT
    min_decay = math.log(HY_DECAY_TARGET) / HY_SLOW_PCT
    deltas = jnp.linspace(min_decay, max_decay, HY_WIDTH, dtype=F32)[None]
    fw = w3.shape[1]
    full = lambda shape: pl.BlockSpec(shape, lambda i: (0, 0))
    return pl.pallas_call(
        functools.partial(_hy_filter_kernel, seq=seq, tl=tl),
        grid=(seq // tl,),
        in_specs=[pl.BlockSpec((tl, 1), lambda i: (i, 0)), full((1, LANES)), full((1, hid)),
                  full((LANES, hid)), full((LANES, hid)), full((1, hid)), full((hid, hid)), full((1, hid)),
                  full((hid, fw)), full((1, hid)), full((1, HY_WIDTH))],
        out_specs=[pl.BlockSpec((tl, 2 * HY_WIDTH), lambda i: (i, 0)),
                   pl.BlockSpec((tl, 2 * HY_WIDTH), lambda i: (i, 0))],
        out_shape=[jax.ShapeDtypeStruct((seq, 2 * HY_WIDTH), F32)] * 2,
        compiler_params=_cparams(("parallel",)),
        name="hyena_filter_mlp",
    )(t, bands, w1t, w1c, w1s, b1p, w2p, b2p, w3p, freqp, deltas)


def dft_tables(seq):
    n2 = 4 * seq
    sub = 64
    f = jnp.arange(seq, dtype=I32)[:, None]
    odd = 2 * f + 1
    s1 = jnp.arange(seq // sub, dtype=I32)[None]
    s0 = jnp.arange(sub, dtype=I32)[None]
    ang_p = ((odd * s1) % (n2 // sub)).astype(F32) * (2.0 * math.pi * sub / n2)
    ang_q = ((odd * s0) % n2).astype(F32) * (2.0 * math.pi / n2)
    pc, ps, qc, qs = jnp.cos(ang_p), jnp.sin(ang_p), jnp.cos(ang_q), jnp.sin(ang_q)
    c = (pc[:, :, None] * qc[:, None, :] - ps[:, :, None] * qs[:, None, :]).reshape(seq, seq)
    s = (ps[:, :, None] * qc[:, None, :] + pc[:, :, None] * qs[:, None, :]).reshape(seq, seq)
    return c.astype(BF16), s.astype(BF16), c.T.astype(BF16), s.T.astype(BF16)


def dft_twiddles(seq):
    g = jnp.arange(seq // 2, dtype=I32)[:, None]
    ang = (2 * g + 1).astype(F32) * (2.0 * math.pi / (4 * seq))
    return jnp.cos(ang), jnp.sin(ang)


def _half_transform(c1, s1, cg, sg, x0, x1):
    pc0, ps0 = _dot(c1, x0), _dot(s1, x0)
    pc1, ps1 = _dot(c1, x1), _dot(s1, x1)
    tr = cg * pc1 - sg * ps1
    ti = -(cg * ps1 + sg * pc1)
    return pc0 + tr, ti - ps0, pc0 - tr, ps0 + ti


def _spectrum_kernel(c_ref, s_ref, cg_ref, sg_ref, f0_ref, f1_ref, b0_ref, b1_ref,
                     kgre_ref, kgim_ref, khre_ref, khim_ref):
    c1, s1, cg, sg = c_ref[...], s_ref[...], cg_ref[...], sg_ref[...]
    fre_g, fim_g, fre_h, fim_h = _half_transform(c1, s1, cg, sg, f0_ref[...].astype(BF16), f1_ref[...].astype(BF16))
    bre_g, bim_g, bre_h, bim_h = _half_transform(c1, s1, cg, sg, b0_ref[...].astype(BF16), b1_ref[...].astype(BF16))
    kgre_ref[...] = fre_g + bre_g
    kgim_ref[...] = fim_g - bim_g
    khre_ref[...] = fre_h + bre_h
    khim_ref[...] = fim_h - bim_h


def filter_spectrum(tables, twiddles, hf, hb):
    seq, cols = hf.shape
    half = seq // 2
    tf, tn = 512, 512
    nj = cols // tn
    hf2 = hf.reshape(half, 2 * cols)
    hb2 = hb.reshape(half, 2 * cols)
    tab = pl.BlockSpec((tf, half), lambda i, j: (i, 0))
    twd = pl.BlockSpec((tf, 1), lambda i, j: (i, 0))
    even = pl.BlockSpec((half, tn), lambda i, j: (0, j))
    odd = pl.BlockSpec((half, tn), lambda i, j: (0, nj + j))
    return pl.pallas_call(
        _spectrum_kernel,
        grid=(half // tf, nj),
        in_specs=[tab, tab, twd, twd, even, odd, even, odd],
        out_specs=[pl.BlockSpec((tf, tn), lambda i, j: (i, j))] * 4,
        out_shape=[jax.ShapeDtypeStruct((half, cols), F32)] * 4,
        compiler_params=_cparams(("parallel", "parallel")),
        name="hyena_filter_spectrum",
    )(tables[0], tables[1], twiddles[0], twiddles[1], hf2, hf2, hb2, hb2)


def _hy_fwd_kernel(z0_ref, z1_ref, c_ref, s_ref, cg_ref, sg_ref, kgre_ref, kgim_ref, khre_ref, khim_ref,
                   u0re_ref, u0im_ref, u1re_ref, u1im_ref):
    cg, sg = cg_ref[...], sg_ref[...]
    zre_g, zim_g, zre_h, zim_h = _half_transform(c_ref[...], s_ref[...], cg, sg,
                                                 z0_ref[0].astype(BF16), z1_ref[0].astype(BF16))
    kgre, kgim, khre, khim = kgre_ref[...], kgim_ref[...], khre_ref[...], khim_ref[...]
    yre_g = zre_g * kgre - zim_g * kgim
    yim_g = zre_g * kgim + zim_g * kgre
    yre_h = zre_h * khre - zim_h * khim
    yim_h = zre_h * khim + zim_h * khre
    u0re_ref[0] = (yre_g + yre_h).astype(BF16)
    u0im_ref[0] = (yim_g - yim_h).astype(BF16)
    a = yre_g - yre_h
    b = yim_g + yim_h
    u1re_ref[0] = (cg * a - sg * b).astype(BF16)
    u1im_ref[0] = (cg * b + sg * a).astype(BF16)


def hyena_fwd(z2, zcols, tables, twiddles, kspec, order):
    b, half, _ = z2.shape
    w = HY_WIDTH
    tf = 512
    tab = pl.BlockSpec((tf, half), lambda i, j: (j, 0))
    twd = pl.BlockSpec((tf, 1), lambda i, j: (j, 0))
    kblk = pl.BlockSpec((tf, w), lambda i, j: (j, order))
    return pl.pallas_call(
        _hy_fwd_kernel,
        grid=(b, half // tf),
        in_specs=[pl.BlockSpec((1, half, w), lambda i, j: (i, 0, zcols[0])),
                  pl.BlockSpec((1, half, w), lambda i, j: (i, 0, zcols[1])),
                  tab, tab, twd, twd, kblk, kblk, kblk, kblk],
        out_specs=[pl.BlockSpec((1, tf, w), lambda i, j: (i, j, 0))] * 4,
        out_shape=[jax.ShapeDtypeStruct((b, half, w), BF16)] * 4,
        compiler_params=_cparams(("parallel", "parallel")),
        name="hyena_dft_fwd",
    )(z2, z2, tables[0], tables[1], twiddles[0], twiddles[1], *kspec)


def _hy_inv_kernel(ct_ref, st_ref, u0re_ref, u0im_ref, u1re_ref, u1im_ref, z0_ref, z1_ref, g0_ref, g1_ref,
                   bias_ref, o_ref, *, scale):
    ct, st = ct_ref[...], st_ref[...]
    w = z0_ref.shape[2]
    y0 = (_dot(ct, u0re_ref[0]) - _dot(st, u0im_ref[0])) * scale
    y1 = (_dot(ct, u1re_ref[0]) - _dot(st, u1im_ref[0])) * scale
    bias = bias_ref[...]
    o_ref[0, :, 0:w] = g0_ref[0] * (y0 + z0_ref[0] * bias)
    o_ref[0, :, w:2 * w] = g1_ref[0] * (y1 + z1_ref[0] * bias)


def hyena_inv(tables, u, z2, zcols, g2, gcols, bias_row):
    b, half, w = u[0].shape
    tt = 512
    tab = pl.BlockSpec((tt, half), lambda i, j: (j, 0))
    ublk = pl.BlockSpec((1, half, w), lambda i, j: (i, 0, 0))

    def rows(col):
        return pl.BlockSpec((1, tt, w), lambda i, j: (i, j, col))

    return pl.pallas_call(
        functools.partial(_hy_inv_kernel, scale=0.5 / half),
        grid=(b, half // tt),
        in_specs=[tab, tab, ublk, ublk, ublk, ublk, rows(zcols[0]), rows(zcols[1]), rows(gcols[0]), rows(gcols[1]),
                  pl.BlockSpec((1, w), lambda i, j: (0, 0))],
        out_specs=pl.BlockSpec((1, tt, 2 * w), lambda i, j: (i, j, 0)),
        out_shape=jax.ShapeDtypeStruct((b, half, 2 * w), F32),
        compiler_params=_cparams(("parallel", "parallel")),
        name="hyena_dft_inv",
    )(tables[2], tables[3], *u, z2, z2, g2, g2, bias_row)


def hyena_mixer(proj3d, tables, twiddles, conv_w, conv_b, w1, b1, w2, b2, w3, freq, hy_bias):
    b, seq, _ = proj3d.shape
    half = seq // 2
    u = conv3(proj3d, conv_w, conv_b)
    u2 = u.reshape(b, half, 6 * HY_WIDTH)
    hf, hb = hyena_filters_time(seq, w1, b1, w2, b2, w3, freq)
    kspec = filter_spectrum(tables, twiddles, hf, hb)
    uu = hyena_fwd(u2, (0, 3), tables, twiddles, kspec, 0)
    z1 = hyena_inv(tables, uu, u2, (0, 3), u2, (1, 4), hy_bias[0:1])
    uu = hyena_fwd(z1, (0, 1), tables, twiddles, kspec, 1)
    y = hyena_inv(tables, uu, z1, (0, 1), u2, (2, 5), hy_bias[1:2])
    return y.reshape(b, seq, HY_WIDTH)


def _diff_attn_kernel(slope_ref, q_ref, k_ref, v_ref, lam_ref, sub_ref, o_ref, bias_ref, *, tq, lam_init):
    h = pl.program_id(0)
    qi = pl.program_id(1)
    seq = k_ref.shape[1]

    @pl.when(pl.program_id(2) == 0)
    def _():
        qpos = qi * tq + lax.broadcasted_iota(I32, (tq, seq), 0)
        kpos = lax.broadcasted_iota(I32, (tq, seq), 1)
        bias_ref[...] = slope_ref[h] * jnp.abs(qpos - kpos).astype(F32)

    q = q_ref[0]
    k = k_ref[0]
    v = v_ref[0]
    lane = lax.broadcasted_iota(I32, (1, 2 * DIFF_HEAD_DIM), 1)
    ck = DIFF_KEY_CHUNK

    def attend(m):
        keep = (lane < DIFF_HEAD_DIM) if m == 0 else (lane >= DIFF_HEAD_DIM)
        qm = jnp.where(keep, q, jnp.zeros_like(q))
        s = [_dot_nt(qm, k[c:c + ck]) - bias_ref[:, c:c + ck] for c in range(0, seq, ck)]
        mx = functools.reduce(jnp.maximum, [jnp.max(sc, axis=-1, keepdims=True) for sc in s])
        acc = jnp.zeros((tq, 2 * DIFF_HEAD_DIM), F32)
        den = jnp.zeros((tq, 1), F32)
        for i, sc in enumerate(s):
            e = jnp.exp2(sc - mx)
            den = den + jnp.sum(e, axis=-1, keepdims=True)
            acc = acc + _dot(e.astype(BF16), v[i * ck:(i + 1) * ck])
        return acc / den

    l = lam_ref[...]
    lam_full = (jnp.exp(jnp.sum(l[0:1] * l[1:2], axis=-1, keepdims=True))
                - jnp.exp(jnp.sum(l[2:3] * l[3:4], axis=-1, keepdims=True)) + lam_init)
    o = attend(0) - lam_full * attend(1)
    o_ref[0] = _rms(o, sub_ref[...]) * (1.0 - lam_init)


def diff_attention(proj3d, lam, subln, layer_idx):
    b, seq, _ = proj3d.shape
    tq = DIFF_Q_BLOCK
    hw = 2 * DIFF_HEAD_DIM
    qb, kb, vb = 0, DIFF_HEADS, 2 * DIFF_HEADS
    lam_init = 0.8 - 0.6 * math.exp(-0.3 * layer_idx)
    slopes = 2.0 ** (-8.0 * jnp.arange(1, DIFF_HEADS + 1, dtype=F32) / DIFF_HEADS) * LOG2E
    return pl.pallas_call(
        functools.partial(_diff_attn_kernel, tq=tq, lam_init=lam_init),
        grid=(DIFF_HEADS, seq // tq, b),
        in_specs=[pl.BlockSpec(memory_space=pltpu.SMEM),
                  pl.BlockSpec((1, tq, hw), lambda h, j, i: (i, j, qb + h)),
                  pl.BlockSpec((1, seq, hw), lambda h, j, i: (i, 0, kb + h)),
                  pl.BlockSpec((1, seq, hw), lambda h, j, i: (i, 0, vb + h)),
                  pl.BlockSpec((4, DIFF_HEAD_DIM), lambda h, j, i: (0, 0)),
                  pl.BlockSpec((1, hw), lambda h, j, i: (0, 0))],
        out_specs=pl.BlockSpec((1, tq, hw), lambda h, j, i: (i, j, h)),
        out_shape=jax.ShapeDtypeStruct((b, seq, DIFF_HEADS * hw), F32),
        scratch_shapes=[pltpu.VMEM((tq, seq), F32)],
        compiler_params=_cparams(("parallel", "parallel", "arbitrary")),
        name="diff_attention",
    )(slopes, proj3d, proj3d, proj3d, lam, subln.reshape(1, hw))


def _win_attn_kernel(slope_ref, sink_ref, q_ref, kp_ref, kc_ref, kn_ref, vp_ref, vc_ref, vn_ref, o_ref, bias_ref,
                     *, tq, seq):
    qi = pl.program_id(0)
    span = 3 * tq
    group = GQA_HEADS // GQA_KV

    @pl.when(pl.program_id(1) == 0)
    def _():
        qpos = qi * tq + lax.broadcasted_iota(I32, (tq, span), 0)
        kpos = (qi - 1) * tq + lax.broadcasted_iota(I32, (tq, span), 1)
        rel = jnp.abs(qpos - kpos)
        relf = rel.astype(F32)
        masked = jnp.where((rel <= WIN) & (kpos >= 0) & (kpos < seq), 0.0, -NEG)
        for head in range(GQA_HEADS):
            bias_ref[head] = slope_ref[head] * relf + masked

    kwin = jnp.concatenate([kp_ref[0], kc_ref[0], kn_ref[0]], axis=0)
    vwin = jnp.concatenate([vp_ref[0], vc_ref[0], vn_ref[0]], axis=0)
    low = lax.broadcasted_iota(I32, (1, LANES), 1) < GQA_HD
    for p in range(GQA_KV // 2):
        kb = kwin[:, p * LANES:(p + 1) * LANES]
        vb = vwin[:, p * LANES:(p + 1) * LANES]
        outs = []
        for half in range(2):
            kv = 2 * p + half
            mine = low if half == 0 else jnp.logical_not(low)
            qblocks = [q_ref[0, :, (group * p + r) * LANES:(group * p + r + 1) * LANES] for r in range(group)]
            qs = jnp.concatenate([jnp.where(mine, qb, jnp.zeros_like(qb)) for qb in qblocks], axis=0)
            s = _dot_nt(qs, kb)
            es, inv = [], []
            for r in range(group):
                head = kv * group + r
                sink = sink_ref[head]
                sr = s[r * tq:(r + 1) * tq] - bias_ref[head]
                m = jnp.maximum(jnp.max(sr, axis=-1, keepdims=True), sink)
                e = jnp.exp2(sr - m)
                inv.append(1.0 / (jnp.sum(e, axis=-1, keepdims=True) + jnp.exp2(sink - m)))
                es.append(e.astype(BF16))
            o = _dot(jnp.concatenate(es, axis=0), vb)
            outs.append([o[r * tq:(r + 1) * tq] * inv[r] for r in range(group)])
        for r in range(group):
            col = (group * p + r) * LANES
            o_ref[0, :, col:col + LANES] = jnp.where(low, outs[0][r], outs[1][r])


def window_head_order():
    group = GQA_HEADS // GQA_KV
    order = []
    for p in range(GQA_KV // 2):
        for r in range(group):
            order += [(2 * p) * group + r, (2 * p + 1) * group + r]
    return order


def window_column_perm():
    cols = []
    for head in window_head_order():
        cols += list(range(head * GQA_HD, (head + 1) * GQA_HD))
    return jnp.asarray(cols, dtype=I32)


def window_gqa(proj3d, sink):
    b, seq, _ = proj3d.shape
    tq = WIN
    nq = seq // tq
    oq = GQA_HEADS * GQA_HD
    okv = GQA_KV * GQA_HD
    kcol, vcol = oq // okv, oq // okv + 1
    slopes = 2.0 ** (-8.0 * jnp.arange(1, GQA_HEADS + 1, dtype=F32) / GQA_HEADS) * LOG2E

    def neighbour(col, step):
        return pl.BlockSpec((1, tq, okv), lambda j, i: (i, jnp.clip(j + step, 0, nq - 1), col))

    return pl.pallas_call(
        functools.partial(_win_attn_kernel, tq=tq, seq=seq),
        grid=(nq, b),
        in_specs=[pl.BlockSpec(memory_space=pltpu.SMEM),
                  pl.BlockSpec(memory_space=pltpu.SMEM),
                  pl.BlockSpec((1, tq, oq), lambda j, i: (i, j, 0)),
                  neighbour(kcol, -1), neighbour(kcol, 0), neighbour(kcol, 1),
                  neighbour(vcol, -1), neighbour(vcol, 0), neighbour(vcol, 1)],
        out_specs=pl.BlockSpec((1, tq, oq), lambda j, i: (i, j, 0)),
        out_shape=jax.ShapeDtypeStruct((b, seq, oq), F32),
        scratch_shapes=[pltpu.VMEM((GQA_HEADS, tq, 3 * tq), F32)],
        compiler_params=_cparams(("parallel", "arbitrary")),
        name="window_gqa",
    )(slopes, sink.astype(F32) * LOG2E, proj3d, proj3d, proj3d, proj3d, proj3d, proj3d, proj3d)


def _cross_attend(x, g, wq, kv, wo):
    h = _rms(x, g).astype(BF16)
    q = _dot(h, wq)
    scale = X_HD ** -0.5
    outs = []
    for hd in range(X_HEADS):
        qh = q[:, hd * X_HD:(hd + 1) * X_HD].astype(BF16)
        kh = kv[:, hd * X_HD:(hd + 1) * X_HD]
        vh = kv[:, X_W + hd * X_HD:X_W + (hd + 1) * X_HD]
        s = _dot_nt(qh, kh) * scale
        e = jnp.exp(s - jnp.max(s, axis=-1, keepdims=True))
        p = e / jnp.sum(e, axis=-1, keepdims=True)
        outs.append(_dot(p.astype(BF16), vh))
    o = jnp.concatenate(outs, axis=-1).astype(BF16)
    return x + _dot(o, wo)


def _split_bf16(x):
    hi = x.astype(BF16)
    lo = (x - hi.astype(F32)).astype(BF16)
    return hi, lo


def _pack_halves(h):
    c = h.shape[1] // 2
    left = lax.bitcast_convert_type(h[:, :c].astype(BF16).astype(F32), U32)
    right = lax.bitcast_convert_type(h[:, c:].astype(BF16).astype(F32), U32)
    return left | (right >> 16)


def _unpack_halves(p):
    left = lax.bitcast_convert_type(p & jnp.uint32(0xFFFF0000), F32)
    right = lax.bitcast_convert_type(p << 16, F32)
    return jnp.concatenate([left, right], axis=-1)


def _route(x, g, whi, wlo, bias, hp_ref, eid_ref, wt_ref, rank_ref, cnt_ref, base_ref):
    tm = x.shape[0]
    h = _rms(x, g)
    hp_ref[...] = _pack_halves(h)
    hhi, hlo = _split_bf16(h)
    logits = _dot(hhi, whi) + _dot(hlo, whi) + _dot(hhi, wlo) + bias
    lane = lax.broadcasted_iota(I32, logits.shape, 1)
    big = jnp.int32(LANES)
    ninf = -jnp.inf

    gl = jnp.where(lane < N_GROUPS, logits, ninf)
    gmax = jnp.max(gl, axis=-1, keepdims=True)
    gsel = jnp.min(jnp.where(gl == gmax, lane, big), axis=-1, keepdims=True)
    ggate = 1.0 / jnp.sum(jnp.exp(gl - gmax), axis=-1, keepdims=True)

    lo_lane = N_GROUPS + gsel * EXP_PER_GROUP
    el = jnp.where((lane >= lo_lane) & (lane < lo_lane + EXP_PER_GROUP), logits, ninf)
    v1 = jnp.max(el, axis=-1, keepdims=True)
    i1 = jnp.min(jnp.where(el == v1, lane, big), axis=-1, keepdims=True)
    el2 = jnp.where(lane == i1, ninf, el)
    v2 = jnp.max(el2, axis=-1, keepdims=True)
    i2 = jnp.min(jnp.where(el2 == v2, lane, big), axis=-1, keepdims=True)
    e2 = jnp.exp(v2 - v1)
    w1 = ggate / (1.0 + e2)
    w2 = ggate * e2 / (1.0 + e2)

    one1 = lane == i1
    one2 = lane == i2
    onehot = (one1 | one2).astype(F32)
    r = lax.broadcasted_iota(I32, (tm, tm), 0)
    c = lax.broadcasted_iota(I32, (tm, tm), 1)
    tri = (c < r).astype(BF16)
    before = _dot(tri, onehot.astype(BF16)) + base_ref[...]
    rank1 = jnp.sum(jnp.where(one1, before, 0.0), axis=-1, keepdims=True)
    rank2 = jnp.sum(jnp.where(one2, before, 0.0), axis=-1, keepdims=True)
    total = base_ref[...] + jnp.sum(onehot, axis=0, keepdims=True)
    base_ref[...] = total
    cnt_ref[...] = total

    col = lax.broadcasted_iota(I32, (tm, 2), 1)
    eid_ref[...] = jnp.where(col == 0, i1, i2) - N_GROUPS
    wt_ref[...] = jnp.where(col == 0, w1, w2)
    rank_ref[...] = jnp.where(col == 0, rank1, rank2).astype(I32)


def _post_mixer_kernel(*refs, n_lhs):
    x_ref = refs[0]
    a_refs = refs[1:1 + n_lhs]
    w_refs = refs[1 + n_lhs:1 + 2 * n_lhs]
    (cg_ref, wq_ref, kv_ref, wo_ref, fg_ref, whi_ref, wlo_ref, rb_ref,
     x_out_ref, hp_ref, eid_ref, wt_ref, rank_ref, cnt_ref, base_ref) = refs[1 + 2 * n_lhs:]

    @pl.when((pl.program_id(0) == 0) & (pl.program_id(1) == 0))
    def _():
        base_ref[...] = jnp.zeros_like(base_ref)

    x = x_ref[0]
    for a_ref, w_ref in zip(a_refs, w_refs):
        x = x + _dot(a_ref[0].astype(BF16), w_ref[...])
    x = _cross_attend(x, cg_ref[...], wq_ref[...], kv_ref[0], wo_ref[...])
    x_out_ref[0] = x
    _route(x, fg_ref[...], whi_ref[...], wlo_ref[...], rb_ref[...], hp_ref, eid_ref, wt_ref, rank_ref, cnt_ref,
           base_ref)


def post_mixer(x3d, lhs_list, w_list, c_gamma, wq, kv, wo, f_gamma, w_grp, b_grp, w_exp, b_exp):
    b, seq, d = x3d.shape
    n = b * seq
    m = kv.shape[1]
    tm = 512
    nt = seq // tm
    wcat = jnp.pad(jnp.concatenate([w_grp, w_exp], axis=1), ((0, 0), (0, LANES - N_GROUPS - N_EXPERTS)))
    bcat = jnp.pad(jnp.concatenate([b_grp, b_exp]), (0, LANES - N_GROUPS - N_EXPERTS)).reshape(1, LANES)
    whi = wcat.astype(BF16)
    wlo = (wcat - whi.astype(F32)).astype(BF16)
    const = lambda shape: pl.BlockSpec(shape, lambda i, j: (0,) * len(shape))
    tok = lambda width: pl.BlockSpec((tm, width), lambda i, j: (i * nt + j, 0))
    in_specs = [pl.BlockSpec((1, tm, d), lambda i, j: (i, j, 0))]
    in_specs += [pl.BlockSpec((1, tm, a.shape[2]), lambda i, j: (i, j, 0)) for a in lhs_list]
    in_specs += [const(w.shape) for w in w_list]
    in_specs += [const((1, d)), const((d, X_W)), pl.BlockSpec((1, m, 2 * X_W), lambda i, j: (i, 0, 0)),
                 const((X_W, d)), const((1, d)), const((d, LANES)), const((d, LANES)), const((1, LANES))]
    return pl.pallas_call(
        functools.partial(_post_mixer_kernel, n_lhs=len(lhs_list)),
        grid=(b, nt),
        in_specs=in_specs,
        out_specs=[pl.BlockSpec((1, tm, d), lambda i, j: (i, j, 0)), tok(d // 2), tok(2), tok(2), tok(2),
                   const((1, LANES))],
        out_shape=[jax.ShapeDtypeStruct((b, seq, d), F32),
                   jax.ShapeDtypeStruct((n, d // 2), U32),
                   jax.ShapeDtypeStruct((n, 2), I32),
                   jax.ShapeDtypeStruct((n, 2), F32),
                   jax.ShapeDtypeStruct((n, 2), I32),
                   jax.ShapeDtypeStruct((1, LANES), F32)],
        scratch_shapes=[pltpu.VMEM((1, LANES), F32)],
        compiler_params=_cparams(("arbitrary", "arbitrary")),
        name="post_mixer",
    )(x3d, *lhs_list, *w_list, c_gamma.reshape(1, d), wq, kv, wo, f_gamma.reshape(1, d), whi, wlo, bcat)


def _row_copy(src_ref, src_row, dst_ref, dst_row, sem):
    return pltpu.make_async_copy(src_ref.at[pl.ds(src_row, 1)], dst_ref.at[pl.ds(dst_row, 1)], sem)


def _dispatch_kernel(dest_ref, hp_ref, xs_in_ref, xs_ref, sem, *, tm):
    del xs_in_ref

    def issue(r, carry):
        _row_copy(hp_ref, r, xs_ref, dest_ref[2 * r], sem).start(priority=0)
        _row_copy(hp_ref, r, xs_ref, dest_ref[2 * r + 1], sem).start(priority=1)
        return carry

    lax.fori_loop(0, tm, issue, 0, unroll=ROW_DMA_UNROLL)
    for _ in range(2):
        pltpu.make_async_copy(hp_ref, xs_ref.at[pl.ds(0, tm)], sem).wait()


def moe_dispatch(hp, dest_flat, cap):
    n, c = hp.shape
    tm = 512
    zeros = jnp.zeros((cap, c), U32)
    return pl.pallas_call(
        functools.partial(_dispatch_kernel, tm=tm),
        grid=(n // tm,),
        in_specs=[pl.BlockSpec((2 * tm,), lambda i: (i,), memory_space=pltpu.SMEM),
                  pl.BlockSpec((tm, c), lambda i: (i, 0)),
                  pl.BlockSpec(memory_space=pl.ANY)],
        out_specs=pl.BlockSpec(memory_space=pl.ANY),
        out_shape=jax.ShapeDtypeStruct((cap, c), U32),
        scratch_shapes=[pltpu.SemaphoreType.DMA(())],
        input_output_aliases={2: 0},
        compiler_params=_cparams(("arbitrary",)),
        name="moe_dispatch",
    )(dest_flat, hp, zeros)


def _expert_kernel(be_ref, nused_ref, xs_ref, wg_ref, wu_ref, wd_ref, y_ref, wg_s, wu_s, wd_s):
    b = pl.program_id(0)
    prev = be_ref[jnp.maximum(b - 1, 0)]
    active = b < nused_ref[0]

    @pl.when(active & ((b == 0) | (be_ref[b] != prev)))
    def _():
        wg_s[...] = wg_ref[0].astype(BF16)
        wu_s[...] = wu_ref[0].astype(BF16)
        wd_s[...] = wd_ref[0].astype(BF16)

    @pl.when(active)
    def _():
        x = _unpack_halves(xs_ref[...]).astype(BF16)
        a = _dot(x, wg_s[...])
        u = _dot(x, wu_s[...])
        hmid = (a / (1.0 + jnp.exp(-a)) * u).astype(BF16)
        y_ref[...] = _pack_halves(_dot(hmid, wd_s[...]))

    @pl.when(jnp.logical_not(active))
    def _():
        y_ref[...] = jnp.zeros_like(y_ref)


def moe_experts(xs, blk_e, nused, w_gate, w_up, w_down):
    cap, c = xs.shape
    d = 2 * c
    hid = w_gate.shape[2]
    nblk = cap // MOE_BLOCK
    grid_spec = pltpu.PrefetchScalarGridSpec(
        num_scalar_prefetch=2,
        grid=(nblk,),
        in_specs=[pl.BlockSpec((MOE_BLOCK, c), lambda b, be, nu: (b, 0)),
                  pl.BlockSpec((1, d, hid), lambda b, be, nu: (be[b], 0, 0)),
                  pl.BlockSpec((1, d, hid), lambda b, be, nu: (be[b], 0, 0)),
                  pl.BlockSpec((1, hid, d), lambda b, be, nu: (be[b], 0, 0))],
        out_specs=pl.BlockSpec((MOE_BLOCK, c), lambda b, be, nu: (b, 0)),
        scratch_shapes=[pltpu.VMEM((d, hid), BF16), pltpu.VMEM((d, hid), BF16), pltpu.VMEM((hid, d), BF16)],
    )
    return pl.pallas_call(
        _expert_kernel,
        grid_spec=grid_spec,
        out_shape=jax.ShapeDtypeStruct((cap, c), U32),
        compiler_params=_cparams(("arbitrary",)),
        name="moe_experts",
    )(blk_e, nused, xs, w_gate, w_up, w_down)


def _combine_kernel(dest_ref, x_ref, wt_ref, g_ref, yb_ref, o_ref, buf, sem, *, tm, final_norm):
    def issue(r, carry):
        _row_copy(yb_ref, dest_ref[2 * r], buf.at[0], r, sem).start(priority=0)
        _row_copy(yb_ref, dest_ref[2 * r + 1], buf.at[1], r, sem).start(priority=1)
        return carry

    lax.fori_loop(0, tm, issue, 0, unroll=ROW_DMA_UNROLL)
    for k in range(2):
        pltpu.make_async_copy(yb_ref.at[pl.ds(0, tm)], buf.at[k], sem).wait()
    wt = wt_ref[...]
    y = x_ref[...] + (_unpack_halves(buf[0]) * wt[:, 0:1] + _unpack_halves(buf[1]) * wt[:, 1:2])
    o_ref[...] = _rms(y, g_ref[...]) if final_norm else y


def moe_combine(x2d, yb, dest_flat, wt, g_final, final_norm):
    n, d = x2d.shape
    c = yb.shape[1]
    tm = 512
    return pl.pallas_call(
        functools.partial(_combine_kernel, tm=tm, final_norm=final_norm),
        grid=(n // tm,),
        in_specs=[pl.BlockSpec((2 * tm,), lambda i: (i,), memory_space=pltpu.SMEM),
                  pl.BlockSpec((tm, d), lambda i: (i, 0)),
                  pl.BlockSpec((tm, 2), lambda i: (i, 0)),
                  pl.BlockSpec((1, d), lambda i: (0, 0)),
                  pl.BlockSpec(memory_space=pl.ANY)],
        out_specs=pl.BlockSpec((tm, d), lambda i: (i, 0)),
        out_shape=jax.ShapeDtypeStruct((n, d), F32),
        scratch_shapes=[pltpu.VMEM((2, tm, c), U32), pltpu.SemaphoreType.DMA(())],
        compiler_params=_cparams(("arbitrary",)),
        name="moe_combine",
    )(dest_flat, x2d, wt, g_final.reshape(1, d), yb)


def hier_moe_block(x2d, routing, w_gate, w_up, w_down, layer, g_final, final_norm):
    n = x2d.shape[0]
    cap = 2 * n + N_EXPERTS * MOE_BLOCK
    hp, eid, wt, rank, cnt = routing
    counts = cnt[0, N_GROUPS:N_GROUPS + N_EXPERTS].astype(I32)
    padded = (counts + MOE_BLOCK - 1) // MOE_BLOCK * MOE_BLOCK
    p_ends = jnp.cumsum(padded)
    p_starts = p_ends - padded
    experts = jnp.arange(N_EXPERTS, dtype=I32)
    dest = (jnp.sum(jnp.where(eid[..., None] == experts, p_starts, 0), axis=-1) + rank).reshape(-1)
    nblk = cap // MOE_BLOCK
    blk_start = jnp.arange(nblk, dtype=I32) * MOE_BLOCK
    blk_e = jnp.minimum(jnp.sum((p_ends[None, :] <= blk_start[:, None]).astype(I32), axis=1), N_EXPERTS - 1)
    nused = (p_ends[-1:] // MOE_BLOCK).astype(I32)
    xs = moe_dispatch(hp, dest, cap)
    yb = moe_experts(xs, blk_e + layer * N_EXPERTS, nused, w_gate, w_up, w_down)
    return moe_combine(x2d, yb, dest, wt, g_final, final_norm)


def kernel(x, mem, e_norm, e_w_in, e_conv_w, e_conv_b, e_filt_w1, e_filt_b1, e_filt_w2, e_filt_b2, e_filt_w3, e_filt_freq, e_hy_bias, e_lam, e_subln, e_w_out, o_norm, o_w_in, o_sink, o_w_out, c_norm, c_wq, c_wkv, c_wo, f_norm, f_w_grp, f_b_grp, f_w_exp, f_b_exp, f_w_gate, f_w_up, f_w_down, g_mem, g_final):
    b, seq, d = x.shape
    n = b * seq
    m = mem.shape[1]
    tables = dft_tables(seq // 2)
    twiddles = dft_twiddles(seq)
    x2 = x.reshape(n, d)
    mem2 = mem.reshape(b * m, d)
    w_gate = f_w_gate.reshape(DEPTH * N_EXPERTS, d, EXP_HIDDEN)
    w_up = f_w_up.reshape(DEPTH * N_EXPERTS, d, EXP_HIDDEN)
    w_down = f_w_down.reshape(DEPTH * N_EXPERTS, EXP_HIDDEN, d)
    qperm = window_column_perm()
    for i in range(DEPTH):
        j = i // 2
        if i % 2 == 0:
            hyw, qw = 3 * HY_WIDTH, 2 * DIFF_HEADS * DIFF_HEAD_DIM
            w_in = jnp.concatenate([e_w_in[j][:, :hyw],
                                    e_w_in[j][:, hyw:hyw + qw] * (DIFF_HEAD_DIM ** -0.5 * LOG2E),
                                    e_w_in[j][:, hyw + qw:]], axis=1).astype(BF16)
            proj_h, proj_a = norm_matmul(x2, e_norm[j], w_in, 512, [(hyw, F32), (w_in.shape[1] - hyw, BF16)])
            y_hy = hyena_mixer(proj_h.reshape(b, seq, -1), tables, twiddles, e_conv_w[j], e_conv_b[j], e_filt_w1[j],
                               e_filt_b1[j], e_filt_w2[j], e_filt_b2[j], e_filt_w3[j], e_filt_freq[j], e_hy_bias[j])
            y_df = diff_attention(proj_a.reshape(b, seq, -1), e_lam[j], e_subln[j], i)
            w_out = e_w_out[j].astype(BF16)
            mixed, w_mix = [y_hy, y_df], [w_out[:HY_WIDTH], w_out[HY_WIDTH:]]
        else:
            oq = GQA_HEADS * GQA_HD
            w_in = jnp.concatenate([o_w_in[j][:, :oq][:, qperm] * (GQA_HD ** -0.5 * LOG2E), o_w_in[j][:, oq:]],
                                   axis=1).astype(BF16)
            (proj,) = norm_matmul(x2, o_norm[j], w_in, 512, [(w_in.shape[1], BF16)])
            att = window_gqa(proj.reshape(b, seq, -1), o_sink[j])
            mixed, w_mix = [att], [o_w_out[j][qperm].astype(BF16)]
        (kv,) = norm_matmul(mem2, g_mem, c_wkv[i].astype(BF16), 512, [(2 * X_W, BF16)])
        x3, *routing = post_mixer(x2.reshape(b, seq, d), mixed, w_mix, c_norm[i], c_wq[i].astype(BF16),
                                  kv.reshape(b, m, -1), c_wo[i].astype(BF16), f_norm[i], f_w_grp[i], f_b_grp[i],
                                  f_w_exp[i], f_b_exp[i])
        x2 = hier_moe_block(x3.reshape(n, d), routing, w_gate, w_up, w_down, i, g_final, i == DEPTH - 1)
    return x2.reshape(b, seq, d)
```

```python
import functools
import math

import jax
import jax.numpy as jnp
from jax import lax
from jax.experimental import pallas as pl
from jax.experimental.pallas import tpu as pltpu

F32 = jnp.float32
BF16 = jnp.bfloat16
I32 = jnp.int32
U32 = jnp.uint32

D_MODEL = 1024
DEPTH = 4
EPS = 1e-6
NEG = -1e30
HY_WIDTH = 512
HY_BANDS = 16
HY_FILT_HIDDEN = 64
HY_DECAY_TARGET = 1e-2
HY_FAST_PCT = 0.3
HY_SLOW_PCT = 1.5
DIFF_HEADS = 4
DIFF_HEAD_DIM = 64
WIN = 128
GQA_HEADS = 16
GQA_KV = 4
GQA_HD = 64
X_HEADS = 4
X_HD = 128
X_W = X_HEADS * X_HD
N_GROUPS = 4
EXP_PER_GROUP = 8
N_EXPERTS = N_GROUPS * EXP_PER_GROUP
EXP_HIDDEN = 512

LOG2E = 1.4426950408889634
LANES = 128
VMEM_LIMIT = 56 * 1024 * 1024
MOE_BLOCK = 256
DIFF_Q_BLOCK = 256
DIFF_KEY_CHUNK = 512
ROW_DMA_UNROLL = 8


def _cparams(sem):
    return pltpu.CompilerParams(dimension_semantics=sem, vmem_limit_bytes=VMEM_LIMIT)


def _rms(x, g):
    ms = jnp.mean(x * x, axis=-1, keepdims=True)
    return x * lax.rsqrt(ms + EPS) * g


def _dot(a, b):
    return jnp.dot(a, b, preferred_element_type=F32)


def _dot_nt(a, b):
    return lax.dot_general(a, b, (((1,), (1,)), ((), ())), preferred_element_type=F32)


def _norm_matmul_kernel(x_ref, g_ref, w_ref, *o_refs):
    h = _rms(x_ref[...], g_ref[...]).astype(BF16)
    col = 0
    for o_ref in o_refs:
        width = o_ref.shape[1]
        o_ref[...] = _dot(h, w_ref[:, col:col + width]).astype(o_ref.dtype)
        col += width


def norm_matmul(x2d, gamma, w_bf16, tm, outs):
    n, d = x2d.shape
    f = w_bf16.shape[1]
    assert sum(width for width, _ in outs) == f
    res = pl.pallas_call(
        _norm_matmul_kernel,
        grid=(n // tm,),
        in_specs=[pl.BlockSpec((tm, d), lambda i: (i, 0)),
                  pl.BlockSpec((1, d), lambda i: (0, 0)),
                  pl.BlockSpec((d, f), lambda i: (0, 0))],
        out_specs=[pl.BlockSpec((tm, width), lambda i: (i, 0)) for width, _ in outs],
        out_shape=[jax.ShapeDtypeStruct((n, width), dtype) for width, dtype in outs],
        compiler_params=_cparams(("parallel",)),
        name="norm_matmul",
    )(x2d, gamma.reshape(1, d), w_bf16)
    return res


def _conv3_kernel(u_ref, w_ref, b_ref, oe_ref, oo_ref):
    half = u_ref.shape[1] // 2
    ue = u_ref[0, pl.ds(0, half, stride=2), :]
    uo = u_ref[0, pl.ds(1, half, stride=2), :]
    row = lax.broadcasted_iota(I32, ue.shape, 0)
    uo_prev = jnp.where(row == 0, 0.0, pltpu.roll(uo, 1, 0))
    ue_next = jnp.where(row == half - 1, 0.0, pltpu.roll(ue, half - 1, 0))
    w = w_ref[...]
    bias = b_ref[...]
    oe_ref[0] = uo_prev * w[0:1] + ue * w[1:2] + uo * w[2:3] + bias
    oo_ref[0] = ue * w[0:1] + uo * w[1:2] + ue_next * w[2:3] + bias


def conv3(proj3d, conv_w, conv_b):
    b, seq, c = proj3d.shape
    tc = LANES
    out = pl.BlockSpec((1, seq // 2, tc), lambda i, j: (i, 0, j))
    return pl.pallas_call(
        _conv3_kernel,
        grid=(b, c // tc),
        in_specs=[pl.BlockSpec((1, seq, tc), lambda i, j: (i, 0, j)),
                  pl.BlockSpec((3, tc), lambda i, j: (0, j)),
                  pl.BlockSpec((1, tc), lambda i, j: (0, j))],
        out_specs=[out, out],
        out_shape=[jax.ShapeDtypeStruct((b, seq // 2, c), F32)] * 2,
        compiler_params=_cparams(("parallel", "parallel")),
        name="hyena_conv3",
    )(proj3d, conv_w, conv_b.reshape(1, c))


def _hy_filter_kernel(t_ref, bands_ref, w1t_ref, w1c_ref, w1s_ref, b1_ref, w2_ref, b2_ref, w3_ref,
                      freq_ref, delta_ref, hf_ref, hb_ref, *, seq, tl):
    hi = lax.Precision.HIGHEST
    i = pl.program_id(0)
    t = t_ref[...]
    pos = (i * tl + lax.broadcasted_iota(I32, (tl, 1), 0)).astype(F32)
    ang = bands_ref[...] * (2.0 * math.pi * pos / seq)
    f = freq_ref[...]
    pre = (t * w1t_ref[...]
           + jnp.dot(jnp.cos(ang), w1c_ref[...], precision=hi, preferred_element_type=F32)
           - jnp.dot(jnp.sin(ang), w1s_ref[...], precision=hi, preferred_element_type=F32)
           + b1_ref[...])
    a = jnp.sin(f * pre)
    a = jnp.sin(f * (jnp.dot(a, w2_ref[...], precision=hi, preferred_element_type=F32) + b2_ref[...]))
    h = jnp.dot(a, w3_ref[...], precision=hi, preferred_element_type=F32)
    decay = jnp.exp(-t * jnp.abs(delta_ref[...]))
    w = HY_WIDTH
    for o in range(2):
        hf_ref[:, o * w:(o + 1) * w] = h[:, o * 2 * w:o * 2 * w + w] * decay
        hb_ref[:, o * w:(o + 1) * w] = jnp.where(pos == 0.0, 0.0, h[:, o * 2 * w + w:(o + 1) * 2 * w] * decay)


def hyena_filters_time(seq, w1, b1, w2, b2, w3, freq):
    tl = 512
    hid = LANES
    pad_h = hid - HY_FILT_HIDDEN
    t = jnp.linspace(0.0, 1.0, seq, dtype=F32)[:, None]
    bands = jnp.pad(jnp.linspace(1e-4, HY_BANDS - 1, HY_BANDS, dtype=F32)[None], ((0, 0), (0, LANES - HY_BANDS)))
    w1p = jnp.pad(w1, ((0, 0), (0, pad_h)))
    w1t = w1p[0:1]
    w1c = jnp.pad(w1p[1:1 + HY_BANDS], ((0, LANES - HY_BANDS), (0, 0)))
    w1s = jnp.pad(w1p[1 + HY_BANDS:], ((0, LANES - HY_BANDS), (0, 0)))
    b1p = jnp.pad(b1, (0, pad_h)).reshape(1, hid)
    w2p = jnp.pad(w2, ((0, pad_h), (0, pad_h)))
    b2p = jnp.pad(b2, (0, pad_h)).reshape(1, hid)
    w3p = jnp.pad(w3, ((0, pad_h), (0, 0)))
    freqp = jnp.pad(freq, (0, pad_h)).reshape(1, hid)
    max_decay = math.log(HY_DECAY_TARGET) / HY_FAST_PCT
    min_decay = math.log(HY_DECAY_TARGET) / HY_SLOW_PCT
    deltas = jnp.linspace(min_decay, max_decay, HY_WIDTH, dtype=F32)[None]
    fw = w3.shape[1]
    full = lambda shape: pl.BlockSpec(shape, lambda i: (0, 0))
    return pl.pallas_call(
        functools.partial(_hy_filter_kernel, seq=seq, tl=tl),
        grid=(seq // tl,),
        in_specs=[pl.BlockSpec((tl, 1), lambda i: (i, 0)), full((1, LANES)), full((1, hid)),
                  full((LANES, hid)), full((LANES, hid)), full((1, hid)), full((hid, hid)), full((1, hid)),
                  full((hid, fw)), full((1, hid)), full((1, HY_WIDTH))],
        out_specs=[pl.BlockSpec((tl, 2 * HY_WIDTH), lambda i: (i, 0)),
                   pl.BlockSpec((tl, 2 * HY_WIDTH), lambda i: (i, 0))],
        out_shape=[jax.ShapeDtypeStruct((seq, 2 * HY_WIDTH), F32)] * 2,
        compiler_params=_cparams(("parallel",)),
        name="hyena_filter_mlp",
    )(t, bands, w1t, w1c, w1s, b1p, w2p, b2p, w3p, freqp, deltas)


def dft_tables(seq):
    n2 = 4 * seq
    sub = 64
    f = jnp.arange(seq, dtype=I32)[:, None]
    odd = 2 * f + 1
    s1 = jnp.arange(seq // sub, dtype=I32)[None]
    s0 = jnp.arange(sub, dtype=I32)[None]
    ang_p = ((odd * s1) % (n2 // sub)).astype(F32) * (2.0 * math.pi * sub / n2)
    ang_q = ((odd * s0) % n2).astype(F32) * (2.0 * math.pi / n2)
    pc, ps, qc, qs = jnp.cos(ang_p), jnp.sin(ang_p), jnp.cos(ang_q), jnp.sin(ang_q)
    c = (pc[:, :, None] * qc[:, None, :] - ps[:, :, None] * qs[:, None, :]).reshape(seq, seq)
    s = (ps[:, :, None] * qc[:, None, :] + pc[:, :, None] * qs[:, None, :]).reshape(seq, seq)
    return c.astype(BF16), s.astype(BF16), c.T.astype(BF16), s.T.astype(BF16)


def dft_twiddles(seq):
    g = jnp.arange(seq // 2, dtype=I32)[:, None]
    ang = (2 * g + 1).astype(F32) * (2.0 * math.pi / (4 * seq))
    return jnp.cos(ang), jnp.sin(ang)


def _half_transform(c1, s1, cg, sg, x0, x1):
    pc0, ps0 = _dot(c1, x0), _dot(s1, x0)
    pc1, ps1 = _dot(c1, x1), _dot(s1, x1)
    tr = cg * pc1 - sg * ps1
    ti = -(cg * ps1 + sg * pc1)
    return pc0 + tr, ti - ps0, pc0 - tr, ps0 + ti


def _spectrum_kernel(c_ref, s_ref, cg_ref, sg_ref, f0_ref, f1_ref, b0_ref, b1_ref,
                     kgre_ref, kgim_ref, khre_ref, khim_ref):
    c1, s1, cg, sg = c_ref[...], s_ref[...], cg_ref[...], sg_ref[...]
    fre_g, fim_g, fre_h, fim_h = _half_transform(c1, s1, cg, sg, f0_ref[...].astype(BF16), f1_ref[...].astype(BF16))
    bre_g, bim_g, bre_h, bim_h = _half_transform(c1, s1, cg, sg, b0_ref[...].astype(BF16), b1_ref[...].astype(BF16))
    kgre_ref[...] = fre_g + bre_g
    kgim_ref[...] = fim_g - bim_g
    khre_ref[...] = fre_h + bre_h
    khim_ref[...] = fim_h - bim_h


def filter_spectrum(tables, twiddles, hf, hb):
    seq, cols = hf.shape
    half = seq // 2
    tf, tn = 512, 512
    nj = cols // tn
    hf2 = hf.reshape(half, 2 * cols)
    hb2 = hb.reshape(half, 2 * cols)
    tab = pl.BlockSpec((tf, half), lambda i, j: (i, 0))
    twd = pl.BlockSpec((tf, 1), lambda i, j: (i, 0))
    even = pl.BlockSpec((half, tn), lambda i, j: (0, j))
    odd = pl.BlockSpec((half, tn), lambda i, j: (0, nj + j))
    return pl.pallas_call(
        _spectrum_kernel,
        grid=(half // tf, nj),
        in_specs=[tab, tab, twd, twd, even, odd, even, odd],
        out_specs=[pl.BlockSpec((tf, tn), lambda i, j: (i, j))] * 4,
        out_shape=[jax.ShapeDtypeStruct((half, cols), F32)] * 4,
        compiler_params=_cparams(("parallel", "parallel")),
        name="hyena_filter_spectrum",
    )(tables[0], tables[1], twiddles[0], twiddles[1], hf2, hf2, hb2, hb2)


def _hy_fwd_kernel(z0_ref, z1_ref, c_ref, s_ref, cg_ref, sg_ref, kgre_ref, kgim_ref, khre_ref, khim_ref,
                   u0re_ref, u0im_ref, u1re_ref, u1im_ref):
    cg, sg = cg_ref[...], sg_ref[...]
    zre_g, zim_g, zre_h, zim_h = _half_transform(c_ref[...], s_ref[...], cg, sg,
                                                 z0_ref[0].astype(BF16), z1_ref[0].astype(BF16))
    kgre, kgim, khre, khim = kgre_ref[...], kgim_ref[...], khre_ref[...], khim_ref[...]
    yre_g = zre_g * kgre - zim_g * kgim
    yim_g = zre_g * kgim + zim_g * kgre
    yre_h = zre_h * khre - zim_h * khim
    yim_h = zre_h * khim + zim_h * khre
    u0re_ref[0] = (yre_g + yre_h).astype(BF16)
    u0im_ref[0] = (yim_g - yim_h).astype(BF16)
    a = yre_g - yre_h
    b = yim_g + yim_h
    u1re_ref[0] = (cg * a - sg * b).astype(BF16)
    u1im_ref[0] = (cg * b + sg * a).astype(BF16)


def hyena_fwd(zsrc, tables, twiddles, kspec, order):
    (ze, ce), (zo, co) = zsrc
    b, half, _ = ze.shape
    w = HY_WIDTH
    tf = 512
    tab = pl.BlockSpec((tf, half), lambda i, j: (j, 0))
    twd = pl.BlockSpec((tf, 1), lambda i, j: (j, 0))
    kblk = pl.BlockSpec((tf, w), lambda i, j: (j, order))
    return pl.pallas_call(
        _hy_fwd_kernel,
        grid=(b, half // tf),
        in_specs=[pl.BlockSpec((1, half, w), lambda i, j: (i, 0, ce)),
                  pl.BlockSpec((1, half, w), lambda i, j: (i, 0, co)),
                  tab, tab, twd, twd, kblk, kblk, kblk, kblk],
        out_specs=[pl.BlockSpec((1, tf, w), lambda i, j: (i, j, 0))] * 4,
        out_shape=[jax.ShapeDtypeStruct((b, half, w), BF16)] * 4,
        compiler_params=_cparams(("parallel", "parallel")),
        name="hyena_dft_fwd",
    )(ze, zo, tables[0], tables[1], twiddles[0], twiddles[1], *kspec)


def _hy_inv_kernel(ct_ref, st_ref, u0re_ref, u0im_ref, u1re_ref, u1im_ref, z0_ref, z1_ref, g0_ref, g1_ref,
                   bias_ref, o_ref, *, scale):
    ct, st = ct_ref[...], st_ref[...]
    w = z0_ref.shape[2]
    y0 = (_dot(ct, u0re_ref[0]) - _dot(st, u0im_ref[0])) * scale
    y1 = (_dot(ct, u1re_ref[0]) - _dot(st, u1im_ref[0])) * scale
    bias = bias_ref[...]
    o_ref[0, :, 0:w] = g0_ref[0] * (y0 + z0_ref[0] * bias)
    o_ref[0, :, w:2 * w] = g1_ref[0] * (y1 + z1_ref[0] * bias)


def hyena_inv(tables, u, zsrc, gsrc, bias_row):
    b, half, w = u[0].shape
    tt = 512
    tab = pl.BlockSpec((tt, half), lambda i, j: (j, 0))
    ublk = pl.BlockSpec((1, half, w), lambda i, j: (i, 0, 0))

    def rows(col):
        return pl.BlockSpec((1, tt, w), lambda i, j: (i, j, col))

    (ze, ce), (zo, co) = zsrc
    (ge, gce), (go, gco) = gsrc
    return pl.pallas_call(
        functools.partial(_hy_inv_kernel, scale=0.5 / half),
        grid=(b, half // tt),
        in_specs=[tab, tab, ublk, ublk, ublk, ublk, rows(ce), rows(co), rows(gce), rows(gco),
                  pl.BlockSpec((1, w), lambda i, j: (0, 0))],
        out_specs=pl.BlockSpec((1, tt, 2 * w), lambda i, j: (i, j, 0)),
        out_shape=jax.ShapeDtypeStruct((b, half, 2 * w), F32),
        compiler_params=_cparams(("parallel", "parallel")),
        name="hyena_dft_inv",
    )(tables[2], tables[3], *u, ze, zo, ge, go, bias_row)


def hyena_mixer(proj3d, tables, twiddles, conv_w, conv_b, w1, b1, w2, b2, w3, freq, hy_bias):
    seq = proj3d.shape[1]
    ue, uo = conv3(proj3d, conv_w, conv_b)
    hf, hb = hyena_filters_time(seq, w1, b1, w2, b2, w3, freq)
    kspec = filter_spectrum(tables, twiddles, hf, hb)
    v = ((ue, 0), (uo, 0))
    uu = hyena_fwd(v, tables, twiddles, kspec, 0)
    z1 = hyena_inv(tables, uu, v, ((ue, 1), (uo, 1)), hy_bias[0:1])
    z = ((z1, 0), (z1, 1))
    uu = hyena_fwd(z, tables, twiddles, kspec, 1)
    return hyena_inv(tables, uu, z, ((ue, 2), (uo, 2)), hy_bias[1:2])


def _diff_attn_kernel(slope_ref, q_ref, k_ref, v_ref, lam_ref, sub_ref, o_ref, bias_ref, *, tq, lam_init):
    h = pl.program_id(0)
    qi = pl.program_id(1)
    seq = k_ref.shape[1]

    @pl.when(pl.program_id(2) == 0)
    def _():
        qpos = qi * tq + lax.broadcasted_iota(I32, (tq, seq), 0)
        kpos = lax.broadcasted_iota(I32, (tq, seq), 1)
        bias_ref[...] = slope_ref[h] * jnp.abs(qpos - kpos).astype(F32)

    q = q_ref[0]
    k = k_ref[0]
    v = v_ref[0]
    lane = lax.broadcasted_iota(I32, (1, 2 * DIFF_HEAD_DIM), 1)
    ck = DIFF_KEY_CHUNK

    def attend(m):
        keep = (lane < DIFF_HEAD_DIM) if m == 0 else (lane >= DIFF_HEAD_DIM)
        qm = jnp.where(keep, q, jnp.zeros_like(q))
        s = [_dot_nt(qm, k[c:c + ck]) - bias_ref[:, c:c + ck] for c in range(0, seq, ck)]
        mx = functools.reduce(jnp.maximum, [jnp.max(sc, axis=-1, keepdims=True) for sc in s])
        acc = jnp.zeros((tq, 2 * DIFF_HEAD_DIM), F32)
        den = jnp.zeros((tq, 1), F32)
        for i, sc in enumerate(s):
            e = jnp.exp2(sc - mx)
            den = den + jnp.sum(e, axis=-1, keepdims=True)
            acc = acc + _dot(e.astype(BF16), v[i * ck:(i + 1) * ck])
        return acc / den

    l = lam_ref[...]
    lam_full = (jnp.exp(jnp.sum(l[0:1] * l[1:2], axis=-1, keepdims=True))
                - jnp.exp(jnp.sum(l[2:3] * l[3:4], axis=-1, keepdims=True)) + lam_init)
    o = attend(0) - lam_full * attend(1)
    o_ref[0] = _rms(o, sub_ref[...]) * (1.0 - lam_init)


def diff_attention(proj3d, lam, subln, layer_idx):
    b, seq, _ = proj3d.shape
    tq = DIFF_Q_BLOCK
    hw = 2 * DIFF_HEAD_DIM
    qb, kb, vb = 0, DIFF_HEADS, 2 * DIFF_HEADS
    lam_init = 0.8 - 0.6 * math.exp(-0.3 * layer_idx)
    slopes = 2.0 ** (-8.0 * jnp.arange(1, DIFF_HEADS + 1, dtype=F32) / DIFF_HEADS) * LOG2E
    return pl.pallas_call(
        functools.partial(_diff_attn_kernel, tq=tq, lam_init=lam_init),
        grid=(DIFF_HEADS, seq // tq, b),
        in_specs=[pl.BlockSpec(memory_space=pltpu.SMEM),
                  pl.BlockSpec((1, tq, hw), lambda h, j, i: (i, j, qb + h)),
                  pl.BlockSpec((1, seq, hw), lambda h, j, i: (i, 0, kb + h)),
                  pl.BlockSpec((1, seq, hw), lambda h, j, i: (i, 0, vb + h)),
                  pl.BlockSpec((4, DIFF_HEAD_DIM), lambda h, j, i: (0, 0)),
                  pl.BlockSpec((1, hw), lambda h, j, i: (0, 0))],
        out_specs=pl.BlockSpec((1, tq, hw), lambda h, j, i: (i, j, h)),
        out_shape=jax.ShapeDtypeStruct((b, seq, DIFF_HEADS * hw), F32),
        scratch_shapes=[pltpu.VMEM((tq, seq), F32)],
        compiler_params=_cparams(("parallel", "parallel", "arbitrary")),
        name="diff_attention",
    )(slopes, proj3d, proj3d, proj3d, lam, subln.reshape(1, hw))


def _win_attn_kernel(slope_ref, sink_ref, q_ref, kp_ref, kc_ref, kn_ref, vp_ref, vc_ref, vn_ref, o_ref, bias_ref,
                     *, tq, seq):
    qi = pl.program_id(0)
    span = 3 * tq
    group = GQA_HEADS // GQA_KV

    @pl.when(pl.program_id(1) == 0)
    def _():
        qpos = qi * tq + lax.broadcasted_iota(I32, (tq, span), 0)
        kpos = (qi - 1) * tq + lax.broadcasted_iota(I32, (tq, span), 1)
        rel = jnp.abs(qpos - kpos)
        relf = rel.astype(F32)
        masked = jnp.where((rel <= WIN) & (kpos >= 0) & (kpos < seq), 0.0, -NEG)
        for head in range(GQA_HEADS):
            bias_ref[head] = slope_ref[head] * relf + masked

    kwin = jnp.concatenate([kp_ref[0], kc_ref[0], kn_ref[0]], axis=0)
    vwin = jnp.concatenate([vp_ref[0], vc_ref[0], vn_ref[0]], axis=0)
    low = lax.broadcasted_iota(I32, (1, LANES), 1) < GQA_HD
    for p in range(GQA_KV // 2):
        kb = kwin[:, p * LANES:(p + 1) * LANES]
        vb = vwin[:, p * LANES:(p + 1) * LANES]
        outs = []
        for half in range(2):
            kv = 2 * p + half
            mine = low if half == 0 else jnp.logical_not(low)
            qblocks = [q_ref[0, :, (group * p + r) * LANES:(group * p + r + 1) * LANES] for r in range(group)]
            qs = jnp.concatenate([jnp.where(mine, qb, jnp.zeros_like(qb)) for qb in qblocks], axis=0)
            s = _dot_nt(qs, kb)
            es, inv = [], []
            for r in range(group):
                head = kv * group + r
                sink = sink_ref[head]
                sr = s[r * tq:(r + 1) * tq] - bias_ref[head]
                m = jnp.maximum(jnp.max(sr, axis=-1, keepdims=True), sink)
                e = jnp.exp2(sr - m)
                inv.append(1.0 / (jnp.sum(e, axis=-1, keepdims=True) + jnp.exp2(sink - m)))
                es.append(e.astype(BF16))
            o = _dot(jnp.concatenate(es, axis=0), vb)
            outs.append([o[r * tq:(r + 1) * tq] * inv[r] for r in range(group)])
        for r in range(group):
            col = (group * p + r) * LANES
            o_ref[0, :, col:col + LANES] = jnp.where(low, outs[0][r], outs[1][r])


def window_head_order():
    group = GQA_HEADS // GQA_KV
    order = []
    for p in range(GQA_KV // 2):
        for r in range(group):
            order += [(2 * p) * group + r, (2 * p + 1) * group + r]
    return order


def window_column_perm():
    cols = []
    for head in window_head_order():
        cols += list(range(head * GQA_HD, (head + 1) * GQA_HD))
    return jnp.asarray(cols, dtype=I32)


def window_gqa(proj3d, sink):
    b, seq, _ = proj3d.shape
    tq = WIN
    nq = seq // tq
    oq = GQA_HEADS * GQA_HD
    okv = GQA_KV * GQA_HD
    kcol, vcol = oq // okv, oq // okv + 1
    slopes = 2.0 ** (-8.0 * jnp.arange(1, GQA_HEADS + 1, dtype=F32) / GQA_HEADS) * LOG2E

    def neighbour(col, step):
        return pl.BlockSpec((1, tq, okv), lambda j, i: (i, jnp.clip(j + step, 0, nq - 1), col))

    return pl.pallas_call(
        functools.partial(_win_attn_kernel, tq=tq, seq=seq),
        grid=(nq, b),
        in_specs=[pl.BlockSpec(memory_space=pltpu.SMEM),
                  pl.BlockSpec(memory_space=pltpu.SMEM),
                  pl.BlockSpec((1, tq, oq), lambda j, i: (i, j, 0)),
                  neighbour(kcol, -1), neighbour(kcol, 0), neighbour(kcol, 1),
                  neighbour(vcol, -1), neighbour(vcol, 0), neighbour(vcol, 1)],
        out_specs=pl.BlockSpec((1, tq, oq), lambda j, i: (i, j, 0)),
        out_shape=jax.ShapeDtypeStruct((b, seq, oq), F32),
        scratch_shapes=[pltpu.VMEM((GQA_HEADS, tq, 3 * tq), F32)],
        compiler_params=_cparams(("parallel", "arbitrary")),
        name="window_gqa",
    )(slopes, sink.astype(F32) * LOG2E, proj3d, proj3d, proj3d, proj3d, proj3d, proj3d, proj3d)


def _cross_attend(x, g, wq, kv, wo):
    h = _rms(x, g).astype(BF16)
    q = _dot(h, wq)
    scale = X_HD ** -0.5
    outs = []
    for hd in range(X_HEADS):
        qh = q[:, hd * X_HD:(hd + 1) * X_HD].astype(BF16)
        kh = kv[:, hd * X_HD:(hd + 1) * X_HD]
        vh = kv[:, X_W + hd * X_HD:X_W + (hd + 1) * X_HD]
        s = _dot_nt(qh, kh) * scale
        e = jnp.exp(s - jnp.max(s, axis=-1, keepdims=True))
        p = e / jnp.sum(e, axis=-1, keepdims=True)
        outs.append(_dot(p.astype(BF16), vh))
    o = jnp.concatenate(outs, axis=-1).astype(BF16)
    return x + _dot(o, wo)


def _split_bf16(x):
    hi = x.astype(BF16)
    lo = (x - hi.astype(F32)).astype(BF16)
    return hi, lo


def _pack_halves(h):
    c = h.shape[1] // 2
    left = lax.bitcast_convert_type(h[:, :c].astype(BF16).astype(F32), U32)
    right = lax.bitcast_convert_type(h[:, c:].astype(BF16).astype(F32), U32)
    return left | (right >> 16)


def _unpack_halves(p):
    left = lax.bitcast_convert_type(p & jnp.uint32(0xFFFF0000), F32)
    right = lax.bitcast_convert_type(p << 16, F32)
    return jnp.concatenate([left, right], axis=-1)


def _route(x, g, whi, wlo, bias, hp_ref, eid_ref, wt_ref, rank_ref, cnt_ref, base_ref):
    tm = x.shape[0]
    h = _rms(x, g)
    hp_ref[...] = _pack_halves(h)
    hhi, hlo = _split_bf16(h)
    logits = _dot(hhi, whi) + _dot(hlo, whi) + _dot(hhi, wlo) + bias
    lane = lax.broadcasted_iota(I32, logits.shape, 1)
    big = jnp.int32(LANES)
    ninf = -jnp.inf

    gl = jnp.where(lane < N_GROUPS, logits, ninf)
    gmax = jnp.max(gl, axis=-1, keepdims=True)
    gsel = jnp.min(jnp.where(gl == gmax, lane, big), axis=-1, keepdims=True)
    ggate = 1.0 / jnp.sum(jnp.exp(gl - gmax), axis=-1, keepdims=True)

    lo_lane = N_GROUPS + gsel * EXP_PER_GROUP
    el = jnp.where((lane >= lo_lane) & (lane < lo_lane + EXP_PER_GROUP), logits, ninf)
    v1 = jnp.max(el, axis=-1, keepdims=True)
    i1 = jnp.min(jnp.where(el == v1, lane, big), axis=-1, keepdims=True)
    el2 = jnp.where(lane == i1, ninf, el)
    v2 = jnp.max(el2, axis=-1, keepdims=True)
    i2 = jnp.min(jnp.where(el2 == v2, lane, big), axis=-1, keepdims=True)
    e2 = jnp.exp(v2 - v1)
    w1 = ggate / (1.0 + e2)
    w2 = ggate * e2 / (1.0 + e2)

    one1 = lane == i1
    one2 = lane == i2
    onehot = (one1 | one2).astype(F32)
    r = lax.broadcasted_iota(I32, (tm, tm), 0)
    c = lax.broadcasted_iota(I32, (tm, tm), 1)
    tri = (c < r).astype(BF16)
    before = _dot(tri, onehot.astype(BF16)) + base_ref[...]
    rank1 = jnp.sum(jnp.where(one1, before, 0.0), axis=-1, keepdims=True)
    rank2 = jnp.sum(jnp.where(one2, before, 0.0), axis=-1, keepdims=True)
    total = base_ref[...] + jnp.sum(onehot, axis=0, keepdims=True)
    base_ref[...] = total
    cnt_ref[...] = total

    col = lax.broadcasted_iota(I32, (tm, 2), 1)
    eid_ref[...] = jnp.where(col == 0, i1, i2) - N_GROUPS
    wt_ref[...] = jnp.where(col == 0, w1, w2)
    rank_ref[...] = jnp.where(col == 0, rank1, rank2).astype(I32)


def _interleave_rows(pair):
    t, w2 = pair.shape
    w = w2 // 2
    r = lax.broadcasted_iota(I32, (2 * t, t), 0)
    c = lax.broadcasted_iota(I32, (2 * t, t), 1)
    pick_even = (r == 2 * c).astype(BF16)
    pick_odd = (r == 2 * c + 1).astype(BF16)
    return (_dot(pick_even, pair[:, :w]) + _dot(pick_odd, pair[:, w:])).astype(BF16)


def _post_mixer_kernel(*refs, n_lhs, paired):
    x_ref = refs[0]
    a_refs = refs[1:1 + n_lhs]
    w_refs = refs[1 + n_lhs:1 + 2 * n_lhs]
    (cg_ref, wq_ref, kv_ref, wo_ref, fg_ref, whi_ref, wlo_ref, rb_ref,
     x_out_ref, hp_ref, eid_ref, wt_ref, rank_ref, cnt_ref, base_ref) = refs[1 + 2 * n_lhs:]

    @pl.when((pl.program_id(0) == 0) & (pl.program_id(1) == 0))
    def _():
        base_ref[...] = jnp.zeros_like(base_ref)

    x = x_ref[0]
    for a_ref, w_ref, is_paired in zip(a_refs, w_refs, paired):
        a = a_ref[0].astype(BF16)
        x = x + _dot(_interleave_rows(a) if is_paired else a, w_ref[...])
    x = _cross_attend(x, cg_ref[...], wq_ref[...], kv_ref[0], wo_ref[...])
    x_out_ref[0] = x
    _route(x, fg_ref[...], whi_ref[...], wlo_ref[...], rb_ref[...], hp_ref, eid_ref, wt_ref, rank_ref, cnt_ref,
           base_ref)


def post_mixer(x3d, lhs_list, w_list, c_gamma, wq, kv, wo, f_gamma, w_grp, b_grp, w_exp, b_exp):
    b, seq, d = x3d.shape
    paired = tuple(a.shape[1] != seq for a in lhs_list)
    n = b * seq
    m = kv.shape[1]
    tm = 512
    nt = seq // tm
    wcat = jnp.pad(jnp.concatenate([w_grp, w_exp], axis=1), ((0, 0), (0, LANES - N_GROUPS - N_EXPERTS)))
    bcat = jnp.pad(jnp.concatenate([b_grp, b_exp]), (0, LANES - N_GROUPS - N_EXPERTS)).reshape(1, LANES)
    whi = wcat.astype(BF16)
    wlo = (wcat - whi.astype(F32)).astype(BF16)
    const = lambda shape: pl.BlockSpec(shape, lambda i, j: (0,) * len(shape))
    tok = lambda width: pl.BlockSpec((tm, width), lambda i, j: (i * nt + j, 0))
    in_specs = [pl.BlockSpec((1, tm, d), lambda i, j: (i, j, 0))]
    in_specs += [pl.BlockSpec((1, tm // 2 if p else tm, a.shape[2]), lambda i, j: (i, j, 0))
                 for a, p in zip(lhs_list, paired)]
    in_specs += [const(w.shape) for w in w_list]
    in_specs += [const((1, d)), const((d, X_W)), pl.BlockSpec((1, m, 2 * X_W), lambda i, j: (i, 0, 0)),
                 const((X_W, d)), const((1, d)), const((d, LANES)), const((d, LANES)), const((1, LANES))]
    return pl.pallas_call(
        functools.partial(_post_mixer_kernel, n_lhs=len(lhs_list), paired=paired),
        grid=(b, nt),
        in_specs=in_specs,
        out_specs=[pl.BlockSpec((1, tm, d), lambda i, j: (i, j, 0)), tok(d // 2), tok(2), tok(2), tok(2),
                   const((1, LANES))],
        out_shape=[jax.ShapeDtypeStruct((b, seq, d), F32),
                   jax.ShapeDtypeStruct((n, d // 2), U32),
                   jax.ShapeDtypeStruct((n, 2), I32),
                   jax.ShapeDtypeStruct((n, 2), F32),
                   jax.ShapeDtypeStruct((n, 2), I32),
                   jax.ShapeDtypeStruct((1, LANES), F32)],
        scratch_shapes=[pltpu.VMEM((1, LANES), F32)],
        compiler_params=_cparams(("arbitrary", "arbitrary")),
        name="post_mixer",
    )(x3d, *lhs_list, *w_list, c_gamma.reshape(1, d), wq, kv, wo, f_gamma.reshape(1, d), whi, wlo, bcat)


def _row_copy(src_ref, src_row, dst_ref, dst_row, sem):
    return pltpu.make_async_copy(src_ref.at[pl.ds(src_row, 1)], dst_ref.at[pl.ds(dst_row, 1)], sem)


def _dispatch_kernel(dest_ref, hp_ref, xs_in_ref, xs_ref, sem, *, tm):
    del xs_in_ref

    def issue(r, carry):
        _row_copy(hp_ref, r, xs_ref, dest_ref[2 * r], sem).start(priority=0)
        _row_copy(hp_ref, r, xs_ref, dest_ref[2 * r + 1], sem).start(priority=1)
        return carry

    lax.fori_loop(0, tm, issue, 0, unroll=ROW_DMA_UNROLL)
    for _ in range(2):
        pltpu.make_async_copy(hp_ref, xs_ref.at[pl.ds(0, tm)], sem).wait()


def moe_dispatch(hp, dest_flat, cap):
    n, c = hp.shape
    tm = 512
    zeros = jnp.zeros((cap, c), U32)
    return pl.pallas_call(
        functools.partial(_dispatch_kernel, tm=tm),
        grid=(n // tm,),
        in_specs=[pl.BlockSpec((2 * tm,), lambda i: (i,), memory_space=pltpu.SMEM),
                  pl.BlockSpec((tm, c), lambda i: (i, 0)),
                  pl.BlockSpec(memory_space=pl.ANY)],
        out_specs=pl.BlockSpec(memory_space=pl.ANY),
        out_shape=jax.ShapeDtypeStruct((cap, c), U32),
        scratch_shapes=[pltpu.SemaphoreType.DMA(())],
        input_output_aliases={2: 0},
        compiler_params=_cparams(("arbitrary",)),
        name="moe_dispatch",
    )(dest_flat, hp, zeros)


def _expert_kernel(start_ref, cnt_ref, xs_ref, wg_ref, wu_ref, wd_ref, yb_ref, xbuf, ybuf, sem_in, sem_out,
                   wg_s, wu_s, wd_s, *, nblk):
    e = pl.program_id(0)
    nb = cnt_ref[e]
    b0 = start_ref[e]

    def fetch(blk, slot):
        return pltpu.make_async_copy(xs_ref.at[pl.ds(blk * MOE_BLOCK, MOE_BLOCK)], xbuf.at[slot], sem_in.at[slot])

    def put(blk, slot):
        return pltpu.make_async_copy(ybuf.at[slot], yb_ref.at[pl.ds(blk * MOE_BLOCK, MOE_BLOCK)], sem_out.at[slot])

    @pl.when(nb > 0)
    def _():
        fetch(b0, 0).start()
        wg_s[...] = wg_ref[0].astype(BF16)
        wu_s[...] = wu_ref[0].astype(BF16)
        wd_s[...] = wd_ref[0].astype(BF16)

    def block(i, carry):
        slot = i & 1
        fetch(b0 + i, slot).wait()

        @pl.when(i + 1 < nb)
        def _():
            fetch(b0 + i + 1, 1 - slot).start()

        @pl.when(i >= 2)
        def _():
            put(b0 + i - 2, slot).wait()

        x = _unpack_halves(xbuf[slot]).astype(BF16)
        a = _dot(x, wg_s[...])
        u = _dot(x, wu_s[...])
        hmid = (a / (1.0 + jnp.exp(-a)) * u).astype(BF16)
        ybuf[slot] = _pack_halves(_dot(hmid, wd_s[...]))
        put(b0 + i, slot).start()
        return carry

    lax.fori_loop(0, nb, block, 0)

    @pl.when(nb >= 2)
    def _():
        put(b0 + nb - 2, nb & 1).wait()

    @pl.when(nb >= 1)
    def _():
        put(b0 + nb - 1, (nb - 1) & 1).wait()

    @pl.when(e == pl.num_programs(0) - 1)
    def _():
        ybuf[0] = jnp.zeros(ybuf.shape[1:], ybuf.dtype)

        def fill(blk, carry):
            copy = put(blk, 0)
            copy.start()
            copy.wait()
            return carry

        lax.fori_loop(b0 + nb, nblk, fill, 0)


def moe_experts(xs, blk_start, blk_cnt, w_gate, w_up, w_down, layer):
    cap, c = xs.shape
    d = 2 * c
    hid = w_gate.shape[2]
    nblk = cap // MOE_BLOCK
    wmap = lambda e, st, cn: (layer * N_EXPERTS + e, 0, 0)
    grid_spec = pltpu.PrefetchScalarGridSpec(
        num_scalar_prefetch=2,
        grid=(N_EXPERTS,),
        in_specs=[pl.BlockSpec(memory_space=pl.ANY),
                  pl.BlockSpec((1, d, hid), wmap),
                  pl.BlockSpec((1, d, hid), wmap),
                  pl.BlockSpec((1, hid, d), wmap)],
        out_specs=pl.BlockSpec(memory_space=pl.ANY),
        scratch_shapes=[pltpu.VMEM((2, MOE_BLOCK, c), U32), pltpu.VMEM((2, MOE_BLOCK, c), U32),
                        pltpu.SemaphoreType.DMA((2,)), pltpu.SemaphoreType.DMA((2,)),
                        pltpu.VMEM((d, hid), BF16), pltpu.VMEM((d, hid), BF16), pltpu.VMEM((hid, d), BF16)],
    )
    return pl.pallas_call(
        functools.partial(_expert_kernel, nblk=nblk),
        grid_spec=grid_spec,
        out_shape=jax.ShapeDtypeStruct((cap, c), U32),
        compiler_params=_cparams(("arbitrary",)),
        name="moe_experts",
    )(blk_start, blk_cnt, xs, w_gate, w_up, w_down)


def _combine_kernel(dest_ref, x_ref, wt_ref, g_ref, yb_ref, o_ref, buf, sem, *, tm, final_norm):
    def issue(r, carry):
        _row_copy(yb_ref, dest_ref[2 * r], buf.at[0], r, sem).start(priority=0)
        _row_copy(yb_ref, dest_ref[2 * r + 1], buf.at[1], r, sem).start(priority=1)
        return carry

    lax.fori_loop(0, tm, issue, 0, unroll=ROW_DMA_UNROLL)
    for k in range(2):
        pltpu.make_async_copy(yb_ref.at[pl.ds(0, tm)], buf.at[k], sem).wait()
    wt = wt_ref[...]
    y = x_ref[...] + (_unpack_halves(buf[0]) * wt[:, 0:1] + _unpack_halves(buf[1]) * wt[:, 1:2])
    o_ref[...] = _rms(y, g_ref[...]) if final_norm else y


def moe_combine(x2d, yb, dest_flat, wt, g_final, final_norm):
    n, d = x2d.shape
    c = yb.shape[1]
    tm = 512
    return pl.pallas_call(
        functools.partial(_combine_kernel, tm=tm, final_norm=final_norm),
        grid=(n // tm,),
        in_specs=[pl.BlockSpec((2 * tm,), lambda i: (i,), memory_space=pltpu.SMEM),
                  pl.BlockSpec((tm, d), lambda i: (i, 0)),
                  pl.BlockSpec((tm, 2), lambda i: (i, 0)),
                  pl.BlockSpec((1, d), lambda i: (0, 0)),
                  pl.BlockSpec(memory_space=pl.ANY)],
        out_specs=pl.BlockSpec((tm, d), lambda i: (i, 0)),
        out_shape=jax.ShapeDtypeStruct((n, d), F32),
        scratch_shapes=[pltpu.VMEM((2, tm, c), U32), pltpu.SemaphoreType.DMA(())],
        compiler_params=_cparams(("arbitrary",)),
        name="moe_combine",
    )(dest_flat, x2d, wt, g_final.reshape(1, d), yb)


def hier_moe_block(x2d, routing, w_gate, w_up, w_down, layer, g_final, final_norm):
    n = x2d.shape[0]
    cap = 2 * n + N_EXPERTS * MOE_BLOCK
    hp, eid, wt, rank, cnt = routing
    counts = cnt[0, N_GROUPS:N_GROUPS + N_EXPERTS].astype(I32)
    padded = (counts + MOE_BLOCK - 1) // MOE_BLOCK * MOE_BLOCK
    p_ends = jnp.cumsum(padded)
    p_starts = p_ends - padded
    experts = jnp.arange(N_EXPERTS, dtype=I32)
    dest = (jnp.sum(jnp.where(eid[..., None] == experts, p_starts, 0), axis=-1) + rank).reshape(-1)
    xs = moe_dispatch(hp, dest, cap)
    yb = moe_experts(xs, p_starts // MOE_BLOCK, padded // MOE_BLOCK, w_gate, w_up, w_down, layer)
    return moe_combine(x2d, yb, dest, wt, g_final, final_norm)


def kernel(x, mem, e_norm, e_w_in, e_conv_w, e_conv_b, e_filt_w1, e_filt_b1, e_filt_w2, e_filt_b2, e_filt_w3, e_filt_freq, e_hy_bias, e_lam, e_subln, e_w_out, o_norm, o_w_in, o_sink, o_w_out, c_norm, c_wq, c_wkv, c_wo, f_norm, f_w_grp, f_b_grp, f_w_exp, f_b_exp, f_w_gate, f_w_up, f_w_down, g_mem, g_final):
    b, seq, d = x.shape
    n = b * seq
    m = mem.shape[1]
    tables = dft_tables(seq // 2)
    twiddles = dft_twiddles(seq)
    x2 = x.reshape(n, d)
    mem2 = mem.reshape(b * m, d)
    w_gate = f_w_gate.reshape(DEPTH * N_EXPERTS, d, EXP_HIDDEN)
    w_up = f_w_up.reshape(DEPTH * N_EXPERTS, d, EXP_HIDDEN)
    w_down = f_w_down.reshape(DEPTH * N_EXPERTS, EXP_HIDDEN, d)
    qperm = window_column_perm()
    for i in range(DEPTH):
        j = i // 2
        if i % 2 == 0:
            hyw, qw = 3 * HY_WIDTH, 2 * DIFF_HEADS * DIFF_HEAD_DIM
            w_in = jnp.concatenate([e_w_in[j][:, :hyw],
                                    e_w_in[j][:, hyw:hyw + qw] * (DIFF_HEAD_DIM ** -0.5 * LOG2E),
                                    e_w_in[j][:, hyw + qw:]], axis=1).astype(BF16)
            proj_h, proj_a = norm_matmul(x2, e_norm[j], w_in, 512, [(hyw, F32), (w_in.shape[1] - hyw, BF16)])
            y_hy = hyena_mixer(proj_h.reshape(b, seq, -1), tables, twiddles, e_conv_w[j], e_conv_b[j], e_filt_w1[j],
                               e_filt_b1[j], e_filt_w2[j], e_filt_b2[j], e_filt_w3[j], e_filt_freq[j], e_hy_bias[j])
            y_df = diff_attention(proj_a.reshape(b, seq, -1), e_lam[j], e_subln[j], i)
            w_out = e_w_out[j].astype(BF16)
            mixed, w_mix = [y_hy, y_df], [w_out[:HY_WIDTH], w_out[HY_WIDTH:]]
        else:
            oq = GQA_HEADS * GQA_HD
            w_in = jnp.concatenate([o_w_in[j][:, :oq][:, qperm] * (GQA_HD ** -0.5 * LOG2E), o_w_in[j][:, oq:]],
                                   axis=1).astype(BF16)
            (proj,) = norm_matmul(x2, o_norm[j], w_in, 512, [(w_in.shape[1], BF16)])
            att = window_gqa(proj.reshape(b, seq, -1), o_sink[j])
            mixed, w_mix = [att], [o_w_out[j][qperm].astype(BF16)]
        (kv,) = norm_matmul(mem2, g_mem, c_wkv[i].astype(BF16), 512, [(2 * X_W, BF16)])
        x3, *routing = post_mixer(x2.reshape(b, seq, d), mixed, w_mix, c_norm[i], c_wq[i].astype(BF16),
                                  kv.reshape(b, m, -1), c_wo[i].astype(BF16), f_norm[i], f_w_grp[i], f_b_grp[i],
                                  f_w_exp[i], f_b_exp[i])
        x2 = hier_moe_block(x3.reshape(n, d), routing, w_gate, w_up, w_down, i, g_final, i == DEPTH - 1)
    return x2.reshape(b, seq, d)
```

```python
import functools
import math

import jax
import jax.numpy as jnp
from jax import lax
from jax.experimental import pallas as pl
from jax.experimental.pallas import tpu as pltpu

F32 = jnp.float32
BF16 = jnp.bfloat16
I32 = jnp.int32
U32 = jnp.uint32

D_MODEL = 1024
DEPTH = 4
EPS = 1e-6
NEG = -1e30
HY_WIDTH = 512
HY_BANDS = 16
HY_FILT_HIDDEN = 64
HY_DECAY_TARGET = 1e-2
HY_FAST_PCT = 0.3
HY_SLOW_PCT = 1.5
DIFF_HEADS = 4
DIFF_HEAD_DIM = 64
WIN = 128
GQA_HEADS = 16
GQA_KV = 4
GQA_HD = 64
X_HEADS = 4
X_HD = 128
X_W = X_HEADS * X_HD
N_GROUPS = 4
EXP_PER_GROUP = 8
N_EXPERTS = N_GROUPS * EXP_PER_GROUP
EXP_HIDDEN = 512

LOG2E = 1.4426950408889634
LANES = 128
VMEM_LIMIT = 56 * 1024 * 1024
MOE_BLOCK = 256
DIFF_Q_BLOCK = 256
DIFF_KEY_CHUNK = 512
ROW_DMA_UNROLL = 8


def _cparams(sem):
    return pltpu.CompilerParams(dimension_semantics=sem, vmem_limit_bytes=VMEM_LIMIT)


def _rms(x, g):
    ms = jnp.mean(x * x, axis=-1, keepdims=True)
    return x * lax.rsqrt(ms + EPS) * g


def _dot(a, b):
    return jnp.dot(a, b, preferred_element_type=F32)


def _dot_nt(a, b):
    return lax.dot_general(a, b, (((1,), (1,)), ((), ())), preferred_element_type=F32)


def _norm_matmul_kernel(x_ref, g_ref, w_ref, *o_refs):
    h = _rms(x_ref[...], g_ref[...]).astype(BF16)
    col = 0
    for o_ref in o_refs:
        width = o_ref.shape[1]
        o_ref[...] = _dot(h, w_ref[:, col:col + width]).astype(o_ref.dtype)
        col += width


def norm_matmul(x2d, gamma, w_bf16, tm, outs):
    n, d = x2d.shape
    f = w_bf16.shape[1]
    assert sum(width for width, _ in outs) == f
    res = pl.pallas_call(
        _norm_matmul_kernel,
        grid=(n // tm,),
        in_specs=[pl.BlockSpec((tm, d), lambda i: (i, 0)),
                  pl.BlockSpec((1, d), lambda i: (0, 0)),
                  pl.BlockSpec((d, f), lambda i: (0, 0))],
        out_specs=[pl.BlockSpec((tm, width), lambda i: (i, 0)) for width, _ in outs],
        out_shape=[jax.ShapeDtypeStruct((n, width), dtype) for width, dtype in outs],
        compiler_params=_cparams(("parallel",)),
        name="norm_matmul",
    )(x2d, gamma.reshape(1, d), w_bf16)
    return res


def _conv3_kernel(u_ref, w_ref, b_ref, oe_ref, oo_ref, slab_ref):
    half = u_ref.shape[1] // 2
    w = w_ref[...]
    bias = b_ref[...]
    row = lax.broadcasted_iota(I32, (half, LANES), 0)
    for k in range(u_ref.shape[2] // LANES):
        cols = slice(k * LANES, (k + 1) * LANES)
        slab_ref[...] = u_ref[0, :, cols]
        ue = slab_ref[pl.ds(0, half, stride=2), :]
        uo = slab_ref[pl.ds(1, half, stride=2), :]
        uo_prev = jnp.where(row == 0, 0.0, pltpu.roll(uo, 1, 0))
        ue_next = jnp.where(row == half - 1, 0.0, pltpu.roll(ue, half - 1, 0))
        oe_ref[0, :, cols] = uo_prev * w[0:1, cols] + ue * w[1:2, cols] + uo * w[2:3, cols] + bias[:, cols]
        oo_ref[0, :, cols] = ue * w[0:1, cols] + uo * w[1:2, cols] + ue_next * w[2:3, cols] + bias[:, cols]


def conv3(proj3d, conv_w, conv_b):
    b, seq, c = proj3d.shape
    tc = HY_WIDTH
    out = pl.BlockSpec((1, seq // 2, tc), lambda i, j: (i, 0, j))
    return pl.pallas_call(
        _conv3_kernel,
        grid=(b, c // tc),
        in_specs=[pl.BlockSpec((1, seq, tc), lambda i, j: (i, 0, j)),
                  pl.BlockSpec((3, tc), lambda i, j: (0, j)),
                  pl.BlockSpec((1, tc), lambda i, j: (0, j))],
        out_specs=[out, out],
        out_shape=[jax.ShapeDtypeStruct((b, seq // 2, c), F32)] * 2,
        scratch_shapes=[pltpu.VMEM((seq, LANES), F32)],
        compiler_params=_cparams(("parallel", "parallel")),
        name="hyena_conv3",
    )(proj3d, conv_w, conv_b.reshape(1, c))


def _hy_filter_kernel(t_ref, bands_ref, w1t_ref, w1c_ref, w1s_ref, b1_ref, w2_ref, b2_ref, w3_ref,
                      freq_ref, delta_ref, hf_ref, hb_ref, *, seq, tl):
    hi = lax.Precision.HIGHEST
    i = pl.program_id(0)
    t = t_ref[...]
    pos = (i * tl + lax.broadcasted_iota(I32, (tl, 1), 0)).astype(F32)
    ang = bands_ref[...] * (2.0 * math.pi * pos / seq)
    f = freq_ref[...]
    pre = (t * w1t_ref[...]
           + jnp.dot(jnp.cos(ang), w1c_ref[...], precision=hi, preferred_element_type=F32)
           - jnp.dot(jnp.sin(ang), w1s_ref[...], precision=hi, preferred_element_type=F32)
           + b1_ref[...])
    a = jnp.sin(f * pre)
    a = jnp.sin(f * (jnp.dot(a, w2_ref[...], precision=hi, preferred_element_type=F32) + b2_ref[...]))
    h = jnp.dot(a, w3_ref[...], precision=hi, preferred_element_type=F32)
    decay = jnp.exp(-t * jnp.abs(delta_ref[...]))
    w = HY_WIDTH
    for o in range(2):
        hf_ref[:, o * w:(o + 1) * w] = h[:, o * 2 * w:o * 2 * w + w] * decay
        hb_ref[:, o * w:(o + 1) * w] = jnp.where(pos == 0.0, 0.0, h[:, o * 2 * w + w:(o + 1) * 2 * w] * decay)


def hyena_filters_time(seq, w1, b1, w2, b2, w3, freq):
    tl = 512
    hid = LANES
    pad_h = hid - HY_FILT_HIDDEN
    t = jnp.linspace(0.0, 1.0, seq, dtype=F32)[:, None]
    bands = jnp.pad(jnp.linspace(1e-4, HY_BANDS - 1, HY_BANDS, dtype=F32)[None], ((0, 0), (0, LANES - HY_BANDS)))
    w1p = jnp.pad(w1, ((0, 0), (0, pad_h)))
    w1t = w1p[0:1]
    w1c = jnp.pad(w1p[1:1 + HY_BANDS], ((0, LANES - HY_BANDS), (0, 0)))
    w1s = jnp.pad(w1p[1 + HY_BANDS:], ((0, LANES - HY_BANDS), (0, 0)))
    b1p = jnp.pad(b1, (0, pad_h)).reshape(1, hid)
    w2p = jnp.pad(w2, ((0, pad_h), (0, pad_h)))
    b2p = jnp.pad(b2, (0, pad_h)).reshape(1, hid)
    w3p = jnp.pad(w3, ((0, pad_h), (0, 0)))
    freqp = jnp.pad(freq, (0, pad_h)).reshape(1, hid)
    max_decay = math.log(HY_DECAY_TARGET) / HY_FAST_PCT
    min_decay = math.log(HY_DECAY_TARGET) / HY_SLOW_PCT
    deltas = jnp.linspace(min_decay, max_decay, HY_WIDTH, dtype=F32)[None]
    fw = w3.shape[1]
    full = lambda shape: pl.BlockSpec(shape, lambda i: (0, 0))
    return pl.pallas_call(
        functools.partial(_hy_filter_kernel, seq=seq, tl=tl),
        grid=(seq // tl,),
        in_specs=[pl.BlockSpec((tl, 1), lambda i: (i, 0)), full((1, LANES)), full((1, hid)),
                  full((LANES, hid)), full((LANES, hid)), full((1, hid)), full((hid, hid)), full((1, hid)),
                  full((hid, fw)), full((1, hid)), full((1, HY_WIDTH))],
        out_specs=[pl.BlockSpec((tl, 2 * HY_WIDTH), lambda i: (i, 0)),
                   pl.BlockSpec((tl, 2 * HY_WIDTH), lambda i: (i, 0))],
        out_shape=[jax.ShapeDtypeStruct((seq, 2 * HY_WIDTH), F32)] * 2,
        compiler_params=_cparams(("parallel",)),
        name="hyena_filter_mlp",
    )(t, bands, w1t, w1c, w1s, b1p, w2p, b2p, w3p, freqp, deltas)


def dft_tables(seq):
    n2 = 4 * seq
    sub = 64
    f = jnp.arange(seq, dtype=I32)[:, None]
    odd = 2 * f + 1
    s1 = jnp.arange(seq // sub, dtype=I32)[None]
    s0 = jnp.arange(sub, dtype=I32)[None]
    ang_p = ((odd * s1) % (n2 // sub)).astype(F32) * (2.0 * math.pi * sub / n2)
    ang_q = ((odd * s0) % n2).astype(F32) * (2.0 * math.pi / n2)
    pc, ps, qc, qs = jnp.cos(ang_p), jnp.sin(ang_p), jnp.cos(ang_q), jnp.sin(ang_q)
    c = (pc[:, :, None] * qc[:, None, :] - ps[:, :, None] * qs[:, None, :]).reshape(seq, seq)
    s = (ps[:, :, None] * qc[:, None, :] + pc[:, :, None] * qs[:, None, :]).reshape(seq, seq)
    return c.astype(BF16), s.astype(BF16), c.T.astype(BF16), s.T.astype(BF16)


def dft_twiddles(seq):
    g = jnp.arange(seq // 2, dtype=I32)[:, None]
    ang = (2 * g + 1).astype(F32) * (2.0 * math.pi / (4 * seq))
    return jnp.cos(ang), jnp.sin(ang)


def _half_transform(c1, s1, cg, sg, x0, x1):
    pc0, ps0 = _dot(c1, x0), _dot(s1, x0)
    pc1, ps1 = _dot(c1, x1), _dot(s1, x1)
    tr = cg * pc1 - sg * ps1
    ti = -(cg * ps1 + sg * pc1)
    return pc0 + tr, ti - ps0, pc0 - tr, ps0 + ti


def _spectrum_kernel(c_ref, s_ref, cg_ref, sg_ref, f0_ref, f1_ref, b0_ref, b1_ref,
                     kgre_ref, kgim_ref, khre_ref, khim_ref):
    c1, s1, cg, sg = c_ref[...], s_ref[...], cg_ref[...], sg_ref[...]
    fre_g, fim_g, fre_h, fim_h = _half_transform(c1, s1, cg, sg, f0_ref[...].astype(BF16), f1_ref[...].astype(BF16))
    bre_g, bim_g, bre_h, bim_h = _half_transform(c1, s1, cg, sg, b0_ref[...].astype(BF16), b1_ref[...].astype(BF16))
    kgre_ref[...] = fre_g + bre_g
    kgim_ref[...] = fim_g - bim_g
    khre_ref[...] = fre_h + bre_h
    khim_ref[...] = fim_h - bim_h


def filter_spectrum(tables, twiddles, hf, hb):
    seq, cols = hf.shape
    half = seq // 2
    tf, tn = 512, 512
    nj = cols // tn
    hf2 = hf.reshape(half, 2 * cols)
    hb2 = hb.reshape(half, 2 * cols)
    tab = pl.BlockSpec((tf, half), lambda i, j: (i, 0))
    twd = pl.BlockSpec((tf, 1), lambda i, j: (i, 0))
    even = pl.BlockSpec((half, tn), lambda i, j: (0, j))
    odd = pl.BlockSpec((half, tn), lambda i, j: (0, nj + j))
    return pl.pallas_call(
        _spectrum_kernel,
        grid=(half // tf, nj),
        in_specs=[tab, tab, twd, twd, even, odd, even, odd],
        out_specs=[pl.BlockSpec((tf, tn), lambda i, j: (i, j))] * 4,
        out_shape=[jax.ShapeDtypeStruct((half, cols), F32)] * 4,
        compiler_params=_cparams(("parallel", "parallel")),
        name="hyena_filter_spectrum",
    )(tables[0], tables[1], twiddles[0], twiddles[1], hf2, hf2, hb2, hb2)


def _hy_fwd_kernel(z0_ref, z1_ref, c_ref, s_ref, cg_ref, sg_ref, kgre_ref, kgim_ref, khre_ref, khim_ref,
                   u0re_ref, u0im_ref, u1re_ref, u1im_ref):
    cg, sg = cg_ref[...], sg_ref[...]
    zre_g, zim_g, zre_h, zim_h = _half_transform(c_ref[...], s_ref[...], cg, sg,
                                                 z0_ref[0].astype(BF16), z1_ref[0].astype(BF16))
    kgre, kgim, khre, khim = kgre_ref[...], kgim_ref[...], khre_ref[...], khim_ref[...]
    yre_g = zre_g * kgre - zim_g * kgim
    yim_g = zre_g * kgim + zim_g * kgre
    yre_h = zre_h * khre - zim_h * khim
    yim_h = zre_h * khim + zim_h * khre
    u0re_ref[0] = (yre_g + yre_h).astype(BF16)
    u0im_ref[0] = (yim_g - yim_h).astype(BF16)
    a = yre_g - yre_h
    b = yim_g + yim_h
    u1re_ref[0] = (cg * a - sg * b).astype(BF16)
    u1im_ref[0] = (cg * b + sg * a).astype(BF16)


def hyena_fwd(zsrc, tables, twiddles, kspec, order):
    (ze, ce), (zo, co) = zsrc
    b, half, _ = ze.shape
    w = HY_WIDTH
    tf = 512
    tab = pl.BlockSpec((tf, half), lambda i, j: (j, 0))
    twd = pl.BlockSpec((tf, 1), lambda i, j: (j, 0))
    kblk = pl.BlockSpec((tf, w), lambda i, j: (j, order))
    return pl.pallas_call(
        _hy_fwd_kernel,
        grid=(b, half // tf),
        in_specs=[pl.BlockSpec((1, half, w), lambda i, j: (i, 0, ce)),
                  pl.BlockSpec((1, half, w), lambda i, j: (i, 0, co)),
                  tab, tab, twd, twd, kblk, kblk, kblk, kblk],
        out_specs=[pl.BlockSpec((1, tf, w), lambda i, j: (i, j, 0))] * 4,
        out_shape=[jax.ShapeDtypeStruct((b, half, w), BF16)] * 4,
        compiler_params=_cparams(("parallel", "parallel")),
        name="hyena_dft_fwd",
    )(ze, zo, tables[0], tables[1], twiddles[0], twiddles[1], *kspec)


def _hy_inv_kernel(ct_ref, st_ref, u0re_ref, u0im_ref, u1re_ref, u1im_ref, z0_ref, z1_ref, g0_ref, g1_ref,
                   bias_ref, o_ref, *, scale):
    ct, st = ct_ref[...], st_ref[...]
    w = z0_ref.shape[2]
    y0 = (_dot(ct, u0re_ref[0]) - _dot(st, u0im_ref[0])) * scale
    y1 = (_dot(ct, u1re_ref[0]) - _dot(st, u1im_ref[0])) * scale
    bias = bias_ref[...]
    o_ref[0, :, 0:w] = g0_ref[0] * (y0 + z0_ref[0] * bias)
    o_ref[0, :, w:2 * w] = g1_ref[0] * (y1 + z1_ref[0] * bias)


def hyena_inv(tables, u, zsrc, gsrc, bias_row):
    b, half, w = u[0].shape
    tt = 512
    tab = pl.BlockSpec((tt, half), lambda i, j: (j, 0))
    ublk = pl.BlockSpec((1, half, w), lambda i, j: (i, 0, 0))

    def rows(col):
        return pl.BlockSpec((1, tt, w), lambda i, j: (i, j, col))

    (ze, ce), (zo, co) = zsrc
    (ge, gce), (go, gco) = gsrc
    return pl.pallas_call(
        functools.partial(_hy_inv_kernel, scale=0.5 / half),
        grid=(b, half // tt),
        in_specs=[tab, tab, ublk, ublk, ublk, ublk, rows(ce), rows(co), rows(gce), rows(gco),
                  pl.BlockSpec((1, w), lambda i, j: (0, 0))],
        out_specs=pl.BlockSpec((1, tt, 2 * w), lambda i, j: (i, j, 0)),
        out_shape=jax.ShapeDtypeStruct((b, half, 2 * w), F32),
        compiler_params=_cparams(("parallel", "parallel")),
        name="hyena_dft_inv",
    )(tables[2], tables[3], *u, ze, zo, ge, go, bias_row)


def hyena_mixer(proj3d, tables, twiddles, conv_w, conv_b, w1, b1, w2, b2, w3, freq, hy_bias):
    seq = proj3d.shape[1]
    ue, uo = conv3(proj3d, conv_w, conv_b)
    hf, hb = hyena_filters_time(seq, w1, b1, w2, b2, w3, freq)
    kspec = filter_spectrum(tables, twiddles, hf, hb)
    v = ((ue, 0), (uo, 0))
    uu = hyena_fwd(v, tables, twiddles, kspec, 0)
    z1 = hyena_inv(tables, uu, v, ((ue, 1), (uo, 1)), hy_bias[0:1])
    z = ((z1, 0), (z1, 1))
    uu = hyena_fwd(z, tables, twiddles, kspec, 1)
    return hyena_inv(tables, uu, z, ((ue, 2), (uo, 2)), hy_bias[1:2])


def _diff_attn_kernel(slope_ref, q_ref, k_ref, v_ref, lam_ref, sub_ref, o_ref, bias_ref, *, tq, lam_init):
    h = pl.program_id(0)
    qi = pl.program_id(1)
    seq = k_ref.shape[1]

    @pl.when(pl.program_id(2) == 0)
    def _():
        qpos = qi * tq + lax.broadcasted_iota(I32, (tq, seq), 0)
        kpos = lax.broadcasted_iota(I32, (tq, seq), 1)
        bias_ref[...] = slope_ref[h] * jnp.abs(qpos - kpos).astype(F32)

    q = q_ref[0]
    k = k_ref[0]
    v = v_ref[0]
    lane = lax.broadcasted_iota(I32, (1, 2 * DIFF_HEAD_DIM), 1)
    ck = DIFF_KEY_CHUNK

    def attend(m):
        keep = (lane < DIFF_HEAD_DIM) if m == 0 else (lane >= DIFF_HEAD_DIM)
        qm = jnp.where(keep, q, jnp.zeros_like(q))
        s = [_dot_nt(qm, k[c:c + ck]) - bias_ref[:, c:c + ck] for c in range(0, seq, ck)]
        mx = functools.reduce(jnp.maximum, [jnp.max(sc, axis=-1, keepdims=True) for sc in s])
        acc = jnp.zeros((tq, 2 * DIFF_HEAD_DIM), F32)
        den = jnp.zeros((tq, 1), F32)
        for i, sc in enumerate(s):
            e = jnp.exp2(sc - mx)
            den = den + jnp.sum(e, axis=-1, keepdims=True)
            acc = acc + _dot(e.astype(BF16), v[i * ck:(i + 1) * ck])
        return acc / den

    l = lam_ref[...]
    lam_full = (jnp.exp(jnp.sum(l[0:1] * l[1:2], axis=-1, keepdims=True))
                - jnp.exp(jnp.sum(l[2:3] * l[3:4], axis=-1, keepdims=True)) + lam_init)
    o = attend(0) - lam_full * attend(1)
    o_ref[0] = _rms(o, sub_ref[...]) * (1.0 - lam_init)


def diff_attention(proj3d, lam, subln, layer_idx):
    b, seq, _ = proj3d.shape
    tq = DIFF_Q_BLOCK
    hw = 2 * DIFF_HEAD_DIM
    qb, kb, vb = 0, DIFF_HEADS, 2 * DIFF_HEADS
    lam_init = 0.8 - 0.6 * math.exp(-0.3 * layer_idx)
    slopes = 2.0 ** (-8.0 * jnp.arange(1, DIFF_HEADS + 1, dtype=F32) / DIFF_HEADS) * LOG2E
    return pl.pallas_call(
        functools.partial(_diff_attn_kernel, tq=tq, lam_init=lam_init),
        grid=(DIFF_HEADS, seq // tq, b),
        in_specs=[pl.BlockSpec(memory_space=pltpu.SMEM),
                  pl.BlockSpec((1, tq, hw), lambda h, j, i: (i, j, qb + h)),
                  pl.BlockSpec((1, seq, hw), lambda h, j, i: (i, 0, kb + h)),
                  pl.BlockSpec((1, seq, hw), lambda h, j, i: (i, 0, vb + h)),
                  pl.BlockSpec((4, DIFF_HEAD_DIM), lambda h, j, i: (0, 0)),
                  pl.BlockSpec((1, hw), lambda h, j, i: (0, 0))],
        out_specs=pl.BlockSpec((1, tq, hw), lambda h, j, i: (i, j, h)),
        out_shape=jax.ShapeDtypeStruct((b, seq, DIFF_HEADS * hw), F32),
        scratch_shapes=[pltpu.VMEM((tq, seq), F32)],
        compiler_params=_cparams(("parallel", "parallel", "arbitrary")),
        name="diff_attention",
    )(slopes, proj3d, proj3d, proj3d, lam, subln.reshape(1, hw))


def _win_attn_kernel(slope_ref, sink_ref, q_ref, kp_ref, kc_ref, kn_ref, vp_ref, vc_ref, vn_ref, o_ref, bias_ref,
                     *, tq, seq):
    qi = pl.program_id(0)
    span = 3 * tq
    group = GQA_HEADS // GQA_KV

    @pl.when(pl.program_id(1) == 0)
    def _():
        qpos = qi * tq + lax.broadcasted_iota(I32, (tq, span), 0)
        kpos = (qi - 1) * tq + lax.broadcasted_iota(I32, (tq, span), 1)
        rel = jnp.abs(qpos - kpos)
        relf = rel.astype(F32)
        masked = jnp.where((rel <= WIN) & (kpos >= 0) & (kpos < seq), 0.0, -NEG)
        for head in range(GQA_HEADS):
            bias_ref[head] = slope_ref[head] * relf + masked

    kwin = jnp.concatenate([kp_ref[0], kc_ref[0], kn_ref[0]], axis=0)
    vwin = jnp.concatenate([vp_ref[0], vc_ref[0], vn_ref[0]], axis=0)
    low = lax.broadcasted_iota(I32, (1, LANES), 1) < GQA_HD
    for p in range(GQA_KV // 2):
        kb = kwin[:, p * LANES:(p + 1) * LANES]
        vb = vwin[:, p * LANES:(p + 1) * LANES]
        outs = []
        for half in range(2):
            kv = 2 * p + half
            mine = low if half == 0 else jnp.logical_not(low)
            qblocks = [q_ref[0, :, (group * p + r) * LANES:(group * p + r + 1) * LANES] for r in range(group)]
            qs = jnp.concatenate([jnp.where(mine, qb, jnp.zeros_like(qb)) for qb in qblocks], axis=0)
            s = _dot_nt(qs, kb)
            es, inv = [], []
            for r in range(group):
                head = kv * group + r
                sink = sink_ref[head]
                sr = s[r * tq:(r + 1) * tq] - bias_ref[head]
                m = jnp.maximum(jnp.max(sr, axis=-1, keepdims=True), sink)
                e = jnp.exp2(sr - m)
                inv.append(1.0 / (jnp.sum(e, axis=-1, keepdims=True) + jnp.exp2(sink - m)))
                es.append(e.astype(BF16))
            o = _dot(jnp.concatenate(es, axis=0), vb)
            outs.append([o[r * tq:(r + 1) * tq] * inv[r] for r in range(group)])
        for r in range(group):
            col = (group * p + r) * LANES
            o_ref[0, :, col:col + LANES] = jnp.where(low, outs[0][r], outs[1][r])


def window_head_order():
    group = GQA_HEADS // GQA_KV
    order = []
    for p in range(GQA_KV // 2):
        for r in range(group):
            order += [(2 * p) * group + r, (2 * p + 1) * group + r]
    return order


def window_column_perm():
    cols = []
    for head in window_head_order():
        cols += list(range(head * GQA_HD, (head + 1) * GQA_HD))
    return jnp.asarray(cols, dtype=I32)


def window_gqa(proj3d, sink):
    b, seq, _ = proj3d.shape
    tq = WIN
    nq = seq // tq
    oq = GQA_HEADS * GQA_HD
    okv = GQA_KV * GQA_HD
    kcol, vcol = oq // okv, oq // okv + 1
    slopes = 2.0 ** (-8.0 * jnp.arange(1, GQA_HEADS + 1, dtype=F32) / GQA_HEADS) * LOG2E

    def neighbour(col, step):
        return pl.BlockSpec((1, tq, okv), lambda j, i: (i, jnp.clip(j + step, 0, nq - 1), col))

    return pl.pallas_call(
        functools.partial(_win_attn_kernel, tq=tq, seq=seq),
        grid=(nq, b),
        in_specs=[pl.BlockSpec(memory_space=pltpu.SMEM),
                  pl.BlockSpec(memory_space=pltpu.SMEM),
                  pl.BlockSpec((1, tq, oq), lambda j, i: (i, j, 0)),
                  neighbour(kcol, -1), neighbour(kcol, 0), neighbour(kcol, 1),
                  neighbour(vcol, -1), neighbour(vcol, 0), neighbour(vcol, 1)],
        out_specs=pl.BlockSpec((1, tq, oq), lambda j, i: (i, j, 0)),
        out_shape=jax.ShapeDtypeStruct((b, seq, oq), F32),
        scratch_shapes=[pltpu.VMEM((GQA_HEADS, tq, 3 * tq), F32)],
        compiler_params=_cparams(("parallel", "arbitrary")),
        name="window_gqa",
    )(slopes, sink.astype(F32) * LOG2E, proj3d, proj3d, proj3d, proj3d, proj3d, proj3d, proj3d)


def _cross_attend(x, g, wq, kv, wo):
    h = _rms(x, g).astype(BF16)
    q = _dot(h, wq)
    scale = X_HD ** -0.5
    outs = []
    for hd in range(X_HEADS):
        qh = q[:, hd * X_HD:(hd + 1) * X_HD].astype(BF16)
        kh = kv[:, hd * X_HD:(hd + 1) * X_HD]
        vh = kv[:, X_W + hd * X_HD:X_W + (hd + 1) * X_HD]
        s = _dot_nt(qh, kh) * scale
        e = jnp.exp(s - jnp.max(s, axis=-1, keepdims=True))
        p = e / jnp.sum(e, axis=-1, keepdims=True)
        outs.append(_dot(p.astype(BF16), vh))
    o = jnp.concatenate(outs, axis=-1).astype(BF16)
    return x + _dot(o, wo)


def _split_bf16(x):
    hi = x.astype(BF16)
    lo = (x - hi.astype(F32)).astype(BF16)
    return hi, lo


def _pack_halves(h):
    c = h.shape[1] // 2
    left = lax.bitcast_convert_type(h[:, :c].astype(BF16).astype(F32), U32)
    right = lax.bitcast_convert_type(h[:, c:].astype(BF16).astype(F32), U32)
    return left | (right >> 16)


def _unpack_halves(p):
    left = lax.bitcast_convert_type(p & jnp.uint32(0xFFFF0000), F32)
    right = lax.bitcast_convert_type(p << 16, F32)
    return jnp.concatenate([left, right], axis=-1)


def _route(x, g, whi, wlo, bias, hp_ref, eid_ref, wt_ref, rank_ref, cnt_ref, base_ref):
    tm = x.shape[0]
    h = _rms(x, g)
    hp_ref[...] = _pack_halves(h)
    hhi, hlo = _split_bf16(h)
    logits = _dot(hhi, whi) + _dot(hlo, whi) + _dot(hhi, wlo) + bias
    lane = lax.broadcasted_iota(I32, logits.shape, 1)
    big = jnp.int32(LANES)
    ninf = -jnp.inf

    gl = jnp.where(lane < N_GROUPS, logits, ninf)
    gmax = jnp.max(gl, axis=-1, keepdims=True)
    gsel = jnp.min(jnp.where(gl == gmax, lane, big), axis=-1, keepdims=True)
    ggate = 1.0 / jnp.sum(jnp.exp(gl - gmax), axis=-1, keepdims=True)

    lo_lane = N_GROUPS + gsel * EXP_PER_GROUP
    el = jnp.where((lane >= lo_lane) & (lane < lo_lane + EXP_PER_GROUP), logits, ninf)
    v1 = jnp.max(el, axis=-1, keepdims=True)
    i1 = jnp.min(jnp.where(el == v1, lane, big), axis=-1, keepdims=True)
    el2 = jnp.where(lane == i1, ninf, el)
    v2 = jnp.max(el2, axis=-1, keepdims=True)
    i2 = jnp.min(jnp.where(el2 == v2, lane, big), axis=-1, keepdims=True)
    e2 = jnp.exp(v2 - v1)
    w1 = ggate / (1.0 + e2)
    w2 = ggate * e2 / (1.0 + e2)

    one1 = lane == i1
    one2 = lane == i2
    onehot = (one1 | one2).astype(F32)
    r = lax.broadcasted_iota(I32, (tm, tm), 0)
    c = lax.broadcasted_iota(I32, (tm, tm), 1)
    tri = (c < r).astype(BF16)
    before = _dot(tri, onehot.astype(BF16)) + base_ref[...]
    rank1 = jnp.sum(jnp.where(one1, before, 0.0), axis=-1, keepdims=True)
    rank2 = jnp.sum(jnp.where(one2, before, 0.0), axis=-1, keepdims=True)
    total = base_ref[...] + jnp.sum(onehot, axis=0, keepdims=True)
    base_ref[...] = total
    cnt_ref[...] = total

    col = lax.broadcasted_iota(I32, (tm, 2), 1)
    eid_ref[...] = jnp.where(col == 0, i1, i2) - N_GROUPS
    wt_ref[...] = jnp.where(col == 0, w1, w2)
    rank_ref[...] = jnp.where(col == 0, rank1, rank2).astype(I32)


def _interleave_rows(pair):
    t, w2 = pair.shape
    w = w2 // 2
    r = lax.broadcasted_iota(I32, (2 * t, t), 0)
    c = lax.broadcasted_iota(I32, (2 * t, t), 1)
    pick_even = (r == 2 * c).astype(BF16)
    pick_odd = (r == 2 * c + 1).astype(BF16)
    return (_dot(pick_even, pair[:, :w]) + _dot(pick_odd, pair[:, w:])).astype(BF16)


def _post_mixer_kernel(*refs, n_lhs, paired):
    x_ref = refs[0]
    a_refs = refs[1:1 + n_lhs]
    w_refs = refs[1 + n_lhs:1 + 2 * n_lhs]
    (cg_ref, wq_ref, kv_ref, wo_ref, fg_ref, whi_ref, wlo_ref, rb_ref,
     x_out_ref, hp_ref, eid_ref, wt_ref, rank_ref, cnt_ref, base_ref) = refs[1 + 2 * n_lhs:]

    @pl.when((pl.program_id(0) == 0) & (pl.program_id(1) == 0))
    def _():
        base_ref[...] = jnp.zeros_like(base_ref)

    x = x_ref[0]
    for a_ref, w_ref, is_paired in zip(a_refs, w_refs, paired):
        a = a_ref[0].astype(BF16)
        x = x + _dot(_interleave_rows(a) if is_paired else a, w_ref[...])
    x = _cross_attend(x, cg_ref[...], wq_ref[...], kv_ref[0], wo_ref[...])
    x_out_ref[0] = x
    _route(x, fg_ref[...], whi_ref[...], wlo_ref[...], rb_ref[...], hp_ref, eid_ref, wt_ref, rank_ref, cnt_ref,
           base_ref)


def post_mixer(x3d, lhs_list, w_list, c_gamma, wq, kv, wo, f_gamma, w_grp, b_grp, w_exp, b_exp):
    b, seq, d = x3d.shape
    paired = tuple(a.shape[1] != seq for a in lhs_list)
    n = b * seq
    m = kv.shape[1]
    tm = 512
    nt = seq // tm
    wcat = jnp.pad(jnp.concatenate([w_grp, w_exp], axis=1), ((0, 0), (0, LANES - N_GROUPS - N_EXPERTS)))
    bcat = jnp.pad(jnp.concatenate([b_grp, b_exp]), (0, LANES - N_GROUPS - N_EXPERTS)).reshape(1, LANES)
    whi = wcat.astype(BF16)
    wlo = (wcat - whi.astype(F32)).astype(BF16)
    const = lambda shape: pl.BlockSpec(shape, lambda i, j: (0,) * len(shape))
    tok = lambda width: pl.BlockSpec((tm, width), lambda i, j: (i * nt + j, 0))
    in_specs = [pl.BlockSpec((1, tm, d), lambda i, j: (i, j, 0))]
    in_specs += [pl.BlockSpec((1, tm // 2 if p else tm, a.shape[2]), lambda i, j: (i, j, 0))
                 for a, p in zip(lhs_list, paired)]
    in_specs += [const(w.shape) for w in w_list]
    in_specs += [const((1, d)), const((d, X_W)), pl.BlockSpec((1, m, 2 * X_W), lambda i, j: (i, 0, 0)),
                 const((X_W, d)), const((1, d)), const((d, LANES)), const((d, LANES)), const((1, LANES))]
    return pl.pallas_call(
        functools.partial(_post_mixer_kernel, n_lhs=len(lhs_list), paired=paired),
        grid=(b, nt),
        in_specs=in_specs,
        out_specs=[pl.BlockSpec((1, tm, d), lambda i, j: (i, j, 0)), tok(d // 2), tok(2), tok(2), tok(2),
                   const((1, LANES))],
        out_shape=[jax.ShapeDtypeStruct((b, seq, d), F32),
                   jax.ShapeDtypeStruct((n, d // 2), U32),
                   jax.ShapeDtypeStruct((n, 2), I32),
                   jax.ShapeDtypeStruct((n, 2), F32),
                   jax.ShapeDtypeStruct((n, 2), I32),
                   jax.ShapeDtypeStruct((1, LANES), F32)],
        scratch_shapes=[pltpu.VMEM((1, LANES), F32)],
        compiler_params=_cparams(("arbitrary", "arbitrary")),
        name="post_mixer",
    )(x3d, *lhs_list, *w_list, c_gamma.reshape(1, d), wq, kv, wo, f_gamma.reshape(1, d), whi, wlo, bcat)


def _row_copy(src_ref, src_row, dst_ref, dst_row, sem):
    return pltpu.make_async_copy(src_ref.at[pl.ds(src_row, 1)], dst_ref.at[pl.ds(dst_row, 1)], sem)


def _dispatch_kernel(dest_ref, seg_ref, hp_ref, xs_ref, zbuf, sem, zsem, *, tm):
    @pl.when(pl.program_id(0) == 0)
    def _():
        zbuf[...] = jnp.zeros_like(zbuf)

        def zero_tail(e):
            tail = pl.multiple_of(seg_ref[0, e] - MOE_BLOCK, MOE_BLOCK)
            return pltpu.make_async_copy(zbuf, xs_ref.at[pl.ds(tail, MOE_BLOCK)], zsem)

        def start(e, carry):
            @pl.when(seg_ref[1, e] > 0)
            def _():
                zero_tail(e).start()
            return carry

        def finish(e, carry):
            @pl.when(seg_ref[1, e] > 0)
            def _():
                zero_tail(e).wait()
            return carry

        lax.fori_loop(0, N_EXPERTS, start, 0)
        lax.fori_loop(0, N_EXPERTS, finish, 0)

        def unused(blk):
            return pltpu.make_async_copy(zbuf, xs_ref.at[pl.ds(blk * MOE_BLOCK, MOE_BLOCK)], zsem)

        def start_unused(blk, carry):
            unused(blk).start()
            return carry

        def finish_unused(blk, carry):
            unused(blk).wait()
            return carry

        first_unused = seg_ref[0, N_EXPERTS - 1] // MOE_BLOCK
        lax.fori_loop(first_unused, xs_ref.shape[0] // MOE_BLOCK, start_unused, 0)
        lax.fori_loop(first_unused, xs_ref.shape[0] // MOE_BLOCK, finish_unused, 0)

    def issue(r, carry):
        _row_copy(hp_ref, r, xs_ref, dest_ref[2 * r], sem).start(priority=0)
        _row_copy(hp_ref, r, xs_ref, dest_ref[2 * r + 1], sem).start(priority=1)
        return carry

    lax.fori_loop(0, tm, issue, 0, unroll=ROW_DMA_UNROLL)
    for _ in range(2):
        pltpu.make_async_copy(hp_ref, xs_ref.at[pl.ds(0, tm)], sem).wait()


def moe_dispatch(hp, dest_flat, segments, cap):
    n, c = hp.shape
    tm = 512
    return pl.pallas_call(
        functools.partial(_dispatch_kernel, tm=tm),
        grid=(n // tm,),
        in_specs=[pl.BlockSpec((2 * tm,), lambda i: (i,), memory_space=pltpu.SMEM),
                  pl.BlockSpec(memory_space=pltpu.SMEM),
                  pl.BlockSpec((tm, c), lambda i: (i, 0))],
        out_specs=pl.BlockSpec(memory_space=pl.ANY),
        out_shape=jax.ShapeDtypeStruct((cap, c), U32),
        scratch_shapes=[pltpu.VMEM((MOE_BLOCK, c), U32), pltpu.SemaphoreType.DMA(()), pltpu.SemaphoreType.DMA(())],
        compiler_params=_cparams(("arbitrary",)),
        name="moe_dispatch",
    )(dest_flat, segments, hp)


def _expert_kernel(start_ref, cnt_ref, xs_ref, wg_ref, wu_ref, wd_ref, yb_ref, xbuf, ybuf, sem_in, sem_out,
                   wg_s, wu_s, wd_s, *, nblk):
    e = pl.program_id(0)
    nb = cnt_ref[e]
    b0 = start_ref[e]

    def fetch(blk, slot):
        return pltpu.make_async_copy(xs_ref.at[pl.ds(blk * MOE_BLOCK, MOE_BLOCK)], xbuf.at[slot], sem_in.at[slot])

    def put(blk, slot):
        return pltpu.make_async_copy(ybuf.at[slot], yb_ref.at[pl.ds(blk * MOE_BLOCK, MOE_BLOCK)], sem_out.at[slot])

    @pl.when(nb > 0)
    def _():
        fetch(b0, 0).start(priority=1)
        wg_s[...] = wg_ref[0].astype(BF16)
        wu_s[...] = wu_ref[0].astype(BF16)
        wd_s[...] = wd_ref[0].astype(BF16)

    def block(i, carry):
        slot = i & 1
        fetch(b0 + i, slot).wait()

        @pl.when(i + 1 < nb)
        def _():
            fetch(b0 + i + 1, 1 - slot).start(priority=1)

        @pl.when(i >= 2)
        def _():
            put(b0 + i - 2, slot).wait()

        x = _unpack_halves(xbuf[slot]).astype(BF16)
        a = _dot(x, wg_s[...])
        u = _dot(x, wu_s[...])
        hmid = (a / (1.0 + jnp.exp(-a)) * u).astype(BF16)
        ybuf[slot] = _pack_halves(_dot(hmid, wd_s[...]))
        put(b0 + i, slot).start()
        return carry

    lax.fori_loop(0, nb, block, 0)

    @pl.when(nb >= 2)
    def _():
        put(b0 + nb - 2, nb & 1).wait()

    @pl.when(nb >= 1)
    def _():
        put(b0 + nb - 1, (nb - 1) & 1).wait()

    @pl.when(e == pl.num_programs(0) - 1)
    def _():
        ybuf[0] = jnp.zeros(ybuf.shape[1:], ybuf.dtype)

        def fill(blk, carry):
            copy = put(blk, 0)
            copy.start()
            copy.wait()
            return carry

        lax.fori_loop(b0 + nb, nblk, fill, 0)


def moe_experts(xs, blk_start, blk_cnt, w_gate, w_up, w_down, layer):
    cap, c = xs.shape
    d = 2 * c
    hid = w_gate.shape[2]
    nblk = cap // MOE_BLOCK
    wmap = lambda e, st, cn: (layer * N_EXPERTS + e, 0, 0)
    grid_spec = pltpu.PrefetchScalarGridSpec(
        num_scalar_prefetch=2,
        grid=(N_EXPERTS,),
        in_specs=[pl.BlockSpec(memory_space=pl.ANY),
                  pl.BlockSpec((1, d, hid), wmap),
                  pl.BlockSpec((1, d, hid), wmap),
                  pl.BlockSpec((1, hid, d), wmap)],
        out_specs=pl.BlockSpec(memory_space=pl.ANY),
        scratch_shapes=[pltpu.VMEM((2, MOE_BLOCK, c), U32), pltpu.VMEM((2, MOE_BLOCK, c), U32),
                        pltpu.SemaphoreType.DMA((2,)), pltpu.SemaphoreType.DMA((2,)),
                        pltpu.VMEM((d, hid), BF16), pltpu.VMEM((d, hid), BF16), pltpu.VMEM((hid, d), BF16)],
    )
    return pl.pallas_call(
        functools.partial(_expert_kernel, nblk=nblk),
        grid_spec=grid_spec,
        out_shape=jax.ShapeDtypeStruct((cap, c), U32),
        compiler_params=_cparams(("arbitrary",)),
        name="moe_experts",
    )(blk_start, blk_cnt, xs, w_gate, w_up, w_down)


def _combine_kernel(dest_ref, next_dest_ref, x_ref, wt_ref, g_ref, yb_ref, o_ref, buf, sem, *, tm, final_norm):
    i = pl.program_id(0)
    slot = i & 1

    def gather(idx_ref, s):
        def issue(r, carry):
            _row_copy(yb_ref, idx_ref[2 * r], buf.at[s, 0], r, sem.at[s]).start(priority=0)
            _row_copy(yb_ref, idx_ref[2 * r + 1], buf.at[s, 1], r, sem.at[s]).start(priority=1)
            return carry

        lax.fori_loop(0, tm, issue, 0, unroll=ROW_DMA_UNROLL)

    @pl.when(i == 0)
    def _():
        gather(dest_ref, 0)

    @pl.when(i + 1 < pl.num_programs(0))
    def _():
        gather(next_dest_ref, 1 - slot)

    for k in range(2):
        pltpu.make_async_copy(yb_ref.at[pl.ds(0, tm)], buf.at[slot, k], sem.at[slot]).wait()
    wt = wt_ref[...]
    y = x_ref[...] + (_unpack_halves(buf[slot, 0]) * wt[:, 0:1] + _unpack_halves(buf[slot, 1]) * wt[:, 1:2])
    o_ref[...] = _rms(y, g_ref[...]) if final_norm else y


def moe_combine(x2d, yb, dest_flat, wt, g_final, final_norm):
    n, d = x2d.shape
    c = yb.shape[1]
    tm = 512
    nt = n // tm
    return pl.pallas_call(
        functools.partial(_combine_kernel, tm=tm, final_norm=final_norm),
        grid=(nt,),
        in_specs=[pl.BlockSpec((2 * tm,), lambda i: (i,), memory_space=pltpu.SMEM),
                  pl.BlockSpec((2 * tm,), lambda i: (jnp.minimum(i + 1, nt - 1),), memory_space=pltpu.SMEM),
                  pl.BlockSpec((tm, d), lambda i: (i, 0)),
                  pl.BlockSpec((tm, 2), lambda i: (i, 0)),
                  pl.BlockSpec((1, d), lambda i: (0, 0)),
                  pl.BlockSpec(memory_space=pl.ANY)],
        out_specs=pl.BlockSpec((tm, d), lambda i: (i, 0)),
        out_shape=jax.ShapeDtypeStruct((n, d), F32),
        scratch_shapes=[pltpu.VMEM((2, 2, tm, c), U32), pltpu.SemaphoreType.DMA((2,))],
        compiler_params=_cparams(("arbitrary",)),
        name="moe_combine",
    )(dest_flat, dest_flat, x2d, wt, g_final.reshape(1, d), yb)


def hier_moe_block(x2d, routing, w_gate, w_up, w_down, layer, g_final, final_norm):
    n = x2d.shape[0]
    cap = 2 * n + N_EXPERTS * MOE_BLOCK
    hp, eid, wt, rank, cnt = routing
    counts = cnt[0, N_GROUPS:N_GROUPS + N_EXPERTS].astype(I32)
    padded = (counts + MOE_BLOCK - 1) // MOE_BLOCK * MOE_BLOCK
    p_ends = jnp.cumsum(padded)
    p_starts = p_ends - padded
    experts = jnp.arange(N_EXPERTS, dtype=I32)
    dest = (jnp.sum(jnp.where(eid[..., None] == experts, p_starts, 0), axis=-1) + rank).reshape(-1)
    xs = moe_dispatch(hp, dest, jnp.stack([p_ends, padded]).astype(I32), cap)
    yb = moe_experts(xs, p_starts // MOE_BLOCK, padded // MOE_BLOCK, w_gate, w_up, w_down, layer)
    return moe_combine(x2d, yb, dest, wt, g_final, final_norm)


def kernel(x, mem, e_norm, e_w_in, e_conv_w, e_conv_b, e_filt_w1, e_filt_b1, e_filt_w2, e_filt_b2, e_filt_w3, e_filt_freq, e_hy_bias, e_lam, e_subln, e_w_out, o_norm, o_w_in, o_sink, o_w_out, c_norm, c_wq, c_wkv, c_wo, f_norm, f_w_grp, f_b_grp, f_w_exp, f_b_exp, f_w_gate, f_w_up, f_w_down, g_mem, g_final):
    b, seq, d = x.shape
    n = b * seq
    m = mem.shape[1]
    tables = dft_tables(seq // 2)
    twiddles = dft_twiddles(seq)
    x2 = x.reshape(n, d)
    mem2 = mem.reshape(b * m, d)
    w_gate = f_w_gate.reshape(DEPTH * N_EXPERTS, d, EXP_HIDDEN)
    w_up = f_w_up.reshape(DEPTH * N_EXPERTS, d, EXP_HIDDEN)
    w_down = f_w_down.reshape(DEPTH * N_EXPERTS, EXP_HIDDEN, d)
    qperm = window_column_perm()
    for i in range(DEPTH):
        j = i // 2
        if i % 2 == 0:
            hyw, qw = 3 * HY_WIDTH, 2 * DIFF_HEADS * DIFF_HEAD_DIM
            w_in = jnp.concatenate([e_w_in[j][:, :hyw],
                                    e_w_in[j][:, hyw:hyw + qw] * (DIFF_HEAD_DIM ** -0.5 * LOG2E),
                                    e_w_in[j][:, hyw + qw:]], axis=1).astype(BF16)
            proj_h, proj_a = norm_matmul(x2, e_norm[j], w_in, 512, [(hyw, F32), (w_in.shape[1] - hyw, BF16)])
            y_hy = hyena_mixer(proj_h.reshape(b, seq, -1), tables, twiddles, e_conv_w[j], e_conv_b[j], e_filt_w1[j],
                               e_filt_b1[j], e_filt_w2[j], e_filt_b2[j], e_filt_w3[j], e_filt_freq[j], e_hy_bias[j])
            y_df = diff_attention(proj_a.reshape(b, seq, -1), e_lam[j], e_subln[j], i)
            w_out = e_w_out[j].astype(BF16)
            mixed, w_mix = [y_hy, y_df], [w_out[:HY_WIDTH], w_out[HY_WIDTH:]]
        else:
            oq = GQA_HEADS * GQA_HD
            w_in = jnp.concatenate([o_w_in[j][:, :oq][:, qperm] * (GQA_HD ** -0.5 * LOG2E), o_w_in[j][:, oq:]],
                                   axis=1).astype(BF16)
            (proj,) = norm_matmul(x2, o_norm[j], w_in, 512, [(w_in.shape[1], BF16)])
            att = window_gqa(proj.reshape(b, seq, -1), o_sink[j])
            mixed, w_mix = [att], [o_w_out[j][qperm].astype(BF16)]
        (kv,) = norm_matmul(mem2, g_mem, c_wkv[i].astype(BF16), 512, [(2 * X_W, BF16)])
        x3, *routing = post_mixer(x2.reshape(b, seq, d), mixed, w_mix, c_norm[i], c_wq[i].astype(BF16),
                                  kv.reshape(b, m, -1), c_wo[i].astype(BF16), f_norm[i], f_w_grp[i], f_b_grp[i],
                                  f_w_exp[i], f_b_exp[i])
        x2 = hier_moe_block(x3.reshape(n, d), routing, w_gate, w_up, w_down, i, g_final, i == DEPTH - 1)
    return x2.reshape(b, seq, d)
```

```python
import functools
import math

import jax
import jax.numpy as jnp
from jax import lax
from jax.experimental import pallas as pl
from jax.experimental.pallas import tpu as pltpu

F32 = jnp.float32
BF16 = jnp.bfloat16
I32 = jnp.int32
U32 = jnp.uint32

D_MODEL = 1024
DEPTH = 4
EPS = 1e-6
NEG = -1e30
HY_WIDTH = 512
HY_BANDS = 16
HY_FILT_HIDDEN = 64
HY_DECAY_TARGET = 1e-2
HY_FAST_PCT = 0.3
HY_SLOW_PCT = 1.5
DIFF_HEADS = 4
DIFF_HEAD_DIM = 64
WIN = 128
GQA_HEADS = 16
GQA_KV = 4
GQA_HD = 64
X_HEADS = 4
X_HD = 128
X_W = X_HEADS * X_HD
N_GROUPS = 4
EXP_PER_GROUP = 8
N_EXPERTS = N_GROUPS * EXP_PER_GROUP
EXP_HIDDEN = 512

LOG2E = 1.4426950408889634
LANES = 128
VMEM_LIMIT = 56 * 1024 * 1024
MOE_BLOCK = 256
DIFF_Q_BLOCK = 256
DIFF_KEY_CHUNK = 512
DIFF_BATCH_PER_STEP = 4
WIN_BATCH_PER_STEP = 4
EXPERT_CHUNK_BLOCKS = 4
ROW_DMA_UNROLL = 8


def _cparams(sem):
    return pltpu.CompilerParams(dimension_semantics=sem, vmem_limit_bytes=VMEM_LIMIT)


def _rms(x, g):
    ms = jnp.mean(x * x, axis=-1, keepdims=True)
    return x * lax.rsqrt(ms + EPS) * g


def _dot(a, b):
    return jnp.dot(a, b, preferred_element_type=F32)


def _dot_nt(a, b):
    return lax.dot_general(a, b, (((1,), (1,)), ((), ())), preferred_element_type=F32)


def _norm_matmul_kernel(x_ref, g_ref, w_ref, *o_refs):
    h = _rms(x_ref[...], g_ref[...]).astype(BF16)
    col = 0
    for o_ref in o_refs:
        width = o_ref.shape[1]
        o_ref[...] = _dot(h, w_ref[:, col:col + width]).astype(o_ref.dtype)
        col += width


def norm_matmul(x2d, gamma, w_bf16, tm, outs):
    n, d = x2d.shape
    f = w_bf16.shape[1]
    assert sum(width for width, _ in outs) == f
    res = pl.pallas_call(
        _norm_matmul_kernel,
        grid=(n // tm,),
        in_specs=[pl.BlockSpec((tm, d), lambda i: (i, 0)),
                  pl.BlockSpec((1, d), lambda i: (0, 0)),
                  pl.BlockSpec((d, f), lambda i: (0, 0))],
        out_specs=[pl.BlockSpec((tm, width), lambda i: (i, 0)) for width, _ in outs],
        out_shape=[jax.ShapeDtypeStruct((n, width), dtype) for width, dtype in outs],
        compiler_params=_cparams(("parallel",)),
        name="norm_matmul",
    )(x2d, gamma.reshape(1, d), w_bf16)
    return res


def _conv3_kernel(u_ref, w_ref, b_ref, oe_ref, oo_ref, slab_ref):
    half = u_ref.shape[1] // 2
    w = w_ref[...]
    bias = b_ref[...]
    row = lax.broadcasted_iota(I32, (half, LANES), 0)
    for k in range(u_ref.shape[2] // LANES):
        cols = slice(k * LANES, (k + 1) * LANES)
        slab_ref[...] = u_ref[0, :, cols]
        ue = slab_ref[pl.ds(0, half, stride=2), :]
        uo = slab_ref[pl.ds(1, half, stride=2), :]
        uo_prev = jnp.where(row == 0, 0.0, pltpu.roll(uo, 1, 0))
        ue_next = jnp.where(row == half - 1, 0.0, pltpu.roll(ue, half - 1, 0))
        oe_ref[0, :, cols] = uo_prev * w[0:1, cols] + ue * w[1:2, cols] + uo * w[2:3, cols] + bias[:, cols]
        oo_ref[0, :, cols] = ue * w[0:1, cols] + uo * w[1:2, cols] + ue_next * w[2:3, cols] + bias[:, cols]


def conv3(proj3d, conv_w, conv_b):
    b, seq, c = proj3d.shape
    tc = HY_WIDTH
    out = pl.BlockSpec((1, seq // 2, tc), lambda i, j: (i, 0, j))
    return pl.pallas_call(
        _conv3_kernel,
        grid=(b, c // tc),
        in_specs=[pl.BlockSpec((1, seq, tc), lambda i, j: (i, 0, j)),
                  pl.BlockSpec((3, tc), lambda i, j: (0, j)),
                  pl.BlockSpec((1, tc), lambda i, j: (0, j))],
        out_specs=[out, out],
        out_shape=[jax.ShapeDtypeStruct((b, seq // 2, c), F32)] * 2,
        scratch_shapes=[pltpu.VMEM((seq, LANES), F32)],
        compiler_params=_cparams(("parallel", "parallel")),
        name="hyena_conv3",
    )(proj3d, conv_w, conv_b.reshape(1, c))


def _hy_filter_kernel(t_ref, bands_ref, w1t_ref, w1c_ref, w1s_ref, b1_ref, w2_ref, b2_ref, w3_ref,
                      freq_ref, delta_ref, hf_ref, hb_ref, *, seq, tl):
    hi = lax.Precision.HIGHEST
    i = pl.program_id(0)
    t = t_ref[...]
    pos = (i * tl + lax.broadcasted_iota(I32, (tl, 1), 0)).astype(F32)
    ang = bands_ref[...] * (2.0 * math.pi * pos / seq)
    f = freq_ref[...]
    pre = (t * w1t_ref[...]
           + jnp.dot(jnp.cos(ang), w1c_ref[...], precision=hi, preferred_element_type=F32)
           - jnp.dot(jnp.sin(ang), w1s_ref[...], precision=hi, preferred_element_type=F32)
           + b1_ref[...])
    a = jnp.sin(f * pre)
    a = jnp.sin(f * (jnp.dot(a, w2_ref[...], precision=hi, preferred_element_type=F32) + b2_ref[...]))
    h = jnp.dot(a, w3_ref[...], precision=hi, preferred_element_type=F32)
    decay = jnp.exp(-t * jnp.abs(delta_ref[...]))
    w = HY_WIDTH
    for o in range(2):
        hf_ref[:, o * w:(o + 1) * w] = h[:, o * 2 * w:o * 2 * w + w] * decay
        hb_ref[:, o * w:(o + 1) * w] = jnp.where(pos == 0.0, 0.0, h[:, o * 2 * w + w:(o + 1) * 2 * w] * decay)


def hyena_filters_time(seq, w1, b1, w2, b2, w3, freq):
    tl = 512
    hid = LANES
    pad_h = hid - HY_FILT_HIDDEN
    t = jnp.linspace(0.0, 1.0, seq, dtype=F32)[:, None]
    bands = jnp.pad(jnp.linspace(1e-4, HY_BANDS - 1, HY_BANDS, dtype=F32)[None], ((0, 0), (0, LANES - HY_BANDS)))
    w1p = jnp.pad(w1, ((0, 0), (0, pad_h)))
    w1t = w1p[0:1]
    w1c = jnp.pad(w1p[1:1 + HY_BANDS], ((0, LANES - HY_BANDS), (0, 0)))
    w1s = jnp.pad(w1p[1 + HY_BANDS:], ((0, LANES - HY_BANDS), (0, 0)))
    b1p = jnp.pad(b1, (0, pad_h)).reshape(1, hid)
    w2p = jnp.pad(w2, ((0, pad_h), (0, pad_h)))
    b2p = jnp.pad(b2, (0, pad_h)).reshape(1, hid)
    w3p = jnp.pad(w3, ((0, pad_h), (0, 0)))
    freqp = jnp.pad(freq, (0, pad_h)).reshape(1, hid)
    max_decay = math.log(HY_DECAY_TARGET) / HY_FAST_PCT
    min_decay = math.log(HY_DECAY_TARGET) / HY_SLOW_PCT
    deltas = jnp.linspace(min_decay, max_decay, HY_WIDTH, dtype=F32)[None]
    fw = w3.shape[1]
    full = lambda shape: pl.BlockSpec(shape, lambda i: (0, 0))
    return pl.pallas_call(
        functools.partial(_hy_filter_kernel, seq=seq, tl=tl),
        grid=(seq // tl,),
        in_specs=[pl.BlockSpec((tl, 1), lambda i: (i, 0)), full((1, LANES)), full((1, hid)),
                  full((LANES, hid)), full((LANES, hid)), full((1, hid)), full((hid, hid)), full((1, hid)),
                  full((hid, fw)), full((1, hid)), full((1, HY_WIDTH))],
        out_specs=[pl.BlockSpec((tl, 2 * HY_WIDTH), lambda i: (i, 0)),
                   pl.BlockSpec((tl, 2 * HY_WIDTH), lambda i: (i, 0))],
        out_shape=[jax.ShapeDtypeStruct((seq, 2 * HY_WIDTH), F32)] * 2,
        compiler_params=_cparams(("parallel",)),
        name="hyena_filter_mlp",
    )(t, bands, w1t, w1c, w1s, b1p, w2p, b2p, w3p, freqp, deltas)


def dft_tables(seq):
    n2 = 4 * seq
    sub = 64
    f = jnp.arange(seq, dtype=I32)[:, None]
    odd = 2 * f + 1
    s1 = jnp.arange(seq // sub, dtype=I32)[None]
    s0 = jnp.arange(sub, dtype=I32)[None]
    ang_p = ((odd * s1) % (n2 // sub)).astype(F32) * (2.0 * math.pi * sub / n2)
    ang_q = ((odd * s0) % n2).astype(F32) * (2.0 * math.pi / n2)
    pc, ps, qc, qs = jnp.cos(ang_p), jnp.sin(ang_p), jnp.cos(ang_q), jnp.sin(ang_q)
    c = (pc[:, :, None] * qc[:, None, :] - ps[:, :, None] * qs[:, None, :]).reshape(seq, seq)
    s = (ps[:, :, None] * qc[:, None, :] + pc[:, :, None] * qs[:, None, :]).reshape(seq, seq)
    return c.astype(BF16), s.astype(BF16), c.T.astype(BF16), s.T.astype(BF16)


def dft_twiddles(seq):
    g = jnp.arange(seq // 2, dtype=I32)[:, None]
    ang = (2 * g + 1).astype(F32) * (2.0 * math.pi / (4 * seq))
    return jnp.cos(ang), jnp.sin(ang)


def _half_transform(c1, s1, cg, sg, x0, x1):
    pc0, ps0 = _dot(c1, x0), _dot(s1, x0)
    pc1, ps1 = _dot(c1, x1), _dot(s1, x1)
    tr = cg * pc1 - sg * ps1
    ti = -(cg * ps1 + sg * pc1)
    return pc0 + tr, ti - ps0, pc0 - tr, ps0 + ti


def _spectrum_kernel(c_ref, s_ref, cg_ref, sg_ref, f0_ref, f1_ref, b0_ref, b1_ref,
                     kgre_ref, kgim_ref, khre_ref, khim_ref):
    c1, s1, cg, sg = c_ref[...], s_ref[...], cg_ref[...], sg_ref[...]
    fre_g, fim_g, fre_h, fim_h = _half_transform(c1, s1, cg, sg, f0_ref[...].astype(BF16), f1_ref[...].astype(BF16))
    bre_g, bim_g, bre_h, bim_h = _half_transform(c1, s1, cg, sg, b0_ref[...].astype(BF16), b1_ref[...].astype(BF16))
    kgre_ref[...] = fre_g + bre_g
    kgim_ref[...] = fim_g - bim_g
    khre_ref[...] = fre_h + bre_h
    khim_ref[...] = fim_h - bim_h


def filter_spectrum(tables, twiddles, hf, hb):
    seq, cols = hf.shape
    half = seq // 2
    tf, tn = 512, 512
    nj = cols // tn
    hf2 = hf.reshape(half, 2 * cols)
    hb2 = hb.reshape(half, 2 * cols)
    tab = pl.BlockSpec((tf, half), lambda i, j: (i, 0))
    twd = pl.BlockSpec((tf, 1), lambda i, j: (i, 0))
    even = pl.BlockSpec((half, tn), lambda i, j: (0, j))
    odd = pl.BlockSpec((half, tn), lambda i, j: (0, nj + j))
    return pl.pallas_call(
        _spectrum_kernel,
        grid=(half // tf, nj),
        in_specs=[tab, tab, twd, twd, even, odd, even, odd],
        out_specs=[pl.BlockSpec((tf, tn), lambda i, j: (i, j))] * 4,
        out_shape=[jax.ShapeDtypeStruct((half, cols), F32)] * 4,
        compiler_params=_cparams(("parallel", "parallel")),
        name="hyena_filter_spectrum",
    )(tables[0], tables[1], twiddles[0], twiddles[1], hf2, hf2, hb2, hb2)


def _hy_fwd_kernel(z0_ref, z1_ref, c_ref, s_ref, cg_ref, sg_ref, kgre_ref, kgim_ref, khre_ref, khim_ref,
                   u0re_ref, u0im_ref, u1re_ref, u1im_ref):
    cg, sg = cg_ref[...], sg_ref[...]
    zre_g, zim_g, zre_h, zim_h = _half_transform(c_ref[...], s_ref[...], cg, sg,
                                                 z0_ref[0].astype(BF16), z1_ref[0].astype(BF16))
    kgre, kgim, khre, khim = kgre_ref[...], kgim_ref[...], khre_ref[...], khim_ref[...]
    yre_g = zre_g * kgre - zim_g * kgim
    yim_g = zre_g * kgim + zim_g * kgre
    yre_h = zre_h * khre - zim_h * khim
    yim_h = zre_h * khim + zim_h * khre
    u0re_ref[0] = (yre_g + yre_h).astype(BF16)
    u0im_ref[0] = (yim_g - yim_h).astype(BF16)
    a = yre_g - yre_h
    b = yim_g + yim_h
    u1re_ref[0] = (cg * a - sg * b).astype(BF16)
    u1im_ref[0] = (cg * b + sg * a).astype(BF16)


def hyena_fwd(zsrc, tables, twiddles, kspec, order):
    (ze, ce), (zo, co) = zsrc
    b, half, _ = ze.shape
    w = HY_WIDTH
    tf = 512
    tab = pl.BlockSpec((tf, half), lambda i, j: (j, 0))
    twd = pl.BlockSpec((tf, 1), lambda i, j: (j, 0))
    kblk = pl.BlockSpec((tf, w), lambda i, j: (j, order))
    return pl.pallas_call(
        _hy_fwd_kernel,
        grid=(b, half // tf),
        in_specs=[pl.BlockSpec((1, half, w), lambda i, j: (i, 0, ce)),
                  pl.BlockSpec((1, half, w), lambda i, j: (i, 0, co)),
                  tab, tab, twd, twd, kblk, kblk, kblk, kblk],
        out_specs=[pl.BlockSpec((1, tf, w), lambda i, j: (i, j, 0))] * 4,
        out_shape=[jax.ShapeDtypeStruct((b, half, w), BF16)] * 4,
        compiler_params=_cparams(("parallel", "parallel")),
        name="hyena_dft_fwd",
    )(ze, zo, tables[0], tables[1], twiddles[0], twiddles[1], *kspec)


def _hy_inv_kernel(ct_ref, st_ref, u0re_ref, u0im_ref, u1re_ref, u1im_ref, z0_ref, z1_ref, g0_ref, g1_ref,
                   bias_ref, o_ref, *, scale):
    ct, st = ct_ref[...], st_ref[...]
    w = z0_ref.shape[2]
    y0 = (_dot(ct, u0re_ref[0]) - _dot(st, u0im_ref[0])) * scale
    y1 = (_dot(ct, u1re_ref[0]) - _dot(st, u1im_ref[0])) * scale
    bias = bias_ref[...]
    o_ref[0, :, 0:w] = g0_ref[0] * (y0 + z0_ref[0] * bias)
    o_ref[0, :, w:2 * w] = g1_ref[0] * (y1 + z1_ref[0] * bias)


def hyena_inv(tables, u, zsrc, gsrc, bias_row):
    b, half, w = u[0].shape
    tt = 512
    tab = pl.BlockSpec((tt, half), lambda i, j: (j, 0))
    ublk = pl.BlockSpec((1, half, w), lambda i, j: (i, 0, 0))

    def rows(col):
        return pl.BlockSpec((1, tt, w), lambda i, j: (i, j, col))

    (ze, ce), (zo, co) = zsrc
    (ge, gce), (go, gco) = gsrc
    return pl.pallas_call(
        functools.partial(_hy_inv_kernel, scale=0.5 / half),
        grid=(b, half // tt),
        in_specs=[tab, tab, ublk, ublk, ublk, ublk, rows(ce), rows(co), rows(gce), rows(gco),
                  pl.BlockSpec((1, w), lambda i, j: (0, 0))],
        out_specs=pl.BlockSpec((1, tt, 2 * w), lambda i, j: (i, j, 0)),
        out_shape=jax.ShapeDtypeStruct((b, half, 2 * w), F32),
        compiler_params=_cparams(("parallel", "parallel")),
        name="hyena_dft_inv",
    )(tables[2], tables[3], *u, ze, zo, ge, go, bias_row)


def hyena_mixer(proj3d, tables, twiddles, conv_w, conv_b, w1, b1, w2, b2, w3, freq, hy_bias):
    seq = proj3d.shape[1]
    ue, uo = conv3(proj3d, conv_w, conv_b)
    hf, hb = hyena_filters_time(seq, w1, b1, w2, b2, w3, freq)
    kspec = filter_spectrum(tables, twiddles, hf, hb)
    v = ((ue, 0), (uo, 0))
    uu = hyena_fwd(v, tables, twiddles, kspec, 0)
    z1 = hyena_inv(tables, uu, v, ((ue, 1), (uo, 1)), hy_bias[0:1])
    z = ((z1, 0), (z1, 1))
    uu = hyena_fwd(z, tables, twiddles, kspec, 1)
    return hyena_inv(tables, uu, z, ((ue, 2), (uo, 2)), hy_bias[1:2])


def _diff_attn_kernel(slope_ref, q_ref, k_ref, v_ref, lam_ref, sub_ref, o_ref, bias_ref, *, tq, lam_init):
    h = pl.program_id(0)
    qi = pl.program_id(1)
    seq = k_ref.shape[1]

    @pl.when(pl.program_id(2) == 0)
    def _():
        qpos = qi * tq + lax.broadcasted_iota(I32, (tq, seq), 0)
        kpos = lax.broadcasted_iota(I32, (tq, seq), 1)
        bias_ref[...] = slope_ref[h] * jnp.abs(qpos - kpos).astype(F32)

    lane = lax.broadcasted_iota(I32, (1, 2 * DIFF_HEAD_DIM), 1)
    ck = DIFF_KEY_CHUNK
    l = lam_ref[...]
    lam_full = (jnp.exp(jnp.sum(l[0:1] * l[1:2], axis=-1, keepdims=True))
                - jnp.exp(jnp.sum(l[2:3] * l[3:4], axis=-1, keepdims=True)) + lam_init)

    def attend(bb, m):
        q = q_ref[bb]
        k = k_ref[bb]
        v = v_ref[bb]
        keep = (lane < DIFF_HEAD_DIM) if m == 0 else (lane >= DIFF_HEAD_DIM)
        qm = jnp.where(keep, q, jnp.zeros_like(q))
        s = [_dot_nt(qm, k[c:c + ck]) - bias_ref[:, c:c + ck] for c in range(0, seq, ck)]
        mx = functools.reduce(jnp.maximum, [jnp.max(sc, axis=-1, keepdims=True) for sc in s])
        acc = jnp.zeros((tq, 2 * DIFF_HEAD_DIM), F32)
        den = jnp.zeros((tq, 1), F32)
        for i, sc in enumerate(s):
            e = jnp.exp2(sc - mx)
            den = den + jnp.sum(e, axis=-1, keepdims=True)
            acc = acc + _dot(e.astype(BF16), v[i * ck:(i + 1) * ck])
        return acc / den

    for bb in range(q_ref.shape[0]):
        o = attend(bb, 0) - lam_full * attend(bb, 1)
        o_ref[bb] = _rms(o, sub_ref[...]) * (1.0 - lam_init)


def diff_attention(proj3d, lam, subln, layer_idx):
    b, seq, _ = proj3d.shape
    tq = DIFF_Q_BLOCK
    nb = DIFF_BATCH_PER_STEP
    hw = 2 * DIFF_HEAD_DIM
    qb, kb, vb = 0, DIFF_HEADS, 2 * DIFF_HEADS
    lam_init = 0.8 - 0.6 * math.exp(-0.3 * layer_idx)
    slopes = 2.0 ** (-8.0 * jnp.arange(1, DIFF_HEADS + 1, dtype=F32) / DIFF_HEADS) * LOG2E
    return pl.pallas_call(
        functools.partial(_diff_attn_kernel, tq=tq, lam_init=lam_init),
        grid=(DIFF_HEADS, seq // tq, b // nb),
        in_specs=[pl.BlockSpec(memory_space=pltpu.SMEM),
                  pl.BlockSpec((nb, tq, hw), lambda h, j, i: (i, j, qb + h)),
                  pl.BlockSpec((nb, seq, hw), lambda h, j, i: (i, 0, kb + h)),
                  pl.BlockSpec((nb, seq, hw), lambda h, j, i: (i, 0, vb + h)),
                  pl.BlockSpec((4, DIFF_HEAD_DIM), lambda h, j, i: (0, 0)),
                  pl.BlockSpec((1, hw), lambda h, j, i: (0, 0))],
        out_specs=pl.BlockSpec((nb, tq, hw), lambda h, j, i: (i, j, h)),
        out_shape=jax.ShapeDtypeStruct((b, seq, DIFF_HEADS * hw), F32),
        scratch_shapes=[pltpu.VMEM((tq, seq), F32)],
        compiler_params=_cparams(("parallel", "parallel", "arbitrary")),
        name="diff_attention",
    )(slopes, proj3d, proj3d, proj3d, lam, subln.reshape(1, hw))


def _win_attn_kernel(slope_ref, sink_ref, q_ref, kp_ref, kc_ref, kn_ref, vp_ref, vc_ref, vn_ref, o_ref, bias_ref,
                     *, tq, seq):
    qi = pl.program_id(0)
    span = 3 * tq
    group = GQA_HEADS // GQA_KV

    @pl.when(pl.program_id(1) == 0)
    def _():
        qpos = qi * tq + lax.broadcasted_iota(I32, (tq, span), 0)
        kpos = (qi - 1) * tq + lax.broadcasted_iota(I32, (tq, span), 1)
        rel = jnp.abs(qpos - kpos)
        relf = rel.astype(F32)
        masked = jnp.where((rel <= WIN) & (kpos >= 0) & (kpos < seq), 0.0, -NEG)
        for head in range(GQA_HEADS):
            bias_ref[head] = slope_ref[head] * relf + masked

    low = lax.broadcasted_iota(I32, (1, LANES), 1) < GQA_HD
    for bb in range(q_ref.shape[0]):
        kwin = jnp.concatenate([kp_ref[bb], kc_ref[bb], kn_ref[bb]], axis=0)
        vwin = jnp.concatenate([vp_ref[bb], vc_ref[bb], vn_ref[bb]], axis=0)
        for p in range(GQA_KV // 2):
            kb = kwin[:, p * LANES:(p + 1) * LANES]
            vb = vwin[:, p * LANES:(p + 1) * LANES]
            outs = []
            for half in range(2):
                kv = 2 * p + half
                mine = low if half == 0 else jnp.logical_not(low)
                qblocks = [q_ref[bb, :, (group * p + r) * LANES:(group * p + r + 1) * LANES] for r in range(group)]
                qs = jnp.concatenate([jnp.where(mine, qb, jnp.zeros_like(qb)) for qb in qblocks], axis=0)
                s = _dot_nt(qs, kb)
                es, inv = [], []
                for r in range(group):
                    head = kv * group + r
                    sink = sink_ref[head]
                    sr = s[r * tq:(r + 1) * tq] - bias_ref[head]
                    m = jnp.maximum(jnp.max(sr, axis=-1, keepdims=True), sink)
                    e = jnp.exp2(sr - m)
                    inv.append(1.0 / (jnp.sum(e, axis=-1, keepdims=True) + jnp.exp2(sink - m)))
                    es.append(e.astype(BF16))
                o = _dot(jnp.concatenate(es, axis=0), vb)
                outs.append([o[r * tq:(r + 1) * tq] * inv[r] for r in range(group)])
            for r in range(group):
                col = (group * p + r) * LANES
                o_ref[bb, :, col:col + LANES] = jnp.where(low, outs[0][r], outs[1][r])


def window_head_order():
    group = GQA_HEADS // GQA_KV
    order = []
    for p in range(GQA_KV // 2):
        for r in range(group):
            order += [(2 * p) * group + r, (2 * p + 1) * group + r]
    return order


def window_column_perm():
    cols = []
    for head in window_head_order():
        cols += list(range(head * GQA_HD, (head + 1) * GQA_HD))
    return jnp.asarray(cols, dtype=I32)


def window_gqa(proj3d, sink):
    b, seq, _ = proj3d.shape
    tq = WIN
    nq = seq // tq
    oq = GQA_HEADS * GQA_HD
    okv = GQA_KV * GQA_HD
    kcol, vcol = oq // okv, oq // okv + 1
    slopes = 2.0 ** (-8.0 * jnp.arange(1, GQA_HEADS + 1, dtype=F32) / GQA_HEADS) * LOG2E

    nb = WIN_BATCH_PER_STEP

    def neighbour(col, step):
        return pl.BlockSpec((nb, tq, okv), lambda j, i: (i, jnp.clip(j + step, 0, nq - 1), col))

    return pl.pallas_call(
        functools.partial(_win_attn_kernel, tq=tq, seq=seq),
        grid=(nq, b // nb),
        in_specs=[pl.BlockSpec(memory_space=pltpu.SMEM),
                  pl.BlockSpec(memory_space=pltpu.SMEM),
                  pl.BlockSpec((nb, tq, oq), lambda j, i: (i, j, 0)),
                  neighbour(kcol, -1), neighbour(kcol, 0), neighbour(kcol, 1),
                  neighbour(vcol, -1), neighbour(vcol, 0), neighbour(vcol, 1)],
        out_specs=pl.BlockSpec((nb, tq, oq), lambda j, i: (i, j, 0)),
        out_shape=jax.ShapeDtypeStruct((b, seq, oq), F32),
        scratch_shapes=[pltpu.VMEM((GQA_HEADS, tq, 3 * tq), F32)],
        compiler_params=_cparams(("parallel", "arbitrary")),
        name="window_gqa",
    )(slopes, sink.astype(F32) * LOG2E, proj3d, proj3d, proj3d, proj3d, proj3d, proj3d, proj3d)


def _cross_attend(x, g, wq, kv, wo):
    h = _rms(x, g).astype(BF16)
    q = _dot(h, wq)
    scale = X_HD ** -0.5
    outs = []
    for hd in range(X_HEADS):
        qh = q[:, hd * X_HD:(hd + 1) * X_HD].astype(BF16)
        kh = kv[:, hd * X_HD:(hd + 1) * X_HD]
        vh = kv[:, X_W + hd * X_HD:X_W + (hd + 1) * X_HD]
        s = _dot_nt(qh, kh) * scale
        e = jnp.exp(s - jnp.max(s, axis=-1, keepdims=True))
        p = e / jnp.sum(e, axis=-1, keepdims=True)
        outs.append(_dot(p.astype(BF16), vh))
    o = jnp.concatenate(outs, axis=-1).astype(BF16)
    return x + _dot(o, wo)


def _split_bf16(x):
    hi = x.astype(BF16)
    lo = (x - hi.astype(F32)).astype(BF16)
    return hi, lo


def _pack_halves(h):
    c = h.shape[1] // 2
    left = lax.bitcast_convert_type(h[:, :c].astype(BF16).astype(F32), U32)
    right = lax.bitcast_convert_type(h[:, c:].astype(BF16).astype(F32), U32)
    return left | (right >> 16)


def _unpack_halves(p):
    left = lax.bitcast_convert_type(p & jnp.uint32(0xFFFF0000), F32)
    right = lax.bitcast_convert_type(p << 16, F32)
    return jnp.concatenate([left, right], axis=-1)


def _route(x, g, whi, wlo, bias, hp_ref, eid_ref, wt_ref, rank_ref, cnt_ref, base_ref):
    tm = x.shape[0]
    h = _rms(x, g)
    hp_ref[...] = _pack_halves(h)
    hhi, hlo = _split_bf16(h)
    logits = _dot(hhi, whi) + _dot(hlo, whi) + _dot(hhi, wlo) + bias
    lane = lax.broadcasted_iota(I32, logits.shape, 1)
    big = jnp.int32(LANES)
    ninf = -jnp.inf

    gl = jnp.where(lane < N_GROUPS, logits, ninf)
    gmax = jnp.max(gl, axis=-1, keepdims=True)
    gsel = jnp.min(jnp.where(gl == gmax, lane, big), axis=-1, keepdims=True)
    ggate = 1.0 / jnp.sum(jnp.exp(gl - gmax), axis=-1, keepdims=True)

    lo_lane = N_GROUPS + gsel * EXP_PER_GROUP
    el = jnp.where((lane >= lo_lane) & (lane < lo_lane + EXP_PER_GROUP), logits, ninf)
    v1 = jnp.max(el, axis=-1, keepdims=True)
    i1 = jnp.min(jnp.where(el == v1, lane, big), axis=-1, keepdims=True)
    el2 = jnp.where(lane == i1, ninf, el)
    v2 = jnp.max(el2, axis=-1, keepdims=True)
    i2 = jnp.min(jnp.where(el2 == v2, lane, big), axis=-1, keepdims=True)
    e2 = jnp.exp(v2 - v1)
    w1 = ggate / (1.0 + e2)
    w2 = ggate * e2 / (1.0 + e2)

    one1 = lane == i1
    one2 = lane == i2
    onehot = (one1 | one2).astype(F32)
    r = lax.broadcasted_iota(I32, (tm, tm), 0)
    c = lax.broadcasted_iota(I32, (tm, tm), 1)
    tri = (c < r).astype(BF16)
    before = _dot(tri, onehot.astype(BF16)) + base_ref[...]
    rank1 = jnp.sum(jnp.where(one1, before, 0.0), axis=-1, keepdims=True)
    rank2 = jnp.sum(jnp.where(one2, before, 0.0), axis=-1, keepdims=True)
    total = base_ref[...] + jnp.sum(onehot, axis=0, keepdims=True)
    base_ref[...] = total
    cnt_ref[...] = total

    col = lax.broadcasted_iota(I32, (tm, 2), 1)
    eid_ref[...] = jnp.where(col == 0, i1, i2) - N_GROUPS
    wt_ref[...] = jnp.where(col == 0, w1, w2)
    rank_ref[...] = jnp.where(col == 0, rank1, rank2).astype(I32)


def _interleave_rows(pair):
    t, w2 = pair.shape
    w = w2 // 2
    r = lax.broadcasted_iota(I32, (2 * t, t), 0)
    c = lax.broadcasted_iota(I32, (2 * t, t), 1)
    pick_even = (r == 2 * c).astype(BF16)
    pick_odd = (r == 2 * c + 1).astype(BF16)
    return (_dot(pick_even, pair[:, :w]) + _dot(pick_odd, pair[:, w:])).astype(BF16)


def _post_mixer_kernel(*refs, n_lhs, paired):
    x_ref = refs[0]
    a_refs = refs[1:1 + n_lhs]
    w_refs = refs[1 + n_lhs:1 + 2 * n_lhs]
    (cg_ref, wq_ref, kv_ref, wo_ref, fg_ref, whi_ref, wlo_ref, rb_ref,
     x_out_ref, hp_ref, eid_ref, wt_ref, rank_ref, cnt_ref, base_ref) = refs[1 + 2 * n_lhs:]

    @pl.when((pl.program_id(0) == 0) & (pl.program_id(1) == 0))
    def _():
        base_ref[...] = jnp.zeros_like(base_ref)

    x = x_ref[0]
    for a_ref, w_ref, is_paired in zip(a_refs, w_refs, paired):
        a = a_ref[0].astype(BF16)
        x = x + _dot(_interleave_rows(a) if is_paired else a, w_ref[...])
    x = _cross_attend(x, cg_ref[...], wq_ref[...], kv_ref[0], wo_ref[...])
    x_out_ref[0] = x
    _route(x, fg_ref[...], whi_ref[...], wlo_ref[...], rb_ref[...], hp_ref, eid_ref, wt_ref, rank_ref, cnt_ref,
           base_ref)


def post_mixer(x3d, lhs_list, w_list, c_gamma, wq, kv, wo, f_gamma, w_grp, b_grp, w_exp, b_exp):
    b, seq, d = x3d.shape
    paired = tuple(a.shape[1] != seq for a in lhs_list)
    n = b * seq
    m = kv.shape[1]
    tm = 512
    nt = seq // tm
    wcat = jnp.pad(jnp.concatenate([w_grp, w_exp], axis=1), ((0, 0), (0, LANES - N_GROUPS - N_EXPERTS)))
    bcat = jnp.pad(jnp.concatenate([b_grp, b_exp]), (0, LANES - N_GROUPS - N_EXPERTS)).reshape(1, LANES)
    whi = wcat.astype(BF16)
    wlo = (wcat - whi.astype(F32)).astype(BF16)
    const = lambda shape: pl.BlockSpec(shape, lambda i, j: (0,) * len(shape))
    tok = lambda width: pl.BlockSpec((tm, width), lambda i, j: (i * nt + j, 0))
    in_specs = [pl.BlockSpec((1, tm, d), lambda i, j: (i, j, 0))]
    in_specs += [pl.BlockSpec((1, tm // 2 if p else tm, a.shape[2]), lambda i, j: (i, j, 0))
                 for a, p in zip(lhs_list, paired)]
    in_specs += [const(w.shape) for w in w_list]
    in_specs += [const((1, d)), const((d, X_W)), pl.BlockSpec((1, m, 2 * X_W), lambda i, j: (i, 0, 0)),
                 const((X_W, d)), const((1, d)), const((d, LANES)), const((d, LANES)), const((1, LANES))]
    return pl.pallas_call(
        functools.partial(_post_mixer_kernel, n_lhs=len(lhs_list), paired=paired),
        grid=(b, nt),
        in_specs=in_specs,
        out_specs=[pl.BlockSpec((1, tm, d), lambda i, j: (i, j, 0)), tok(d // 2), tok(2), tok(2), tok(2),
                   const((1, LANES))],
        out_shape=[jax.ShapeDtypeStruct((b, seq, d), F32),
                   jax.ShapeDtypeStruct((n, d // 2), U32),
                   jax.ShapeDtypeStruct((n, 2), I32),
                   jax.ShapeDtypeStruct((n, 2), F32),
                   jax.ShapeDtypeStruct((n, 2), I32),
                   jax.ShapeDtypeStruct((1, LANES), F32)],
        scratch_shapes=[pltpu.VMEM((1, LANES), F32)],
        compiler_params=_cparams(("arbitrary", "arbitrary")),
        name="post_mixer",
    )(x3d, *lhs_list, *w_list, c_gamma.reshape(1, d), wq, kv, wo, f_gamma.reshape(1, d), whi, wlo, bcat)


def _row_copy(src_ref, src_row, dst_ref, dst_row, sem):
    return pltpu.make_async_copy(src_ref.at[pl.ds(src_row, 1)], dst_ref.at[pl.ds(dst_row, 1)], sem)


def _dispatch_kernel(dest_ref, seg_ref, hp_ref, xs_ref, zbuf, sem, zsem, *, tm):
    @pl.when(pl.program_id(0) == 0)
    def _():
        zbuf[...] = jnp.zeros_like(zbuf)

        def zero_tail(e):
            tail = pl.multiple_of(seg_ref[0, e] - MOE_BLOCK, MOE_BLOCK)
            return pltpu.make_async_copy(zbuf, xs_ref.at[pl.ds(tail, MOE_BLOCK)], zsem)

        def start(e, carry):
            @pl.when(seg_ref[1, e] > 0)
            def _():
                zero_tail(e).start()
            return carry

        def finish(e, carry):
            @pl.when(seg_ref[1, e] > 0)
            def _():
                zero_tail(e).wait()
            return carry

        lax.fori_loop(0, N_EXPERTS, start, 0)
        lax.fori_loop(0, N_EXPERTS, finish, 0)

        def unused(blk):
            return pltpu.make_async_copy(zbuf, xs_ref.at[pl.ds(blk * MOE_BLOCK, MOE_BLOCK)], zsem)

        def start_unused(blk, carry):
            unused(blk).start()
            return carry

        def finish_unused(blk, carry):
            unused(blk).wait()
            return carry

        first_unused = seg_ref[0, N_EXPERTS - 1] // MOE_BLOCK
        lax.fori_loop(first_unused, xs_ref.shape[0] // MOE_BLOCK, start_unused, 0)
        lax.fori_loop(first_unused, xs_ref.shape[0] // MOE_BLOCK, finish_unused, 0)

    def issue(r, carry):
        _row_copy(hp_ref, r, xs_ref, dest_ref[2 * r], sem).start(priority=0)
        _row_copy(hp_ref, r, xs_ref, dest_ref[2 * r + 1], sem).start(priority=1)
        return carry

    lax.fori_loop(0, tm, issue, 0, unroll=ROW_DMA_UNROLL)
    for _ in range(2):
        pltpu.make_async_copy(hp_ref, xs_ref.at[pl.ds(0, tm)], sem).wait()


def moe_dispatch(hp, dest_flat, segments, cap):
    n, c = hp.shape
    tm = 512
    return pl.pallas_call(
        functools.partial(_dispatch_kernel, tm=tm),
        grid=(n // tm,),
        in_specs=[pl.BlockSpec((2 * tm,), lambda i: (i,), memory_space=pltpu.SMEM),
                  pl.BlockSpec(memory_space=pltpu.SMEM),
                  pl.BlockSpec((tm, c), lambda i: (i, 0))],
        out_specs=pl.BlockSpec(memory_space=pl.ANY),
        out_shape=jax.ShapeDtypeStruct((cap, c), U32),
        scratch_shapes=[pltpu.VMEM((MOE_BLOCK, c), U32), pltpu.SemaphoreType.DMA(()), pltpu.SemaphoreType.DMA(())],
        compiler_params=_cparams(("arbitrary",)),
        name="moe_dispatch",
    )(dest_flat, segments, hp)


def _expert_kernel(start_ref, cnt_ref, xs_ref, wg_ref, wu_ref, wd_ref, yb_ref, xbuf, ybuf, sem_in, sem_out,
                   wg_s, wu_s, wd_s, *, nblk):
    e = pl.program_id(0)
    nb = cnt_ref[e]
    b0 = start_ref[e]

    @pl.when(nb > 0)
    def _():
        wg_s[...] = wg_ref[0].astype(BF16)
        wu_s[...] = wu_ref[0].astype(BF16)
        wd_s[...] = wd_ref[0].astype(BF16)

    def run_chunks(first_blk, count, blocks_per_chunk):
        rows = blocks_per_chunk * MOE_BLOCK

        def fetch(i, slot):
            start = (first_blk + i * blocks_per_chunk) * MOE_BLOCK
            return pltpu.make_async_copy(xs_ref.at[pl.ds(start, rows)], xbuf.at[slot, pl.ds(0, rows)],
                                         sem_in.at[slot])

        def put(i, slot):
            start = (first_blk + i * blocks_per_chunk) * MOE_BLOCK
            return pltpu.make_async_copy(ybuf.at[slot, pl.ds(0, rows)], yb_ref.at[pl.ds(start, rows)],
                                         sem_out.at[slot])

        @pl.when(count > 0)
        def _():
            fetch(0, 0).start(priority=1)

        def chunk(i, carry):
            slot = i & 1
            fetch(i, slot).wait()

            @pl.when(i + 1 < count)
            def _():
                fetch(i + 1, 1 - slot).start(priority=1)

            @pl.when(i >= 2)
            def _():
                put(i - 2, slot).wait()

            x = _unpack_halves(xbuf[slot, 0:rows]).astype(BF16)
            a = _dot(x, wg_s[...])
            u = _dot(x, wu_s[...])
            hmid = (a / (1.0 + jnp.exp(-a)) * u).astype(BF16)
            ybuf[slot, 0:rows] = _pack_halves(_dot(hmid, wd_s[...]))
            put(i, slot).start()
            return carry

        lax.fori_loop(0, count, chunk, 0)

        @pl.when(count >= 2)
        def _():
            put(count - 2, count & 1).wait()

        @pl.when(count >= 1)
        def _():
            put(count - 1, (count - 1) & 1).wait()

    n_big = nb // EXPERT_CHUNK_BLOCKS
    run_chunks(b0, n_big, EXPERT_CHUNK_BLOCKS)
    run_chunks(b0 + n_big * EXPERT_CHUNK_BLOCKS, nb - n_big * EXPERT_CHUNK_BLOCKS, 1)

    @pl.when(e == pl.num_programs(0) - 1)
    def _():
        ybuf[0, 0:MOE_BLOCK] = jnp.zeros((MOE_BLOCK, ybuf.shape[2]), ybuf.dtype)

        def fill(blk, carry):
            copy = pltpu.make_async_copy(ybuf.at[0, pl.ds(0, MOE_BLOCK)],
                                         yb_ref.at[pl.ds(blk * MOE_BLOCK, MOE_BLOCK)], sem_out.at[0])
            copy.start()
            copy.wait()
            return carry

        lax.fori_loop(b0 + nb, nblk, fill, 0)


def moe_experts(xs, blk_start, blk_cnt, w_gate, w_up, w_down, layer):
    cap, c = xs.shape
    d = 2 * c
    hid = w_gate.shape[2]
    nblk = cap // MOE_BLOCK
    wmap = lambda e, st, cn: (layer * N_EXPERTS + e, 0, 0)
    grid_spec = pltpu.PrefetchScalarGridSpec(
        num_scalar_prefetch=2,
        grid=(N_EXPERTS,),
        in_specs=[pl.BlockSpec(memory_space=pl.ANY),
                  pl.BlockSpec((1, d, hid), wmap),
                  pl.BlockSpec((1, d, hid), wmap),
                  pl.BlockSpec((1, hid, d), wmap)],
        out_specs=pl.BlockSpec(memory_space=pl.ANY),
        scratch_shapes=[pltpu.VMEM((2, EXPERT_CHUNK_BLOCKS * MOE_BLOCK, c), U32),
                        pltpu.VMEM((2, EXPERT_CHUNK_BLOCKS * MOE_BLOCK, c), U32),
                        pltpu.SemaphoreType.DMA((2,)), pltpu.SemaphoreType.DMA((2,)),
                        pltpu.VMEM((d, hid), BF16), pltpu.VMEM((d, hid), BF16), pltpu.VMEM((hid, d), BF16)],
    )
    return pl.pallas_call(
        functools.partial(_expert_kernel, nblk=nblk),
        grid_spec=grid_spec,
        out_shape=jax.ShapeDtypeStruct((cap, c), U32),
        compiler_params=_cparams(("arbitrary",)),
        name="moe_experts",
    )(blk_start, blk_cnt, xs, w_gate, w_up, w_down)


def _combine_kernel(dest_ref, next_dest_ref, x_ref, wt_ref, g_ref, yb_ref, o_ref, buf, sem, *, tm, final_norm):
    i = pl.program_id(0)
    slot = i & 1

    def gather(idx_ref, s):
        def issue(r, carry):
            _row_copy(yb_ref, idx_ref[2 * r], buf.at[s, 0], r, sem.at[s]).start(priority=0)
            _row_copy(yb_ref, idx_ref[2 * r + 1], buf.at[s, 1], r, sem.at[s]).start(priority=1)
            return carry

        lax.fori_loop(0, tm, issue, 0, unroll=ROW_DMA_UNROLL)

    @pl.when(i == 0)
    def _():
        gather(dest_ref, 0)

    @pl.when(i + 1 < pl.num_programs(0))
    def _():
        gather(next_dest_ref, 1 - slot)

    for k in range(2):
        pltpu.make_async_copy(yb_ref.at[pl.ds(0, tm)], buf.at[slot, k], sem.at[slot]).wait()
    wt = wt_ref[...]
    y = x_ref[...] + (_unpack_halves(buf[slot, 0]) * wt[:, 0:1] + _unpack_halves(buf[slot, 1]) * wt[:, 1:2])
    o_ref[...] = _rms(y, g_ref[...]) if final_norm else y


def moe_combine(x2d, yb, dest_flat, wt, g_final, final_norm):
    n, d = x2d.shape
    c = yb.shape[1]
    tm = 512
    nt = n // tm
    return pl.pallas_call(
        functools.partial(_combine_kernel, tm=tm, final_norm=final_norm),
        grid=(nt,),
        in_specs=[pl.BlockSpec((2 * tm,), lambda i: (i,), memory_space=pltpu.SMEM),
                  pl.BlockSpec((2 * tm,), lambda i: (jnp.minimum(i + 1, nt - 1),), memory_space=pltpu.SMEM),
                  pl.BlockSpec((tm, d), lambda i: (i, 0)),
                  pl.BlockSpec((tm, 2), lambda i: (i, 0)),
                  pl.BlockSpec((1, d), lambda i: (0, 0)),
                  pl.BlockSpec(memory_space=pl.ANY)],
        out_specs=pl.BlockSpec((tm, d), lambda i: (i, 0)),
        out_shape=jax.ShapeDtypeStruct((n, d), F32),
        scratch_shapes=[pltpu.VMEM((2, 2, tm, c), U32), pltpu.SemaphoreType.DMA((2,))],
        compiler_params=_cparams(("arbitrary",)),
        name="moe_combine",
    )(dest_flat, dest_flat, x2d, wt, g_final.reshape(1, d), yb)


def hier_moe_block(x2d, routing, w_gate, w_up, w_down, layer, g_final, final_norm):
    n = x2d.shape[0]
    cap = 2 * n + N_EXPERTS * MOE_BLOCK
    hp, eid, wt, rank, cnt = routing
    counts = cnt[0, N_GROUPS:N_GROUPS + N_EXPERTS].astype(I32)
    padded = (counts + MOE_BLOCK - 1) // MOE_BLOCK * MOE_BLOCK
    p_ends = jnp.cumsum(padded)
    p_starts = p_ends - padded
    experts = jnp.arange(N_EXPERTS, dtype=I32)
    dest = (jnp.sum(jnp.where(eid[..., None] == experts, p_starts, 0), axis=-1) + rank).reshape(-1)
    xs = moe_dispatch(hp, dest, jnp.stack([p_ends, padded]).astype(I32), cap)
    yb = moe_experts(xs, p_starts // MOE_BLOCK, padded // MOE_BLOCK, w_gate, w_up, w_down, layer)
    return moe_combine(x2d, yb, dest, wt, g_final, final_norm)


def kernel(x, mem, e_norm, e_w_in, e_conv_w, e_conv_b, e_filt_w1, e_filt_b1, e_filt_w2, e_filt_b2, e_filt_w3, e_filt_freq, e_hy_bias, e_lam, e_subln, e_w_out, o_norm, o_w_in, o_sink, o_w_out, c_norm, c_wq, c_wkv, c_wo, f_norm, f_w_grp, f_b_grp, f_w_exp, f_b_exp, f_w_gate, f_w_up, f_w_down, g_mem, g_final):
    b, seq, d = x.shape
    n = b * seq
    m = mem.shape[1]
    tables = dft_tables(seq // 2)
    twiddles = dft_twiddles(seq)
    x2 = x.reshape(n, d)
    mem2 = mem.reshape(b * m, d)
    w_gate = f_w_gate.reshape(DEPTH * N_EXPERTS, d, EXP_HIDDEN)
    w_up = f_w_up.reshape(DEPTH * N_EXPERTS, d, EXP_HIDDEN)
    w_down = f_w_down.reshape(DEPTH * N_EXPERTS, EXP_HIDDEN, d)
    qperm = window_column_perm()
    for i in range(DEPTH):
        j = i // 2
        if i % 2 == 0:
            hyw, qw = 3 * HY_WIDTH, 2 * DIFF_HEADS * DIFF_HEAD_DIM
            w_in = jnp.concatenate([e_w_in[j][:, :hyw],
                                    e_w_in[j][:, hyw:hyw + qw] * (DIFF_HEAD_DIM ** -0.5 * LOG2E),
                                    e_w_in[j][:, hyw + qw:]], axis=1).astype(BF16)
            proj_h, proj_a = norm_matmul(x2, e_norm[j], w_in, 512, [(hyw, F32), (w_in.shape[1] - hyw, BF16)])
            y_hy = hyena_mixer(proj_h.reshape(b, seq, -1), tables, twiddles, e_conv_w[j], e_conv_b[j], e_filt_w1[j],
                               e_filt_b1[j], e_filt_w2[j], e_filt_b2[j], e_filt_w3[j], e_filt_freq[j], e_hy_bias[j])
            y_df = diff_attention(proj_a.reshape(b, seq, -1), e_lam[j], e_subln[j], i)
            w_out = e_w_out[j].astype(BF16)
            mixed, w_mix = [y_hy, y_df], [w_out[:HY_WIDTH], w_out[HY_WIDTH:]]
        else:
            oq = GQA_HEADS * GQA_HD
            w_in = jnp.concatenate([o_w_in[j][:, :oq][:, qperm] * (GQA_HD ** -0.5 * LOG2E), o_w_in[j][:, oq:]],
                                   axis=1).astype(BF16)
            (proj,) = norm_matmul(x2, o_norm[j], w_in, 512, [(w_in.shape[1], BF16)])
            att = window_gqa(proj.reshape(b, seq, -1), o_sink[j])
            mixed, w_mix = [att], [o_w_out[j][qperm].astype(BF16)]
        (kv,) = norm_matmul(mem2, g_mem, c_wkv[i].astype(BF16), 512, [(2 * X_W, BF16)])
        x3, *routing = post_mixer(x2.reshape(b, seq, d), mixed, w_mix, c_norm[i], c_wq[i].astype(BF16),
                                  kv.reshape(b, m, -1), c_wo[i].astype(BF16), f_norm[i], f_w_grp[i], f_b_grp[i],
                                  f_w_exp[i], f_b_exp[i])
        x2 = hier_moe_block(x3.reshape(n, d), routing, w_gate, w_up, w_down, i, g_final, i == DEPTH - 1)
    return x2.reshape(b, seq, d)
```

```python
import functools
import math

import jax
import jax.numpy as jnp
from jax import lax
from jax.experimental import pallas as pl
from jax.experimental.pallas import tpu as pltpu

F32 = jnp.float32
BF16 = jnp.bfloat16
I32 = jnp.int32
U32 = jnp.uint32

D_MODEL = 1024
DEPTH = 4
EPS = 1e-6
NEG = -1e30
HY_WIDTH = 512
HY_BANDS = 16
HY_FILT_HIDDEN = 64
HY_DECAY_TARGET = 1e-2
HY_FAST_PCT = 0.3
HY_SLOW_PCT = 1.5
DIFF_HEADS = 4
DIFF_HEAD_DIM = 64
WIN = 128
GQA_HEADS = 16
GQA_KV = 4
GQA_HD = 64
X_HEADS = 4
X_HD = 128
X_W = X_HEADS * X_HD
N_GROUPS = 4
EXP_PER_GROUP = 8
N_EXPERTS = N_GROUPS * EXP_PER_GROUP
EXP_HIDDEN = 512

LOG2E = 1.4426950408889634
LANES = 128
VMEM_LIMIT = 56 * 1024 * 1024
MOE_BLOCK = 256
DIFF_Q_BLOCK = 256
DIFF_KEY_CHUNK = 512
DIFF_BATCH_PER_STEP = 4
WIN_BATCH_PER_STEP = 4
EXPERT_SLOTS = 4
EXPERT_AHEAD = EXPERT_SLOTS - 1
ROW_DMA_UNROLL = 8


def _cparams(sem):
    return pltpu.CompilerParams(dimension_semantics=sem, vmem_limit_bytes=VMEM_LIMIT)


def _rms(x, g):
    ms = jnp.mean(x * x, axis=-1, keepdims=True)
    return x * lax.rsqrt(ms + EPS) * g


def _dot(a, b):
    return jnp.dot(a, b, preferred_element_type=F32)


def _dot_nt(a, b):
    return lax.dot_general(a, b, (((1,), (1,)), ((), ())), preferred_element_type=F32)


def _norm_matmul_kernel(x_ref, g_ref, w_ref, *o_refs):
    h = _rms(x_ref[...], g_ref[...]).astype(BF16)
    col = 0
    for o_ref in o_refs:
        width = o_ref.shape[1]
        o_ref[...] = _dot(h, w_ref[:, col:col + width]).astype(o_ref.dtype)
        col += width


def norm_matmul(x2d, gamma, w_bf16, tm, outs):
    n, d = x2d.shape
    f = w_bf16.shape[1]
    assert sum(width for width, _ in outs) == f
    res = pl.pallas_call(
        _norm_matmul_kernel,
        grid=(n // tm,),
        in_specs=[pl.BlockSpec((tm, d), lambda i: (i, 0)),
                  pl.BlockSpec((1, d), lambda i: (0, 0)),
                  pl.BlockSpec((d, f), lambda i: (0, 0))],
        out_specs=[pl.BlockSpec((tm, width), lambda i: (i, 0)) for width, _ in outs],
        out_shape=[jax.ShapeDtypeStruct((n, width), dtype) for width, dtype in outs],
        compiler_params=_cparams(("parallel",)),
        name="norm_matmul",
    )(x2d, gamma.reshape(1, d), w_bf16)
    return res


def _conv3_kernel(u_ref, w_ref, b_ref, oe_ref, oo_ref, slab_ref):
    half = u_ref.shape[1] // 2
    w = w_ref[...]
    bias = b_ref[...]
    row = lax.broadcasted_iota(I32, (half, LANES), 0)
    for k in range(u_ref.shape[2] // LANES):
        cols = slice(k * LANES, (k + 1) * LANES)
        slab_ref[...] = u_ref[0, :, cols]
        ue = slab_ref[pl.ds(0, half, stride=2), :]
        uo = slab_ref[pl.ds(1, half, stride=2), :]
        uo_prev = jnp.where(row == 0, 0.0, pltpu.roll(uo, 1, 0))
        ue_next = jnp.where(row == half - 1, 0.0, pltpu.roll(ue, half - 1, 0))
        oe_ref[0, :, cols] = uo_prev * w[0:1, cols] + ue * w[1:2, cols] + uo * w[2:3, cols] + bias[:, cols]
        oo_ref[0, :, cols] = ue * w[0:1, cols] + uo * w[1:2, cols] + ue_next * w[2:3, cols] + bias[:, cols]


def conv3(proj3d, conv_w, conv_b):
    b, seq, c = proj3d.shape
    tc = HY_WIDTH
    out = pl.BlockSpec((1, seq // 2, tc), lambda i, j: (i, 0, j))
    return pl.pallas_call(
        _conv3_kernel,
        grid=(b, c // tc),
        in_specs=[pl.BlockSpec((1, seq, tc), lambda i, j: (i, 0, j)),
                  pl.BlockSpec((3, tc), lambda i, j: (0, j)),
                  pl.BlockSpec((1, tc), lambda i, j: (0, j))],
        out_specs=[out, out],
        out_shape=[jax.ShapeDtypeStruct((b, seq // 2, c), F32)] * 2,
        scratch_shapes=[pltpu.VMEM((seq, LANES), F32)],
        compiler_params=_cparams(("parallel", "parallel")),
        name="hyena_conv3",
    )(proj3d, conv_w, conv_b.reshape(1, c))


def _hy_filter_kernel(t_ref, bands_ref, w1t_ref, w1c_ref, w1s_ref, b1_ref, w2_ref, b2_ref, w3_ref,
                      freq_ref, delta_ref, hf_ref, hb_ref, *, seq, tl):
    hi = lax.Precision.HIGHEST
    i = pl.program_id(0)
    t = t_ref[...]
    pos = (i * tl + lax.broadcasted_iota(I32, (tl, 1), 0)).astype(F32)
    ang = bands_ref[...] * (2.0 * math.pi * pos / seq)
    f = freq_ref[...]
    pre = (t * w1t_ref[...]
           + jnp.dot(jnp.cos(ang), w1c_ref[...], precision=hi, preferred_element_type=F32)
           - jnp.dot(jnp.sin(ang), w1s_ref[...], precision=hi, preferred_element_type=F32)
           + b1_ref[...])
    a = jnp.sin(f * pre)
    a = jnp.sin(f * (jnp.dot(a, w2_ref[...], precision=hi, preferred_element_type=F32) + b2_ref[...]))
    h = jnp.dot(a, w3_ref[...], precision=hi, preferred_element_type=F32)
    decay = jnp.exp(-t * jnp.abs(delta_ref[...]))
    w = HY_WIDTH
    for o in range(2):
        hf_ref[:, o * w:(o + 1) * w] = h[:, o * 2 * w:o * 2 * w + w] * decay
        hb_ref[:, o * w:(o + 1) * w] = jnp.where(pos == 0.0, 0.0, h[:, o * 2 * w + w:(o + 1) * 2 * w] * decay)


def hyena_filters_time(seq, w1, b1, w2, b2, w3, freq):
    tl = 512
    hid = LANES
    pad_h = hid - HY_FILT_HIDDEN
    t = jnp.linspace(0.0, 1.0, seq, dtype=F32)[:, None]
    bands = jnp.pad(jnp.linspace(1e-4, HY_BANDS - 1, HY_BANDS, dtype=F32)[None], ((0, 0), (0, LANES - HY_BANDS)))
    w1p = jnp.pad(w1, ((0, 0), (0, pad_h)))
    w1t = w1p[0:1]
    w1c = jnp.pad(w1p[1:1 + HY_BANDS], ((0, LANES - HY_BANDS), (0, 0)))
    w1s = jnp.pad(w1p[1 + HY_BANDS:], ((0, LANES - HY_BANDS), (0, 0)))
    b1p = jnp.pad(b1, (0, pad_h)).reshape(1, hid)
    w2p = jnp.pad(w2, ((0, pad_h), (0, pad_h)))
    b2p = jnp.pad(b2, (0, pad_h)).reshape(1, hid)
    w3p = jnp.pad(w3, ((0, pad_h), (0, 0)))
    freqp = jnp.pad(freq, (0, pad_h)).reshape(1, hid)
    max_decay = math.log(HY_DECAY_TARGET) / HY_FAST_PCT
    min_decay = math.log(HY_DECAY_TARGET) / HY_SLOW_PCT
    deltas = jnp.linspace(min_decay, max_decay, HY_WIDTH, dtype=F32)[None]
    fw = w3.shape[1]
    full = lambda shape: pl.BlockSpec(shape, lambda i: (0, 0))
    return pl.pallas_call(
        functools.partial(_hy_filter_kernel, seq=seq, tl=tl),
        grid=(seq // tl,),
        in_specs=[pl.BlockSpec((tl, 1), lambda i: (i, 0)), full((1, LANES)), full((1, hid)),
                  full((LANES, hid)), full((LANES, hid)), full((1, hid)), full((hid, hid)), full((1, hid)),
                  full((hid, fw)), full((1, hid)), full((1, HY_WIDTH))],
        out_specs=[pl.BlockSpec((tl, 2 * HY_WIDTH), lambda i: (i, 0)),
                   pl.BlockSpec((tl, 2 * HY_WIDTH), lambda i: (i, 0))],
        out_shape=[jax.ShapeDtypeStruct((seq, 2 * HY_WIDTH), F32)] * 2,
        compiler_params=_cparams(("parallel",)),
        name="hyena_filter_mlp",
    )(t, bands, w1t, w1c, w1s, b1p, w2p, b2p, w3p, freqp, deltas)


def dft_tables(seq):
    n2 = 4 * seq
    sub = 64
    f = jnp.arange(seq, dtype=I32)[:, None]
    odd = 2 * f + 1
    s1 = jnp.arange(seq // sub, dtype=I32)[None]
    s0 = jnp.arange(sub, dtype=I32)[None]
    ang_p = ((odd * s1) % (n2 // sub)).astype(F32) * (2.0 * math.pi * sub / n2)
    ang_q = ((odd * s0) % n2).astype(F32) * (2.0 * math.pi / n2)
    pc, ps, qc, qs = jnp.cos(ang_p), jnp.sin(ang_p), jnp.cos(ang_q), jnp.sin(ang_q)
    c = (pc[:, :, None] * qc[:, None, :] - ps[:, :, None] * qs[:, None, :]).reshape(seq, seq)
    s = (ps[:, :, None] * qc[:, None, :] + pc[:, :, None] * qs[:, None, :]).reshape(seq, seq)
    return c.astype(BF16), s.astype(BF16), c.T.astype(BF16), s.T.astype(BF16)


def dft_twiddles(seq):
    g = jnp.arange(seq // 2, dtype=I32)[:, None]
    ang = (2 * g + 1).astype(F32) * (2.0 * math.pi / (4 * seq))
    return jnp.cos(ang), jnp.sin(ang)


def _half_transform(c1, s1, cg, sg, x0, x1):
    pc0, ps0 = _dot(c1, x0), _dot(s1, x0)
    pc1, ps1 = _dot(c1, x1), _dot(s1, x1)
    tr = cg * pc1 - sg * ps1
    ti = -(cg * ps1 + sg * pc1)
    return pc0 + tr, ti - ps0, pc0 - tr, ps0 + ti


def _spectrum_kernel(c_ref, s_ref, cg_ref, sg_ref, f0_ref, f1_ref, b0_ref, b1_ref,
                     kgre_ref, kgim_ref, khre_ref, khim_ref):
    c1, s1, cg, sg = c_ref[...], s_ref[...], cg_ref[...], sg_ref[...]
    fre_g, fim_g, fre_h, fim_h = _half_transform(c1, s1, cg, sg, f0_ref[...].astype(BF16), f1_ref[...].astype(BF16))
    bre_g, bim_g, bre_h, bim_h = _half_transform(c1, s1, cg, sg, b0_ref[...].astype(BF16), b1_ref[...].astype(BF16))
    kgre_ref[...] = fre_g + bre_g
    kgim_ref[...] = fim_g - bim_g
    khre_ref[...] = fre_h + bre_h
    khim_ref[...] = fim_h - bim_h


def filter_spectrum(tables, twiddles, hf, hb):
    seq, cols = hf.shape
    half = seq // 2
    tf, tn = 512, 512
    nj = cols // tn
    hf2 = hf.reshape(half, 2 * cols)
    hb2 = hb.reshape(half, 2 * cols)
    tab = pl.BlockSpec((tf, half), lambda i, j: (i, 0))
    twd = pl.BlockSpec((tf, 1), lambda i, j: (i, 0))
    even = pl.BlockSpec((half, tn), lambda i, j: (0, j))
    odd = pl.BlockSpec((half, tn), lambda i, j: (0, nj + j))
    return pl.pallas_call(
        _spectrum_kernel,
        grid=(half // tf, nj),
        in_specs=[tab, tab, twd, twd, even, odd, even, odd],
        out_specs=[pl.BlockSpec((tf, tn), lambda i, j: (i, j))] * 4,
        out_shape=[jax.ShapeDtypeStruct((half, cols), F32)] * 4,
        compiler_params=_cparams(("parallel", "parallel")),
        name="hyena_filter_spectrum",
    )(tables[0], tables[1], twiddles[0], twiddles[1], hf2, hf2, hb2, hb2)


def _hy_fwd_kernel(z0_ref, z1_ref, c_ref, s_ref, cg_ref, sg_ref, kgre_ref, kgim_ref, khre_ref, khim_ref,
                   u0re_ref, u0im_ref, u1re_ref, u1im_ref):
    cg, sg = cg_ref[...], sg_ref[...]
    zre_g, zim_g, zre_h, zim_h = _half_transform(c_ref[...], s_ref[...], cg, sg,
                                                 z0_ref[0].astype(BF16), z1_ref[0].astype(BF16))
    kgre, kgim, khre, khim = kgre_ref[...], kgim_ref[...], khre_ref[...], khim_ref[...]
    yre_g = zre_g * kgre - zim_g * kgim
    yim_g = zre_g * kgim + zim_g * kgre
    yre_h = zre_h * khre - zim_h * khim
    yim_h = zre_h * khim + zim_h * khre
    u0re_ref[0] = (yre_g + yre_h).astype(BF16)
    u0im_ref[0] = (yim_g - yim_h).astype(BF16)
    a = yre_g - yre_h
    b = yim_g + yim_h
    u1re_ref[0] = (cg * a - sg * b).astype(BF16)
    u1im_ref[0] = (cg * b + sg * a).astype(BF16)


def hyena_fwd(zsrc, tables, twiddles, kspec, order):
    (ze, ce), (zo, co) = zsrc
    b, half, _ = ze.shape
    w = HY_WIDTH
    tf = 512
    tab = pl.BlockSpec((tf, half), lambda i, j: (j, 0))
    twd = pl.BlockSpec((tf, 1), lambda i, j: (j, 0))
    kblk = pl.BlockSpec((tf, w), lambda i, j: (j, order))
    return pl.pallas_call(
        _hy_fwd_kernel,
        grid=(b, half // tf),
        in_specs=[pl.BlockSpec((1, half, w), lambda i, j: (i, 0, ce)),
                  pl.BlockSpec((1, half, w), lambda i, j: (i, 0, co)),
                  tab, tab, twd, twd, kblk, kblk, kblk, kblk],
        out_specs=[pl.BlockSpec((1, tf, w), lambda i, j: (i, j, 0))] * 4,
        out_shape=[jax.ShapeDtypeStruct((b, half, w), BF16)] * 4,
        compiler_params=_cparams(("parallel", "parallel")),
        name="hyena_dft_fwd",
    )(ze, zo, tables[0], tables[1], twiddles[0], twiddles[1], *kspec)


def _hy_inv_kernel(ct_ref, st_ref, u0re_ref, u0im_ref, u1re_ref, u1im_ref, z0_ref, z1_ref, g0_ref, g1_ref,
                   bias_ref, o_ref, *, scale):
    ct, st = ct_ref[...], st_ref[...]
    w = z0_ref.shape[2]
    y0 = (_dot(ct, u0re_ref[0]) - _dot(st, u0im_ref[0])) * scale
    y1 = (_dot(ct, u1re_ref[0]) - _dot(st, u1im_ref[0])) * scale
    bias = bias_ref[...]
    o_ref[0, :, 0:w] = g0_ref[0] * (y0 + z0_ref[0] * bias)
    o_ref[0, :, w:2 * w] = g1_ref[0] * (y1 + z1_ref[0] * bias)


def hyena_inv(tables, u, zsrc, gsrc, bias_row):
    b, half, w = u[0].shape
    tt = 512
    tab = pl.BlockSpec((tt, half), lambda i, j: (j, 0))
    ublk = pl.BlockSpec((1, half, w), lambda i, j: (i, 0, 0))

    def rows(col):
        return pl.BlockSpec((1, tt, w), lambda i, j: (i, j, col))

    (ze, ce), (zo, co) = zsrc
    (ge, gce), (go, gco) = gsrc
    return pl.pallas_call(
        functools.partial(_hy_inv_kernel, scale=0.5 / half),
        grid=(b, half // tt),
        in_specs=[tab, tab, ublk, ublk, ublk, ublk, rows(ce), rows(co), rows(gce), rows(gco),
                  pl.BlockSpec((1, w), lambda i, j: (0, 0))],
        out_specs=pl.BlockSpec((1, tt, 2 * w), lambda i, j: (i, j, 0)),
        out_shape=jax.ShapeDtypeStruct((b, half, 2 * w), F32),
        compiler_params=_cparams(("parallel", "parallel")),
        name="hyena_dft_inv",
    )(tables[2], tables[3], *u, ze, zo, ge, go, bias_row)


def hyena_mixer(proj3d, tables, twiddles, conv_w, conv_b, w1, b1, w2, b2, w3, freq, hy_bias):
    seq = proj3d.shape[1]
    ue, uo = conv3(proj3d, conv_w, conv_b)
    hf, hb = hyena_filters_time(seq, w1, b1, w2, b2, w3, freq)
    kspec = filter_spectrum(tables, twiddles, hf, hb)
    v = ((ue, 0), (uo, 0))
    uu = hyena_fwd(v, tables, twiddles, kspec, 0)
    z1 = hyena_inv(tables, uu, v, ((ue, 1), (uo, 1)), hy_bias[0:1])
    z = ((z1, 0), (z1, 1))
    uu = hyena_fwd(z, tables, twiddles, kspec, 1)
    return hyena_inv(tables, uu, z, ((ue, 2), (uo, 2)), hy_bias[1:2])


def _diff_attn_kernel(slope_ref, q_ref, k_ref, v_ref, lam_ref, sub_ref, o_ref, bias_ref, *, tq, lam_init):
    h = pl.program_id(0)
    qi = pl.program_id(1)
    seq = k_ref.shape[1]

    @pl.when(pl.program_id(2) == 0)
    def _():
        qpos = qi * tq + lax.broadcasted_iota(I32, (tq, seq), 0)
        kpos = lax.broadcasted_iota(I32, (tq, seq), 1)
        bias_ref[...] = slope_ref[h] * jnp.abs(qpos - kpos).astype(F32)

    lane = lax.broadcasted_iota(I32, (1, 2 * DIFF_HEAD_DIM), 1)
    ck = DIFF_KEY_CHUNK
    l = lam_ref[...]
    lam_full = (jnp.exp(jnp.sum(l[0:1] * l[1:2], axis=-1, keepdims=True))
                - jnp.exp(jnp.sum(l[2:3] * l[3:4], axis=-1, keepdims=True)) + lam_init)

    def attend(bb, m):
        q = q_ref[bb]
        k = k_ref[bb]
        v = v_ref[bb]
        keep = (lane < DIFF_HEAD_DIM) if m == 0 else (lane >= DIFF_HEAD_DIM)
        qm = jnp.where(keep, q, jnp.zeros_like(q))
        s = [_dot_nt(qm, k[c:c + ck]) - bias_ref[:, c:c + ck] for c in range(0, seq, ck)]
        mx = functools.reduce(jnp.maximum, [jnp.max(sc, axis=-1, keepdims=True) for sc in s])
        acc = jnp.zeros((tq, 2 * DIFF_HEAD_DIM), F32)
        den = jnp.zeros((tq, 1), F32)
        for i, sc in enumerate(s):
            e = jnp.exp2(sc - mx)
            den = den + jnp.sum(e, axis=-1, keepdims=True)
            acc = acc + _dot(e.astype(BF16), v[i * ck:(i + 1) * ck])
        return acc / den

    for bb in range(q_ref.shape[0]):
        o = attend(bb, 0) - lam_full * attend(bb, 1)
        o_ref[bb] = _rms(o, sub_ref[...]) * (1.0 - lam_init)


def diff_attention(proj3d, lam, subln, layer_idx):
    b, seq, _ = proj3d.shape
    tq = DIFF_Q_BLOCK
    nb = DIFF_BATCH_PER_STEP
    hw = 2 * DIFF_HEAD_DIM
    qb, kb, vb = 0, DIFF_HEADS, 2 * DIFF_HEADS
    lam_init = 0.8 - 0.6 * math.exp(-0.3 * layer_idx)
    slopes = 2.0 ** (-8.0 * jnp.arange(1, DIFF_HEADS + 1, dtype=F32) / DIFF_HEADS) * LOG2E
    return pl.pallas_call(
        functools.partial(_diff_attn_kernel, tq=tq, lam_init=lam_init),
        grid=(DIFF_HEADS, seq // tq, b // nb),
        in_specs=[pl.BlockSpec(memory_space=pltpu.SMEM),
                  pl.BlockSpec((nb, tq, hw), lambda h, j, i: (i, j, qb + h)),
                  pl.BlockSpec((nb, seq, hw), lambda h, j, i: (i, 0, kb + h)),
                  pl.BlockSpec((nb, seq, hw), lambda h, j, i: (i, 0, vb + h)),
                  pl.BlockSpec((4, DIFF_HEAD_DIM), lambda h, j, i: (0, 0)),
                  pl.BlockSpec((1, hw), lambda h, j, i: (0, 0))],
        out_specs=pl.BlockSpec((nb, tq, hw), lambda h, j, i: (i, j, h)),
        out_shape=jax.ShapeDtypeStruct((b, seq, DIFF_HEADS * hw), F32),
        scratch_shapes=[pltpu.VMEM((tq, seq), F32)],
        compiler_params=_cparams(("parallel", "parallel", "arbitrary")),
        name="diff_attention",
    )(slopes, proj3d, proj3d, proj3d, lam, subln.reshape(1, hw))


def _win_attn_kernel(slope_ref, sink_ref, q_ref, kp_ref, kc_ref, kn_ref, vp_ref, vc_ref, vn_ref, o_ref, bias_ref,
                     *, tq, seq):
    qi = pl.program_id(0)
    span = 3 * tq
    group = GQA_HEADS // GQA_KV

    @pl.when(pl.program_id(1) == 0)
    def _():
        qpos = qi * tq + lax.broadcasted_iota(I32, (tq, span), 0)
        kpos = (qi - 1) * tq + lax.broadcasted_iota(I32, (tq, span), 1)
        rel = jnp.abs(qpos - kpos)
        relf = rel.astype(F32)
        masked = jnp.where((rel <= WIN) & (kpos >= 0) & (kpos < seq), 0.0, -NEG)
        for head in range(GQA_HEADS):
            bias_ref[head] = slope_ref[head] * relf + masked

    low = lax.broadcasted_iota(I32, (1, LANES), 1) < GQA_HD
    for bb in range(q_ref.shape[0]):
        kwin = jnp.concatenate([kp_ref[bb], kc_ref[bb], kn_ref[bb]], axis=0)
        vwin = jnp.concatenate([vp_ref[bb], vc_ref[bb], vn_ref[bb]], axis=0)
        for p in range(GQA_KV // 2):
            kb = kwin[:, p * LANES:(p + 1) * LANES]
            vb = vwin[:, p * LANES:(p + 1) * LANES]
            outs = []
            for half in range(2):
                kv = 2 * p + half
                mine = low if half == 0 else jnp.logical_not(low)
                qblocks = [q_ref[bb, :, (group * p + r) * LANES:(group * p + r + 1) * LANES] for r in range(group)]
                qs = jnp.concatenate([jnp.where(mine, qb, jnp.zeros_like(qb)) for qb in qblocks], axis=0)
                s = _dot_nt(qs, kb)
                es, inv = [], []
                for r in range(group):
                    head = kv * group + r
                    sink = sink_ref[head]
                    sr = s[r * tq:(r + 1) * tq] - bias_ref[head]
                    m = jnp.maximum(jnp.max(sr, axis=-1, keepdims=True), sink)
                    e = jnp.exp2(sr - m)
                    inv.append(1.0 / (jnp.sum(e, axis=-1, keepdims=True) + jnp.exp2(sink - m)))
                    es.append(e.astype(BF16))
                o = _dot(jnp.concatenate(es, axis=0), vb)
                outs.append([o[r * tq:(r + 1) * tq] * inv[r] for r in range(group)])
            for r in range(group):
                col = (group * p + r) * LANES
                o_ref[bb, :, col:col + LANES] = jnp.where(low, outs[0][r], outs[1][r])


def window_head_order():
    group = GQA_HEADS // GQA_KV
    order = []
    for p in range(GQA_KV // 2):
        for r in range(group):
            order += [(2 * p) * group + r, (2 * p + 1) * group + r]
    return order


def window_column_perm():
    cols = []
    for head in window_head_order():
        cols += list(range(head * GQA_HD, (head + 1) * GQA_HD))
    return jnp.asarray(cols, dtype=I32)


def window_gqa(proj3d, sink):
    b, seq, _ = proj3d.shape
    tq = WIN
    nq = seq // tq
    oq = GQA_HEADS * GQA_HD
    okv = GQA_KV * GQA_HD
    kcol, vcol = oq // okv, oq // okv + 1
    slopes = 2.0 ** (-8.0 * jnp.arange(1, GQA_HEADS + 1, dtype=F32) / GQA_HEADS) * LOG2E

    nb = WIN_BATCH_PER_STEP

    def neighbour(col, step):
        return pl.BlockSpec((nb, tq, okv), lambda j, i: (i, jnp.clip(j + step, 0, nq - 1), col))

    return pl.pallas_call(
        functools.partial(_win_attn_kernel, tq=tq, seq=seq),
        grid=(nq, b // nb),
        in_specs=[pl.BlockSpec(memory_space=pltpu.SMEM),
                  pl.BlockSpec(memory_space=pltpu.SMEM),
                  pl.BlockSpec((nb, tq, oq), lambda j, i: (i, j, 0)),
                  neighbour(kcol, -1), neighbour(kcol, 0), neighbour(kcol, 1),
                  neighbour(vcol, -1), neighbour(vcol, 0), neighbour(vcol, 1)],
        out_specs=pl.BlockSpec((nb, tq, oq), lambda j, i: (i, j, 0)),
        out_shape=jax.ShapeDtypeStruct((b, seq, oq), F32),
        scratch_shapes=[pltpu.VMEM((GQA_HEADS, tq, 3 * tq), F32)],
        compiler_params=_cparams(("parallel", "arbitrary")),
        name="window_gqa",
    )(slopes, sink.astype(F32) * LOG2E, proj3d, proj3d, proj3d, proj3d, proj3d, proj3d, proj3d)


def _cross_attend(x, g, wq, kv, wo):
    h = _rms(x, g).astype(BF16)
    q = _dot(h, wq)
    scale = X_HD ** -0.5
    outs = []
    for hd in range(X_HEADS):
        qh = q[:, hd * X_HD:(hd + 1) * X_HD].astype(BF16)
        kh = kv[:, hd * X_HD:(hd + 1) * X_HD]
        vh = kv[:, X_W + hd * X_HD:X_W + (hd + 1) * X_HD]
        s = _dot_nt(qh, kh) * scale
        e = jnp.exp(s - jnp.max(s, axis=-1, keepdims=True))
        p = e / jnp.sum(e, axis=-1, keepdims=True)
        outs.append(_dot(p.astype(BF16), vh))
    o = jnp.concatenate(outs, axis=-1).astype(BF16)
    return x + _dot(o, wo)


def _split_bf16(x):
    hi = x.astype(BF16)
    lo = (x - hi.astype(F32)).astype(BF16)
    return hi, lo


def _pack_halves(h):
    c = h.shape[1] // 2
    left = lax.bitcast_convert_type(h[:, :c].astype(BF16).astype(F32), U32)
    right = lax.bitcast_convert_type(h[:, c:].astype(BF16).astype(F32), U32)
    return left | (right >> 16)


def _unpack_halves(p):
    left = lax.bitcast_convert_type(p & jnp.uint32(0xFFFF0000), F32)
    right = lax.bitcast_convert_type(p << 16, F32)
    return jnp.concatenate([left, right], axis=-1)


def _route(x, g, whi, wlo, bias, hp_ref, eid_ref, wt_ref, rank_ref, cnt_ref, base_ref):
    tm = x.shape[0]
    h = _rms(x, g)
    hp_ref[...] = _pack_halves(h)
    hhi, hlo = _split_bf16(h)
    logits = _dot(hhi, whi) + _dot(hlo, whi) + _dot(hhi, wlo) + bias
    lane = lax.broadcasted_iota(I32, logits.shape, 1)
    big = jnp.int32(LANES)
    ninf = -jnp.inf

    gl = jnp.where(lane < N_GROUPS, logits, ninf)
    gmax = jnp.max(gl, axis=-1, keepdims=True)
    gsel = jnp.min(jnp.where(gl == gmax, lane, big), axis=-1, keepdims=True)
    ggate = 1.0 / jnp.sum(jnp.exp(gl - gmax), axis=-1, keepdims=True)

    lo_lane = N_GROUPS + gsel * EXP_PER_GROUP
    el = jnp.where((lane >= lo_lane) & (lane < lo_lane + EXP_PER_GROUP), logits, ninf)
    v1 = jnp.max(el, axis=-1, keepdims=True)
    i1 = jnp.min(jnp.where(el == v1, lane, big), axis=-1, keepdims=True)
    el2 = jnp.where(lane == i1, ninf, el)
    v2 = jnp.max(el2, axis=-1, keepdims=True)
    i2 = jnp.min(jnp.where(el2 == v2, lane, big), axis=-1, keepdims=True)
    e2 = jnp.exp(v2 - v1)
    w1 = ggate / (1.0 + e2)
    w2 = ggate * e2 / (1.0 + e2)

    one1 = lane == i1
    one2 = lane == i2
    onehot = (one1 | one2).astype(F32)
    r = lax.broadcasted_iota(I32, (tm, tm), 0)
    c = lax.broadcasted_iota(I32, (tm, tm), 1)
    tri = (c < r).astype(BF16)
    before = _dot(tri, onehot.astype(BF16)) + base_ref[...]
    rank1 = jnp.sum(jnp.where(one1, before, 0.0), axis=-1, keepdims=True)
    rank2 = jnp.sum(jnp.where(one2, before, 0.0), axis=-1, keepdims=True)
    total = base_ref[...] + jnp.sum(onehot, axis=0, keepdims=True)
    base_ref[...] = total
    cnt_ref[...] = total

    col = lax.broadcasted_iota(I32, (tm, 2), 1)
    eid_ref[...] = jnp.where(col == 0, i1, i2) - N_GROUPS
    wt_ref[...] = jnp.where(col == 0, w1, w2)
    rank_ref[...] = jnp.where(col == 0, rank1, rank2).astype(I32)


def _interleave_rows(pair):
    t, w2 = pair.shape
    w = w2 // 2
    r = lax.broadcasted_iota(I32, (2 * t, t), 0)
    c = lax.broadcasted_iota(I32, (2 * t, t), 1)
    pick_even = (r == 2 * c).astype(BF16)
    pick_odd = (r == 2 * c + 1).astype(BF16)
    return (_dot(pick_even, pair[:, :w]) + _dot(pick_odd, pair[:, w:])).astype(BF16)


def _post_mixer_kernel(*refs, n_lhs, paired):
    x_ref = refs[0]
    a_refs = refs[1:1 + n_lhs]
    w_refs = refs[1 + n_lhs:1 + 2 * n_lhs]
    (cg_ref, wq_ref, kv_ref, wo_ref, fg_ref, whi_ref, wlo_ref, rb_ref,
     x_out_ref, hp_ref, eid_ref, wt_ref, rank_ref, cnt_ref, base_ref) = refs[1 + 2 * n_lhs:]

    @pl.when((pl.program_id(0) == 0) & (pl.program_id(1) == 0))
    def _():
        base_ref[...] = jnp.zeros_like(base_ref)

    x = x_ref[0]
    for a_ref, w_ref, is_paired in zip(a_refs, w_refs, paired):
        a = a_ref[0].astype(BF16)
        x = x + _dot(_interleave_rows(a) if is_paired else a, w_ref[...])
    x = _cross_attend(x, cg_ref[...], wq_ref[...], kv_ref[0], wo_ref[...])
    x_out_ref[0] = x
    _route(x, fg_ref[...], whi_ref[...], wlo_ref[...], rb_ref[...], hp_ref, eid_ref, wt_ref, rank_ref, cnt_ref,
           base_ref)


def post_mixer(x3d, lhs_list, w_list, c_gamma, wq, kv, wo, f_gamma, w_grp, b_grp, w_exp, b_exp):
    b, seq, d = x3d.shape
    paired = tuple(a.shape[1] != seq for a in lhs_list)
    n = b * seq
    m = kv.shape[1]
    tm = 512
    nt = seq // tm
    wcat = jnp.pad(jnp.concatenate([w_grp, w_exp], axis=1), ((0, 0), (0, LANES - N_GROUPS - N_EXPERTS)))
    bcat = jnp.pad(jnp.concatenate([b_grp, b_exp]), (0, LANES - N_GROUPS - N_EXPERTS)).reshape(1, LANES)
    whi = wcat.astype(BF16)
    wlo = (wcat - whi.astype(F32)).astype(BF16)
    const = lambda shape: pl.BlockSpec(shape, lambda i, j: (0,) * len(shape))
    tok = lambda width: pl.BlockSpec((tm, width), lambda i, j: (i * nt + j, 0))
    in_specs = [pl.BlockSpec((1, tm, d), lambda i, j: (i, j, 0))]
    in_specs += [pl.BlockSpec((1, tm // 2 if p else tm, a.shape[2]), lambda i, j: (i, j, 0))
                 for a, p in zip(lhs_list, paired)]
    in_specs += [const(w.shape) for w in w_list]
    in_specs += [const((1, d)), const((d, X_W)), pl.BlockSpec((1, m, 2 * X_W), lambda i, j: (i, 0, 0)),
                 const((X_W, d)), const((1, d)), const((d, LANES)), const((d, LANES)), const((1, LANES))]
    return pl.pallas_call(
        functools.partial(_post_mixer_kernel, n_lhs=len(lhs_list), paired=paired),
        grid=(b, nt),
        in_specs=in_specs,
        out_specs=[pl.BlockSpec((1, tm, d), lambda i, j: (i, j, 0)), tok(d // 2), tok(2), tok(2), tok(2),
                   const((1, LANES))],
        out_shape=[jax.ShapeDtypeStruct((b, seq, d), F32),
                   jax.ShapeDtypeStruct((n, d // 2), U32),
                   jax.ShapeDtypeStruct((n, 2), I32),
                   jax.ShapeDtypeStruct((n, 2), F32),
                   jax.ShapeDtypeStruct((n, 2), I32),
                   jax.ShapeDtypeStruct((1, LANES), F32)],
        scratch_shapes=[pltpu.VMEM((1, LANES), F32)],
        compiler_params=_cparams(("arbitrary", "arbitrary")),
        name="post_mixer",
    )(x3d, *lhs_list, *w_list, c_gamma.reshape(1, d), wq, kv, wo, f_gamma.reshape(1, d), whi, wlo, bcat)


def _row_copy(src_ref, src_row, dst_ref, dst_row, sem):
    return pltpu.make_async_copy(src_ref.at[pl.ds(src_row, 1)], dst_ref.at[pl.ds(dst_row, 1)], sem)


def _dispatch_kernel(dest_ref, seg_ref, hp_ref, xs_ref, zbuf, sem, zsem, *, tm):
    @pl.when(pl.program_id(0) == 0)
    def _():
        zbuf[...] = jnp.zeros_like(zbuf)

        def zero_tail(e):
            tail = pl.multiple_of(seg_ref[0, e] - MOE_BLOCK, MOE_BLOCK)
            return pltpu.make_async_copy(zbuf, xs_ref.at[pl.ds(tail, MOE_BLOCK)], zsem)

        def start(e, carry):
            @pl.when(seg_ref[1, e] > 0)
            def _():
                zero_tail(e).start()
            return carry

        def finish(e, carry):
            @pl.when(seg_ref[1, e] > 0)
            def _():
                zero_tail(e).wait()
            return carry

        lax.fori_loop(0, N_EXPERTS, start, 0)
        lax.fori_loop(0, N_EXPERTS, finish, 0)

        def unused(blk):
            return pltpu.make_async_copy(zbuf, xs_ref.at[pl.ds(blk * MOE_BLOCK, MOE_BLOCK)], zsem)

        def start_unused(blk, carry):
            unused(blk).start()
            return carry

        def finish_unused(blk, carry):
            unused(blk).wait()
            return carry

        first_unused = seg_ref[0, N_EXPERTS - 1] // MOE_BLOCK
        lax.fori_loop(first_unused, xs_ref.shape[0] // MOE_BLOCK, start_unused, 0)
        lax.fori_loop(first_unused, xs_ref.shape[0] // MOE_BLOCK, finish_unused, 0)

    def issue(r, carry):
        _row_copy(hp_ref, r, xs_ref, dest_ref[2 * r], sem).start(priority=0)
        _row_copy(hp_ref, r, xs_ref, dest_ref[2 * r + 1], sem).start(priority=1)
        return carry

    lax.fori_loop(0, tm, issue, 0, unroll=ROW_DMA_UNROLL)
    for _ in range(2):
        pltpu.make_async_copy(hp_ref, xs_ref.at[pl.ds(0, tm)], sem).wait()


def moe_dispatch(hp, dest_flat, segments, cap):
    n, c = hp.shape
    tm = 512
    return pl.pallas_call(
        functools.partial(_dispatch_kernel, tm=tm),
        grid=(n // tm,),
        in_specs=[pl.BlockSpec((2 * tm,), lambda i: (i,), memory_space=pltpu.SMEM),
                  pl.BlockSpec(memory_space=pltpu.SMEM),
                  pl.BlockSpec((tm, c), lambda i: (i, 0))],
        out_specs=pl.BlockSpec(memory_space=pl.ANY),
        out_shape=jax.ShapeDtypeStruct((cap, c), U32),
        scratch_shapes=[pltpu.VMEM((MOE_BLOCK, c), U32), pltpu.SemaphoreType.DMA(()), pltpu.SemaphoreType.DMA(())],
        compiler_params=_cparams(("arbitrary",)),
        name="moe_dispatch",
    )(dest_flat, segments, hp)


def _expert_kernel(start_ref, cnt_ref, xs_ref, wg_ref, wu_ref, wd_ref, yb_ref, xbuf, ybuf, sem_in, sem_out,
                   wg_s, wu_s, wd_s, *, nblk):
    e = pl.program_id(0)
    last = pl.num_programs(0) - 1
    nb = cnt_ref[e]
    b0 = start_ref[e]
    total = start_ref[last] + cnt_ref[last]

    def fetch(g):
        slot = g % EXPERT_SLOTS
        return pltpu.make_async_copy(xs_ref.at[pl.ds(g * MOE_BLOCK, MOE_BLOCK)], xbuf.at[slot], sem_in.at[slot])

    def put(g):
        slot = g % EXPERT_SLOTS
        return pltpu.make_async_copy(ybuf.at[slot], yb_ref.at[pl.ds(g * MOE_BLOCK, MOE_BLOCK)], sem_out.at[slot])

    @pl.when(e == 0)
    def _():
        for g in range(EXPERT_AHEAD):
            @pl.when(g < total)
            def _():
                fetch(g).start(priority=1)

    @pl.when(nb > 0)
    def _():
        wg_s[...] = wg_ref[0].astype(BF16)
        wu_s[...] = wu_ref[0].astype(BF16)
        wd_s[...] = wd_ref[0].astype(BF16)

    def block(g, carry):
        slot = g % EXPERT_SLOTS
        fetch(g).wait()

        @pl.when(g + EXPERT_AHEAD < total)
        def _():
            fetch(g + EXPERT_AHEAD).start(priority=1)

        @pl.when(g >= EXPERT_SLOTS)
        def _():
            put(g - EXPERT_SLOTS).wait()

        x = _unpack_halves(xbuf[slot]).astype(BF16)
        a = _dot(x, wg_s[...])
        u = _dot(x, wu_s[...])
        hmid = (a / (1.0 + jnp.exp(-a)) * u).astype(BF16)
        ybuf[slot] = _pack_halves(_dot(hmid, wd_s[...]))
        put(g).start()
        return carry

    lax.fori_loop(b0, b0 + nb, block, 0)

    @pl.when(e == last)
    def _():
        for back in range(1, EXPERT_SLOTS + 1):
            @pl.when(total >= back)
            def _():
                put(total - back).wait()

        ybuf[0] = jnp.zeros(ybuf.shape[1:], ybuf.dtype)

        def fill(blk, carry):
            copy = pltpu.make_async_copy(ybuf.at[0], yb_ref.at[pl.ds(blk * MOE_BLOCK, MOE_BLOCK)], sem_out.at[0])
            copy.start()
            copy.wait()
            return carry

        lax.fori_loop(total, nblk, fill, 0)


def moe_experts(xs, blk_start, blk_cnt, w_gate, w_up, w_down, layer):
    cap, c = xs.shape
    d = 2 * c
    hid = w_gate.shape[2]
    nblk = cap // MOE_BLOCK
    wmap = lambda e, st, cn: (layer * N_EXPERTS + e, 0, 0)
    grid_spec = pltpu.PrefetchScalarGridSpec(
        num_scalar_prefetch=2,
        grid=(N_EXPERTS,),
        in_specs=[pl.BlockSpec(memory_space=pl.ANY),
                  pl.BlockSpec((1, d, hid), wmap),
                  pl.BlockSpec((1, d, hid), wmap),
                  pl.BlockSpec((1, hid, d), wmap)],
        out_specs=pl.BlockSpec(memory_space=pl.ANY),
        scratch_shapes=[pltpu.VMEM((EXPERT_SLOTS, MOE_BLOCK, c), U32), pltpu.VMEM((EXPERT_SLOTS, MOE_BLOCK, c), U32),
                        pltpu.SemaphoreType.DMA((EXPERT_SLOTS,)), pltpu.SemaphoreType.DMA((EXPERT_SLOTS,)),
                        pltpu.VMEM((d, hid), BF16), pltpu.VMEM((d, hid), BF16), pltpu.VMEM((hid, d), BF16)],
    )
    return pl.pallas_call(
        functools.partial(_expert_kernel, nblk=nblk),
        grid_spec=grid_spec,
        out_shape=jax.ShapeDtypeStruct((cap, c), U32),
        compiler_params=_cparams(("arbitrary",)),
        name="moe_experts",
    )(blk_start, blk_cnt, xs, w_gate, w_up, w_down)


def _combine_kernel(dest_ref, next_dest_ref, x_ref, wt_ref, g_ref, yb_ref, o_ref, buf, sem, *, tm, final_norm):
    i = pl.program_id(0)
    slot = i & 1

    def gather(idx_ref, s):
        def issue(r, carry):
            _row_copy(yb_ref, idx_ref[2 * r], buf.at[s, 0], r, sem.at[s]).start(priority=0)
            _row_copy(yb_ref, idx_ref[2 * r + 1], buf.at[s, 1], r, sem.at[s]).start(priority=1)
            return carry

        lax.fori_loop(0, tm, issue, 0, unroll=ROW_DMA_UNROLL)

    @pl.when(i == 0)
    def _():
        gather(dest_ref, 0)

    @pl.when(i + 1 < pl.num_programs(0))
    def _():
        gather(next_dest_ref, 1 - slot)

    for k in range(2):
        pltpu.make_async_copy(yb_ref.at[pl.ds(0, tm)], buf.at[slot, k], sem.at[slot]).wait()
    wt = wt_ref[...]
    y = x_ref[...] + (_unpack_halves(buf[slot, 0]) * wt[:, 0:1] + _unpack_halves(buf[slot, 1]) * wt[:, 1:2])
    o_ref[...] = _rms(y, g_ref[...]) if final_norm else y


def moe_combine(x2d, yb, dest_flat, wt, g_final, final_norm):
    n, d = x2d.shape
    c = yb.shape[1]
    tm = 512
    nt = n // tm
    return pl.pallas_call(
        functools.partial(_combine_kernel, tm=tm, final_norm=final_norm),
        grid=(nt,),
        in_specs=[pl.BlockSpec((2 * tm,), lambda i: (i,), memory_space=pltpu.SMEM),
                  pl.BlockSpec((2 * tm,), lambda i: (jnp.minimum(i + 1, nt - 1),), memory_space=pltpu.SMEM),
                  pl.BlockSpec((tm, d), lambda i: (i, 0)),
                  pl.BlockSpec((tm, 2), lambda i: (i, 0)),
                  pl.BlockSpec((1, d), lambda i: (0, 0)),
                  pl.BlockSpec(memory_space=pl.ANY)],
        out_specs=pl.BlockSpec((tm, d), lambda i: (i, 0)),
        out_shape=jax.ShapeDtypeStruct((n, d), F32),
        scratch_shapes=[pltpu.VMEM((2, 2, tm, c), U32), pltpu.SemaphoreType.DMA((2,))],
        compiler_params=_cparams(("arbitrary",)),
        name="moe_combine",
    )(dest_flat, dest_flat, x2d, wt, g_final.reshape(1, d), yb)


def hier_moe_block(x2d, routing, w_gate, w_up, w_down, layer, g_final, final_norm):
    n = x2d.shape[0]
    cap = 2 * n + N_EXPERTS * MOE_BLOCK
    hp, eid, wt, rank, cnt = routing
    counts = cnt[0, N_GROUPS:N_GROUPS + N_EXPERTS].astype(I32)
    padded = (counts + MOE_BLOCK - 1) // MOE_BLOCK * MOE_BLOCK
    p_ends = jnp.cumsum(padded)
    p_starts = p_ends - padded
    experts = jnp.arange(N_EXPERTS, dtype=I32)
    dest = (jnp.sum(jnp.where(eid[..., None] == experts, p_starts, 0), axis=-1) + rank).reshape(-1)
    xs = moe_dispatch(hp, dest, jnp.stack([p_ends, padded]).astype(I32), cap)
    yb = moe_experts(xs, p_starts // MOE_BLOCK, padded // MOE_BLOCK, w_gate, w_up, w_down, layer)
    return moe_combine(x2d, yb, dest, wt, g_final, final_norm)


def kernel(x, mem, e_norm, e_w_in, e_conv_w, e_conv_b, e_filt_w1, e_filt_b1, e_filt_w2, e_filt_b2, e_filt_w3, e_filt_freq, e_hy_bias, e_lam, e_subln, e_w_out, o_norm, o_w_in, o_sink, o_w_out, c_norm, c_wq, c_wkv, c_wo, f_norm, f_w_grp, f_b_grp, f_w_exp, f_b_exp, f_w_gate, f_w_up, f_w_down, g_mem, g_final):
    b, seq, d = x.shape
    n = b * seq
    m = mem.shape[1]
    tables = dft_tables(seq // 2)
    twiddles = dft_twiddles(seq)
    x2 = x.reshape(n, d)
    mem2 = mem.reshape(b * m, d)
    w_gate = f_w_gate.reshape(DEPTH * N_EXPERTS, d, EXP_HIDDEN)
    w_up = f_w_up.reshape(DEPTH * N_EXPERTS, d, EXP_HIDDEN)
    w_down = f_w_down.reshape(DEPTH * N_EXPERTS, EXP_HIDDEN, d)
    qperm = window_column_perm()
    for i in range(DEPTH):
        j = i // 2
        if i % 2 == 0:
            hyw, qw = 3 * HY_WIDTH, 2 * DIFF_HEADS * DIFF_HEAD_DIM
            w_in = jnp.concatenate([e_w_in[j][:, :hyw],
                                    e_w_in[j][:, hyw:hyw + qw] * (DIFF_HEAD_DIM ** -0.5 * LOG2E),
                                    e_w_in[j][:, hyw + qw:]], axis=1).astype(BF16)
            proj_h, proj_a = norm_matmul(x2, e_norm[j], w_in, 512, [(hyw, F32), (w_in.shape[1] - hyw, BF16)])
            y_hy = hyena_mixer(proj_h.reshape(b, seq, -1), tables, twiddles, e_conv_w[j], e_conv_b[j], e_filt_w1[j],
                               e_filt_b1[j], e_filt_w2[j], e_filt_b2[j], e_filt_w3[j], e_filt_freq[j], e_hy_bias[j])
            y_df = diff_attention(proj_a.reshape(b, seq, -1), e_lam[j], e_subln[j], i)
            w_out = e_w_out[j].astype(BF16)
            mixed, w_mix = [y_hy, y_df], [w_out[:HY_WIDTH], w_out[HY_WIDTH:]]
        else:
            oq = GQA_HEADS * GQA_HD
            w_in = jnp.concatenate([o_w_in[j][:, :oq][:, qperm] * (GQA_HD ** -0.5 * LOG2E), o_w_in[j][:, oq:]],
                                   axis=1).astype(BF16)
            (proj,) = norm_matmul(x2, o_norm[j], w_in, 512, [(w_in.shape[1], BF16)])
            att = window_gqa(proj.reshape(b, seq, -1), o_sink[j])
            mixed, w_mix = [att], [o_w_out[j][qperm].astype(BF16)]
        (kv,) = norm_matmul(mem2, g_mem, c_wkv[i].astype(BF16), 512, [(2 * X_W, BF16)])
        x3, *routing = post_mixer(x2.reshape(b, seq, d), mixed, w_mix, c_norm[i], c_wq[i].astype(BF16),
                                  kv.reshape(b, m, -1), c_wo[i].astype(BF16), f_norm[i], f_w_grp[i], f_b_grp[i],
                                  f_w_exp[i], f_b_exp[i])
        x2 = hier_moe_block(x3.reshape(n, d), routing, w_gate, w_up, w_down, i, g_final, i == DEPTH - 1)
    return x2.reshape(b, seq, d)
```

```python
import functools
import math

import jax
import jax.numpy as jnp
from jax import lax
from jax.experimental import pallas as pl
from jax.experimental.pallas import tpu as pltpu

F32 = jnp.float32
BF16 = jnp.bfloat16
I32 = jnp.int32
U32 = jnp.uint32

D_MODEL = 1024
DEPTH = 4
EPS = 1e-6
NEG = -1e30
HY_WIDTH = 512
HY_BANDS = 16
HY_FILT_HIDDEN = 64
HY_DECAY_TARGET = 1e-2
HY_FAST_PCT = 0.3
HY_SLOW_PCT = 1.5
DIFF_HEADS = 4
DIFF_HEAD_DIM = 64
WIN = 128
GQA_HEADS = 16
GQA_KV = 4
GQA_HD = 64
X_HEADS = 4
X_HD = 128
X_W = X_HEADS * X_HD
N_GROUPS = 4
EXP_PER_GROUP = 8
N_EXPERTS = N_GROUPS * EXP_PER_GROUP
EXP_HIDDEN = 512

LOG2E = 1.4426950408889634
LANES = 128
VMEM_LIMIT = 56 * 1024 * 1024
MOE_BLOCK = 256
DIFF_Q_BLOCK = 256
DIFF_KEY_CHUNK = 512
DIFF_BATCH_PER_STEP = 4
WIN_BATCH_PER_STEP = 4
EXPERT_SLOTS = 4
EXPERT_AHEAD = EXPERT_SLOTS - 1
ROW_DMA_TILE = 1024
ROW_DMA_UNROLL = 8


def _cparams(sem):
    return pltpu.CompilerParams(dimension_semantics=sem, vmem_limit_bytes=VMEM_LIMIT)


def _rms(x, g):
    ms = jnp.mean(x * x, axis=-1, keepdims=True)
    return x * lax.rsqrt(ms + EPS) * g


def _dot(a, b):
    return jnp.dot(a, b, preferred_element_type=F32)


def _dot_nt(a, b):
    return lax.dot_general(a, b, (((1,), (1,)), ((), ())), preferred_element_type=F32)


def _norm_matmul_kernel(x_ref, g_ref, w_ref, *o_refs):
    h = _rms(x_ref[...], g_ref[...]).astype(BF16)
    col = 0
    for o_ref in o_refs:
        width = o_ref.shape[1]
        o_ref[...] = _dot(h, w_ref[:, col:col + width]).astype(o_ref.dtype)
        col += width


def norm_matmul(x2d, gamma, w_bf16, tm, outs):
    n, d = x2d.shape
    f = w_bf16.shape[1]
    assert sum(width for width, _ in outs) == f
    res = pl.pallas_call(
        _norm_matmul_kernel,
        grid=(n // tm,),
        in_specs=[pl.BlockSpec((tm, d), lambda i: (i, 0)),
                  pl.BlockSpec((1, d), lambda i: (0, 0)),
                  pl.BlockSpec((d, f), lambda i: (0, 0))],
        out_specs=[pl.BlockSpec((tm, width), lambda i: (i, 0)) for width, _ in outs],
        out_shape=[jax.ShapeDtypeStruct((n, width), dtype) for width, dtype in outs],
        compiler_params=_cparams(("parallel",)),
        name="norm_matmul",
    )(x2d, gamma.reshape(1, d), w_bf16)
    return res


def _conv3_kernel(u_ref, w_ref, b_ref, oe_ref, oo_ref, slab_ref):
    half = u_ref.shape[1] // 2
    w = w_ref[...]
    bias = b_ref[...]
    row = lax.broadcasted_iota(I32, (half, LANES), 0)
    for k in range(u_ref.shape[2] // LANES):
        cols = slice(k * LANES, (k + 1) * LANES)
        slab_ref[...] = u_ref[0, :, cols]
        ue = slab_ref[pl.ds(0, half, stride=2), :]
        uo = slab_ref[pl.ds(1, half, stride=2), :]
        uo_prev = jnp.where(row == 0, 0.0, pltpu.roll(uo, 1, 0))
        ue_next = jnp.where(row == half - 1, 0.0, pltpu.roll(ue, half - 1, 0))
        oe_ref[0, :, cols] = uo_prev * w[0:1, cols] + ue * w[1:2, cols] + uo * w[2:3, cols] + bias[:, cols]
        oo_ref[0, :, cols] = ue * w[0:1, cols] + uo * w[1:2, cols] + ue_next * w[2:3, cols] + bias[:, cols]


def conv3(proj3d, conv_w, conv_b):
    b, seq, c = proj3d.shape
    tc = HY_WIDTH
    out = pl.BlockSpec((1, seq // 2, tc), lambda i, j: (i, 0, j))
    return pl.pallas_call(
        _conv3_kernel,
        grid=(b, c // tc),
        in_specs=[pl.BlockSpec((1, seq, tc), lambda i, j: (i, 0, j)),
                  pl.BlockSpec((3, tc), lambda i, j: (0, j)),
                  pl.BlockSpec((1, tc), lambda i, j: (0, j))],
        out_specs=[out, out],
        out_shape=[jax.ShapeDtypeStruct((b, seq // 2, c), F32)] * 2,
        scratch_shapes=[pltpu.VMEM((seq, LANES), F32)],
        compiler_params=_cparams(("parallel", "parallel")),
        name="hyena_conv3",
    )(proj3d, conv_w, conv_b.reshape(1, c))


def _hy_filter_kernel(t_ref, bands_ref, w1t_ref, w1c_ref, w1s_ref, b1_ref, w2_ref, b2_ref, w3_ref,
                      freq_ref, delta_ref, hf_ref, hb_ref, *, seq, tl):
    hi = lax.Precision.HIGHEST
    i = pl.program_id(0)
    t = t_ref[...]
    pos = (i * tl + lax.broadcasted_iota(I32, (tl, 1), 0)).astype(F32)
    ang = bands_ref[...] * (2.0 * math.pi * pos / seq)
    f = freq_ref[...]
    pre = (t * w1t_ref[...]
           + jnp.dot(jnp.cos(ang), w1c_ref[...], precision=hi, preferred_element_type=F32)
           - jnp.dot(jnp.sin(ang), w1s_ref[...], precision=hi, preferred_element_type=F32)
           + b1_ref[...])
    a = jnp.sin(f * pre)
    a = jnp.sin(f * (jnp.dot(a, w2_ref[...], precision=hi, preferred_element_type=F32) + b2_ref[...]))
    h = jnp.dot(a, w3_ref[...], precision=hi, preferred_element_type=F32)
    decay = jnp.exp(-t * jnp.abs(delta_ref[...]))
    w = HY_WIDTH
    for o in range(2):
        hf_ref[:, o * w:(o + 1) * w] = h[:, o * 2 * w:o * 2 * w + w] * decay
        hb_ref[:, o * w:(o + 1) * w] = jnp.where(pos == 0.0, 0.0, h[:, o * 2 * w + w:(o + 1) * 2 * w] * decay)


def hyena_filters_time(seq, w1, b1, w2, b2, w3, freq):
    tl = 512
    hid = LANES
    pad_h = hid - HY_FILT_HIDDEN
    t = jnp.linspace(0.0, 1.0, seq, dtype=F32)[:, None]
    bands = jnp.pad(jnp.linspace(1e-4, HY_BANDS - 1, HY_BANDS, dtype=F32)[None], ((0, 0), (0, LANES - HY_BANDS)))
    w1p = jnp.pad(w1, ((0, 0), (0, pad_h)))
    w1t = w1p[0:1]
    w1c = jnp.pad(w1p[1:1 + HY_BANDS], ((0, LANES - HY_BANDS), (0, 0)))
    w1s = jnp.pad(w1p[1 + HY_BANDS:], ((0, LANES - HY_BANDS), (0, 0)))
    b1p = jnp.pad(b1, (0, pad_h)).reshape(1, hid)
    w2p = jnp.pad(w2, ((0, pad_h), (0, pad_h)))
    b2p = jnp.pad(b2, (0, pad_h)).reshape(1, hid)
    w3p = jnp.pad(w3, ((0, pad_h), (0, 0)))
    freqp = jnp.pad(freq, (0, pad_h)).reshape(1, hid)
    max_decay = math.log(HY_DECAY_TARGET) / HY_FAST_PCT
    min_decay = math.log(HY_DECAY_TARGET) / HY_SLOW_PCT
    deltas = jnp.linspace(min_decay, max_decay, HY_WIDTH, dtype=F32)[None]
    fw = w3.shape[1]
    full = lambda shape: pl.BlockSpec(shape, lambda i: (0, 0))
    return pl.pallas_call(
        functools.partial(_hy_filter_kernel, seq=seq, tl=tl),
        grid=(seq // tl,),
        in_specs=[pl.BlockSpec((tl, 1), lambda i: (i, 0)), full((1, LANES)), full((1, hid)),
                  full((LANES, hid)), full((LANES, hid)), full((1, hid)), full((hid, hid)), full((1, hid)),
                  full((hid, fw)), full((1, hid)), full((1, HY_WIDTH))],
        out_specs=[pl.BlockSpec((tl, 2 * HY_WIDTH), lambda i: (i, 0)),
                   pl.BlockSpec((tl, 2 * HY_WIDTH), lambda i: (i, 0))],
        out_shape=[jax.ShapeDtypeStruct((seq, 2 * HY_WIDTH), F32)] * 2,
        compiler_params=_cparams(("parallel",)),
        name="hyena_filter_mlp",
    )(t, bands, w1t, w1c, w1s, b1p, w2p, b2p, w3p, freqp, deltas)


def dft_tables(seq):
    n2 = 4 * seq
    sub = 64
    f = jnp.arange(seq, dtype=I32)[:, None]
    odd = 2 * f + 1
    s1 = jnp.arange(seq // sub, dtype=I32)[None]
    s0 = jnp.arange(sub, dtype=I32)[None]
    ang_p = ((odd * s1) % (n2 // sub)).astype(F32) * (2.0 * math.pi * sub / n2)
    ang_q = ((odd * s0) % n2).astype(F32) * (2.0 * math.pi / n2)
    pc, ps, qc, qs = jnp.cos(ang_p), jnp.sin(ang_p), jnp.cos(ang_q), jnp.sin(ang_q)
    c = (pc[:, :, None] * qc[:, None, :] - ps[:, :, None] * qs[:, None, :]).reshape(seq, seq)
    s = (ps[:, :, None] * qc[:, None, :] + pc[:, :, None] * qs[:, None, :]).reshape(seq, seq)
    return c.astype(BF16), s.astype(BF16), c.T.astype(BF16), s.T.astype(BF16)


def dft_twiddles(seq):
    g = jnp.arange(seq // 2, dtype=I32)[:, None]
    ang = (2 * g + 1).astype(F32) * (2.0 * math.pi / (4 * seq))
    return jnp.cos(ang), jnp.sin(ang)


def _half_transform(c1, s1, cg, sg, x0, x1):
    pc0, ps0 = _dot(c1, x0), _dot(s1, x0)
    pc1, ps1 = _dot(c1, x1), _dot(s1, x1)
    tr = cg * pc1 - sg * ps1
    ti = -(cg * ps1 + sg * pc1)
    return pc0 + tr, ti - ps0, pc0 - tr, ps0 + ti


def _spectrum_kernel(c_ref, s_ref, cg_ref, sg_ref, f0_ref, f1_ref, b0_ref, b1_ref,
                     kgre_ref, kgim_ref, khre_ref, khim_ref):
    c1, s1, cg, sg = c_ref[...], s_ref[...], cg_ref[...], sg_ref[...]
    fre_g, fim_g, fre_h, fim_h = _half_transform(c1, s1, cg, sg, f0_ref[...].astype(BF16), f1_ref[...].astype(BF16))
    bre_g, bim_g, bre_h, bim_h = _half_transform(c1, s1, cg, sg, b0_ref[...].astype(BF16), b1_ref[...].astype(BF16))
    kgre_ref[...] = fre_g + bre_g
    kgim_ref[...] = fim_g - bim_g
    khre_ref[...] = fre_h + bre_h
    khim_ref[...] = fim_h - bim_h


def filter_spectrum(tables, twiddles, hf, hb):
    seq, cols = hf.shape
    half = seq // 2
    tf, tn = 512, 512
    nj = cols // tn
    hf2 = hf.reshape(half, 2 * cols)
    hb2 = hb.reshape(half, 2 * cols)
    tab = pl.BlockSpec((tf, half), lambda i, j: (i, 0))
    twd = pl.BlockSpec((tf, 1), lambda i, j: (i, 0))
    even = pl.BlockSpec((half, tn), lambda i, j: (0, j))
    odd = pl.BlockSpec((half, tn), lambda i, j: (0, nj + j))
    return pl.pallas_call(
        _spectrum_kernel,
        grid=(half // tf, nj),
        in_specs=[tab, tab, twd, twd, even, odd, even, odd],
        out_specs=[pl.BlockSpec((tf, tn), lambda i, j: (i, j))] * 4,
        out_shape=[jax.ShapeDtypeStruct((half, cols), F32)] * 4,
        compiler_params=_cparams(("parallel", "parallel")),
        name="hyena_filter_spectrum",
    )(tables[0], tables[1], twiddles[0], twiddles[1], hf2, hf2, hb2, hb2)


def _hy_fwd_kernel(z0_ref, z1_ref, c_ref, s_ref, cg_ref, sg_ref, kgre_ref, kgim_ref, khre_ref, khim_ref,
                   u0re_ref, u0im_ref, u1re_ref, u1im_ref):
    cg, sg = cg_ref[...], sg_ref[...]
    zre_g, zim_g, zre_h, zim_h = _half_transform(c_ref[...], s_ref[...], cg, sg,
                                                 z0_ref[0].astype(BF16), z1_ref[0].astype(BF16))
    kgre, kgim, khre, khim = kgre_ref[...], kgim_ref[...], khre_ref[...], khim_ref[...]
    yre_g = zre_g * kgre - zim_g * kgim
    yim_g = zre_g * kgim + zim_g * kgre
    yre_h = zre_h * khre - zim_h * khim
    yim_h = zre_h * khim + zim_h * khre
    u0re_ref[0] = (yre_g + yre_h).astype(BF16)
    u0im_ref[0] = (yim_g - yim_h).astype(BF16)
    a = yre_g - yre_h
    b = yim_g + yim_h
    u1re_ref[0] = (cg * a - sg * b).astype(BF16)
    u1im_ref[0] = (cg * b + sg * a).astype(BF16)


def hyena_fwd(zsrc, tables, twiddles, kspec, order):
    (ze, ce), (zo, co) = zsrc
    b, half, _ = ze.shape
    w = HY_WIDTH
    tf = 512
    tab = pl.BlockSpec((tf, half), lambda i, j: (j, 0))
    twd = pl.BlockSpec((tf, 1), lambda i, j: (j, 0))
    kblk = pl.BlockSpec((tf, w), lambda i, j: (j, order))
    return pl.pallas_call(
        _hy_fwd_kernel,
        grid=(b, half // tf),
        in_specs=[pl.BlockSpec((1, half, w), lambda i, j: (i, 0, ce)),
                  pl.BlockSpec((1, half, w), lambda i, j: (i, 0, co)),
                  tab, tab, twd, twd, kblk, kblk, kblk, kblk],
        out_specs=[pl.BlockSpec((1, tf, w), lambda i, j: (i, j, 0))] * 4,
        out_shape=[jax.ShapeDtypeStruct((b, half, w), BF16)] * 4,
        compiler_params=_cparams(("parallel", "parallel")),
        name="hyena_dft_fwd",
    )(ze, zo, tables[0], tables[1], twiddles[0], twiddles[1], *kspec)


def _hy_inv_kernel(ct_ref, st_ref, u0re_ref, u0im_ref, u1re_ref, u1im_ref, z0_ref, z1_ref, g0_ref, g1_ref,
                   bias_ref, o_ref, *, scale):
    ct, st = ct_ref[...], st_ref[...]
    w = z0_ref.shape[2]
    y0 = (_dot(ct, u0re_ref[0]) - _dot(st, u0im_ref[0])) * scale
    y1 = (_dot(ct, u1re_ref[0]) - _dot(st, u1im_ref[0])) * scale
    bias = bias_ref[...]
    o_ref[0, :, 0:w] = (g0_ref[0] * (y0 + z0_ref[0] * bias)).astype(o_ref.dtype)
    o_ref[0, :, w:2 * w] = (g1_ref[0] * (y1 + z1_ref[0] * bias)).astype(o_ref.dtype)


def hyena_inv(tables, u, zsrc, gsrc, bias_row, out_dtype):
    b, half, w = u[0].shape
    tt = 512
    tab = pl.BlockSpec((tt, half), lambda i, j: (j, 0))
    ublk = pl.BlockSpec((1, half, w), lambda i, j: (i, 0, 0))

    def rows(col):
        return pl.BlockSpec((1, tt, w), lambda i, j: (i, j, col))

    (ze, ce), (zo, co) = zsrc
    (ge, gce), (go, gco) = gsrc
    return pl.pallas_call(
        functools.partial(_hy_inv_kernel, scale=0.5 / half),
        grid=(b, half // tt),
        in_specs=[tab, tab, ublk, ublk, ublk, ublk, rows(ce), rows(co), rows(gce), rows(gco),
                  pl.BlockSpec((1, w), lambda i, j: (0, 0))],
        out_specs=pl.BlockSpec((1, tt, 2 * w), lambda i, j: (i, j, 0)),
        out_shape=jax.ShapeDtypeStruct((b, half, 2 * w), out_dtype),
        compiler_params=_cparams(("parallel", "parallel")),
        name="hyena_dft_inv",
    )(tables[2], tables[3], *u, ze, zo, ge, go, bias_row)


def hyena_mixer(proj3d, tables, twiddles, conv_w, conv_b, w1, b1, w2, b2, w3, freq, hy_bias):
    seq = proj3d.shape[1]
    ue, uo = conv3(proj3d, conv_w, conv_b)
    hf, hb = hyena_filters_time(seq, w1, b1, w2, b2, w3, freq)
    kspec = filter_spectrum(tables, twiddles, hf, hb)
    v = ((ue, 0), (uo, 0))
    uu = hyena_fwd(v, tables, twiddles, kspec, 0)
    z1 = hyena_inv(tables, uu, v, ((ue, 1), (uo, 1)), hy_bias[0:1], F32)
    z = ((z1, 0), (z1, 1))
    uu = hyena_fwd(z, tables, twiddles, kspec, 1)
    return hyena_inv(tables, uu, z, ((ue, 2), (uo, 2)), hy_bias[1:2], BF16)


def _diff_attn_kernel(slope_ref, q_ref, k_ref, v_ref, lam_ref, sub_ref, o_ref, bias_ref, *, tq, lam_init):
    h = pl.program_id(0)
    qi = pl.program_id(1)
    seq = k_ref.shape[1]

    @pl.when(pl.program_id(2) == 0)
    def _():
        qpos = qi * tq + lax.broadcasted_iota(I32, (tq, seq), 0)
        kpos = lax.broadcasted_iota(I32, (tq, seq), 1)
        bias_ref[...] = slope_ref[h] * jnp.abs(qpos - kpos).astype(F32)

    lane = lax.broadcasted_iota(I32, (1, 2 * DIFF_HEAD_DIM), 1)
    ck = DIFF_KEY_CHUNK
    l = lam_ref[...]
    lam_full = (jnp.exp(jnp.sum(l[0:1] * l[1:2], axis=-1, keepdims=True))
                - jnp.exp(jnp.sum(l[2:3] * l[3:4], axis=-1, keepdims=True)) + lam_init)

    def attend(bb, m):
        q = q_ref[bb]
        k = k_ref[bb]
        v = v_ref[bb]
        keep = (lane < DIFF_HEAD_DIM) if m == 0 else (lane >= DIFF_HEAD_DIM)
        qm = jnp.where(keep, q, jnp.zeros_like(q))
        s = [_dot_nt(qm, k[c:c + ck]) - bias_ref[:, c:c + ck] for c in range(0, seq, ck)]
        mx = functools.reduce(jnp.maximum, [jnp.max(sc, axis=-1, keepdims=True) for sc in s])
        acc = jnp.zeros((tq, 2 * DIFF_HEAD_DIM), F32)
        den = jnp.zeros((tq, 1), F32)
        for i, sc in enumerate(s):
            e = jnp.exp2(sc - mx)
            den = den + jnp.sum(e, axis=-1, keepdims=True)
            acc = acc + _dot(e.astype(BF16), v[i * ck:(i + 1) * ck])
        return acc / den

    for bb in range(q_ref.shape[0]):
        o = attend(bb, 0) - lam_full * attend(bb, 1)
        o_ref[bb] = (_rms(o, sub_ref[...]) * (1.0 - lam_init)).astype(o_ref.dtype)


def diff_attention(proj3d, lam, subln, layer_idx):
    b, seq, _ = proj3d.shape
    tq = DIFF_Q_BLOCK
    nb = DIFF_BATCH_PER_STEP
    hw = 2 * DIFF_HEAD_DIM
    qb, kb, vb = 0, DIFF_HEADS, 2 * DIFF_HEADS
    lam_init = 0.8 - 0.6 * math.exp(-0.3 * layer_idx)
    slopes = 2.0 ** (-8.0 * jnp.arange(1, DIFF_HEADS + 1, dtype=F32) / DIFF_HEADS) * LOG2E
    return pl.pallas_call(
        functools.partial(_diff_attn_kernel, tq=tq, lam_init=lam_init),
        grid=(DIFF_HEADS, seq // tq, b // nb),
        in_specs=[pl.BlockSpec(memory_space=pltpu.SMEM),
                  pl.BlockSpec((nb, tq, hw), lambda h, j, i: (i, j, qb + h)),
                  pl.BlockSpec((nb, seq, hw), lambda h, j, i: (i, 0, kb + h)),
                  pl.BlockSpec((nb, seq, hw), lambda h, j, i: (i, 0, vb + h)),
                  pl.BlockSpec((4, DIFF_HEAD_DIM), lambda h, j, i: (0, 0)),
                  pl.BlockSpec((1, hw), lambda h, j, i: (0, 0))],
        out_specs=pl.BlockSpec((nb, tq, hw), lambda h, j, i: (i, j, h)),
        out_shape=jax.ShapeDtypeStruct((b, seq, DIFF_HEADS * hw), BF16),
        scratch_shapes=[pltpu.VMEM((tq, seq), F32)],
        compiler_params=_cparams(("parallel", "parallel", "arbitrary")),
        name="diff_attention",
    )(slopes, proj3d, proj3d, proj3d, lam, subln.reshape(1, hw))


def _win_attn_kernel(slope_ref, sink_ref, q_ref, kp_ref, kc_ref, kn_ref, vp_ref, vc_ref, vn_ref, o_ref, bias_ref,
                     *, tq, seq):
    qi = pl.program_id(0)
    span = 3 * tq
    group = GQA_HEADS // GQA_KV

    @pl.when(pl.program_id(1) == 0)
    def _():
        qpos = qi * tq + lax.broadcasted_iota(I32, (tq, span), 0)
        kpos = (qi - 1) * tq + lax.broadcasted_iota(I32, (tq, span), 1)
        rel = jnp.abs(qpos - kpos)
        relf = rel.astype(F32)
        masked = jnp.where((rel <= WIN) & (kpos >= 0) & (kpos < seq), 0.0, -NEG)
        for head in range(GQA_HEADS):
            bias_ref[head] = slope_ref[head] * relf + masked

    low = lax.broadcasted_iota(I32, (1, LANES), 1) < GQA_HD
    for bb in range(q_ref.shape[0]):
        kwin = jnp.concatenate([kp_ref[bb], kc_ref[bb], kn_ref[bb]], axis=0)
        vwin = jnp.concatenate([vp_ref[bb], vc_ref[bb], vn_ref[bb]], axis=0)
        for p in range(GQA_KV // 2):
            kb = kwin[:, p * LANES:(p + 1) * LANES]
            vb = vwin[:, p * LANES:(p + 1) * LANES]
            outs = []
            for half in range(2):
                kv = 2 * p + half
                mine = low if half == 0 else jnp.logical_not(low)
                qblocks = [q_ref[bb, :, (group * p + r) * LANES:(group * p + r + 1) * LANES] for r in range(group)]
                qs = jnp.concatenate([jnp.where(mine, qb, jnp.zeros_like(qb)) for qb in qblocks], axis=0)
                s = _dot_nt(qs, kb)
                es, inv = [], []
                for r in range(group):
                    head = kv * group + r
                    sink = sink_ref[head]
                    sr = s[r * tq:(r + 1) * tq] - bias_ref[head]
                    m = jnp.maximum(jnp.max(sr, axis=-1, keepdims=True), sink)
                    e = jnp.exp2(sr - m)
                    inv.append(1.0 / (jnp.sum(e, axis=-1, keepdims=True) + jnp.exp2(sink - m)))
                    es.append(e.astype(BF16))
                o = _dot(jnp.concatenate(es, axis=0), vb)
                outs.append([o[r * tq:(r + 1) * tq] * inv[r] for r in range(group)])
            for r in range(group):
                col = (group * p + r) * LANES
                o_ref[bb, :, col:col + LANES] = jnp.where(low, outs[0][r], outs[1][r]).astype(o_ref.dtype)


def window_head_order():
    group = GQA_HEADS // GQA_KV
    order = []
    for p in range(GQA_KV // 2):
        for r in range(group):
            order += [(2 * p) * group + r, (2 * p + 1) * group + r]
    return order


def window_column_perm():
    cols = []
    for head in window_head_order():
        cols += list(range(head * GQA_HD, (head + 1) * GQA_HD))
    return jnp.asarray(cols, dtype=I32)


def window_gqa(proj3d, sink):
    b, seq, _ = proj3d.shape
    tq = WIN
    nq = seq // tq
    oq = GQA_HEADS * GQA_HD
    okv = GQA_KV * GQA_HD
    kcol, vcol = oq // okv, oq // okv + 1
    slopes = 2.0 ** (-8.0 * jnp.arange(1, GQA_HEADS + 1, dtype=F32) / GQA_HEADS) * LOG2E

    nb = WIN_BATCH_PER_STEP

    def neighbour(col, step):
        return pl.BlockSpec((nb, tq, okv), lambda j, i: (i, jnp.clip(j + step, 0, nq - 1), col))

    return pl.pallas_call(
        functools.partial(_win_attn_kernel, tq=tq, seq=seq),
        grid=(nq, b // nb),
        in_specs=[pl.BlockSpec(memory_space=pltpu.SMEM),
                  pl.BlockSpec(memory_space=pltpu.SMEM),
                  pl.BlockSpec((nb, tq, oq), lambda j, i: (i, j, 0)),
                  neighbour(kcol, -1), neighbour(kcol, 0), neighbour(kcol, 1),
                  neighbour(vcol, -1), neighbour(vcol, 0), neighbour(vcol, 1)],
        out_specs=pl.BlockSpec((nb, tq, oq), lambda j, i: (i, j, 0)),
        out_shape=jax.ShapeDtypeStruct((b, seq, oq), BF16),
        scratch_shapes=[pltpu.VMEM((GQA_HEADS, tq, 3 * tq), F32)],
        compiler_params=_cparams(("parallel", "arbitrary")),
        name="window_gqa",
    )(slopes, sink.astype(F32) * LOG2E, proj3d, proj3d, proj3d, proj3d, proj3d, proj3d, proj3d)


def _cross_attend(x, g, wq, kv, wo):
    h = _rms(x, g).astype(BF16)
    q = _dot(h, wq)
    scale = X_HD ** -0.5
    outs = []
    for hd in range(X_HEADS):
        qh = q[:, hd * X_HD:(hd + 1) * X_HD].astype(BF16)
        kh = kv[:, hd * X_HD:(hd + 1) * X_HD]
        vh = kv[:, X_W + hd * X_HD:X_W + (hd + 1) * X_HD]
        s = _dot_nt(qh, kh) * scale
        e = jnp.exp(s - jnp.max(s, axis=-1, keepdims=True))
        p = e / jnp.sum(e, axis=-1, keepdims=True)
        outs.append(_dot(p.astype(BF16), vh))
    o = jnp.concatenate(outs, axis=-1).astype(BF16)
    return x + _dot(o, wo)


def _split_bf16(x):
    hi = x.astype(BF16)
    lo = (x - hi.astype(F32)).astype(BF16)
    return hi, lo


def _pack_halves(h):
    c = h.shape[1] // 2
    left = lax.bitcast_convert_type(h[:, :c].astype(BF16).astype(F32), U32)
    right = lax.bitcast_convert_type(h[:, c:].astype(BF16).astype(F32), U32)
    return left | (right >> 16)


def _unpack_halves(p):
    left = lax.bitcast_convert_type(p & jnp.uint32(0xFFFF0000), F32)
    right = lax.bitcast_convert_type(p << 16, F32)
    return jnp.concatenate([left, right], axis=-1)


def _route(x, g, whi, wlo, bias, hp_ref, eid_ref, wt_ref, rank_ref, cnt_ref, base_ref):
    tm = x.shape[0]
    h = _rms(x, g)
    hp_ref[...] = _pack_halves(h)
    hhi, hlo = _split_bf16(h)
    logits = _dot(hhi, whi) + _dot(hlo, whi) + _dot(hhi, wlo) + bias
    lane = lax.broadcasted_iota(I32, logits.shape, 1)
    big = jnp.int32(LANES)
    ninf = -jnp.inf

    gl = jnp.where(lane < N_GROUPS, logits, ninf)
    gmax = jnp.max(gl, axis=-1, keepdims=True)
    gsel = jnp.min(jnp.where(gl == gmax, lane, big), axis=-1, keepdims=True)
    ggate = 1.0 / jnp.sum(jnp.exp(gl - gmax), axis=-1, keepdims=True)

    lo_lane = N_GROUPS + gsel * EXP_PER_GROUP
    el = jnp.where((lane >= lo_lane) & (lane < lo_lane + EXP_PER_GROUP), logits, ninf)
    v1 = jnp.max(el, axis=-1, keepdims=True)
    i1 = jnp.min(jnp.where(el == v1, lane, big), axis=-1, keepdims=True)
    el2 = jnp.where(lane == i1, ninf, el)
    v2 = jnp.max(el2, axis=-1, keepdims=True)
    i2 = jnp.min(jnp.where(el2 == v2, lane, big), axis=-1, keepdims=True)
    e2 = jnp.exp(v2 - v1)
    w1 = ggate / (1.0 + e2)
    w2 = ggate * e2 / (1.0 + e2)

    one1 = lane == i1
    one2 = lane == i2
    onehot = (one1 | one2).astype(F32)
    r = lax.broadcasted_iota(I32, (tm, tm), 0)
    c = lax.broadcasted_iota(I32, (tm, tm), 1)
    tri = (c < r).astype(BF16)
    before = _dot(tri, onehot.astype(BF16)) + base_ref[...]
    rank1 = jnp.sum(jnp.where(one1, before, 0.0), axis=-1, keepdims=True)
    rank2 = jnp.sum(jnp.where(one2, before, 0.0), axis=-1, keepdims=True)
    total = base_ref[...] + jnp.sum(onehot, axis=0, keepdims=True)
    base_ref[...] = total
    cnt_ref[...] = total

    col = lax.broadcasted_iota(I32, (tm, 2), 1)
    eid_ref[...] = jnp.where(col == 0, i1, i2) - N_GROUPS
    wt_ref[...] = jnp.where(col == 0, w1, w2)
    rank_ref[...] = jnp.where(col == 0, rank1, rank2).astype(I32)


def _interleave_rows(pair):
    t, w2 = pair.shape
    w = w2 // 2
    r = lax.broadcasted_iota(I32, (2 * t, t), 0)
    c = lax.broadcasted_iota(I32, (2 * t, t), 1)
    pick_even = (r == 2 * c).astype(BF16)
    pick_odd = (r == 2 * c + 1).astype(BF16)
    return (_dot(pick_even, pair[:, :w]) + _dot(pick_odd, pair[:, w:])).astype(BF16)


def _post_mixer_kernel(*refs, n_lhs, paired):
    x_ref = refs[0]
    a_refs = refs[1:1 + n_lhs]
    w_refs = refs[1 + n_lhs:1 + 2 * n_lhs]
    (cg_ref, wq_ref, kv_ref, wo_ref, fg_ref, whi_ref, wlo_ref, rb_ref,
     x_out_ref, hp_ref, eid_ref, wt_ref, rank_ref, cnt_ref, base_ref) = refs[1 + 2 * n_lhs:]

    @pl.when((pl.program_id(0) == 0) & (pl.program_id(1) == 0))
    def _():
        base_ref[...] = jnp.zeros_like(base_ref)

    x = x_ref[0]
    for a_ref, w_ref, is_paired in zip(a_refs, w_refs, paired):
        a = a_ref[0].astype(BF16)
        x = x + _dot(_interleave_rows(a) if is_paired else a, w_ref[...])
    x = _cross_attend(x, cg_ref[...], wq_ref[...], kv_ref[0], wo_ref[...])
    x_out_ref[0] = x
    _route(x, fg_ref[...], whi_ref[...], wlo_ref[...], rb_ref[...], hp_ref, eid_ref, wt_ref, rank_ref, cnt_ref,
           base_ref)


def post_mixer(x3d, lhs_list, w_list, c_gamma, wq, kv, wo, f_gamma, w_grp, b_grp, w_exp, b_exp):
    b, seq, d = x3d.shape
    paired = tuple(a.shape[1] != seq for a in lhs_list)
    n = b * seq
    m = kv.shape[1]
    tm = 512
    nt = seq // tm
    wcat = jnp.pad(jnp.concatenate([w_grp, w_exp], axis=1), ((0, 0), (0, LANES - N_GROUPS - N_EXPERTS)))
    bcat = jnp.pad(jnp.concatenate([b_grp, b_exp]), (0, LANES - N_GROUPS - N_EXPERTS)).reshape(1, LANES)
    whi = wcat.astype(BF16)
    wlo = (wcat - whi.astype(F32)).astype(BF16)
    const = lambda shape: pl.BlockSpec(shape, lambda i, j: (0,) * len(shape))
    tok = lambda width: pl.BlockSpec((tm, width), lambda i, j: (i * nt + j, 0))
    in_specs = [pl.BlockSpec((1, tm, d), lambda i, j: (i, j, 0))]
    in_specs += [pl.BlockSpec((1, tm // 2 if p else tm, a.shape[2]), lambda i, j: (i, j, 0))
                 for a, p in zip(lhs_list, paired)]
    in_specs += [const(w.shape) for w in w_list]
    in_specs += [const((1, d)), const((d, X_W)), pl.BlockSpec((1, m, 2 * X_W), lambda i, j: (i, 0, 0)),
                 const((X_W, d)), const((1, d)), const((d, LANES)), const((d, LANES)), const((1, LANES))]
    return pl.pallas_call(
        functools.partial(_post_mixer_kernel, n_lhs=len(lhs_list), paired=paired),
        grid=(b, nt),
        in_specs=in_specs,
        out_specs=[pl.BlockSpec((1, tm, d), lambda i, j: (i, j, 0)), tok(d // 2), tok(2), tok(2), tok(2),
                   const((1, LANES))],
        out_shape=[jax.ShapeDtypeStruct((b, seq, d), F32),
                   jax.ShapeDtypeStruct((n, d // 2), U32),
                   jax.ShapeDtypeStruct((n, 2), I32),
                   jax.ShapeDtypeStruct((n, 2), F32),
                   jax.ShapeDtypeStruct((n, 2), I32),
                   jax.ShapeDtypeStruct((1, LANES), F32)],
        scratch_shapes=[pltpu.VMEM((1, LANES), F32)],
        compiler_params=_cparams(("arbitrary", "arbitrary")),
        name="post_mixer",
    )(x3d, *lhs_list, *w_list, c_gamma.reshape(1, d), wq, kv, wo, f_gamma.reshape(1, d), whi, wlo, bcat)


def _row_copy(src_ref, src_row, dst_ref, dst_row, sem):
    return pltpu.make_async_copy(src_ref.at[pl.ds(src_row, 1)], dst_ref.at[pl.ds(dst_row, 1)], sem)


def _dispatch_kernel(dest_ref, seg_ref, hp_ref, xs_ref, zbuf, sem, zsem, *, tm):
    @pl.when(pl.program_id(0) == 0)
    def _():
        zbuf[...] = jnp.zeros_like(zbuf)

        def zero_tail(e):
            tail = pl.multiple_of(seg_ref[0, e] - MOE_BLOCK, MOE_BLOCK)
            return pltpu.make_async_copy(zbuf, xs_ref.at[pl.ds(tail, MOE_BLOCK)], zsem)

        def start(e, carry):
            @pl.when(seg_ref[1, e] > 0)
            def _():
                zero_tail(e).start()
            return carry

        def finish(e, carry):
            @pl.when(seg_ref[1, e] > 0)
            def _():
                zero_tail(e).wait()
            return carry

        lax.fori_loop(0, N_EXPERTS, start, 0)
        lax.fori_loop(0, N_EXPERTS, finish, 0)

        def unused(blk):
            return pltpu.make_async_copy(zbuf, xs_ref.at[pl.ds(blk * MOE_BLOCK, MOE_BLOCK)], zsem)

        def start_unused(blk, carry):
            unused(blk).start()
            return carry

        def finish_unused(blk, carry):
            unused(blk).wait()
            return carry

        first_unused = seg_ref[0, N_EXPERTS - 1] // MOE_BLOCK
        lax.fori_loop(first_unused, xs_ref.shape[0] // MOE_BLOCK, start_unused, 0)
        lax.fori_loop(first_unused, xs_ref.shape[0] // MOE_BLOCK, finish_unused, 0)

    def issue(r, carry):
        _row_copy(hp_ref, r, xs_ref, dest_ref[2 * r], sem).start(priority=0)
        _row_copy(hp_ref, r, xs_ref, dest_ref[2 * r + 1], sem).start(priority=1)
        return carry

    lax.fori_loop(0, tm, issue, 0, unroll=ROW_DMA_UNROLL)
    for _ in range(2):
        pltpu.make_async_copy(hp_ref, xs_ref.at[pl.ds(0, tm)], sem).wait()


def moe_dispatch(hp, dest_flat, segments, cap):
    n, c = hp.shape
    tm = ROW_DMA_TILE
    return pl.pallas_call(
        functools.partial(_dispatch_kernel, tm=tm),
        grid=(n // tm,),
        in_specs=[pl.BlockSpec((2 * tm,), lambda i: (i,), memory_space=pltpu.SMEM),
                  pl.BlockSpec(memory_space=pltpu.SMEM),
                  pl.BlockSpec((tm, c), lambda i: (i, 0))],
        out_specs=pl.BlockSpec(memory_space=pl.ANY),
        out_shape=jax.ShapeDtypeStruct((cap, c), U32),
        scratch_shapes=[pltpu.VMEM((MOE_BLOCK, c), U32), pltpu.SemaphoreType.DMA(()), pltpu.SemaphoreType.DMA(())],
        compiler_params=_cparams(("arbitrary",)),
        name="moe_dispatch",
    )(dest_flat, segments, hp)


def _expert_kernel(start_ref, cnt_ref, xs_ref, wg_ref, wu_ref, wd_ref, yb_ref, xbuf, ybuf, sem_in, sem_out,
                   wg_s, wu_s, wd_s, *, nblk):
    e = pl.program_id(0)
    last = pl.num_programs(0) - 1
    nb = cnt_ref[e]
    b0 = start_ref[e]
    total = start_ref[last] + cnt_ref[last]

    def fetch(g):
        slot = g % EXPERT_SLOTS
        return pltpu.make_async_copy(xs_ref.at[pl.ds(g * MOE_BLOCK, MOE_BLOCK)], xbuf.at[slot], sem_in.at[slot])

    def put(g):
        slot = g % EXPERT_SLOTS
        return pltpu.make_async_copy(ybuf.at[slot], yb_ref.at[pl.ds(g * MOE_BLOCK, MOE_BLOCK)], sem_out.at[slot])

    @pl.when(e == 0)
    def _():
        for g in range(EXPERT_AHEAD):
            @pl.when(g < total)
            def _():
                fetch(g).start(priority=1)

    @pl.when(nb > 0)
    def _():
        wg_s[...] = wg_ref[0].astype(BF16)
        wu_s[...] = wu_ref[0].astype(BF16)
        wd_s[...] = wd_ref[0].astype(BF16)

    def block(g, carry):
        slot = g % EXPERT_SLOTS
        fetch(g).wait()

        @pl.when(g + EXPERT_AHEAD < total)
        def _():
            fetch(g + EXPERT_AHEAD).start(priority=1)

        @pl.when(g >= EXPERT_SLOTS)
        def _():
            put(g - EXPERT_SLOTS).wait()

        x = _unpack_halves(xbuf[slot]).astype(BF16)
        a = _dot(x, wg_s[...])
        u = _dot(x, wu_s[...])
        hmid = (a / (1.0 + jnp.exp(-a)) * u).astype(BF16)
        ybuf[slot] = _pack_halves(_dot(hmid, wd_s[...]))
        put(g).start()
        return carry

    lax.fori_loop(b0, b0 + nb, block, 0)

    @pl.when(e == last)
    def _():
        for back in range(1, EXPERT_SLOTS + 1):
            @pl.when(total >= back)
            def _():
                put(total - back).wait()

        ybuf[0] = jnp.zeros(ybuf.shape[1:], ybuf.dtype)

        def fill(blk, carry):
            copy = pltpu.make_async_copy(ybuf.at[0], yb_ref.at[pl.ds(blk * MOE_BLOCK, MOE_BLOCK)], sem_out.at[0])
            copy.start()
            copy.wait()
            return carry

        lax.fori_loop(total, nblk, fill, 0)


def moe_experts(xs, blk_start, blk_cnt, w_gate, w_up, w_down, layer):
    cap, c = xs.shape
    d = 2 * c
    hid = w_gate.shape[2]
    nblk = cap // MOE_BLOCK
    wmap = lambda e, st, cn: (layer * N_EXPERTS + e, 0, 0)
    grid_spec = pltpu.PrefetchScalarGridSpec(
        num_scalar_prefetch=2,
        grid=(N_EXPERTS,),
        in_specs=[pl.BlockSpec(memory_space=pl.ANY),
                  pl.BlockSpec((1, d, hid), wmap),
                  pl.BlockSpec((1, d, hid), wmap),
                  pl.BlockSpec((1, hid, d), wmap)],
        out_specs=pl.BlockSpec(memory_space=pl.ANY),
        scratch_shapes=[pltpu.VMEM((EXPERT_SLOTS, MOE_BLOCK, c), U32), pltpu.VMEM((EXPERT_SLOTS, MOE_BLOCK, c), U32),
                        pltpu.SemaphoreType.DMA((EXPERT_SLOTS,)), pltpu.SemaphoreType.DMA((EXPERT_SLOTS,)),
                        pltpu.VMEM((d, hid), BF16), pltpu.VMEM((d, hid), BF16), pltpu.VMEM((hid, d), BF16)],
    )
    return pl.pallas_call(
        functools.partial(_expert_kernel, nblk=nblk),
        grid_spec=grid_spec,
        out_shape=jax.ShapeDtypeStruct((cap, c), U32),
        compiler_params=_cparams(("arbitrary",)),
        name="moe_experts",
    )(blk_start, blk_cnt, xs, w_gate, w_up, w_down)


def _combine_kernel(dest_ref, next_dest_ref, x_ref, wt_ref, g_ref, yb_ref, o_ref, buf, sem, *, tm, final_norm):
    i = pl.program_id(0)
    slot = i & 1

    def gather(idx_ref, s):
        def issue(r, carry):
            _row_copy(yb_ref, idx_ref[2 * r], buf.at[s, 0], r, sem.at[s]).start(priority=0)
            _row_copy(yb_ref, idx_ref[2 * r + 1], buf.at[s, 1], r, sem.at[s]).start(priority=1)
            return carry

        lax.fori_loop(0, tm, issue, 0, unroll=ROW_DMA_UNROLL)

    @pl.when(i == 0)
    def _():
        gather(dest_ref, 0)

    @pl.when(i + 1 < pl.num_programs(0))
    def _():
        gather(next_dest_ref, 1 - slot)

    for k in range(2):
        pltpu.make_async_copy(yb_ref.at[pl.ds(0, tm)], buf.at[slot, k], sem.at[slot]).wait()
    wt = wt_ref[...]
    y = x_ref[...] + (_unpack_halves(buf[slot, 0]) * wt[:, 0:1] + _unpack_halves(buf[slot, 1]) * wt[:, 1:2])
    o_ref[...] = _rms(y, g_ref[...]) if final_norm else y


def moe_combine(x2d, yb, dest_flat, wt, g_final, final_norm):
    n, d = x2d.shape
    c = yb.shape[1]
    tm = ROW_DMA_TILE
    nt = n // tm
    return pl.pallas_call(
        functools.partial(_combine_kernel, tm=tm, final_norm=final_norm),
        grid=(nt,),
        in_specs=[pl.BlockSpec((2 * tm,), lambda i: (i,), memory_space=pltpu.SMEM),
                  pl.BlockSpec((2 * tm,), lambda i: (jnp.minimum(i + 1, nt - 1),), memory_space=pltpu.SMEM),
                  pl.BlockSpec((tm, d), lambda i: (i, 0)),
                  pl.BlockSpec((tm, 2), lambda i: (i, 0)),
                  pl.BlockSpec((1, d), lambda i: (0, 0)),
                  pl.BlockSpec(memory_space=pl.ANY)],
        out_specs=pl.BlockSpec((tm, d), lambda i: (i, 0)),
        out_shape=jax.ShapeDtypeStruct((n, d), F32),
        scratch_shapes=[pltpu.VMEM((2, 2, tm, c), U32), pltpu.SemaphoreType.DMA((2,))],
        compiler_params=_cparams(("arbitrary",)),
        name="moe_combine",
    )(dest_flat, dest_flat, x2d, wt, g_final.reshape(1, d), yb)


def hier_moe_block(x2d, routing, w_gate, w_up, w_down, layer, g_final, final_norm):
    n = x2d.shape[0]
    cap = 2 * n + N_EXPERTS * MOE_BLOCK
    hp, eid, wt, rank, cnt = routing
    counts = cnt[0, N_GROUPS:N_GROUPS + N_EXPERTS].astype(I32)
    padded = (counts + MOE_BLOCK - 1) // MOE_BLOCK * MOE_BLOCK
    p_ends = jnp.cumsum(padded)
    p_starts = p_ends - padded
    experts = jnp.arange(N_EXPERTS, dtype=I32)
    dest = (jnp.sum(jnp.where(eid[..., None] == experts, p_starts, 0), axis=-1) + rank).reshape(-1)
    xs = moe_dispatch(hp, dest, jnp.stack([p_ends, padded]).astype(I32), cap)
    yb = moe_experts(xs, p_starts // MOE_BLOCK, padded // MOE_BLOCK, w_gate, w_up, w_down, layer)
    return moe_combine(x2d, yb, dest, wt, g_final, final_norm)


def kernel(x, mem, e_norm, e_w_in, e_conv_w, e_conv_b, e_filt_w1, e_filt_b1, e_filt_w2, e_filt_b2, e_filt_w3, e_filt_freq, e_hy_bias, e_lam, e_subln, e_w_out, o_norm, o_w_in, o_sink, o_w_out, c_norm, c_wq, c_wkv, c_wo, f_norm, f_w_grp, f_b_grp, f_w_exp, f_b_exp, f_w_gate, f_w_up, f_w_down, g_mem, g_final):
    b, seq, d = x.shape
    n = b * seq
    m = mem.shape[1]
    tables = dft_tables(seq // 2)
    twiddles = dft_twiddles(seq)
    x2 = x.reshape(n, d)
    mem2 = mem.reshape(b * m, d)
    w_gate = f_w_gate.reshape(DEPTH * N_EXPERTS, d, EXP_HIDDEN)
    w_up = f_w_up.reshape(DEPTH * N_EXPERTS, d, EXP_HIDDEN)
    w_down = f_w_down.reshape(DEPTH * N_EXPERTS, EXP_HIDDEN, d)
    qperm = window_column_perm()
    for i in range(DEPTH):
        j = i // 2
        if i % 2 == 0:
            hyw, qw = 3 * HY_WIDTH, 2 * DIFF_HEADS * DIFF_HEAD_DIM
            w_in = jnp.concatenate([e_w_in[j][:, :hyw],
                                    e_w_in[j][:, hyw:hyw + qw] * (DIFF_HEAD_DIM ** -0.5 * LOG2E),
                                    e_w_in[j][:, hyw + qw:]], axis=1).astype(BF16)
            proj_h, proj_a = norm_matmul(x2, e_norm[j], w_in, 512, [(hyw, F32), (w_in.shape[1] - hyw, BF16)])
            y_hy = hyena_mixer(proj_h.reshape(b, seq, -1), tables, twiddles, e_conv_w[j], e_conv_b[j], e_filt_w1[j],
                               e_filt_b1[j], e_filt_w2[j], e_filt_b2[j], e_filt_w3[j], e_filt_freq[j], e_hy_bias[j])
            y_df = diff_attention(proj_a.reshape(b, seq, -1), e_lam[j], e_subln[j], i)
            w_out = e_w_out[j].astype(BF16)
            mixed, w_mix = [y_hy, y_df], [w_out[:HY_WIDTH], w_out[HY_WIDTH:]]
        else:
            oq = GQA_HEADS * GQA_HD
            w_in = jnp.concatenate([o_w_in[j][:, :oq][:, qperm] * (GQA_HD ** -0.5 * LOG2E), o_w_in[j][:, oq:]],
                                   axis=1).astype(BF16)
            (proj,) = norm_matmul(x2, o_norm[j], w_in, 512, [(w_in.shape[1], BF16)])
            att = window_gqa(proj.reshape(b, seq, -1), o_sink[j])
            mixed, w_mix = [att], [o_w_out[j][qperm].astype(BF16)]
        (kv,) = norm_matmul(mem2, g_mem, c_wkv[i].astype(BF16), 512, [(2 * X_W, BF16)])
        x3, *routing = post_mixer(x2.reshape(b, seq, d), mixed, w_mix, c_norm[i], c_wq[i].astype(BF16),
                                  kv.reshape(b, m, -1), c_wo[i].astype(BF16), f_norm[i], f_w_grp[i], f_b_grp[i],
                                  f_w_exp[i], f_b_exp[i])
        x2 = hier_moe_block(x3.reshape(n, d), routing, w_gate, w_up, w_down, i, g_final, i == DEPTH - 1)
    return x2.reshape(b, seq, d)
```

```python
import functools
import math

import jax
import jax.numpy as jnp
from jax import lax
from jax.experimental import pallas as pl
from jax.experimental.pallas import tpu as pltpu

F32 = jnp.float32
BF16 = jnp.bfloat16
I32 = jnp.int32
U32 = jnp.uint32

D_MODEL = 1024
DEPTH = 4
EPS = 1e-6
NEG = -1e30
HY_WIDTH = 512
HY_BANDS = 16
HY_FILT_HIDDEN = 64
HY_DECAY_TARGET = 1e-2
HY_FAST_PCT = 0.3
HY_SLOW_PCT = 1.5
DIFF_HEADS = 4
DIFF_HEAD_DIM = 64
WIN = 128
GQA_HEADS = 16
GQA_KV = 4
GQA_HD = 64
X_HEADS = 4
X_HD = 128
X_W = X_HEADS * X_HD
N_GROUPS = 4
EXP_PER_GROUP = 8
N_EXPERTS = N_GROUPS * EXP_PER_GROUP
EXP_HIDDEN = 512

LOG2E = 1.4426950408889634
LANES = 128
VMEM_LIMIT = 56 * 1024 * 1024
MOE_BLOCK = 256
DIFF_Q_BLOCK = 256
DIFF_KEY_CHUNK = 512
DIFF_BATCH_PER_STEP = 8
WIN_BATCH_PER_STEP = 4
EXPERT_SLOTS = 4
EXPERT_AHEAD = EXPERT_SLOTS - 1
ROW_DMA_TILE = 1024
ROW_DMA_UNROLL = 8


def _cparams(sem):
    return pltpu.CompilerParams(dimension_semantics=sem, vmem_limit_bytes=VMEM_LIMIT)


def _rms(x, g):
    ms = jnp.mean(x * x, axis=-1, keepdims=True)
    return x * lax.rsqrt(ms + EPS) * g


def _dot(a, b):
    return jnp.dot(a, b, preferred_element_type=F32)


def _dot_nt(a, b):
    return lax.dot_general(a, b, (((1,), (1,)), ((), ())), preferred_element_type=F32)


def _norm_matmul_kernel(x_ref, g_ref, w_ref, *o_refs):
    h = _rms(x_ref[...], g_ref[...]).astype(BF16)
    col = 0
    for o_ref in o_refs:
        width = o_ref.shape[1]
        o_ref[...] = _dot(h, w_ref[:, col:col + width]).astype(o_ref.dtype)
        col += width


def norm_matmul(x2d, gamma, w_bf16, tm, outs):
    n, d = x2d.shape
    f = w_bf16.shape[1]
    assert sum(width for width, _ in outs) == f
    res = pl.pallas_call(
        _norm_matmul_kernel,
        grid=(n // tm,),
        in_specs=[pl.BlockSpec((tm, d), lambda i: (i, 0)),
                  pl.BlockSpec((1, d), lambda i: (0, 0)),
                  pl.BlockSpec((d, f), lambda i: (0, 0))],
        out_specs=[pl.BlockSpec((tm, width), lambda i: (i, 0)) for width, _ in outs],
        out_shape=[jax.ShapeDtypeStruct((n, width), dtype) for width, dtype in outs],
        compiler_params=_cparams(("parallel",)),
        name="norm_matmul",
    )(x2d, gamma.reshape(1, d), w_bf16)
    return res


def _conv3_kernel(u_ref, w_ref, b_ref, oe_ref, oo_ref, slab_ref):
    half = u_ref.shape[1] // 2
    w = w_ref[...]
    bias = b_ref[...]
    row = lax.broadcasted_iota(I32, (half, LANES), 0)
    for k in range(u_ref.shape[2] // LANES):
        cols = slice(k * LANES, (k + 1) * LANES)
        slab_ref[...] = u_ref[0, :, cols].astype(F32)
        ue = slab_ref[pl.ds(0, half, stride=2), :]
        uo = slab_ref[pl.ds(1, half, stride=2), :]
        uo_prev = jnp.where(row == 0, 0.0, pltpu.roll(uo, 1, 0))
        ue_next = jnp.where(row == half - 1, 0.0, pltpu.roll(ue, half - 1, 0))
        even = uo_prev * w[0:1, cols] + ue * w[1:2, cols] + uo * w[2:3, cols] + bias[:, cols]
        odd = ue * w[0:1, cols] + uo * w[1:2, cols] + ue_next * w[2:3, cols] + bias[:, cols]
        oe_ref[0, :, cols] = even.astype(oe_ref.dtype)
        oo_ref[0, :, cols] = odd.astype(oo_ref.dtype)


def conv3(proj3d, conv_w, conv_b):
    b, seq, c = proj3d.shape
    tc = HY_WIDTH
    out = pl.BlockSpec((1, seq // 2, tc), lambda i, j: (i, 0, j))
    return pl.pallas_call(
        _conv3_kernel,
        grid=(b, c // tc),
        in_specs=[pl.BlockSpec((1, seq, tc), lambda i, j: (i, 0, j)),
                  pl.BlockSpec((3, tc), lambda i, j: (0, j)),
                  pl.BlockSpec((1, tc), lambda i, j: (0, j))],
        out_specs=[out, out],
        out_shape=[jax.ShapeDtypeStruct((b, seq // 2, c), BF16)] * 2,
        scratch_shapes=[pltpu.VMEM((seq, LANES), F32)],
        compiler_params=_cparams(("parallel", "parallel")),
        name="hyena_conv3",
    )(proj3d, conv_w, conv_b.reshape(1, c))


def _hy_filter_kernel(t_ref, bands_ref, w1t_ref, w1c_ref, w1s_ref, b1_ref, w2_ref, b2_ref, w3_ref,
                      freq_ref, delta_ref, hf_ref, hb_ref, *, seq, tl):
    hi = lax.Precision.HIGHEST
    parity = pl.program_id(0)
    i = pl.program_id(1)
    t = t_ref[0]
    pos = (2 * (i * tl + lax.broadcasted_iota(I32, (tl, 1), 0)) + parity).astype(F32)
    ang = bands_ref[...] * (2.0 * math.pi * pos / seq)
    f = freq_ref[...]
    pre = (t * w1t_ref[...]
           + jnp.dot(jnp.cos(ang), w1c_ref[...], precision=hi, preferred_element_type=F32)
           - jnp.dot(jnp.sin(ang), w1s_ref[...], precision=hi, preferred_element_type=F32)
           + b1_ref[...])
    a = jnp.sin(f * pre)
    a = jnp.sin(f * (jnp.dot(a, w2_ref[...], precision=hi, preferred_element_type=F32) + b2_ref[...]))
    h = jnp.dot(a, w3_ref[...], precision=hi, preferred_element_type=F32)
    decay = jnp.exp(-t * jnp.abs(delta_ref[...]))
    w = HY_WIDTH
    for o in range(2):
        hf_ref[0, :, o * w:(o + 1) * w] = h[:, o * 2 * w:o * 2 * w + w] * decay
        hb_ref[0, :, o * w:(o + 1) * w] = jnp.where(pos == 0.0, 0.0, h[:, o * 2 * w + w:(o + 1) * 2 * w] * decay)


def hyena_filters_time(seq, w1, b1, w2, b2, w3, freq):
    tl = 512
    half = seq // 2
    hid = LANES
    pad_h = hid - HY_FILT_HIDDEN
    t = jnp.linspace(0.0, 1.0, seq, dtype=F32).reshape(half, 2).T.reshape(2, half, 1)
    bands = jnp.pad(jnp.linspace(1e-4, HY_BANDS - 1, HY_BANDS, dtype=F32)[None], ((0, 0), (0, LANES - HY_BANDS)))
    w1p = jnp.pad(w1, ((0, 0), (0, pad_h)))
    w1t = w1p[0:1]
    w1c = jnp.pad(w1p[1:1 + HY_BANDS], ((0, LANES - HY_BANDS), (0, 0)))
    w1s = jnp.pad(w1p[1 + HY_BANDS:], ((0, LANES - HY_BANDS), (0, 0)))
    b1p = jnp.pad(b1, (0, pad_h)).reshape(1, hid)
    w2p = jnp.pad(w2, ((0, pad_h), (0, pad_h)))
    b2p = jnp.pad(b2, (0, pad_h)).reshape(1, hid)
    w3p = jnp.pad(w3, ((0, pad_h), (0, 0)))
    freqp = jnp.pad(freq, (0, pad_h)).reshape(1, hid)
    max_decay = math.log(HY_DECAY_TARGET) / HY_FAST_PCT
    min_decay = math.log(HY_DECAY_TARGET) / HY_SLOW_PCT
    deltas = jnp.linspace(min_decay, max_decay, HY_WIDTH, dtype=F32)[None]
    fw = w3.shape[1]
    full = lambda shape: pl.BlockSpec(shape, lambda p, i: (0, 0))
    out = pl.BlockSpec((1, tl, 2 * HY_WIDTH), lambda p, i: (p, i, 0))
    return pl.pallas_call(
        functools.partial(_hy_filter_kernel, seq=seq, tl=tl),
        grid=(2, half // tl),
        in_specs=[pl.BlockSpec((1, tl, 1), lambda p, i: (p, i, 0)), full((1, LANES)), full((1, hid)),
                  full((LANES, hid)), full((LANES, hid)), full((1, hid)), full((hid, hid)), full((1, hid)),
                  full((hid, fw)), full((1, hid)), full((1, HY_WIDTH))],
        out_specs=[out, out],
        out_shape=[jax.ShapeDtypeStruct((2, half, 2 * HY_WIDTH), F32)] * 2,
        compiler_params=_cparams(("parallel", "parallel")),
        name="hyena_filter_mlp",
    )(t, bands, w1t, w1c, w1s, b1p, w2p, b2p, w3p, freqp, deltas)


def dft_tables(seq):
    n2 = 4 * seq
    sub = 64
    f = jnp.arange(seq, dtype=I32)[:, None]
    odd = 2 * f + 1
    s1 = jnp.arange(seq // sub, dtype=I32)[None]
    s0 = jnp.arange(sub, dtype=I32)[None]
    ang_p = ((odd * s1) % (n2 // sub)).astype(F32) * (2.0 * math.pi * sub / n2)
    ang_q = ((odd * s0) % n2).astype(F32) * (2.0 * math.pi / n2)
    pc, ps, qc, qs = jnp.cos(ang_p), jnp.sin(ang_p), jnp.cos(ang_q), jnp.sin(ang_q)
    c = (pc[:, :, None] * qc[:, None, :] - ps[:, :, None] * qs[:, None, :]).reshape(seq, seq)
    s = (ps[:, :, None] * qc[:, None, :] + pc[:, :, None] * qs[:, None, :]).reshape(seq, seq)
    return c.astype(BF16), s.astype(BF16), c.T.astype(BF16), s.T.astype(BF16)


def dft_twiddles(seq):
    g = jnp.arange(seq // 2, dtype=I32)[:, None]
    ang = (2 * g + 1).astype(F32) * (2.0 * math.pi / (4 * seq))
    return jnp.cos(ang), jnp.sin(ang)


def _half_transform(c1, s1, cg, sg, x0, x1):
    pc0, ps0 = _dot(c1, x0), _dot(s1, x0)
    pc1, ps1 = _dot(c1, x1), _dot(s1, x1)
    tr = cg * pc1 - sg * ps1
    ti = -(cg * ps1 + sg * pc1)
    return pc0 + tr, ti - ps0, pc0 - tr, ps0 + ti


def _spectrum_kernel(c_ref, s_ref, cg_ref, sg_ref, f0_ref, f1_ref, b0_ref, b1_ref,
                     kgre_ref, kgim_ref, khre_ref, khim_ref):
    c1, s1, cg, sg = c_ref[...], s_ref[...], cg_ref[...], sg_ref[...]
    fre_g, fim_g, fre_h, fim_h = _half_transform(c1, s1, cg, sg, f0_ref[0].astype(BF16), f1_ref[0].astype(BF16))
    bre_g, bim_g, bre_h, bim_h = _half_transform(c1, s1, cg, sg, b0_ref[0].astype(BF16), b1_ref[0].astype(BF16))
    kgre_ref[...] = fre_g + bre_g
    kgim_ref[...] = fim_g - bim_g
    khre_ref[...] = fre_h + bre_h
    khim_ref[...] = fim_h - bim_h


def filter_spectrum(tables, twiddles, hf, hb):
    _, half, cols = hf.shape
    tf, tn = 512, 512
    tab = pl.BlockSpec((tf, half), lambda i, j: (i, 0))
    twd = pl.BlockSpec((tf, 1), lambda i, j: (i, 0))
    even = pl.BlockSpec((1, half, tn), lambda i, j: (0, 0, j))
    odd = pl.BlockSpec((1, half, tn), lambda i, j: (1, 0, j))
    return pl.pallas_call(
        _spectrum_kernel,
        grid=(half // tf, cols // tn),
        in_specs=[tab, tab, twd, twd, even, odd, even, odd],
        out_specs=[pl.BlockSpec((tf, tn), lambda i, j: (i, j))] * 4,
        out_shape=[jax.ShapeDtypeStruct((half, cols), F32)] * 4,
        compiler_params=_cparams(("parallel", "parallel")),
        name="hyena_filter_spectrum",
    )(tables[0], tables[1], twiddles[0], twiddles[1], hf, hf, hb, hb)


def _hy_fwd_kernel(z0_ref, z1_ref, c_ref, s_ref, cg_ref, sg_ref, kgre_ref, kgim_ref, khre_ref, khim_ref,
                   u0re_ref, u0im_ref, u1re_ref, u1im_ref):
    cg, sg = cg_ref[...], sg_ref[...]
    zre_g, zim_g, zre_h, zim_h = _half_transform(c_ref[...], s_ref[...], cg, sg,
                                                 z0_ref[0].astype(BF16), z1_ref[0].astype(BF16))
    kgre, kgim, khre, khim = kgre_ref[...], kgim_ref[...], khre_ref[...], khim_ref[...]
    yre_g = zre_g * kgre - zim_g * kgim
    yim_g = zre_g * kgim + zim_g * kgre
    yre_h = zre_h * khre - zim_h * khim
    yim_h = zre_h * khim + zim_h * khre
    u0re_ref[0] = (yre_g + yre_h).astype(BF16)
    u0im_ref[0] = (yim_g - yim_h).astype(BF16)
    a = yre_g - yre_h
    b = yim_g + yim_h
    u1re_ref[0] = (cg * a - sg * b).astype(BF16)
    u1im_ref[0] = (cg * b + sg * a).astype(BF16)


def hyena_fwd(zsrc, tables, twiddles, kspec, order):
    (ze, ce), (zo, co) = zsrc
    b, half, _ = ze.shape
    w = HY_WIDTH
    tf = 512
    tab = pl.BlockSpec((tf, half), lambda i, j: (j, 0))
    twd = pl.BlockSpec((tf, 1), lambda i, j: (j, 0))
    kblk = pl.BlockSpec((tf, w), lambda i, j: (j, order))
    return pl.pallas_call(
        _hy_fwd_kernel,
        grid=(b, half // tf),
        in_specs=[pl.BlockSpec((1, half, w), lambda i, j: (i, 0, ce)),
                  pl.BlockSpec((1, half, w), lambda i, j: (i, 0, co)),
                  tab, tab, twd, twd, kblk, kblk, kblk, kblk],
        out_specs=[pl.BlockSpec((1, tf, w), lambda i, j: (i, j, 0))] * 4,
        out_shape=[jax.ShapeDtypeStruct((b, half, w), BF16)] * 4,
        compiler_params=_cparams(("parallel", "parallel")),
        name="hyena_dft_fwd",
    )(ze, zo, tables[0], tables[1], twiddles[0], twiddles[1], *kspec)


def _hy_inv_kernel(ct_ref, st_ref, u0re_ref, u0im_ref, u1re_ref, u1im_ref, z0_ref, z1_ref, g0_ref, g1_ref,
                   bias_ref, o_ref, *, scale):
    ct, st = ct_ref[...], st_ref[...]
    w = z0_ref.shape[2]
    y0 = (_dot(ct, u0re_ref[0]) - _dot(st, u0im_ref[0])) * scale
    y1 = (_dot(ct, u1re_ref[0]) - _dot(st, u1im_ref[0])) * scale
    bias = bias_ref[...]
    o_ref[0, :, 0:w] = (g0_ref[0].astype(F32) * (y0 + z0_ref[0].astype(F32) * bias)).astype(o_ref.dtype)
    o_ref[0, :, w:2 * w] = (g1_ref[0].astype(F32) * (y1 + z1_ref[0].astype(F32) * bias)).astype(o_ref.dtype)


def hyena_inv(tables, u, zsrc, gsrc, bias_row, out_dtype):
    b, half, w = u[0].shape
    tt = 512
    tab = pl.BlockSpec((tt, half), lambda i, j: (j, 0))
    ublk = pl.BlockSpec((1, half, w), lambda i, j: (i, 0, 0))

    def rows(col):
        return pl.BlockSpec((1, tt, w), lambda i, j: (i, j, col))

    (ze, ce), (zo, co) = zsrc
    (ge, gce), (go, gco) = gsrc
    return pl.pallas_call(
        functools.partial(_hy_inv_kernel, scale=0.5 / half),
        grid=(b, half // tt),
        in_specs=[tab, tab, ublk, ublk, ublk, ublk, rows(ce), rows(co), rows(gce), rows(gco),
                  pl.BlockSpec((1, w), lambda i, j: (0, 0))],
        out_specs=pl.BlockSpec((1, tt, 2 * w), lambda i, j: (i, j, 0)),
        out_shape=jax.ShapeDtypeStruct((b, half, 2 * w), out_dtype),
        compiler_params=_cparams(("parallel", "parallel")),
        name="hyena_dft_inv",
    )(tables[2], tables[3], *u, ze, zo, ge, go, bias_row)


def hyena_mixer(proj3d, tables, twiddles, conv_w, conv_b, w1, b1, w2, b2, w3, freq, hy_bias):
    seq = proj3d.shape[1]
    ue, uo = conv3(proj3d, conv_w, conv_b)
    hf, hb = hyena_filters_time(seq, w1, b1, w2, b2, w3, freq)
    kspec = filter_spectrum(tables, twiddles, hf, hb)
    v = ((ue, 0), (uo, 0))
    uu = hyena_fwd(v, tables, twiddles, kspec, 0)
    z1 = hyena_inv(tables, uu, v, ((ue, 1), (uo, 1)), hy_bias[0:1], F32)
    z = ((z1, 0), (z1, 1))
    uu = hyena_fwd(z, tables, twiddles, kspec, 1)
    return hyena_inv(tables, uu, z, ((ue, 2), (uo, 2)), hy_bias[1:2], BF16)


def _diff_attn_kernel(slope_ref, q_ref, k_ref, v_ref, lam_ref, sub_ref, o_ref, bias_ref, *, tq, lam_init):
    h = pl.program_id(0)
    qi = pl.program_id(1)
    seq = k_ref.shape[1]

    @pl.when(pl.program_id(2) == 0)
    def _():
        qpos = qi * tq + lax.broadcasted_iota(I32, (tq, seq), 0)
        kpos = lax.broadcasted_iota(I32, (tq, seq), 1)
        bias_ref[...] = slope_ref[h] * jnp.abs(qpos - kpos).astype(F32)

    lane = lax.broadcasted_iota(I32, (1, 2 * DIFF_HEAD_DIM), 1)
    ck = DIFF_KEY_CHUNK
    l = lam_ref[...]
    lam_full = (jnp.exp(jnp.sum(l[0:1] * l[1:2], axis=-1, keepdims=True))
                - jnp.exp(jnp.sum(l[2:3] * l[3:4], axis=-1, keepdims=True)) + lam_init)

    def attend(bb, m):
        q = q_ref[bb]
        k = k_ref[bb]
        v = v_ref[bb]
        keep = (lane < DIFF_HEAD_DIM) if m == 0 else (lane >= DIFF_HEAD_DIM)
        qm = jnp.where(keep, q, jnp.zeros_like(q))
        s = [_dot_nt(qm, k[c:c + ck]) - bias_ref[:, c:c + ck] for c in range(0, seq, ck)]
        mx = functools.reduce(jnp.maximum, [jnp.max(sc, axis=-1, keepdims=True) for sc in s])
        acc = jnp.zeros((tq, 2 * DIFF_HEAD_DIM), F32)
        den = jnp.zeros((tq, 1), F32)
        for i, sc in enumerate(s):
            e = jnp.exp2(sc - mx)
            den = den + jnp.sum(e, axis=-1, keepdims=True)
            acc = acc + _dot(e.astype(BF16), v[i * ck:(i + 1) * ck])
        return acc / den

    for bb in range(q_ref.shape[0]):
        o = attend(bb, 0) - lam_full * attend(bb, 1)
        o_ref[bb] = (_rms(o, sub_ref[...]) * (1.0 - lam_init)).astype(o_ref.dtype)


def diff_attention(proj3d, lam, subln, layer_idx):
    b, seq, _ = proj3d.shape
    tq = DIFF_Q_BLOCK
    nb = DIFF_BATCH_PER_STEP
    hw = 2 * DIFF_HEAD_DIM
    qb, kb, vb = 0, DIFF_HEADS, 2 * DIFF_HEADS
    lam_init = 0.8 - 0.6 * math.exp(-0.3 * layer_idx)
    slopes = 2.0 ** (-8.0 * jnp.arange(1, DIFF_HEADS + 1, dtype=F32) / DIFF_HEADS) * LOG2E
    return pl.pallas_call(
        functools.partial(_diff_attn_kernel, tq=tq, lam_init=lam_init),
        grid=(DIFF_HEADS, seq // tq, b // nb),
        in_specs=[pl.BlockSpec(memory_space=pltpu.SMEM),
                  pl.BlockSpec((nb, tq, hw), lambda h, j, i: (i, j, qb + h)),
                  pl.BlockSpec((nb, seq, hw), lambda h, j, i: (i, 0, kb + h)),
                  pl.BlockSpec((nb, seq, hw), lambda h, j, i: (i, 0, vb + h)),
                  pl.BlockSpec((4, DIFF_HEAD_DIM), lambda h, j, i: (0, 0)),
                  pl.BlockSpec((1, hw), lambda h, j, i: (0, 0))],
        out_specs=pl.BlockSpec((nb, tq, hw), lambda h, j, i: (i, j, h)),
        out_shape=jax.ShapeDtypeStruct((b, seq, DIFF_HEADS * hw), BF16),
        scratch_shapes=[pltpu.VMEM((tq, seq), F32)],
        compiler_params=_cparams(("parallel", "parallel", "arbitrary")),
        name="diff_attention",
    )(slopes, proj3d, proj3d, proj3d, lam, subln.reshape(1, hw))


def _win_attn_kernel(slope_ref, sink_ref, q_ref, kp_ref, kc_ref, kn_ref, vp_ref, vc_ref, vn_ref, o_ref, bias_ref,
                     *, tq, seq):
    qi = pl.program_id(0)
    span = 3 * tq
    group = GQA_HEADS // GQA_KV

    @pl.when(pl.program_id(1) == 0)
    def _():
        qpos = qi * tq + lax.broadcasted_iota(I32, (tq, span), 0)
        kpos = (qi - 1) * tq + lax.broadcasted_iota(I32, (tq, span), 1)
        rel = jnp.abs(qpos - kpos)
        relf = rel.astype(F32)
        masked = jnp.where((rel <= WIN) & (kpos >= 0) & (kpos < seq), 0.0, -NEG)
        for head in range(GQA_HEADS):
            bias_ref[head] = slope_ref[head] * relf + masked

    low = lax.broadcasted_iota(I32, (1, LANES), 1) < GQA_HD
    for bb in range(q_ref.shape[0]):
        kwin = jnp.concatenate([kp_ref[bb], kc_ref[bb], kn_ref[bb]], axis=0)
        vwin = jnp.concatenate([vp_ref[bb], vc_ref[bb], vn_ref[bb]], axis=0)
        for p in range(GQA_KV // 2):
            kb = kwin[:, p * LANES:(p + 1) * LANES]
            vb = vwin[:, p * LANES:(p + 1) * LANES]
            outs = []
            for half in range(2):
                kv = 2 * p + half
                mine = low if half == 0 else jnp.logical_not(low)
                qblocks = [q_ref[bb, :, (group * p + r) * LANES:(group * p + r + 1) * LANES] for r in range(group)]
                qs = jnp.concatenate([jnp.where(mine, qb, jnp.zeros_like(qb)) for qb in qblocks], axis=0)
                s = _dot_nt(qs, kb)
                es, inv = [], []
                for r in range(group):
                    head = kv * group + r
                    sink = sink_ref[head]
                    sr = s[r * tq:(r + 1) * tq] - bias_ref[head]
                    m = jnp.maximum(jnp.max(sr, axis=-1, keepdims=True), sink)
                    e = jnp.exp2(sr - m)
                    inv.append(1.0 / (jnp.sum(e, axis=-1, keepdims=True) + jnp.exp2(sink - m)))
                    es.append(e.astype(BF16))
                o = _dot(jnp.concatenate(es, axis=0), vb)
                outs.append([o[r * tq:(r + 1) * tq] * inv[r] for r in range(group)])
            for r in range(group):
                col = (group * p + r) * LANES
                o_ref[bb, :, col:col + LANES] = jnp.where(low, outs[0][r], outs[1][r]).astype(o_ref.dtype)


def window_head_order():
    group = GQA_HEADS // GQA_KV
    order = []
    for p in range(GQA_KV // 2):
        for r in range(group):
            order += [(2 * p) * group + r, (2 * p + 1) * group + r]
    return order


def window_column_perm():
    cols = []
    for head in window_head_order():
        cols += list(range(head * GQA_HD, (head + 1) * GQA_HD))
    return jnp.asarray(cols, dtype=I32)


def window_gqa(proj3d, sink):
    b, seq, _ = proj3d.shape
    tq = WIN
    nq = seq // tq
    oq = GQA_HEADS * GQA_HD
    okv = GQA_KV * GQA_HD
    kcol, vcol = oq // okv, oq // okv + 1
    slopes = 2.0 ** (-8.0 * jnp.arange(1, GQA_HEADS + 1, dtype=F32) / GQA_HEADS) * LOG2E

    nb = WIN_BATCH_PER_STEP

    def neighbour(col, step):
        return pl.BlockSpec((nb, tq, okv), lambda j, i: (i, jnp.clip(j + step, 0, nq - 1), col))

    return pl.pallas_call(
        functools.partial(_win_attn_kernel, tq=tq, seq=seq),
        grid=(nq, b // nb),
        in_specs=[pl.BlockSpec(memory_space=pltpu.SMEM),
                  pl.BlockSpec(memory_space=pltpu.SMEM),
                  pl.BlockSpec((nb, tq, oq), lambda j, i: (i, j, 0)),
                  neighbour(kcol, -1), neighbour(kcol, 0), neighbour(kcol, 1),
                  neighbour(vcol, -1), neighbour(vcol, 0), neighbour(vcol, 1)],
        out_specs=pl.BlockSpec((nb, tq, oq), lambda j, i: (i, j, 0)),
        out_shape=jax.ShapeDtypeStruct((b, seq, oq), BF16),
        scratch_shapes=[pltpu.VMEM((GQA_HEADS, tq, 3 * tq), F32)],
        compiler_params=_cparams(("parallel", "arbitrary")),
        name="window_gqa",
    )(slopes, sink.astype(F32) * LOG2E, proj3d, proj3d, proj3d, proj3d, proj3d, proj3d, proj3d)


def _cross_attend(x, g, wq, kv, wo):
    h = _rms(x, g).astype(BF16)
    q = _dot(h, wq)
    scale = X_HD ** -0.5
    outs = []
    for hd in range(X_HEADS):
        qh = q[:, hd * X_HD:(hd + 1) * X_HD].astype(BF16)
        kh = kv[:, hd * X_HD:(hd + 1) * X_HD]
        vh = kv[:, X_W + hd * X_HD:X_W + (hd + 1) * X_HD]
        s = _dot_nt(qh, kh) * scale
        e = jnp.exp(s - jnp.max(s, axis=-1, keepdims=True))
        p = e / jnp.sum(e, axis=-1, keepdims=True)
        outs.append(_dot(p.astype(BF16), vh))
    o = jnp.concatenate(outs, axis=-1).astype(BF16)
    return x + _dot(o, wo)


def _split_bf16(x):
    hi = x.astype(BF16)
    lo = (x - hi.astype(F32)).astype(BF16)
    return hi, lo


def _pack_halves(h):
    c = h.shape[1] // 2
    left = lax.bitcast_convert_type(h[:, :c].astype(BF16).astype(F32), U32)
    right = lax.bitcast_convert_type(h[:, c:].astype(BF16).astype(F32), U32)
    return left | (right >> 16)


def _unpack_halves(p):
    left = lax.bitcast_convert_type(p & jnp.uint32(0xFFFF0000), F32)
    right = lax.bitcast_convert_type(p << 16, F32)
    return jnp.concatenate([left, right], axis=-1)


def _route(x, g, whi, wlo, bias, hp_ref, eid_ref, wt_ref, rank_ref, cnt_ref, base_ref):
    tm = x.shape[0]
    h = _rms(x, g)
    hp_ref[...] = _pack_halves(h)
    hhi, hlo = _split_bf16(h)
    logits = _dot(hhi, whi) + _dot(hlo, whi) + _dot(hhi, wlo) + bias
    lane = lax.broadcasted_iota(I32, logits.shape, 1)
    big = jnp.int32(LANES)
    ninf = -jnp.inf

    gl = jnp.where(lane < N_GROUPS, logits, ninf)
    gmax = jnp.max(gl, axis=-1, keepdims=True)
    gsel = jnp.min(jnp.where(gl == gmax, lane, big), axis=-1, keepdims=True)
    ggate = 1.0 / jnp.sum(jnp.exp(gl - gmax), axis=-1, keepdims=True)

    lo_lane = N_GROUPS + gsel * EXP_PER_GROUP
    el = jnp.where((lane >= lo_lane) & (lane < lo_lane + EXP_PER_GROUP), logits, ninf)
    v1 = jnp.max(el, axis=-1, keepdims=True)
    i1 = jnp.min(jnp.where(el == v1, lane, big), axis=-1, keepdims=True)
    el2 = jnp.where(lane == i1, ninf, el)
    v2 = jnp.max(el2, axis=-1, keepdims=True)
    i2 = jnp.min(jnp.where(el2 == v2, lane, big), axis=-1, keepdims=True)
    e2 = jnp.exp(v2 - v1)
    w1 = ggate / (1.0 + e2)
    w2 = ggate * e2 / (1.0 + e2)

    one1 = lane == i1
    one2 = lane == i2
    onehot = (one1 | one2).astype(F32)
    r = lax.broadcasted_iota(I32, (tm, tm), 0)
    c = lax.broadcasted_iota(I32, (tm, tm), 1)
    tri = (c < r).astype(BF16)
    before = _dot(tri, onehot.astype(BF16)) + base_ref[...]
    rank1 = jnp.sum(jnp.where(one1, before, 0.0), axis=-1, keepdims=True)
    rank2 = jnp.sum(jnp.where(one2, before, 0.0), axis=-1, keepdims=True)
    total = base_ref[...] + jnp.sum(onehot, axis=0, keepdims=True)
    base_ref[...] = total
    cnt_ref[...] = total

    col = lax.broadcasted_iota(I32, (tm, 2), 1)
    eid_ref[...] = jnp.where(col == 0, i1, i2) - N_GROUPS
    wt_ref[...] = jnp.where(col == 0, w1, w2)
    rank_ref[...] = jnp.where(col == 0, rank1, rank2).astype(I32)


def _interleave_rows(pair):
    t, w2 = pair.shape
    w = w2 // 2
    r = lax.broadcasted_iota(I32, (2 * t, t), 0)
    c = lax.broadcasted_iota(I32, (2 * t, t), 1)
    pick_even = (r == 2 * c).astype(BF16)
    pick_odd = (r == 2 * c + 1).astype(BF16)
    return (_dot(pick_even, pair[:, :w]) + _dot(pick_odd, pair[:, w:])).astype(BF16)


def _post_mixer_kernel(*refs, n_lhs, paired):
    x_ref = refs[0]
    a_refs = refs[1:1 + n_lhs]
    w_refs = refs[1 + n_lhs:1 + 2 * n_lhs]
    (cg_ref, wq_ref, kv_ref, wo_ref, fg_ref, whi_ref, wlo_ref, rb_ref,
     x_out_ref, hp_ref, eid_ref, wt_ref, rank_ref, cnt_ref, base_ref) = refs[1 + 2 * n_lhs:]

    @pl.when((pl.program_id(0) == 0) & (pl.program_id(1) == 0))
    def _():
        base_ref[...] = jnp.zeros_like(base_ref)

    x = x_ref[0]
    for a_ref, w_ref, is_paired in zip(a_refs, w_refs, paired):
        a = a_ref[0].astype(BF16)
        x = x + _dot(_interleave_rows(a) if is_paired else a, w_ref[...])
    x = _cross_attend(x, cg_ref[...], wq_ref[...], kv_ref[0], wo_ref[...])
    x_out_ref[0] = x
    _route(x, fg_ref[...], whi_ref[...], wlo_ref[...], rb_ref[...], hp_ref, eid_ref, wt_ref, rank_ref, cnt_ref,
           base_ref)


def post_mixer(x3d, lhs_list, w_list, c_gamma, wq, kv, wo, f_gamma, w_grp, b_grp, w_exp, b_exp):
    b, seq, d = x3d.shape
    paired = tuple(a.shape[1] != seq for a in lhs_list)
    n = b * seq
    m = kv.shape[1]
    tm = 512
    nt = seq // tm
    wcat = jnp.pad(jnp.concatenate([w_grp, w_exp], axis=1), ((0, 0), (0, LANES - N_GROUPS - N_EXPERTS)))
    bcat = jnp.pad(jnp.concatenate([b_grp, b_exp]), (0, LANES - N_GROUPS - N_EXPERTS)).reshape(1, LANES)
    whi = wcat.astype(BF16)
    wlo = (wcat - whi.astype(F32)).astype(BF16)
    const = lambda shape: pl.BlockSpec(shape, lambda i, j: (0,) * len(shape))
    tok = lambda width: pl.BlockSpec((tm, width), lambda i, j: (i * nt + j, 0))
    in_specs = [pl.BlockSpec((1, tm, d), lambda i, j: (i, j, 0))]
    in_specs += [pl.BlockSpec((1, tm // 2 if p else tm, a.shape[2]), lambda i, j: (i, j, 0))
                 for a, p in zip(lhs_list, paired)]
    in_specs += [const(w.shape) for w in w_list]
    in_specs += [const((1, d)), const((d, X_W)), pl.BlockSpec((1, m, 2 * X_W), lambda i, j: (i, 0, 0)),
                 const((X_W, d)), const((1, d)), const((d, LANES)), const((d, LANES)), const((1, LANES))]
    return pl.pallas_call(
        functools.partial(_post_mixer_kernel, n_lhs=len(lhs_list), paired=paired),
        grid=(b, nt),
        in_specs=in_specs,
        out_specs=[pl.BlockSpec((1, tm, d), lambda i, j: (i, j, 0)), tok(d // 2), tok(2), tok(2), tok(2),
                   const((1, LANES))],
        out_shape=[jax.ShapeDtypeStruct((b, seq, d), F32),
                   jax.ShapeDtypeStruct((n, d // 2), U32),
                   jax.ShapeDtypeStruct((n, 2), I32),
                   jax.ShapeDtypeStruct((n, 2), F32),
                   jax.ShapeDtypeStruct((n, 2), I32),
                   jax.ShapeDtypeStruct((1, LANES), F32)],
        scratch_shapes=[pltpu.VMEM((1, LANES), F32)],
        compiler_params=_cparams(("arbitrary", "arbitrary")),
        name="post_mixer",
    )(x3d, *lhs_list, *w_list, c_gamma.reshape(1, d), wq, kv, wo, f_gamma.reshape(1, d), whi, wlo, bcat)


def _row_copy(src_ref, src_row, dst_ref, dst_row, sem):
    return pltpu.make_async_copy(src_ref.at[pl.ds(src_row, 1)], dst_ref.at[pl.ds(dst_row, 1)], sem)


def _dispatch_kernel(dest_ref, seg_ref, hp_ref, xs_ref, zbuf, sem, zsem, *, tm):
    @pl.when(pl.program_id(0) == 0)
    def _():
        zbuf[...] = jnp.zeros_like(zbuf)

        def zero_tail(e):
            tail = pl.multiple_of(seg_ref[0, e] - MOE_BLOCK, MOE_BLOCK)
            return pltpu.make_async_copy(zbuf, xs_ref.at[pl.ds(tail, MOE_BLOCK)], zsem)

        def start(e, carry):
            @pl.when(seg_ref[1, e] > 0)
            def _():
                zero_tail(e).start()
            return carry

        def finish(e, carry):
            @pl.when(seg_ref[1, e] > 0)
            def _():
                zero_tail(e).wait()
            return carry

        lax.fori_loop(0, N_EXPERTS, start, 0)
        lax.fori_loop(0, N_EXPERTS, finish, 0)

        def unused(blk):
            return pltpu.make_async_copy(zbuf, xs_ref.at[pl.ds(blk * MOE_BLOCK, MOE_BLOCK)], zsem)

        def start_unused(blk, carry):
            unused(blk).start()
            return carry

        def finish_unused(blk, carry):
            unused(blk).wait()
            return carry

        first_unused = seg_ref[0, N_EXPERTS - 1] // MOE_BLOCK
        lax.fori_loop(first_unused, xs_ref.shape[0] // MOE_BLOCK, start_unused, 0)
        lax.fori_loop(first_unused, xs_ref.shape[0] // MOE_BLOCK, finish_unused, 0)

    def issue(r, carry):
        _row_copy(hp_ref, r, xs_ref, dest_ref[2 * r], sem).start(priority=0)
        _row_copy(hp_ref, r, xs_ref, dest_ref[2 * r + 1], sem).start(priority=1)
        return carry

    lax.fori_loop(0, tm, issue, 0, unroll=ROW_DMA_UNROLL)
    for _ in range(2):
        pltpu.make_async_copy(hp_ref, xs_ref.at[pl.ds(0, tm)], sem).wait()


def moe_dispatch(hp, dest_flat, segments, cap):
    n, c = hp.shape
    tm = ROW_DMA_TILE
    return pl.pallas_call(
        functools.partial(_dispatch_kernel, tm=tm),
        grid=(n // tm,),
        in_specs=[pl.BlockSpec((2 * tm,), lambda i: (i,), memory_space=pltpu.SMEM),
                  pl.BlockSpec(memory_space=pltpu.SMEM),
                  pl.BlockSpec((tm, c), lambda i: (i, 0))],
        out_specs=pl.BlockSpec(memory_space=pl.ANY),
        out_shape=jax.ShapeDtypeStruct((cap, c), U32),
        scratch_shapes=[pltpu.VMEM((MOE_BLOCK, c), U32), pltpu.SemaphoreType.DMA(()), pltpu.SemaphoreType.DMA(())],
        compiler_params=_cparams(("arbitrary",)),
        name="moe_dispatch",
    )(dest_flat, segments, hp)


def _expert_kernel(start_ref, cnt_ref, xs_ref, wg_ref, wu_ref, wd_ref, yb_ref, xbuf, ybuf, sem_in, sem_out,
                   wg_s, wu_s, wd_s, *, nblk):
    e = pl.program_id(0)
    last = pl.num_programs(0) - 1
    nb = cnt_ref[e]
    b0 = start_ref[e]
    total = start_ref[last] + cnt_ref[last]

    def fetch(g):
        slot = g % EXPERT_SLOTS
        return pltpu.make_async_copy(xs_ref.at[pl.ds(g * MOE_BLOCK, MOE_BLOCK)], xbuf.at[slot], sem_in.at[slot])

    def put(g):
        slot = g % EXPERT_SLOTS
        return pltpu.make_async_copy(ybuf.at[slot], yb_ref.at[pl.ds(g * MOE_BLOCK, MOE_BLOCK)], sem_out.at[slot])

    @pl.when(e == 0)
    def _():
        for g in range(EXPERT_AHEAD):
            @pl.when(g < total)
            def _():
                fetch(g).start(priority=1)

    @pl.when(nb > 0)
    def _():
        wg_s[...] = wg_ref[0].astype(BF16)
        wu_s[...] = wu_ref[0].astype(BF16)
        wd_s[...] = wd_ref[0].astype(BF16)

    def block(g, carry):
        slot = g % EXPERT_SLOTS
        fetch(g).wait()

        @pl.when(g + EXPERT_AHEAD < total)
        def _():
            fetch(g + EXPERT_AHEAD).start(priority=1)

        @pl.when(g >= EXPERT_SLOTS)
        def _():
            put(g - EXPERT_SLOTS).wait()

        x = _unpack_halves(xbuf[slot]).astype(BF16)
        a = _dot(x, wg_s[...])
        u = _dot(x, wu_s[...])
        hmid = (a / (1.0 + jnp.exp(-a)) * u).astype(BF16)
        ybuf[slot] = _pack_halves(_dot(hmid, wd_s[...]))
        put(g).start()
        return carry

    lax.fori_loop(b0, b0 + nb, block, 0)

    @pl.when(e == last)
    def _():
        for back in range(1, EXPERT_SLOTS + 1):
            @pl.when(total >= back)
            def _():
                put(total - back).wait()

        ybuf[0] = jnp.zeros(ybuf.shape[1:], ybuf.dtype)

        def fill(blk, carry):
            copy = pltpu.make_async_copy(ybuf.at[0], yb_ref.at[pl.ds(blk * MOE_BLOCK, MOE_BLOCK)], sem_out.at[0])
            copy.start()
            copy.wait()
            return carry

        lax.fori_loop(total, nblk, fill, 0)


def moe_experts(xs, blk_start, blk_cnt, w_gate, w_up, w_down, layer):
    cap, c = xs.shape
    d = 2 * c
    hid = w_gate.shape[2]
    nblk = cap // MOE_BLOCK
    wmap = lambda e, st, cn: (layer * N_EXPERTS + e, 0, 0)
    grid_spec = pltpu.PrefetchScalarGridSpec(
        num_scalar_prefetch=2,
        grid=(N_EXPERTS,),
        in_specs=[pl.BlockSpec(memory_space=pl.ANY),
                  pl.BlockSpec((1, d, hid), wmap),
                  pl.BlockSpec((1, d, hid), wmap),
                  pl.BlockSpec((1, hid, d), wmap)],
        out_specs=pl.BlockSpec(memory_space=pl.ANY),
        scratch_shapes=[pltpu.VMEM((EXPERT_SLOTS, MOE_BLOCK, c), U32), pltpu.VMEM((EXPERT_SLOTS, MOE_BLOCK, c), U32),
                        pltpu.SemaphoreType.DMA((EXPERT_SLOTS,)), pltpu.SemaphoreType.DMA((EXPERT_SLOTS,)),
                        pltpu.VMEM((d, hid), BF16), pltpu.VMEM((d, hid), BF16), pltpu.VMEM((hid, d), BF16)],
    )
    return pl.pallas_call(
        functools.partial(_expert_kernel, nblk=nblk),
        grid_spec=grid_spec,
        out_shape=jax.ShapeDtypeStruct((cap, c), U32),
        compiler_params=_cparams(("arbitrary",)),
        name="moe_experts",
    )(blk_start, blk_cnt, xs, w_gate, w_up, w_down)


def _combine_kernel(dest_ref, next_dest_ref, x_ref, wt_ref, g_ref, yb_ref, o_ref, buf, sem, *, tm, final_norm):
    i = pl.program_id(0)
    slot = i & 1

    def gather(idx_ref, s):
        def issue(r, carry):
            _row_copy(yb_ref, idx_ref[2 * r], buf.at[s, 0], r, sem.at[s]).start(priority=0)
            _row_copy(yb_ref, idx_ref[2 * r + 1], buf.at[s, 1], r, sem.at[s]).start(priority=1)
            return carry

        lax.fori_loop(0, tm, issue, 0, unroll=ROW_DMA_UNROLL)

    @pl.when(i == 0)
    def _():
        gather(dest_ref, 0)

    @pl.when(i + 1 < pl.num_programs(0))
    def _():
        gather(next_dest_ref, 1 - slot)

    for k in range(2):
        pltpu.make_async_copy(yb_ref.at[pl.ds(0, tm)], buf.at[slot, k], sem.at[slot]).wait()
    wt = wt_ref[...]
    y = x_ref[...] + (_unpack_halves(buf[slot, 0]) * wt[:, 0:1] + _unpack_halves(buf[slot, 1]) * wt[:, 1:2])
    o_ref[...] = _rms(y, g_ref[...]) if final_norm else y


def moe_combine(x2d, yb, dest_flat, wt, g_final, final_norm):
    n, d = x2d.shape
    c = yb.shape[1]
    tm = ROW_DMA_TILE
    nt = n // tm
    return pl.pallas_call(
        functools.partial(_combine_kernel, tm=tm, final_norm=final_norm),
        grid=(nt,),
        in_specs=[pl.BlockSpec((2 * tm,), lambda i: (i,), memory_space=pltpu.SMEM),
                  pl.BlockSpec((2 * tm,), lambda i: (jnp.minimum(i + 1, nt - 1),), memory_space=pltpu.SMEM),
                  pl.BlockSpec((tm, d), lambda i: (i, 0)),
                  pl.BlockSpec((tm, 2), lambda i: (i, 0)),
                  pl.BlockSpec((1, d), lambda i: (0, 0)),
                  pl.BlockSpec(memory_space=pl.ANY)],
        out_specs=pl.BlockSpec((tm, d), lambda i: (i, 0)),
        out_shape=jax.ShapeDtypeStruct((n, d), F32),
        scratch_shapes=[pltpu.VMEM((2, 2, tm, c), U32), pltpu.SemaphoreType.DMA((2,))],
        compiler_params=_cparams(("arbitrary",)),
        name="moe_combine",
    )(dest_flat, dest_flat, x2d, wt, g_final.reshape(1, d), yb)


def hier_moe_block(x2d, routing, w_gate, w_up, w_down, layer, g_final, final_norm):
    n = x2d.shape[0]
    cap = 2 * n + N_EXPERTS * MOE_BLOCK
    hp, eid, wt, rank, cnt = routing
    counts = cnt[0, N_GROUPS:N_GROUPS + N_EXPERTS].astype(I32)
    padded = (counts + MOE_BLOCK - 1) // MOE_BLOCK * MOE_BLOCK
    p_ends = jnp.cumsum(padded)
    p_starts = p_ends - padded
    experts = jnp.arange(N_EXPERTS, dtype=I32)
    dest = (jnp.sum(jnp.where(eid[..., None] == experts, p_starts, 0), axis=-1) + rank).reshape(-1)
    xs = moe_dispatch(hp, dest, jnp.stack([p_ends, padded]).astype(I32), cap)
    yb = moe_experts(xs, p_starts // MOE_BLOCK, padded // MOE_BLOCK, w_gate, w_up, w_down, layer)
    return moe_combine(x2d, yb, dest, wt, g_final, final_norm)


def kernel(x, mem, e_norm, e_w_in, e_conv_w, e_conv_b, e_filt_w1, e_filt_b1, e_filt_w2, e_filt_b2, e_filt_w3, e_filt_freq, e_hy_bias, e_lam, e_subln, e_w_out, o_norm, o_w_in, o_sink, o_w_out, c_norm, c_wq, c_wkv, c_wo, f_norm, f_w_grp, f_b_grp, f_w_exp, f_b_exp, f_w_gate, f_w_up, f_w_down, g_mem, g_final):
    b, seq, d = x.shape
    n = b * seq
    m = mem.shape[1]
    tables = dft_tables(seq // 2)
    twiddles = dft_twiddles(seq)
    x2 = x.reshape(n, d)
    mem2 = mem.reshape(b * m, d)
    w_gate = f_w_gate.reshape(DEPTH * N_EXPERTS, d, EXP_HIDDEN)
    w_up = f_w_up.reshape(DEPTH * N_EXPERTS, d, EXP_HIDDEN)
    w_down = f_w_down.reshape(DEPTH * N_EXPERTS, EXP_HIDDEN, d)
    qperm = window_column_perm()
    for i in range(DEPTH):
        j = i // 2
        if i % 2 == 0:
            hyw, qw = 3 * HY_WIDTH, 2 * DIFF_HEADS * DIFF_HEAD_DIM
            w_in = jnp.concatenate([e_w_in[j][:, :hyw],
                                    e_w_in[j][:, hyw:hyw + qw] * (DIFF_HEAD_DIM ** -0.5 * LOG2E),
                                    e_w_in[j][:, hyw + qw:]], axis=1).astype(BF16)
            proj_h, proj_a = norm_matmul(x2, e_norm[j], w_in, 512, [(hyw, BF16), (w_in.shape[1] - hyw, BF16)])
            y_hy = hyena_mixer(proj_h.reshape(b, seq, -1), tables, twiddles, e_conv_w[j], e_conv_b[j], e_filt_w1[j],
                               e_filt_b1[j], e_filt_w2[j], e_filt_b2[j], e_filt_w3[j], e_filt_freq[j], e_hy_bias[j])
            y_df = diff_attention(proj_a.reshape(b, seq, -1), e_lam[j], e_subln[j], i)
            w_out = e_w_out[j].astype(BF16)
            mixed, w_mix = [y_hy, y_df], [w_out[:HY_WIDTH], w_out[HY_WIDTH:]]
        else:
            oq = GQA_HEADS * GQA_HD
            w_in = jnp.concatenate([o_w_in[j][:, :oq][:, qperm] * (GQA_HD ** -0.5 * LOG2E), o_w_in[j][:, oq:]],
                                   axis=1).astype(BF16)
            (proj,) = norm_matmul(x2, o_norm[j], w_in, 512, [(w_in.shape[1], BF16)])
            att = window_gqa(proj.reshape(b, seq, -1), o_sink[j])
            mixed, w_mix = [att], [o_w_out[j][qperm].astype(BF16)]
        (kv,) = norm_matmul(mem2, g_mem, c_wkv[i].astype(BF16), 512, [(2 * X_W, BF16)])
        x3, *routing = post_mixer(x2.reshape(b, seq, d), mixed, w_mix, c_norm[i], c_wq[i].astype(BF16),
                                  kv.reshape(b, m, -1), c_wo[i].astype(BF16), f_norm[i], f_w_grp[i], f_b_grp[i],
                                  f_w_exp[i], f_b_exp[i])
        x2 = hier_moe_block(x3.reshape(n, d), routing, w_gate, w_up, w_down, i, g_final, i == DEPTH - 1)
    return x2.reshape(b, seq, d)
```

```python
import functools
import math

import jax
import jax.numpy as jnp
from jax import lax
from jax.experimental import pallas as pl
from jax.experimental.pallas import tpu as pltpu

F32 = jnp.float32
BF16 = jnp.bfloat16
I32 = jnp.int32
U32 = jnp.uint32

D_MODEL = 1024
DEPTH = 4
EPS = 1e-6
NEG = -1e30
HY_WIDTH = 512
HY_BANDS = 16
HY_FILT_HIDDEN = 64
HY_DECAY_TARGET = 1e-2
HY_FAST_PCT = 0.3
HY_SLOW_PCT = 1.5
DIFF_HEADS = 4
DIFF_HEAD_DIM = 64
WIN = 128
GQA_HEADS = 16
GQA_KV = 4
GQA_HD = 64
X_HEADS = 4
X_HD = 128
X_W = X_HEADS * X_HD
N_GROUPS = 4
EXP_PER_GROUP = 8
N_EXPERTS = N_GROUPS * EXP_PER_GROUP
EXP_HIDDEN = 512

LOG2E = 1.4426950408889634
LANES = 128
VMEM_LIMIT = 56 * 1024 * 1024
MOE_BLOCK = 256
DIFF_Q_BLOCK = 256
DIFF_KEY_CHUNK = 512
DIFF_BATCH_PER_STEP = 8
WIN_BATCH_PER_STEP = 4
EXPERT_SLOTS = 4
EXPERT_AHEAD = EXPERT_SLOTS - 1
ROW_DMA_TILE = 1024
ROW_DMA_UNROLL = 8


def _cparams(sem):
    return pltpu.CompilerParams(dimension_semantics=sem, vmem_limit_bytes=VMEM_LIMIT)


def _rms(x, g):
    ms = jnp.mean(x * x, axis=-1, keepdims=True)
    return x * lax.rsqrt(ms + EPS) * g


def _dot(a, b):
    return jnp.dot(a, b, preferred_element_type=F32)


def _dot_nt(a, b):
    return lax.dot_general(a, b, (((1,), (1,)), ((), ())), preferred_element_type=F32)


def _norm_matmul_kernel(x_ref, g_ref, w_ref, *o_refs):
    h = _rms(x_ref[...], g_ref[...]).astype(BF16)
    col = 0
    for o_ref in o_refs:
        width = o_ref.shape[1]
        o_ref[...] = _dot(h, w_ref[:, col:col + width]).astype(o_ref.dtype)
        col += width


def norm_matmul(x2d, gamma, w_bf16, tm, outs):
    n, d = x2d.shape
    f = w_bf16.shape[1]
    assert sum(width for width, _ in outs) == f
    res = pl.pallas_call(
        _norm_matmul_kernel,
        grid=(n // tm,),
        in_specs=[pl.BlockSpec((tm, d), lambda i: (i, 0)),
                  pl.BlockSpec((1, d), lambda i: (0, 0)),
                  pl.BlockSpec((d, f), lambda i: (0, 0))],
        out_specs=[pl.BlockSpec((tm, width), lambda i: (i, 0)) for width, _ in outs],
        out_shape=[jax.ShapeDtypeStruct((n, width), dtype) for width, dtype in outs],
        compiler_params=_cparams(("parallel",)),
        name="norm_matmul",
    )(x2d, gamma.reshape(1, d), w_bf16)
    return res


def _conv3_kernel(u_ref, w_ref, b_ref, oe_ref, oo_ref, slab_ref):
    half = u_ref.shape[1] // 2
    w = w_ref[...]
    bias = b_ref[...]
    row = lax.broadcasted_iota(I32, (half, LANES), 0)
    for k in range(u_ref.shape[2] // LANES):
        cols = slice(k * LANES, (k + 1) * LANES)
        slab_ref[...] = u_ref[0, :, cols].astype(F32)
        ue = slab_ref[pl.ds(0, half, stride=2), :]
        uo = slab_ref[pl.ds(1, half, stride=2), :]
        uo_prev = jnp.where(row == 0, 0.0, pltpu.roll(uo, 1, 0))
        ue_next = jnp.where(row == half - 1, 0.0, pltpu.roll(ue, half - 1, 0))
        even = uo_prev * w[0:1, cols] + ue * w[1:2, cols] + uo * w[2:3, cols] + bias[:, cols]
        odd = ue * w[0:1, cols] + uo * w[1:2, cols] + ue_next * w[2:3, cols] + bias[:, cols]
        oe_ref[0, :, cols] = even.astype(oe_ref.dtype)
        oo_ref[0, :, cols] = odd.astype(oo_ref.dtype)


def conv3(proj3d, conv_w, conv_b):
    b, seq, c = proj3d.shape
    tc = HY_WIDTH
    out = pl.BlockSpec((1, seq // 2, tc), lambda i, j: (i, 0, j))
    return pl.pallas_call(
        _conv3_kernel,
        grid=(b, c // tc),
        in_specs=[pl.BlockSpec((1, seq, tc), lambda i, j: (i, 0, j)),
                  pl.BlockSpec((3, tc), lambda i, j: (0, j)),
                  pl.BlockSpec((1, tc), lambda i, j: (0, j))],
        out_specs=[out, out],
        out_shape=[jax.ShapeDtypeStruct((b, seq // 2, c), BF16)] * 2,
        scratch_shapes=[pltpu.VMEM((seq, LANES), F32)],
        compiler_params=_cparams(("parallel", "parallel")),
        name="hyena_conv3",
    )(proj3d, conv_w, conv_b.reshape(1, c))


def _hy_filter_kernel(t_ref, bands_ref, w1t_ref, w1c_ref, w1s_ref, b1_ref, w2_ref, b2_ref, w3_ref,
                      freq_ref, delta_ref, hf_ref, hb_ref, *, seq, tl):
    hi = lax.Precision.HIGHEST
    parity = pl.program_id(0)
    i = pl.program_id(1)
    t = t_ref[0]
    pos = (2 * (i * tl + lax.broadcasted_iota(I32, (tl, 1), 0)) + parity).astype(F32)
    ang = bands_ref[...] * (2.0 * math.pi * pos / seq)
    f = freq_ref[...]
    pre = (t * w1t_ref[...]
           + jnp.dot(jnp.cos(ang), w1c_ref[...], precision=hi, preferred_element_type=F32)
           - jnp.dot(jnp.sin(ang), w1s_ref[...], precision=hi, preferred_element_type=F32)
           + b1_ref[...])
    a = jnp.sin(f * pre)
    a = jnp.sin(f * (jnp.dot(a, w2_ref[...], precision=hi, preferred_element_type=F32) + b2_ref[...]))
    h = jnp.dot(a, w3_ref[...], precision=hi, preferred_element_type=F32)
    decay = jnp.exp(-t * jnp.abs(delta_ref[...]))
    w = HY_WIDTH
    for o in range(2):
        hf_ref[0, :, o * w:(o + 1) * w] = h[:, o * 2 * w:o * 2 * w + w] * decay
        hb_ref[0, :, o * w:(o + 1) * w] = jnp.where(pos == 0.0, 0.0, h[:, o * 2 * w + w:(o + 1) * 2 * w] * decay)


def hyena_filters_time(seq, w1, b1, w2, b2, w3, freq):
    tl = 512
    half = seq // 2
    hid = LANES
    pad_h = hid - HY_FILT_HIDDEN
    t = jnp.linspace(0.0, 1.0, seq, dtype=F32).reshape(half, 2).T.reshape(2, half, 1)
    bands = jnp.pad(jnp.linspace(1e-4, HY_BANDS - 1, HY_BANDS, dtype=F32)[None], ((0, 0), (0, LANES - HY_BANDS)))
    w1p = jnp.pad(w1, ((0, 0), (0, pad_h)))
    w1t = w1p[0:1]
    w1c = jnp.pad(w1p[1:1 + HY_BANDS], ((0, LANES - HY_BANDS), (0, 0)))
    w1s = jnp.pad(w1p[1 + HY_BANDS:], ((0, LANES - HY_BANDS), (0, 0)))
    b1p = jnp.pad(b1, (0, pad_h)).reshape(1, hid)
    w2p = jnp.pad(w2, ((0, pad_h), (0, pad_h)))
    b2p = jnp.pad(b2, (0, pad_h)).reshape(1, hid)
    w3p = jnp.pad(w3, ((0, pad_h), (0, 0)))
    freqp = jnp.pad(freq, (0, pad_h)).reshape(1, hid)
    max_decay = math.log(HY_DECAY_TARGET) / HY_FAST_PCT
    min_decay = math.log(HY_DECAY_TARGET) / HY_SLOW_PCT
    deltas = jnp.linspace(min_decay, max_decay, HY_WIDTH, dtype=F32)[None]
    fw = w3.shape[1]
    full = lambda shape: pl.BlockSpec(shape, lambda p, i: (0, 0))
    out = pl.BlockSpec((1, tl, 2 * HY_WIDTH), lambda p, i: (p, i, 0))
    return pl.pallas_call(
        functools.partial(_hy_filter_kernel, seq=seq, tl=tl),
        grid=(2, half // tl),
        in_specs=[pl.BlockSpec((1, tl, 1), lambda p, i: (p, i, 0)), full((1, LANES)), full((1, hid)),
                  full((LANES, hid)), full((LANES, hid)), full((1, hid)), full((hid, hid)), full((1, hid)),
                  full((hid, fw)), full((1, hid)), full((1, HY_WIDTH))],
        out_specs=[out, out],
        out_shape=[jax.ShapeDtypeStruct((2, half, 2 * HY_WIDTH), F32)] * 2,
        compiler_params=_cparams(("parallel", "parallel")),
        name="hyena_filter_mlp",
    )(t, bands, w1t, w1c, w1s, b1p, w2p, b2p, w3p, freqp, deltas)


def dft_tables(seq):
    n2 = 4 * seq
    sub = 64
    f = jnp.arange(seq, dtype=I32)[:, None]
    odd = 2 * f + 1
    s1 = jnp.arange(seq // sub, dtype=I32)[None]
    s0 = jnp.arange(sub, dtype=I32)[None]
    ang_p = ((odd * s1) % (n2 // sub)).astype(F32) * (2.0 * math.pi * sub / n2)
    ang_q = ((odd * s0) % n2).astype(F32) * (2.0 * math.pi / n2)
    pc, ps, qc, qs = jnp.cos(ang_p), jnp.sin(ang_p), jnp.cos(ang_q), jnp.sin(ang_q)
    c = (pc[:, :, None] * qc[:, None, :] - ps[:, :, None] * qs[:, None, :]).reshape(seq, seq)
    s = (ps[:, :, None] * qc[:, None, :] + pc[:, :, None] * qs[:, None, :]).reshape(seq, seq)
    return c.astype(BF16), s.astype(BF16), c.T.astype(BF16), s.T.astype(BF16)


def dft_twiddles(seq):
    g = jnp.arange(seq // 2, dtype=I32)[:, None]
    ang = (2 * g + 1).astype(F32) * (2.0 * math.pi / (4 * seq))
    return jnp.cos(ang), jnp.sin(ang)


def _half_transform(c1, s1, cg, sg, x0, x1):
    pc0, ps0 = _dot(c1, x0), _dot(s1, x0)
    pc1, ps1 = _dot(c1, x1), _dot(s1, x1)
    tr = cg * pc1 - sg * ps1
    ti = -(cg * ps1 + sg * pc1)
    return pc0 + tr, ti - ps0, pc0 - tr, ps0 + ti


def _spectrum_kernel(c_ref, s_ref, cg_ref, sg_ref, f0_ref, f1_ref, b0_ref, b1_ref,
                     kgre_ref, kgim_ref, khre_ref, khim_ref):
    c1, s1, cg, sg = c_ref[...], s_ref[...], cg_ref[...], sg_ref[...]
    fre_g, fim_g, fre_h, fim_h = _half_transform(c1, s1, cg, sg, f0_ref[0].astype(BF16), f1_ref[0].astype(BF16))
    bre_g, bim_g, bre_h, bim_h = _half_transform(c1, s1, cg, sg, b0_ref[0].astype(BF16), b1_ref[0].astype(BF16))
    kgre_ref[...] = fre_g + bre_g
    kgim_ref[...] = fim_g - bim_g
    khre_ref[...] = fre_h + bre_h
    khim_ref[...] = fim_h - bim_h


def filter_spectrum(tables, twiddles, hf, hb):
    _, half, cols = hf.shape
    tf, tn = 512, 512
    tab = pl.BlockSpec((tf, half), lambda i, j: (i, 0))
    twd = pl.BlockSpec((tf, 1), lambda i, j: (i, 0))
    even = pl.BlockSpec((1, half, tn), lambda i, j: (0, 0, j))
    odd = pl.BlockSpec((1, half, tn), lambda i, j: (1, 0, j))
    return pl.pallas_call(
        _spectrum_kernel,
        grid=(half // tf, cols // tn),
        in_specs=[tab, tab, twd, twd, even, odd, even, odd],
        out_specs=[pl.BlockSpec((tf, tn), lambda i, j: (i, j))] * 4,
        out_shape=[jax.ShapeDtypeStruct((half, cols), F32)] * 4,
        compiler_params=_cparams(("parallel", "parallel")),
        name="hyena_filter_spectrum",
    )(tables[0], tables[1], twiddles[0], twiddles[1], hf, hf, hb, hb)


def _hy_fwd_kernel(z0_ref, z1_ref, c_ref, s_ref, cg_ref, sg_ref, kgre_ref, kgim_ref, khre_ref, khim_ref,
                   u0re_ref, u0im_ref, u1re_ref, u1im_ref):
    cg, sg = cg_ref[...], sg_ref[...]
    zre_g, zim_g, zre_h, zim_h = _half_transform(c_ref[...], s_ref[...], cg, sg,
                                                 z0_ref[0].astype(BF16), z1_ref[0].astype(BF16))
    kgre, kgim, khre, khim = kgre_ref[...], kgim_ref[...], khre_ref[...], khim_ref[...]
    yre_g = zre_g * kgre - zim_g * kgim
    yim_g = zre_g * kgim + zim_g * kgre
    yre_h = zre_h * khre - zim_h * khim
    yim_h = zre_h * khim + zim_h * khre
    u0re_ref[0] = (yre_g + yre_h).astype(BF16)
    u0im_ref[0] = (yim_g - yim_h).astype(BF16)
    a = yre_g - yre_h
    b = yim_g + yim_h
    u1re_ref[0] = (cg * a - sg * b).astype(BF16)
    u1im_ref[0] = (cg * b + sg * a).astype(BF16)


def hyena_fwd(zsrc, tables, twiddles, kspec, order):
    (ze, ce), (zo, co) = zsrc
    b, half, _ = ze.shape
    w = HY_WIDTH
    tf = 512
    tab = pl.BlockSpec((tf, half), lambda i, j: (j, 0))
    twd = pl.BlockSpec((tf, 1), lambda i, j: (j, 0))
    kblk = pl.BlockSpec((tf, w), lambda i, j: (j, order))
    return pl.pallas_call(
        _hy_fwd_kernel,
        grid=(b, half // tf),
        in_specs=[pl.BlockSpec((1, half, w), lambda i, j: (i, 0, ce)),
                  pl.BlockSpec((1, half, w), lambda i, j: (i, 0, co)),
                  tab, tab, twd, twd, kblk, kblk, kblk, kblk],
        out_specs=[pl.BlockSpec((1, tf, w), lambda i, j: (i, j, 0))] * 4,
        out_shape=[jax.ShapeDtypeStruct((b, half, w), BF16)] * 4,
        compiler_params=_cparams(("parallel", "parallel")),
        name="hyena_dft_fwd",
    )(ze, zo, tables[0], tables[1], twiddles[0], twiddles[1], *kspec)


def _hy_inv_kernel(ct_ref, st_ref, u0re_ref, u0im_ref, u1re_ref, u1im_ref, z0_ref, z1_ref, g0_ref, g1_ref,
                   bias_ref, o_ref, *, scale):
    ct, st = ct_ref[...], st_ref[...]
    w = z0_ref.shape[2]
    y0 = (_dot(ct, u0re_ref[0]) - _dot(st, u0im_ref[0])) * scale
    y1 = (_dot(ct, u1re_ref[0]) - _dot(st, u1im_ref[0])) * scale
    bias = bias_ref[...]
    o_ref[0, :, 0:w] = (g0_ref[0].astype(F32) * (y0 + z0_ref[0].astype(F32) * bias)).astype(o_ref.dtype)
    o_ref[0, :, w:2 * w] = (g1_ref[0].astype(F32) * (y1 + z1_ref[0].astype(F32) * bias)).astype(o_ref.dtype)


def hyena_inv(tables, u, zsrc, gsrc, bias_row, out_dtype):
    b, half, w = u[0].shape
    tt = 512
    tab = pl.BlockSpec((tt, half), lambda i, j: (j, 0))
    ublk = pl.BlockSpec((1, half, w), lambda i, j: (i, 0, 0))

    def rows(col):
        return pl.BlockSpec((1, tt, w), lambda i, j: (i, j, col))

    (ze, ce), (zo, co) = zsrc
    (ge, gce), (go, gco) = gsrc
    return pl.pallas_call(
        functools.partial(_hy_inv_kernel, scale=0.5 / half),
        grid=(b, half // tt),
        in_specs=[tab, tab, ublk, ublk, ublk, ublk, rows(ce), rows(co), rows(gce), rows(gco),
                  pl.BlockSpec((1, w), lambda i, j: (0, 0))],
        out_specs=pl.BlockSpec((1, tt, 2 * w), lambda i, j: (i, j, 0)),
        out_shape=jax.ShapeDtypeStruct((b, half, 2 * w), out_dtype),
        compiler_params=_cparams(("parallel", "parallel")),
        name="hyena_dft_inv",
    )(tables[2], tables[3], *u, ze, zo, ge, go, bias_row)


def hyena_mixer(proj3d, tables, twiddles, conv_w, conv_b, w1, b1, w2, b2, w3, freq, hy_bias):
    seq = proj3d.shape[1]
    ue, uo = conv3(proj3d, conv_w, conv_b)
    hf, hb = hyena_filters_time(seq, w1, b1, w2, b2, w3, freq)
    kspec = filter_spectrum(tables, twiddles, hf, hb)
    v = ((ue, 0), (uo, 0))
    uu = hyena_fwd(v, tables, twiddles, kspec, 0)
    z1 = hyena_inv(tables, uu, v, ((ue, 1), (uo, 1)), hy_bias[0:1], F32)
    z = ((z1, 0), (z1, 1))
    uu = hyena_fwd(z, tables, twiddles, kspec, 1)
    return hyena_inv(tables, uu, z, ((ue, 2), (uo, 2)), hy_bias[1:2], BF16)


def _diff_attn_kernel(slope_ref, q_ref, k_ref, v_ref, lam_ref, sub_ref, o_ref, bias_ref, *, tq, lam_init):
    h = pl.program_id(0)
    qi = pl.program_id(1)
    seq = k_ref.shape[1]

    @pl.when(pl.program_id(2) == 0)
    def _():
        qpos = qi * tq + lax.broadcasted_iota(I32, (tq, seq), 0)
        kpos = lax.broadcasted_iota(I32, (tq, seq), 1)
        bias_ref[...] = slope_ref[h] * jnp.abs(qpos - kpos).astype(F32)

    lane = lax.broadcasted_iota(I32, (1, 2 * DIFF_HEAD_DIM), 1)
    ck = DIFF_KEY_CHUNK
    l = lam_ref[...]
    lam_full = (jnp.exp(jnp.sum(l[0:1] * l[1:2], axis=-1, keepdims=True))
                - jnp.exp(jnp.sum(l[2:3] * l[3:4], axis=-1, keepdims=True)) + lam_init)

    def attend(bb, m):
        q = q_ref[bb]
        k = k_ref[bb]
        v = v_ref[bb]
        keep = (lane < DIFF_HEAD_DIM) if m == 0 else (lane >= DIFF_HEAD_DIM)
        qm = jnp.where(keep, q, jnp.zeros_like(q))
        s = [_dot_nt(qm, k[c:c + ck]) - bias_ref[:, c:c + ck] for c in range(0, seq, ck)]
        mx = functools.reduce(jnp.maximum, [jnp.max(sc, axis=-1, keepdims=True) for sc in s])
        acc = jnp.zeros((tq, 2 * DIFF_HEAD_DIM), F32)
        den = jnp.zeros((tq, 1), F32)
        for i, sc in enumerate(s):
            e = jnp.exp2(sc - mx)
            den = den + jnp.sum(e, axis=-1, keepdims=True)
            acc = acc + _dot(e.astype(BF16), v[i * ck:(i + 1) * ck])
        return acc / den

    for bb in range(q_ref.shape[0]):
        o = attend(bb, 0) - lam_full * attend(bb, 1)
        o_ref[bb] = (_rms(o, sub_ref[...]) * (1.0 - lam_init)).astype(o_ref.dtype)


def diff_attention(proj3d, lam, subln, layer_idx):
    b, seq, _ = proj3d.shape
    tq = DIFF_Q_BLOCK
    nb = DIFF_BATCH_PER_STEP
    hw = 2 * DIFF_HEAD_DIM
    qb, kb, vb = 0, DIFF_HEADS, 2 * DIFF_HEADS
    lam_init = 0.8 - 0.6 * math.exp(-0.3 * layer_idx)
    slopes = 2.0 ** (-8.0 * jnp.arange(1, DIFF_HEADS + 1, dtype=F32) / DIFF_HEADS) * LOG2E
    return pl.pallas_call(
        functools.partial(_diff_attn_kernel, tq=tq, lam_init=lam_init),
        grid=(DIFF_HEADS, seq // tq, b // nb),
        in_specs=[pl.BlockSpec(memory_space=pltpu.SMEM),
                  pl.BlockSpec((nb, tq, hw), lambda h, j, i: (i, j, qb + h)),
                  pl.BlockSpec((nb, seq, hw), lambda h, j, i: (i, 0, kb + h)),
                  pl.BlockSpec((nb, seq, hw), lambda h, j, i: (i, 0, vb + h)),
                  pl.BlockSpec((4, DIFF_HEAD_DIM), lambda h, j, i: (0, 0)),
                  pl.BlockSpec((1, hw), lambda h, j, i: (0, 0))],
        out_specs=pl.BlockSpec((nb, tq, hw), lambda h, j, i: (i, j, h)),
        out_shape=jax.ShapeDtypeStruct((b, seq, DIFF_HEADS * hw), BF16),
        scratch_shapes=[pltpu.VMEM((tq, seq), F32)],
        compiler_params=_cparams(("parallel", "parallel", "arbitrary")),
        name="diff_attention",
    )(slopes, proj3d, proj3d, proj3d, lam, subln.reshape(1, hw))


def _win_attn_kernel(slope_ref, sink_ref, q_ref, kp_ref, kc_ref, kn_ref, vp_ref, vc_ref, vn_ref, o_ref, bias_ref,
                     *, tq, seq):
    qi = pl.program_id(0)
    span = 3 * tq
    group = GQA_HEADS // GQA_KV

    @pl.when(pl.program_id(1) == 0)
    def _():
        qpos = qi * tq + lax.broadcasted_iota(I32, (tq, span), 0)
        kpos = (qi - 1) * tq + lax.broadcasted_iota(I32, (tq, span), 1)
        rel = jnp.abs(qpos - kpos)
        relf = rel.astype(F32)
        masked = jnp.where((rel <= WIN) & (kpos >= 0) & (kpos < seq), 0.0, -NEG)
        for head in range(GQA_HEADS):
            bias_ref[head] = slope_ref[head] * relf + masked

    low = lax.broadcasted_iota(I32, (1, LANES), 1) < GQA_HD
    for bb in range(q_ref.shape[0]):
        kwin = jnp.concatenate([kp_ref[bb], kc_ref[bb], kn_ref[bb]], axis=0)
        vwin = jnp.concatenate([vp_ref[bb], vc_ref[bb], vn_ref[bb]], axis=0)
        for p in range(GQA_KV // 2):
            kb = kwin[:, p * LANES:(p + 1) * LANES]
            vb = vwin[:, p * LANES:(p + 1) * LANES]
            outs = []
            for half in range(2):
                kv = 2 * p + half
                mine = low if half == 0 else jnp.logical_not(low)
                qblocks = [q_ref[bb, :, (group * p + r) * LANES:(group * p + r + 1) * LANES] for r in range(group)]
                qs = jnp.concatenate([jnp.where(mine, qb, jnp.zeros_like(qb)) for qb in qblocks], axis=0)
                s = _dot_nt(qs, kb)
                es, inv = [], []
                for r in range(group):
                    head = kv * group + r
                    sink = sink_ref[head]
                    sr = s[r * tq:(r + 1) * tq] - bias_ref[head]
                    m = jnp.maximum(jnp.max(sr, axis=-1, keepdims=True), sink)
                    e = jnp.exp2(sr - m)
                    inv.append(1.0 / (jnp.sum(e, axis=-1, keepdims=True) + jnp.exp2(sink - m)))
                    es.append(e.astype(BF16))
                o = _dot(jnp.concatenate(es, axis=0), vb)
                outs.append([o[r * tq:(r + 1) * tq] * inv[r] for r in range(group)])
            for r in range(group):
                col = (group * p + r) * LANES
                o_ref[bb, :, col:col + LANES] = jnp.where(low, outs[0][r], outs[1][r]).astype(o_ref.dtype)


def window_head_order():
    group = GQA_HEADS // GQA_KV
    order = []
    for p in range(GQA_KV // 2):
        for r in range(group):
            order += [(2 * p) * group + r, (2 * p + 1) * group + r]
    return order


def window_column_perm():
    cols = []
    for head in window_head_order():
        cols += list(range(head * GQA_HD, (head + 1) * GQA_HD))
    return jnp.asarray(cols, dtype=I32)


def window_gqa(proj3d, sink):
    b, seq, _ = proj3d.shape
    tq = WIN
    nq = seq // tq
    oq = GQA_HEADS * GQA_HD
    okv = GQA_KV * GQA_HD
    kcol, vcol = oq // okv, oq // okv + 1
    slopes = 2.0 ** (-8.0 * jnp.arange(1, GQA_HEADS + 1, dtype=F32) / GQA_HEADS) * LOG2E

    nb = WIN_BATCH_PER_STEP

    def neighbour(col, step):
        return pl.BlockSpec((nb, tq, okv), lambda j, i: (i, jnp.clip(j + step, 0, nq - 1), col))

    return pl.pallas_call(
        functools.partial(_win_attn_kernel, tq=tq, seq=seq),
        grid=(nq, b // nb),
        in_specs=[pl.BlockSpec(memory_space=pltpu.SMEM),
                  pl.BlockSpec(memory_space=pltpu.SMEM),
                  pl.BlockSpec((nb, tq, oq), lambda j, i: (i, j, 0)),
                  neighbour(kcol, -1), neighbour(kcol, 0), neighbour(kcol, 1),
                  neighbour(vcol, -1), neighbour(vcol, 0), neighbour(vcol, 1)],
        out_specs=pl.BlockSpec((nb, tq, oq), lambda j, i: (i, j, 0)),
        out_shape=jax.ShapeDtypeStruct((b, seq, oq), BF16),
        scratch_shapes=[pltpu.VMEM((GQA_HEADS, tq, 3 * tq), F32)],
        compiler_params=_cparams(("parallel", "arbitrary")),
        name="window_gqa",
    )(slopes, sink.astype(F32) * LOG2E, proj3d, proj3d, proj3d, proj3d, proj3d, proj3d, proj3d)


def _cross_attend(x, g, wq, kv, wo):
    h = _rms(x, g).astype(BF16)
    q = _dot(h, wq)
    outs = []
    for hd in range(X_HEADS):
        qh = q[:, hd * X_HD:(hd + 1) * X_HD].astype(BF16)
        kh = kv[:, hd * X_HD:(hd + 1) * X_HD]
        vh = kv[:, X_W + hd * X_HD:X_W + (hd + 1) * X_HD]
        s = _dot_nt(qh, kh)
        e = jnp.exp2(s - jnp.max(s, axis=-1, keepdims=True))
        outs.append(_dot(e.astype(BF16), vh) / jnp.sum(e, axis=-1, keepdims=True))
    o = jnp.concatenate(outs, axis=-1).astype(BF16)
    return x + _dot(o, wo)


def _split_bf16(x):
    hi = x.astype(BF16)
    lo = (x - hi.astype(F32)).astype(BF16)
    return hi, lo


def _pack_halves(h):
    c = h.shape[1] // 2
    left = lax.bitcast_convert_type(h[:, :c].astype(BF16).astype(F32), U32)
    right = lax.bitcast_convert_type(h[:, c:].astype(BF16).astype(F32), U32)
    return left | (right >> 16)


def _unpack_halves(p):
    left = lax.bitcast_convert_type(p & jnp.uint32(0xFFFF0000), F32)
    right = lax.bitcast_convert_type(p << 16, F32)
    return jnp.concatenate([left, right], axis=-1)


def _route(x, g, whi, wlo, bias, hp_ref, eid_ref, wt_ref, rank_ref, cnt_ref, base_ref):
    tm = x.shape[0]
    h = _rms(x, g)
    hp_ref[...] = _pack_halves(h)
    hhi, hlo = _split_bf16(h)
    logits = _dot(hhi, whi) + _dot(hlo, whi) + _dot(hhi, wlo) + bias
    lane = lax.broadcasted_iota(I32, logits.shape, 1)
    big = jnp.int32(LANES)
    ninf = -jnp.inf

    gl = jnp.where(lane < N_GROUPS, logits, ninf)
    gmax = jnp.max(gl, axis=-1, keepdims=True)
    gsel = jnp.min(jnp.where(gl == gmax, lane, big), axis=-1, keepdims=True)
    ggate = 1.0 / jnp.sum(jnp.exp(gl - gmax), axis=-1, keepdims=True)

    lo_lane = N_GROUPS + gsel * EXP_PER_GROUP
    el = jnp.where((lane >= lo_lane) & (lane < lo_lane + EXP_PER_GROUP), logits, ninf)
    v1 = jnp.max(el, axis=-1, keepdims=True)
    i1 = jnp.min(jnp.where(el == v1, lane, big), axis=-1, keepdims=True)
    el2 = jnp.where(lane == i1, ninf, el)
    v2 = jnp.max(el2, axis=-1, keepdims=True)
    i2 = jnp.min(jnp.where(el2 == v2, lane, big), axis=-1, keepdims=True)
    e2 = jnp.exp(v2 - v1)
    w1 = ggate / (1.0 + e2)
    w2 = ggate * e2 / (1.0 + e2)

    one1 = lane == i1
    one2 = lane == i2
    onehot = (one1 | one2).astype(F32)
    r = lax.broadcasted_iota(I32, (tm, tm), 0)
    c = lax.broadcasted_iota(I32, (tm, tm), 1)
    tri = (c < r).astype(BF16)
    before = _dot(tri, onehot.astype(BF16)) + base_ref[...]
    rank1 = jnp.sum(jnp.where(one1, before, 0.0), axis=-1, keepdims=True)
    rank2 = jnp.sum(jnp.where(one2, before, 0.0), axis=-1, keepdims=True)
    total = base_ref[...] + jnp.sum(onehot, axis=0, keepdims=True)
    base_ref[...] = total
    cnt_ref[...] = total

    col = lax.broadcasted_iota(I32, (tm, 2), 1)
    eid_ref[...] = jnp.where(col == 0, i1, i2) - N_GROUPS
    wt_ref[...] = jnp.where(col == 0, w1, w2)
    rank_ref[...] = jnp.where(col == 0, rank1, rank2).astype(I32)


def _interleave_rows(pair):
    t, w2 = pair.shape
    w = w2 // 2
    r = lax.broadcasted_iota(I32, (2 * t, t), 0)
    c = lax.broadcasted_iota(I32, (2 * t, t), 1)
    pick_even = (r == 2 * c).astype(BF16)
    pick_odd = (r == 2 * c + 1).astype(BF16)
    return (_dot(pick_even, pair[:, :w]) + _dot(pick_odd, pair[:, w:])).astype(BF16)


def _post_mixer_kernel(*refs, n_lhs, paired):
    x_ref = refs[0]
    a_refs = refs[1:1 + n_lhs]
    w_refs = refs[1 + n_lhs:1 + 2 * n_lhs]
    (cg_ref, wq_ref, kv_ref, wo_ref, fg_ref, whi_ref, wlo_ref, rb_ref,
     x_out_ref, hp_ref, eid_ref, wt_ref, rank_ref, cnt_ref, base_ref) = refs[1 + 2 * n_lhs:]

    @pl.when((pl.program_id(0) == 0) & (pl.program_id(1) == 0))
    def _():
        base_ref[...] = jnp.zeros_like(base_ref)

    x = x_ref[0]
    for a_ref, w_ref, is_paired in zip(a_refs, w_refs, paired):
        a = a_ref[0].astype(BF16)
        x = x + _dot(_interleave_rows(a) if is_paired else a, w_ref[...])
    x = _cross_attend(x, cg_ref[...], wq_ref[...], kv_ref[0], wo_ref[...])
    x_out_ref[0] = x
    _route(x, fg_ref[...], whi_ref[...], wlo_ref[...], rb_ref[...], hp_ref, eid_ref, wt_ref, rank_ref, cnt_ref,
           base_ref)


def post_mixer(x3d, lhs_list, w_list, c_gamma, wq, kv, wo, f_gamma, w_grp, b_grp, w_exp, b_exp):
    b, seq, d = x3d.shape
    paired = tuple(a.shape[1] != seq for a in lhs_list)
    n = b * seq
    m = kv.shape[1]
    tm = 512
    nt = seq // tm
    wcat = jnp.pad(jnp.concatenate([w_grp, w_exp], axis=1), ((0, 0), (0, LANES - N_GROUPS - N_EXPERTS)))
    bcat = jnp.pad(jnp.concatenate([b_grp, b_exp]), (0, LANES - N_GROUPS - N_EXPERTS)).reshape(1, LANES)
    whi = wcat.astype(BF16)
    wlo = (wcat - whi.astype(F32)).astype(BF16)
    const = lambda shape: pl.BlockSpec(shape, lambda i, j: (0,) * len(shape))
    tok = lambda width: pl.BlockSpec((tm, width), lambda i, j: (i * nt + j, 0))
    in_specs = [pl.BlockSpec((1, tm, d), lambda i, j: (i, j, 0))]
    in_specs += [pl.BlockSpec((1, tm // 2 if p else tm, a.shape[2]), lambda i, j: (i, j, 0))
                 for a, p in zip(lhs_list, paired)]
    in_specs += [const(w.shape) for w in w_list]
    in_specs += [const((1, d)), const((d, X_W)), pl.BlockSpec((1, m, 2 * X_W), lambda i, j: (i, 0, 0)),
                 const((X_W, d)), const((1, d)), const((d, LANES)), const((d, LANES)), const((1, LANES))]
    return pl.pallas_call(
        functools.partial(_post_mixer_kernel, n_lhs=len(lhs_list), paired=paired),
        grid=(b, nt),
        in_specs=in_specs,
        out_specs=[pl.BlockSpec((1, tm, d), lambda i, j: (i, j, 0)), tok(d // 2), tok(2), tok(2), tok(2),
                   const((1, LANES))],
        out_shape=[jax.ShapeDtypeStruct((b, seq, d), F32),
                   jax.ShapeDtypeStruct((n, d // 2), U32),
                   jax.ShapeDtypeStruct((n, 2), I32),
                   jax.ShapeDtypeStruct((n, 2), F32),
                   jax.ShapeDtypeStruct((n, 2), I32),
                   jax.ShapeDtypeStruct((1, LANES), F32)],
        scratch_shapes=[pltpu.VMEM((1, LANES), F32)],
        compiler_params=_cparams(("arbitrary", "arbitrary")),
        name="post_mixer",
    )(x3d, *lhs_list, *w_list, c_gamma.reshape(1, d), wq, kv, wo, f_gamma.reshape(1, d), whi, wlo, bcat)


def _row_copy(src_ref, src_row, dst_ref, dst_row, sem):
    return pltpu.make_async_copy(src_ref.at[pl.ds(src_row, 1)], dst_ref.at[pl.ds(dst_row, 1)], sem)


def _dispatch_kernel(dest_ref, seg_ref, hp_ref, xs_ref, zbuf, sem, zsem, *, tm):
    @pl.when(pl.program_id(0) == 0)
    def _():
        zbuf[...] = jnp.zeros_like(zbuf)

        def zero_tail(e):
            tail = pl.multiple_of(seg_ref[0, e] - MOE_BLOCK, MOE_BLOCK)
            return pltpu.make_async_copy(zbuf, xs_ref.at[pl.ds(tail, MOE_BLOCK)], zsem)

        def start(e, carry):
            @pl.when(seg_ref[1, e] > 0)
            def _():
                zero_tail(e).start()
            return carry

        def finish(e, carry):
            @pl.when(seg_ref[1, e] > 0)
            def _():
                zero_tail(e).wait()
            return carry

        lax.fori_loop(0, N_EXPERTS, start, 0)
        lax.fori_loop(0, N_EXPERTS, finish, 0)

        def unused(blk):
            return pltpu.make_async_copy(zbuf, xs_ref.at[pl.ds(blk * MOE_BLOCK, MOE_BLOCK)], zsem)

        def start_unused(blk, carry):
            unused(blk).start()
            return carry

        def finish_unused(blk, carry):
            unused(blk).wait()
            return carry

        first_unused = seg_ref[0, N_EXPERTS - 1] // MOE_BLOCK
        lax.fori_loop(first_unused, xs_ref.shape[0] // MOE_BLOCK, start_unused, 0)
        lax.fori_loop(first_unused, xs_ref.shape[0] // MOE_BLOCK, finish_unused, 0)

    def issue(r, carry):
        _row_copy(hp_ref, r, xs_ref, dest_ref[2 * r], sem).start(priority=0)
        _row_copy(hp_ref, r, xs_ref, dest_ref[2 * r + 1], sem).start(priority=1)
        return carry

    lax.fori_loop(0, tm, issue, 0, unroll=ROW_DMA_UNROLL)
    for _ in range(2):
        pltpu.make_async_copy(hp_ref, xs_ref.at[pl.ds(0, tm)], sem).wait()


def moe_dispatch(hp, dest_flat, segments, cap):
    n, c = hp.shape
    tm = ROW_DMA_TILE
    return pl.pallas_call(
        functools.partial(_dispatch_kernel, tm=tm),
        grid=(n // tm,),
        in_specs=[pl.BlockSpec((2 * tm,), lambda i: (i,), memory_space=pltpu.SMEM),
                  pl.BlockSpec(memory_space=pltpu.SMEM),
                  pl.BlockSpec((tm, c), lambda i: (i, 0))],
        out_specs=pl.BlockSpec(memory_space=pl.ANY),
        out_shape=jax.ShapeDtypeStruct((cap, c), U32),
        scratch_shapes=[pltpu.VMEM((MOE_BLOCK, c), U32), pltpu.SemaphoreType.DMA(()), pltpu.SemaphoreType.DMA(())],
        compiler_params=_cparams(("arbitrary",)),
        name="moe_dispatch",
    )(dest_flat, segments, hp)


def _expert_kernel(start_ref, cnt_ref, xs_ref, wg_ref, wu_ref, wd_ref, yb_ref, xbuf, ybuf, sem_in, sem_out,
                   wg_s, wu_s, wd_s, *, nblk):
    e = pl.program_id(0)
    last = pl.num_programs(0) - 1
    nb = cnt_ref[e]
    b0 = start_ref[e]
    total = start_ref[last] + cnt_ref[last]

    def fetch(g):
        slot = g % EXPERT_SLOTS
        return pltpu.make_async_copy(xs_ref.at[pl.ds(g * MOE_BLOCK, MOE_BLOCK)], xbuf.at[slot], sem_in.at[slot])

    def put(g):
        slot = g % EXPERT_SLOTS
        return pltpu.make_async_copy(ybuf.at[slot], yb_ref.at[pl.ds(g * MOE_BLOCK, MOE_BLOCK)], sem_out.at[slot])

    @pl.when(e == 0)
    def _():
        for g in range(EXPERT_AHEAD):
            @pl.when(g < total)
            def _():
                fetch(g).start(priority=1)

    @pl.when(nb > 0)
    def _():
        wg_s[...] = wg_ref[0].astype(BF16)
        wu_s[...] = wu_ref[0].astype(BF16)
        wd_s[...] = wd_ref[0].astype(BF16)

    def block(g, carry):
        slot = g % EXPERT_SLOTS
        fetch(g).wait()

        @pl.when(g + EXPERT_AHEAD < total)
        def _():
            fetch(g + EXPERT_AHEAD).start(priority=1)

        @pl.when(g >= EXPERT_SLOTS)
        def _():
            put(g - EXPERT_SLOTS).wait()

        x = _unpack_halves(xbuf[slot]).astype(BF16)
        a = _dot(x, wg_s[...])
        u = _dot(x, wu_s[...])
        hmid = (a / (1.0 + jnp.exp(-a)) * u).astype(BF16)
        ybuf[slot] = _pack_halves(_dot(hmid, wd_s[...]))
        put(g).start()
        return carry

    lax.fori_loop(b0, b0 + nb, block, 0)

    @pl.when(e == last)
    def _():
        for back in range(1, EXPERT_SLOTS + 1):
            @pl.when(total >= back)
            def _():
                put(total - back).wait()

        ybuf[0] = jnp.zeros(ybuf.shape[1:], ybuf.dtype)

        def fill(blk, carry):
            copy = pltpu.make_async_copy(ybuf.at[0], yb_ref.at[pl.ds(blk * MOE_BLOCK, MOE_BLOCK)], sem_out.at[0])
            copy.start()
            copy.wait()
            return carry

        lax.fori_loop(total, nblk, fill, 0)


def moe_experts(xs, blk_start, blk_cnt, w_gate, w_up, w_down, layer):
    cap, c = xs.shape
    d = 2 * c
    hid = w_gate.shape[2]
    nblk = cap // MOE_BLOCK
    wmap = lambda e, st, cn: (layer * N_EXPERTS + e, 0, 0)
    grid_spec = pltpu.PrefetchScalarGridSpec(
        num_scalar_prefetch=2,
        grid=(N_EXPERTS,),
        in_specs=[pl.BlockSpec(memory_space=pl.ANY),
                  pl.BlockSpec((1, d, hid), wmap),
                  pl.BlockSpec((1, d, hid), wmap),
                  pl.BlockSpec((1, hid, d), wmap)],
        out_specs=pl.BlockSpec(memory_space=pl.ANY),
        scratch_shapes=[pltpu.VMEM((EXPERT_SLOTS, MOE_BLOCK, c), U32), pltpu.VMEM((EXPERT_SLOTS, MOE_BLOCK, c), U32),
                        pltpu.SemaphoreType.DMA((EXPERT_SLOTS,)), pltpu.SemaphoreType.DMA((EXPERT_SLOTS,)),
                        pltpu.VMEM((d, hid), BF16), pltpu.VMEM((d, hid), BF16), pltpu.VMEM((hid, d), BF16)],
    )
    return pl.pallas_call(
        functools.partial(_expert_kernel, nblk=nblk),
        grid_spec=grid_spec,
        out_shape=jax.ShapeDtypeStruct((cap, c), U32),
        compiler_params=_cparams(("arbitrary",)),
        name="moe_experts",
    )(blk_start, blk_cnt, xs, w_gate, w_up, w_down)


def _combine_kernel(dest_ref, next_dest_ref, x_ref, wt_ref, g_ref, yb_ref, o_ref, buf, sem, *, tm, final_norm):
    i = pl.program_id(0)
    slot = i & 1

    def gather(idx_ref, s):
        def issue(r, carry):
            _row_copy(yb_ref, idx_ref[2 * r], buf.at[s, 0], r, sem.at[s]).start(priority=0)
            _row_copy(yb_ref, idx_ref[2 * r + 1], buf.at[s, 1], r, sem.at[s]).start(priority=1)
            return carry

        lax.fori_loop(0, tm, issue, 0, unroll=ROW_DMA_UNROLL)

    @pl.when(i == 0)
    def _():
        gather(dest_ref, 0)

    @pl.when(i + 1 < pl.num_programs(0))
    def _():
        gather(next_dest_ref, 1 - slot)

    for k in range(2):
        pltpu.make_async_copy(yb_ref.at[pl.ds(0, tm)], buf.at[slot, k], sem.at[slot]).wait()
    wt = wt_ref[...]
    y = x_ref[...] + (_unpack_halves(buf[slot, 0]) * wt[:, 0:1] + _unpack_halves(buf[slot, 1]) * wt[:, 1:2])
    o_ref[...] = _rms(y, g_ref[...]) if final_norm else y


def moe_combine(x2d, yb, dest_flat, wt, g_final, final_norm):
    n, d = x2d.shape
    c = yb.shape[1]
    tm = ROW_DMA_TILE
    nt = n // tm
    return pl.pallas_call(
        functools.partial(_combine_kernel, tm=tm, final_norm=final_norm),
        grid=(nt,),
        in_specs=[pl.BlockSpec((2 * tm,), lambda i: (i,), memory_space=pltpu.SMEM),
                  pl.BlockSpec((2 * tm,), lambda i: (jnp.minimum(i + 1, nt - 1),), memory_space=pltpu.SMEM),
                  pl.BlockSpec((tm, d), lambda i: (i, 0)),
                  pl.BlockSpec((tm, 2), lambda i: (i, 0)),
                  pl.BlockSpec((1, d), lambda i: (0, 0)),
                  pl.BlockSpec(memory_space=pl.ANY)],
        out_specs=pl.BlockSpec((tm, d), lambda i: (i, 0)),
        out_shape=jax.ShapeDtypeStruct((n, d), F32),
        scratch_shapes=[pltpu.VMEM((2, 2, tm, c), U32), pltpu.SemaphoreType.DMA((2,))],
        compiler_params=_cparams(("arbitrary",)),
        name="moe_combine",
    )(dest_flat, dest_flat, x2d, wt, g_final.reshape(1, d), yb)


def hier_moe_block(x2d, routing, w_gate, w_up, w_down, layer, g_final, final_norm):
    n = x2d.shape[0]
    cap = 2 * n + N_EXPERTS * MOE_BLOCK
    hp, eid, wt, rank, cnt = routing
    counts = cnt[0, N_GROUPS:N_GROUPS + N_EXPERTS].astype(I32)
    padded = (counts + MOE_BLOCK - 1) // MOE_BLOCK * MOE_BLOCK
    p_ends = jnp.cumsum(padded)
    p_starts = p_ends - padded
    experts = jnp.arange(N_EXPERTS, dtype=I32)
    dest = (jnp.sum(jnp.where(eid[..., None] == experts, p_starts, 0), axis=-1) + rank).reshape(-1)
    xs = moe_dispatch(hp, dest, jnp.stack([p_ends, padded]).astype(I32), cap)
    yb = moe_experts(xs, p_starts // MOE_BLOCK, padded // MOE_BLOCK, w_gate, w_up, w_down, layer)
    return moe_combine(x2d, yb, dest, wt, g_final, final_norm)


def kernel(x, mem, e_norm, e_w_in, e_conv_w, e_conv_b, e_filt_w1, e_filt_b1, e_filt_w2, e_filt_b2, e_filt_w3, e_filt_freq, e_hy_bias, e_lam, e_subln, e_w_out, o_norm, o_w_in, o_sink, o_w_out, c_norm, c_wq, c_wkv, c_wo, f_norm, f_w_grp, f_b_grp, f_w_exp, f_b_exp, f_w_gate, f_w_up, f_w_down, g_mem, g_final):
    b, seq, d = x.shape
    n = b * seq
    m = mem.shape[1]
    tables = dft_tables(seq // 2)
    twiddles = dft_twiddles(seq)
    x2 = x.reshape(n, d)
    mem2 = mem.reshape(b * m, d)
    w_gate = f_w_gate.reshape(DEPTH * N_EXPERTS, d, EXP_HIDDEN)
    w_up = f_w_up.reshape(DEPTH * N_EXPERTS, d, EXP_HIDDEN)
    w_down = f_w_down.reshape(DEPTH * N_EXPERTS, EXP_HIDDEN, d)
    qperm = window_column_perm()
    for i in range(DEPTH):
        j = i // 2
        if i % 2 == 0:
            hyw, qw = 3 * HY_WIDTH, 2 * DIFF_HEADS * DIFF_HEAD_DIM
            col = jnp.arange(e_w_in.shape[2])
            col_scale = jnp.where((col >= hyw) & (col < hyw + qw), DIFF_HEAD_DIM ** -0.5 * LOG2E, 1.0).astype(F32)
            w_in = (e_w_in[j] * col_scale).astype(BF16)
            proj_h, proj_a = norm_matmul(x2, e_norm[j], w_in, 512, [(hyw, BF16), (w_in.shape[1] - hyw, BF16)])
            y_hy = hyena_mixer(proj_h.reshape(b, seq, -1), tables, twiddles, e_conv_w[j], e_conv_b[j], e_filt_w1[j],
                               e_filt_b1[j], e_filt_w2[j], e_filt_b2[j], e_filt_w3[j], e_filt_freq[j], e_hy_bias[j])
            y_df = diff_attention(proj_a.reshape(b, seq, -1), e_lam[j], e_subln[j], i)
            w_out = e_w_out[j].astype(BF16)
            mixed, w_mix = [y_hy, y_df], [w_out[:HY_WIDTH], w_out[HY_WIDTH:]]
        else:
            oq = GQA_HEADS * GQA_HD
            col = jnp.arange(o_w_in.shape[2])
            col_perm = jnp.concatenate([qperm, col[oq:]])
            col_scale = jnp.where(col < oq, GQA_HD ** -0.5 * LOG2E, 1.0).astype(F32)
            w_in = (o_w_in[j][:, col_perm] * col_scale).astype(BF16)
            (proj,) = norm_matmul(x2, o_norm[j], w_in, 512, [(w_in.shape[1], BF16)])
            att = window_gqa(proj.reshape(b, seq, -1), o_sink[j])
            mixed, w_mix = [att], [o_w_out[j][qperm].astype(BF16)]
        (kv,) = norm_matmul(mem2, g_mem, c_wkv[i].astype(BF16), 512, [(2 * X_W, BF16)])
        wq = (c_wq[i] * (X_HD ** -0.5 * LOG2E)).astype(BF16)
        x3, *routing = post_mixer(x2.reshape(b, seq, d), mixed, w_mix, c_norm[i], wq,
                                  kv.reshape(b, m, -1), c_wo[i].astype(BF16), f_norm[i], f_w_grp[i], f_b_grp[i],
                                  f_w_exp[i], f_b_exp[i])
        x2 = hier_moe_block(x3.reshape(n, d), routing, w_gate, w_up, w_down, i, g_final, i == DEPTH - 1)
    return x2.reshape(b, seq, d)
```

```python
import functools
import math

import jax
import jax.numpy as jnp
from jax import lax
from jax.experimental import pallas as pl
from jax.experimental.pallas import tpu as pltpu

F32 = jnp.float32
BF16 = jnp.bfloat16
I32 = jnp.int32
U32 = jnp.uint32

D_MODEL = 1024
DEPTH = 4
EPS = 1e-6
NEG = -1e30
HY_WIDTH = 512
HY_BANDS = 16
HY_FILT_HIDDEN = 64
HY_DECAY_TARGET = 1e-2
HY_FAST_PCT = 0.3
HY_SLOW_PCT = 1.5
DIFF_HEADS = 4
DIFF_HEAD_DIM = 64
WIN = 128
GQA_HEADS = 16
GQA_KV = 4
GQA_HD = 64
X_HEADS = 4
X_HD = 128
X_W = X_HEADS * X_HD
N_GROUPS = 4
EXP_PER_GROUP = 8
N_EXPERTS = N_GROUPS * EXP_PER_GROUP
EXP_HIDDEN = 512

LOG2E = 1.4426950408889634
LANES = 128
VMEM_LIMIT = 56 * 1024 * 1024
MOE_BLOCK = 256
DIFF_Q_BLOCK = 256
DIFF_KEY_CHUNK = 512
DIFF_BATCH_PER_STEP = 8
WIN_BATCH_PER_STEP = 4
EXPERT_SLOTS = 4
EXPERT_AHEAD = EXPERT_SLOTS - 1
ROW_DMA_TILE = 1024
ROW_DMA_UNROLL = 16


def _cparams(sem):
    return pltpu.CompilerParams(dimension_semantics=sem, vmem_limit_bytes=VMEM_LIMIT)


def _rms(x, g):
    ms = jnp.mean(x * x, axis=-1, keepdims=True)
    return x * lax.rsqrt(ms + EPS) * g


def _dot(a, b):
    return jnp.dot(a, b, preferred_element_type=F32)


def _dot_nt(a, b):
    return lax.dot_general(a, b, (((1,), (1,)), ((), ())), preferred_element_type=F32)


def _norm_matmul_kernel(x_ref, g_ref, w_ref, *o_refs):
    h = _rms(x_ref[...], g_ref[...]).astype(BF16)
    col = 0
    for o_ref in o_refs:
        width = o_ref.shape[1]
        o_ref[...] = _dot(h, w_ref[:, col:col + width]).astype(o_ref.dtype)
        col += width


def norm_matmul(x2d, gamma, w_bf16, tm, outs):
    n, d = x2d.shape
    f = w_bf16.shape[1]
    assert sum(width for width, _ in outs) == f
    res = pl.pallas_call(
        _norm_matmul_kernel,
        grid=(n // tm,),
        in_specs=[pl.BlockSpec((tm, d), lambda i: (i, 0)),
                  pl.BlockSpec((1, d), lambda i: (0, 0)),
                  pl.BlockSpec((d, f), lambda i: (0, 0))],
        out_specs=[pl.BlockSpec((tm, width), lambda i: (i, 0)) for width, _ in outs],
        out_shape=[jax.ShapeDtypeStruct((n, width), dtype) for width, dtype in outs],
        compiler_params=_cparams(("parallel",)),
        name="norm_matmul",
    )(x2d, gamma.reshape(1, d), w_bf16)
    return res


def _conv3_kernel(u_ref, w_ref, b_ref, oe_ref, oo_ref, slab_ref):
    half = u_ref.shape[1] // 2
    w = w_ref[...]
    bias = b_ref[...]
    row = lax.broadcasted_iota(I32, (half, LANES), 0)
    for k in range(u_ref.shape[2] // LANES):
        cols = slice(k * LANES, (k + 1) * LANES)
        slab_ref[...] = u_ref[0, :, cols].astype(F32)
        ue = slab_ref[pl.ds(0, half, stride=2), :]
        uo = slab_ref[pl.ds(1, half, stride=2), :]
        uo_prev = jnp.where(row == 0, 0.0, pltpu.roll(uo, 1, 0))
        ue_next = jnp.where(row == half - 1, 0.0, pltpu.roll(ue, half - 1, 0))
        even = uo_prev * w[0:1, cols] + ue * w[1:2, cols] + uo * w[2:3, cols] + bias[:, cols]
        odd = ue * w[0:1, cols] + uo * w[1:2, cols] + ue_next * w[2:3, cols] + bias[:, cols]
        oe_ref[0, :, cols] = even.astype(oe_ref.dtype)
        oo_ref[0, :, cols] = odd.astype(oo_ref.dtype)


def conv3(proj3d, conv_w, conv_b):
    b, seq, c = proj3d.shape
    tc = HY_WIDTH
    out = pl.BlockSpec((1, seq // 2, tc), lambda i, j: (i, 0, j))
    return pl.pallas_call(
        _conv3_kernel,
        grid=(b, c // tc),
        in_specs=[pl.BlockSpec((1, seq, tc), lambda i, j: (i, 0, j)),
                  pl.BlockSpec((3, tc), lambda i, j: (0, j)),
                  pl.BlockSpec((1, tc), lambda i, j: (0, j))],
        out_specs=[out, out],
        out_shape=[jax.ShapeDtypeStruct((b, seq // 2, c), BF16)] * 2,
        scratch_shapes=[pltpu.VMEM((seq, LANES), F32)],
        compiler_params=_cparams(("parallel", "parallel")),
        name="hyena_conv3",
    )(proj3d, conv_w, conv_b.reshape(1, c))


def _hy_filter_kernel(t_ref, bands_ref, w1t_ref, w1c_ref, w1s_ref, b1_ref, w2_ref, b2_ref, w3_ref,
                      freq_ref, delta_ref, hf_ref, hb_ref, *, seq, tl):
    hi = lax.Precision.HIGHEST
    parity = pl.program_id(0)
    i = pl.program_id(1)
    t = t_ref[0]
    pos = (2 * (i * tl + lax.broadcasted_iota(I32, (tl, 1), 0)) + parity).astype(F32)
    ang = bands_ref[...] * (2.0 * math.pi * pos / seq)
    f = freq_ref[...]
    pre = (t * w1t_ref[...]
           + jnp.dot(jnp.cos(ang), w1c_ref[...], precision=hi, preferred_element_type=F32)
           - jnp.dot(jnp.sin(ang), w1s_ref[...], precision=hi, preferred_element_type=F32)
           + b1_ref[...])
    a = jnp.sin(f * pre)
    a = jnp.sin(f * (jnp.dot(a, w2_ref[...], precision=hi, preferred_element_type=F32) + b2_ref[...]))
    h = jnp.dot(a, w3_ref[...], precision=hi, preferred_element_type=F32)
    decay = jnp.exp(-t * jnp.abs(delta_ref[...]))
    w = HY_WIDTH
    for o in range(2):
        hf_ref[0, :, o * w:(o + 1) * w] = h[:, o * 2 * w:o * 2 * w + w] * decay
        hb_ref[0, :, o * w:(o + 1) * w] = jnp.where(pos == 0.0, 0.0, h[:, o * 2 * w + w:(o + 1) * 2 * w] * decay)


def hyena_filters_time(seq, w1, b1, w2, b2, w3, freq):
    tl = 512
    half = seq // 2
    hid = LANES
    pad_h = hid - HY_FILT_HIDDEN
    t = jnp.linspace(0.0, 1.0, seq, dtype=F32).reshape(half, 2).T.reshape(2, half, 1)
    bands = jnp.pad(jnp.linspace(1e-4, HY_BANDS - 1, HY_BANDS, dtype=F32)[None], ((0, 0), (0, LANES - HY_BANDS)))
    w1p = jnp.pad(w1, ((0, 0), (0, pad_h)))
    w1t = w1p[0:1]
    w1c = jnp.pad(w1p[1:1 + HY_BANDS], ((0, LANES - HY_BANDS), (0, 0)))
    w1s = jnp.pad(w1p[1 + HY_BANDS:], ((0, LANES - HY_BANDS), (0, 0)))
    b1p = jnp.pad(b1, (0, pad_h)).reshape(1, hid)
    w2p = jnp.pad(w2, ((0, pad_h), (0, pad_h)))
    b2p = jnp.pad(b2, (0, pad_h)).reshape(1, hid)
    w3p = jnp.pad(w3, ((0, pad_h), (0, 0)))
    freqp = jnp.pad(freq, (0, pad_h)).reshape(1, hid)
    max_decay = math.log(HY_DECAY_TARGET) / HY_FAST_PCT
    min_decay = math.log(HY_DECAY_TARGET) / HY_SLOW_PCT
    deltas = jnp.linspace(min_decay, max_decay, HY_WIDTH, dtype=F32)[None]
    fw = w3.shape[1]
    full = lambda shape: pl.BlockSpec(shape, lambda p, i: (0, 0))
    out = pl.BlockSpec((1, tl, 2 * HY_WIDTH), lambda p, i: (p, i, 0))
    return pl.pallas_call(
        functools.partial(_hy_filter_kernel, seq=seq, tl=tl),
        grid=(2, half // tl),
        in_specs=[pl.BlockSpec((1, tl, 1), lambda p, i: (p, i, 0)), full((1, LANES)), full((1, hid)),
                  full((LANES, hid)), full((LANES, hid)), full((1, hid)), full((hid, hid)), full((1, hid)),
                  full((hid, fw)), full((1, hid)), full((1, HY_WIDTH))],
        out_specs=[out, out],
        out_shape=[jax.ShapeDtypeStruct((2, half, 2 * HY_WIDTH), F32)] * 2,
        compiler_params=_cparams(("parallel", "parallel")),
        name="hyena_filter_mlp",
    )(t, bands, w1t, w1c, w1s, b1p, w2p, b2p, w3p, freqp, deltas)


def dft_tables(seq):
    n2 = 4 * seq
    sub = 64
    f = jnp.arange(seq, dtype=I32)[:, None]
    odd = 2 * f + 1
    s1 = jnp.arange(seq // sub, dtype=I32)[None]
    s0 = jnp.arange(sub, dtype=I32)[None]
    ang_p = ((odd * s1) % (n2 // sub)).astype(F32) * (2.0 * math.pi * sub / n2)
    ang_q = ((odd * s0) % n2).astype(F32) * (2.0 * math.pi / n2)
    pc, ps, qc, qs = jnp.cos(ang_p), jnp.sin(ang_p), jnp.cos(ang_q), jnp.sin(ang_q)
    c = (pc[:, :, None] * qc[:, None, :] - ps[:, :, None] * qs[:, None, :]).reshape(seq, seq)
    s = (ps[:, :, None] * qc[:, None, :] + pc[:, :, None] * qs[:, None, :]).reshape(seq, seq)
    return c.astype(BF16), s.astype(BF16), c.T.astype(BF16), s.T.astype(BF16)


def dft_twiddles(seq):
    g = jnp.arange(seq // 2, dtype=I32)[:, None]
    ang = (2 * g + 1).astype(F32) * (2.0 * math.pi / (4 * seq))
    return jnp.cos(ang), jnp.sin(ang)


def _half_transform(c1, s1, cg, sg, x0, x1):
    pc0, ps0 = _dot(c1, x0), _dot(s1, x0)
    pc1, ps1 = _dot(c1, x1), _dot(s1, x1)
    tr = cg * pc1 - sg * ps1
    ti = -(cg * ps1 + sg * pc1)
    return pc0 + tr, ti - ps0, pc0 - tr, ps0 + ti


def _spectrum_kernel(c_ref, s_ref, cg_ref, sg_ref, f0_ref, f1_ref, b0_ref, b1_ref,
                     kgre_ref, kgim_ref, khre_ref, khim_ref):
    c1, s1, cg, sg = c_ref[...], s_ref[...], cg_ref[...], sg_ref[...]
    fre_g, fim_g, fre_h, fim_h = _half_transform(c1, s1, cg, sg, f0_ref[0].astype(BF16), f1_ref[0].astype(BF16))
    bre_g, bim_g, bre_h, bim_h = _half_transform(c1, s1, cg, sg, b0_ref[0].astype(BF16), b1_ref[0].astype(BF16))
    kgre_ref[...] = fre_g + bre_g
    kgim_ref[...] = fim_g - bim_g
    khre_ref[...] = fre_h + bre_h
    khim_ref[...] = fim_h - bim_h


def filter_spectrum(tables, twiddles, hf, hb):
    _, half, cols = hf.shape
    tf, tn = 512, 512
    tab = pl.BlockSpec((tf, half), lambda i, j: (i, 0))
    twd = pl.BlockSpec((tf, 1), lambda i, j: (i, 0))
    even = pl.BlockSpec((1, half, tn), lambda i, j: (0, 0, j))
    odd = pl.BlockSpec((1, half, tn), lambda i, j: (1, 0, j))
    return pl.pallas_call(
        _spectrum_kernel,
        grid=(half // tf, cols // tn),
        in_specs=[tab, tab, twd, twd, even, odd, even, odd],
        out_specs=[pl.BlockSpec((tf, tn), lambda i, j: (i, j))] * 4,
        out_shape=[jax.ShapeDtypeStruct((half, cols), F32)] * 4,
        compiler_params=_cparams(("parallel", "parallel")),
        name="hyena_filter_spectrum",
    )(tables[0], tables[1], twiddles[0], twiddles[1], hf, hf, hb, hb)


def _hy_fwd_kernel(z0_ref, z1_ref, c_ref, s_ref, cg_ref, sg_ref, kgre_ref, kgim_ref, khre_ref, khim_ref,
                   u0re_ref, u0im_ref, u1re_ref, u1im_ref):
    cg, sg = cg_ref[...], sg_ref[...]
    zre_g, zim_g, zre_h, zim_h = _half_transform(c_ref[...], s_ref[...], cg, sg,
                                                 z0_ref[0].astype(BF16), z1_ref[0].astype(BF16))
    kgre, kgim, khre, khim = kgre_ref[...], kgim_ref[...], khre_ref[...], khim_ref[...]
    yre_g = zre_g * kgre - zim_g * kgim
    yim_g = zre_g * kgim + zim_g * kgre
    yre_h = zre_h * khre - zim_h * khim
    yim_h = zre_h * khim + zim_h * khre
    u0re_ref[0] = (yre_g + yre_h).astype(BF16)
    u0im_ref[0] = (yim_g - yim_h).astype(BF16)
    a = yre_g - yre_h
    b = yim_g + yim_h
    u1re_ref[0] = (cg * a - sg * b).astype(BF16)
    u1im_ref[0] = (cg * b + sg * a).astype(BF16)


def hyena_fwd(zsrc, tables, twiddles, kspec, order):
    (ze, ce), (zo, co) = zsrc
    b, half, _ = ze.shape
    w = HY_WIDTH
    tf = 512
    tab = pl.BlockSpec((tf, half), lambda i, j: (j, 0))
    twd = pl.BlockSpec((tf, 1), lambda i, j: (j, 0))
    kblk = pl.BlockSpec((tf, w), lambda i, j: (j, order))
    return pl.pallas_call(
        _hy_fwd_kernel,
        grid=(b, half // tf),
        in_specs=[pl.BlockSpec((1, half, w), lambda i, j: (i, 0, ce)),
                  pl.BlockSpec((1, half, w), lambda i, j: (i, 0, co)),
                  tab, tab, twd, twd, kblk, kblk, kblk, kblk],
        out_specs=[pl.BlockSpec((1, tf, w), lambda i, j: (i, j, 0))] * 4,
        out_shape=[jax.ShapeDtypeStruct((b, half, w), BF16)] * 4,
        compiler_params=_cparams(("parallel", "parallel")),
        name="hyena_dft_fwd",
    )(ze, zo, tables[0], tables[1], twiddles[0], twiddles[1], *kspec)


def _hy_inv_kernel(ct_ref, st_ref, u0re_ref, u0im_ref, u1re_ref, u1im_ref, z0_ref, z1_ref, g0_ref, g1_ref,
                   bias_ref, o_ref, *, scale):
    ct, st = ct_ref[...], st_ref[...]
    w = z0_ref.shape[2]
    y0 = (_dot(ct, u0re_ref[0]) - _dot(st, u0im_ref[0])) * scale
    y1 = (_dot(ct, u1re_ref[0]) - _dot(st, u1im_ref[0])) * scale
    bias = bias_ref[...]
    o_ref[0, :, 0:w] = (g0_ref[0].astype(F32) * (y0 + z0_ref[0].astype(F32) * bias)).astype(o_ref.dtype)
    o_ref[0, :, w:2 * w] = (g1_ref[0].astype(F32) * (y1 + z1_ref[0].astype(F32) * bias)).astype(o_ref.dtype)


def hyena_inv(tables, u, zsrc, gsrc, bias_row, out_dtype):
    b, half, w = u[0].shape
    tt = 512
    tab = pl.BlockSpec((tt, half), lambda i, j: (j, 0))
    ublk = pl.BlockSpec((1, half, w), lambda i, j: (i, 0, 0))

    def rows(col):
        return pl.BlockSpec((1, tt, w), lambda i, j: (i, j, col))

    (ze, ce), (zo, co) = zsrc
    (ge, gce), (go, gco) = gsrc
    return pl.pallas_call(
        functools.partial(_hy_inv_kernel, scale=0.5 / half),
        grid=(b, half // tt),
        in_specs=[tab, tab, ublk, ublk, ublk, ublk, rows(ce), rows(co), rows(gce), rows(gco),
                  pl.BlockSpec((1, w), lambda i, j: (0, 0))],
        out_specs=pl.BlockSpec((1, tt, 2 * w), lambda i, j: (i, j, 0)),
        out_shape=jax.ShapeDtypeStruct((b, half, 2 * w), out_dtype),
        compiler_params=_cparams(("parallel", "parallel")),
        name="hyena_dft_inv",
    )(tables[2], tables[3], *u, ze, zo, ge, go, bias_row)


def hyena_mixer(proj3d, tables, twiddles, conv_w, conv_b, w1, b1, w2, b2, w3, freq, hy_bias):
    seq = proj3d.shape[1]
    ue, uo = conv3(proj3d, conv_w, conv_b)
    hf, hb = hyena_filters_time(seq, w1, b1, w2, b2, w3, freq)
    kspec = filter_spectrum(tables, twiddles, hf, hb)
    v = ((ue, 0), (uo, 0))
    uu = hyena_fwd(v, tables, twiddles, kspec, 0)
    z1 = hyena_inv(tables, uu, v, ((ue, 1), (uo, 1)), hy_bias[0:1], F32)
    z = ((z1, 0), (z1, 1))
    uu = hyena_fwd(z, tables, twiddles, kspec, 1)
    return hyena_inv(tables, uu, z, ((ue, 2), (uo, 2)), hy_bias[1:2], BF16)


def _diff_attn_kernel(slope_ref, q_ref, k_ref, v_ref, lam_ref, sub_ref, o_ref, bias_ref, *, tq, lam_init):
    h = pl.program_id(0)
    qi = pl.program_id(1)
    seq = k_ref.shape[1]

    @pl.when(pl.program_id(2) == 0)
    def _():
        qpos = qi * tq + lax.broadcasted_iota(I32, (tq, seq), 0)
        kpos = lax.broadcasted_iota(I32, (tq, seq), 1)
        bias_ref[...] = slope_ref[h] * jnp.abs(qpos - kpos).astype(F32)

    lane = lax.broadcasted_iota(I32, (1, 2 * DIFF_HEAD_DIM), 1)
    ck = DIFF_KEY_CHUNK
    l = lam_ref[...]
    lam_full = (jnp.exp(jnp.sum(l[0:1] * l[1:2], axis=-1, keepdims=True))
                - jnp.exp(jnp.sum(l[2:3] * l[3:4], axis=-1, keepdims=True)) + lam_init)

    def attend(bb, m):
        q = q_ref[bb]
        k = k_ref[bb]
        v = v_ref[bb]
        keep = (lane < DIFF_HEAD_DIM) if m == 0 else (lane >= DIFF_HEAD_DIM)
        qm = jnp.where(keep, q, jnp.zeros_like(q))
        s = [_dot_nt(qm, k[c:c + ck]) - bias_ref[:, c:c + ck] for c in range(0, seq, ck)]
        mx = functools.reduce(jnp.maximum, [jnp.max(sc, axis=-1, keepdims=True) for sc in s])
        acc = jnp.zeros((tq, 2 * DIFF_HEAD_DIM), F32)
        den = jnp.zeros((tq, 1), F32)
        for i, sc in enumerate(s):
            e = jnp.exp2(sc - mx)
            den = den + jnp.sum(e, axis=-1, keepdims=True)
            acc = acc + _dot(e.astype(BF16), v[i * ck:(i + 1) * ck])
        return acc / den

    for bb in range(q_ref.shape[0]):
        o = attend(bb, 0) - lam_full * attend(bb, 1)
        o_ref[bb] = (_rms(o, sub_ref[...]) * (1.0 - lam_init)).astype(o_ref.dtype)


def diff_attention(proj3d, lam, subln, layer_idx):
    b, seq, _ = proj3d.shape
    tq = DIFF_Q_BLOCK
    nb = DIFF_BATCH_PER_STEP
    hw = 2 * DIFF_HEAD_DIM
    qb, kb, vb = 0, DIFF_HEADS, 2 * DIFF_HEADS
    lam_init = 0.8 - 0.6 * math.exp(-0.3 * layer_idx)
    slopes = 2.0 ** (-8.0 * jnp.arange(1, DIFF_HEADS + 1, dtype=F32) / DIFF_HEADS) * LOG2E
    return pl.pallas_call(
        functools.partial(_diff_attn_kernel, tq=tq, lam_init=lam_init),
        grid=(DIFF_HEADS, seq // tq, b // nb),
        in_specs=[pl.BlockSpec(memory_space=pltpu.SMEM),
                  pl.BlockSpec((nb, tq, hw), lambda h, j, i: (i, j, qb + h)),
                  pl.BlockSpec((nb, seq, hw), lambda h, j, i: (i, 0, kb + h)),
                  pl.BlockSpec((nb, seq, hw), lambda h, j, i: (i, 0, vb + h)),
                  pl.BlockSpec((4, DIFF_HEAD_DIM), lambda h, j, i: (0, 0)),
                  pl.BlockSpec((1, hw), lambda h, j, i: (0, 0))],
        out_specs=pl.BlockSpec((nb, tq, hw), lambda h, j, i: (i, j, h)),
        out_shape=jax.ShapeDtypeStruct((b, seq, DIFF_HEADS * hw), BF16),
        scratch_shapes=[pltpu.VMEM((tq, seq), F32)],
        compiler_params=_cparams(("parallel", "parallel", "arbitrary")),
        name="diff_attention",
    )(slopes, proj3d, proj3d, proj3d, lam, subln.reshape(1, hw))


def _win_attn_kernel(slope_ref, sink_ref, q_ref, kp_ref, kc_ref, kn_ref, vp_ref, vc_ref, vn_ref, o_ref, bias_ref,
                     *, tq, seq):
    qi = pl.program_id(0)
    span = 3 * tq
    group = GQA_HEADS // GQA_KV

    @pl.when(pl.program_id(1) == 0)
    def _():
        qpos = qi * tq + lax.broadcasted_iota(I32, (tq, span), 0)
        kpos = (qi - 1) * tq + lax.broadcasted_iota(I32, (tq, span), 1)
        rel = jnp.abs(qpos - kpos)
        relf = rel.astype(F32)
        masked = jnp.where((rel <= WIN) & (kpos >= 0) & (kpos < seq), 0.0, -NEG)
        for head in range(GQA_HEADS):
            bias_ref[head] = slope_ref[head] * relf + masked

    low = lax.broadcasted_iota(I32, (1, LANES), 1) < GQA_HD
    for bb in range(q_ref.shape[0]):
        kwin = jnp.concatenate([kp_ref[bb], kc_ref[bb], kn_ref[bb]], axis=0)
        vwin = jnp.concatenate([vp_ref[bb], vc_ref[bb], vn_ref[bb]], axis=0)
        for p in range(GQA_KV // 2):
            kb = kwin[:, p * LANES:(p + 1) * LANES]
            vb = vwin[:, p * LANES:(p + 1) * LANES]
            outs = []
            for half in range(2):
                kv = 2 * p + half
                mine = low if half == 0 else jnp.logical_not(low)
                qblocks = [q_ref[bb, :, (group * p + r) * LANES:(group * p + r + 1) * LANES] for r in range(group)]
                qs = jnp.concatenate([jnp.where(mine, qb, jnp.zeros_like(qb)) for qb in qblocks], axis=0)
                s = _dot_nt(qs, kb)
                es, inv = [], []
                for r in range(group):
                    head = kv * group + r
                    sink = sink_ref[head]
                    sr = s[r * tq:(r + 1) * tq] - bias_ref[head]
                    m = jnp.maximum(jnp.max(sr, axis=-1, keepdims=True), sink)
                    e = jnp.exp2(sr - m)
                    inv.append(1.0 / (jnp.sum(e, axis=-1, keepdims=True) + jnp.exp2(sink - m)))
                    es.append(e.astype(BF16))
                o = _dot(jnp.concatenate(es, axis=0), vb)
                outs.append([o[r * tq:(r + 1) * tq] * inv[r] for r in range(group)])
            for r in range(group):
                col = (group * p + r) * LANES
                o_ref[bb, :, col:col + LANES] = jnp.where(low, outs[0][r], outs[1][r]).astype(o_ref.dtype)


def window_head_order():
    group = GQA_HEADS // GQA_KV
    order = []
    for p in range(GQA_KV // 2):
        for r in range(group):
            order += [(2 * p) * group + r, (2 * p + 1) * group + r]
    return order


def window_column_perm():
    cols = []
    for head in window_head_order():
        cols += list(range(head * GQA_HD, (head + 1) * GQA_HD))
    return jnp.asarray(cols, dtype=I32)


def window_gqa(proj3d, sink):
    b, seq, _ = proj3d.shape
    tq = WIN
    nq = seq // tq
    oq = GQA_HEADS * GQA_HD
    okv = GQA_KV * GQA_HD
    kcol, vcol = oq // okv, oq // okv + 1
    slopes = 2.0 ** (-8.0 * jnp.arange(1, GQA_HEADS + 1, dtype=F32) / GQA_HEADS) * LOG2E

    nb = WIN_BATCH_PER_STEP

    def neighbour(col, step):
        return pl.BlockSpec((nb, tq, okv), lambda j, i: (i, jnp.clip(j + step, 0, nq - 1), col))

    return pl.pallas_call(
        functools.partial(_win_attn_kernel, tq=tq, seq=seq),
        grid=(nq, b // nb),
        in_specs=[pl.BlockSpec(memory_space=pltpu.SMEM),
                  pl.BlockSpec(memory_space=pltpu.SMEM),
                  pl.BlockSpec((nb, tq, oq), lambda j, i: (i, j, 0)),
                  neighbour(kcol, -1), neighbour(kcol, 0), neighbour(kcol, 1),
                  neighbour(vcol, -1), neighbour(vcol, 0), neighbour(vcol, 1)],
        out_specs=pl.BlockSpec((nb, tq, oq), lambda j, i: (i, j, 0)),
        out_shape=jax.ShapeDtypeStruct((b, seq, oq), BF16),
        scratch_shapes=[pltpu.VMEM((GQA_HEADS, tq, 3 * tq), F32)],
        compiler_params=_cparams(("parallel", "arbitrary")),
        name="window_gqa",
    )(slopes, sink.astype(F32) * LOG2E, proj3d, proj3d, proj3d, proj3d, proj3d, proj3d, proj3d)


def _cross_attend(x, g, wq, kv, wo):
    h = _rms(x, g).astype(BF16)
    q = _dot(h, wq)
    outs = []
    for hd in range(X_HEADS):
        qh = q[:, hd * X_HD:(hd + 1) * X_HD].astype(BF16)
        kh = kv[:, hd * X_HD:(hd + 1) * X_HD]
        vh = kv[:, X_W + hd * X_HD:X_W + (hd + 1) * X_HD]
        s = _dot_nt(qh, kh)
        e = jnp.exp2(s - jnp.max(s, axis=-1, keepdims=True))
        outs.append(_dot(e.astype(BF16), vh) / jnp.sum(e, axis=-1, keepdims=True))
    o = jnp.concatenate(outs, axis=-1).astype(BF16)
    return x + _dot(o, wo)


def _split_bf16(x):
    hi = x.astype(BF16)
    lo = (x - hi.astype(F32)).astype(BF16)
    return hi, lo


def _pack_halves(h):
    c = h.shape[1] // 2
    left = lax.bitcast_convert_type(h[:, :c].astype(BF16).astype(F32), U32)
    right = lax.bitcast_convert_type(h[:, c:].astype(BF16).astype(F32), U32)
    return left | (right >> 16)


def _unpack_halves(p):
    left = lax.bitcast_convert_type(p & jnp.uint32(0xFFFF0000), F32)
    right = lax.bitcast_convert_type(p << 16, F32)
    return jnp.concatenate([left, right], axis=-1)


def _route(x, g, whi, wlo, bias, hp_ref, eid_ref, wt_ref, rank_ref, cnt_ref, base_ref):
    tm = x.shape[0]
    h = _rms(x, g)
    hp_ref[...] = _pack_halves(h)
    hhi, hlo = _split_bf16(h)
    logits = _dot(hhi, whi) + _dot(hlo, whi) + _dot(hhi, wlo) + bias
    lane = lax.broadcasted_iota(I32, logits.shape, 1)
    big = jnp.int32(LANES)
    ninf = -jnp.inf

    gl = jnp.where(lane < N_GROUPS, logits, ninf)
    gmax = jnp.max(gl, axis=-1, keepdims=True)
    gsel = jnp.min(jnp.where(gl == gmax, lane, big), axis=-1, keepdims=True)
    ggate = 1.0 / jnp.sum(jnp.exp(gl - gmax), axis=-1, keepdims=True)

    lo_lane = N_GROUPS + gsel * EXP_PER_GROUP
    el = jnp.where((lane >= lo_lane) & (lane < lo_lane + EXP_PER_GROUP), logits, ninf)
    v1 = jnp.max(el, axis=-1, keepdims=True)
    i1 = jnp.min(jnp.where(el == v1, lane, big), axis=-1, keepdims=True)
    el2 = jnp.where(lane == i1, ninf, el)
    v2 = jnp.max(el2, axis=-1, keepdims=True)
    i2 = jnp.min(jnp.where(el2 == v2, lane, big), axis=-1, keepdims=True)
    e2 = jnp.exp(v2 - v1)
    w1 = ggate / (1.0 + e2)
    w2 = ggate * e2 / (1.0 + e2)

    one1 = lane == i1
    one2 = lane == i2
    onehot = (one1 | one2).astype(F32)
    r = lax.broadcasted_iota(I32, (tm, tm), 0)
    c = lax.broadcasted_iota(I32, (tm, tm), 1)
    tri = (c < r).astype(BF16)
    before = _dot(tri, onehot.astype(BF16)) + base_ref[...]
    rank1 = jnp.sum(jnp.where(one1, before, 0.0), axis=-1, keepdims=True)
    rank2 = jnp.sum(jnp.where(one2, before, 0.0), axis=-1, keepdims=True)
    total = base_ref[...] + jnp.sum(onehot, axis=0, keepdims=True)
    base_ref[...] = total
    cnt_ref[...] = total

    col = lax.broadcasted_iota(I32, (tm, 2), 1)
    eid_ref[...] = jnp.where(col == 0, i1, i2) - N_GROUPS
    wt_ref[...] = jnp.where(col == 0, w1, w2)
    rank_ref[...] = jnp.where(col == 0, rank1, rank2).astype(I32)


def _interleave_rows(pair):
    t, w2 = pair.shape
    w = w2 // 2
    r = lax.broadcasted_iota(I32, (2 * t, t), 0)
    c = lax.broadcasted_iota(I32, (2 * t, t), 1)
    pick_even = (r == 2 * c).astype(BF16)
    pick_odd = (r == 2 * c + 1).astype(BF16)
    return (_dot(pick_even, pair[:, :w]) + _dot(pick_odd, pair[:, w:])).astype(BF16)


def _post_mixer_kernel(*refs, n_lhs, paired):
    x_ref = refs[0]
    a_refs = refs[1:1 + n_lhs]
    w_refs = refs[1 + n_lhs:1 + 2 * n_lhs]
    (cg_ref, wq_ref, kv_ref, wo_ref, fg_ref, whi_ref, wlo_ref, rb_ref,
     x_out_ref, hp_ref, eid_ref, wt_ref, rank_ref, cnt_ref, base_ref) = refs[1 + 2 * n_lhs:]

    @pl.when((pl.program_id(0) == 0) & (pl.program_id(1) == 0))
    def _():
        base_ref[...] = jnp.zeros_like(base_ref)

    x = x_ref[0]
    for a_ref, w_ref, is_paired in zip(a_refs, w_refs, paired):
        a = a_ref[0].astype(BF16)
        x = x + _dot(_interleave_rows(a) if is_paired else a, w_ref[...])
    x = _cross_attend(x, cg_ref[...], wq_ref[...], kv_ref[0], wo_ref[...])
    x_out_ref[0] = x
    _route(x, fg_ref[...], whi_ref[...], wlo_ref[...], rb_ref[...], hp_ref, eid_ref, wt_ref, rank_ref, cnt_ref,
           base_ref)


def post_mixer(x3d, lhs_list, w_list, c_gamma, wq, kv, wo, f_gamma, w_grp, b_grp, w_exp, b_exp):
    b, seq, d = x3d.shape
    paired = tuple(a.shape[1] != seq for a in lhs_list)
    n = b * seq
    m = kv.shape[1]
    tm = 512
    nt = seq // tm
    wcat = jnp.pad(jnp.concatenate([w_grp, w_exp], axis=1), ((0, 0), (0, LANES - N_GROUPS - N_EXPERTS)))
    bcat = jnp.pad(jnp.concatenate([b_grp, b_exp]), (0, LANES - N_GROUPS - N_EXPERTS)).reshape(1, LANES)
    whi = wcat.astype(BF16)
    wlo = (wcat - whi.astype(F32)).astype(BF16)
    const = lambda shape: pl.BlockSpec(shape, lambda i, j: (0,) * len(shape))
    tok = lambda width: pl.BlockSpec((tm, width), lambda i, j: (i * nt + j, 0))
    in_specs = [pl.BlockSpec((1, tm, d), lambda i, j: (i, j, 0))]
    in_specs += [pl.BlockSpec((1, tm // 2 if p else tm, a.shape[2]), lambda i, j: (i, j, 0))
                 for a, p in zip(lhs_list, paired)]
    in_specs += [const(w.shape) for w in w_list]
    in_specs += [const((1, d)), const((d, X_W)), pl.BlockSpec((1, m, 2 * X_W), lambda i, j: (i, 0, 0)),
                 const((X_W, d)), const((1, d)), const((d, LANES)), const((d, LANES)), const((1, LANES))]
    return pl.pallas_call(
        functools.partial(_post_mixer_kernel, n_lhs=len(lhs_list), paired=paired),
        grid=(b, nt),
        in_specs=in_specs,
        out_specs=[pl.BlockSpec((1, tm, d), lambda i, j: (i, j, 0)), tok(d // 2), tok(2), tok(2), tok(2),
                   const((1, LANES))],
        out_shape=[jax.ShapeDtypeStruct((b, seq, d), F32),
                   jax.ShapeDtypeStruct((n, d // 2), U32),
                   jax.ShapeDtypeStruct((n, 2), I32),
                   jax.ShapeDtypeStruct((n, 2), F32),
                   jax.ShapeDtypeStruct((n, 2), I32),
                   jax.ShapeDtypeStruct((1, LANES), F32)],
        scratch_shapes=[pltpu.VMEM((1, LANES), F32)],
        compiler_params=_cparams(("arbitrary", "arbitrary")),
        name="post_mixer",
    )(x3d, *lhs_list, *w_list, c_gamma.reshape(1, d), wq, kv, wo, f_gamma.reshape(1, d), whi, wlo, bcat)


def _row_copy(src_ref, src_row, dst_ref, dst_row, sem):
    return pltpu.make_async_copy(src_ref.at[pl.ds(src_row, 1)], dst_ref.at[pl.ds(dst_row, 1)], sem)


def _dispatch_kernel(dest_ref, seg_ref, hp_ref, xs_ref, zbuf, sem, zsem, *, tm):
    @pl.when(pl.program_id(0) == 0)
    def _():
        zbuf[...] = jnp.zeros_like(zbuf)

        def zero_tail(e):
            tail = pl.multiple_of(seg_ref[0, e] - MOE_BLOCK, MOE_BLOCK)
            return pltpu.make_async_copy(zbuf, xs_ref.at[pl.ds(tail, MOE_BLOCK)], zsem)

        def start(e, carry):
            @pl.when(seg_ref[1, e] > 0)
            def _():
                zero_tail(e).start()
            return carry

        def finish(e, carry):
            @pl.when(seg_ref[1, e] > 0)
            def _():
                zero_tail(e).wait()
            return carry

        lax.fori_loop(0, N_EXPERTS, start, 0)
        lax.fori_loop(0, N_EXPERTS, finish, 0)

        def unused(blk):
            return pltpu.make_async_copy(zbuf, xs_ref.at[pl.ds(blk * MOE_BLOCK, MOE_BLOCK)], zsem)

        def start_unused(blk, carry):
            unused(blk).start()
            return carry

        def finish_unused(blk, carry):
            unused(blk).wait()
            return carry

        first_unused = seg_ref[0, N_EXPERTS - 1] // MOE_BLOCK
        lax.fori_loop(first_unused, xs_ref.shape[0] // MOE_BLOCK, start_unused, 0)
        lax.fori_loop(first_unused, xs_ref.shape[0] // MOE_BLOCK, finish_unused, 0)

    def issue(r, carry):
        _row_copy(hp_ref, r, xs_ref, dest_ref[2 * r], sem).start(priority=0)
        _row_copy(hp_ref, r, xs_ref, dest_ref[2 * r + 1], sem).start(priority=1)
        return carry

    lax.fori_loop(0, tm, issue, 0, unroll=ROW_DMA_UNROLL)
    for _ in range(2):
        pltpu.make_async_copy(hp_ref, xs_ref.at[pl.ds(0, tm)], sem).wait()


def moe_dispatch(hp, dest_flat, segments, cap):
    n, c = hp.shape
    tm = ROW_DMA_TILE
    return pl.pallas_call(
        functools.partial(_dispatch_kernel, tm=tm),
        grid=(n // tm,),
        in_specs=[pl.BlockSpec((2 * tm,), lambda i: (i,), memory_space=pltpu.SMEM),
                  pl.BlockSpec(memory_space=pltpu.SMEM),
                  pl.BlockSpec((tm, c), lambda i: (i, 0))],
        out_specs=pl.BlockSpec(memory_space=pl.ANY),
        out_shape=jax.ShapeDtypeStruct((cap, c), U32),
        scratch_shapes=[pltpu.VMEM((MOE_BLOCK, c), U32), pltpu.SemaphoreType.DMA(()), pltpu.SemaphoreType.DMA(())],
        compiler_params=_cparams(("arbitrary",)),
        name="moe_dispatch",
    )(dest_flat, segments, hp)


def _expert_kernel(start_ref, cnt_ref, xs_ref, wg_ref, wu_ref, wd_ref, yb_ref, xbuf, ybuf, sem_in, sem_out,
                   wg_s, wu_s, wd_s, *, nblk):
    e = pl.program_id(0)
    last = pl.num_programs(0) - 1
    nb = cnt_ref[e]
    b0 = start_ref[e]
    total = start_ref[last] + cnt_ref[last]

    def fetch(g):
        slot = g % EXPERT_SLOTS
        return pltpu.make_async_copy(xs_ref.at[pl.ds(g * MOE_BLOCK, MOE_BLOCK)], xbuf.at[slot], sem_in.at[slot])

    def put(g):
        slot = g % EXPERT_SLOTS
        return pltpu.make_async_copy(ybuf.at[slot], yb_ref.at[pl.ds(g * MOE_BLOCK, MOE_BLOCK)], sem_out.at[slot])

    @pl.when(e == 0)
    def _():
        for g in range(EXPERT_AHEAD):
            @pl.when(g < total)
            def _():
                fetch(g).start(priority=1)

    @pl.when(nb > 0)
    def _():
        wg_s[...] = wg_ref[0].astype(BF16)
        wu_s[...] = wu_ref[0].astype(BF16)
        wd_s[...] = wd_ref[0].astype(BF16)

    def block(g, carry):
        slot = g % EXPERT_SLOTS
        fetch(g).wait()

        @pl.when(g + EXPERT_AHEAD < total)
        def _():
            fetch(g + EXPERT_AHEAD).start(priority=1)

        @pl.when(g >= EXPERT_SLOTS)
        def _():
            put(g - EXPERT_SLOTS).wait()

        x = _unpack_halves(xbuf[slot]).astype(BF16)
        a = _dot(x, wg_s[...])
        u = _dot(x, wu_s[...])
        hmid = (a / (1.0 + jnp.exp(-a)) * u).astype(BF16)
        ybuf[slot] = _pack_halves(_dot(hmid, wd_s[...]))
        put(g).start()
        return carry

    lax.fori_loop(b0, b0 + nb, block, 0)

    @pl.when(e == last)
    def _():
        for back in range(1, EXPERT_SLOTS + 1):
            @pl.when(total >= back)
            def _():
                put(total - back).wait()

        ybuf[0] = jnp.zeros(ybuf.shape[1:], ybuf.dtype)

        def fill(blk, carry):
            copy = pltpu.make_async_copy(ybuf.at[0], yb_ref.at[pl.ds(blk * MOE_BLOCK, MOE_BLOCK)], sem_out.at[0])
            copy.start()
            copy.wait()
            return carry

        lax.fori_loop(total, nblk, fill, 0)


def moe_experts(xs, blk_start, blk_cnt, w_gate, w_up, w_down, layer):
    cap, c = xs.shape
    d = 2 * c
    hid = w_gate.shape[2]
    nblk = cap // MOE_BLOCK
    wmap = lambda e, st, cn: (layer * N_EXPERTS + e, 0, 0)
    grid_spec = pltpu.PrefetchScalarGridSpec(
        num_scalar_prefetch=2,
        grid=(N_EXPERTS,),
        in_specs=[pl.BlockSpec(memory_space=pl.ANY),
                  pl.BlockSpec((1, d, hid), wmap),
                  pl.BlockSpec((1, d, hid), wmap),
                  pl.BlockSpec((1, hid, d), wmap)],
        out_specs=pl.BlockSpec(memory_space=pl.ANY),
        scratch_shapes=[pltpu.VMEM((EXPERT_SLOTS, MOE_BLOCK, c), U32), pltpu.VMEM((EXPERT_SLOTS, MOE_BLOCK, c), U32),
                        pltpu.SemaphoreType.DMA((EXPERT_SLOTS,)), pltpu.SemaphoreType.DMA((EXPERT_SLOTS,)),
                        pltpu.VMEM((d, hid), BF16), pltpu.VMEM((d, hid), BF16), pltpu.VMEM((hid, d), BF16)],
    )
    return pl.pallas_call(
        functools.partial(_expert_kernel, nblk=nblk),
        grid_spec=grid_spec,
        out_shape=jax.ShapeDtypeStruct((cap, c), U32),
        compiler_params=_cparams(("arbitrary",)),
        name="moe_experts",
    )(blk_start, blk_cnt, xs, w_gate, w_up, w_down)


def _combine_kernel(dest_ref, next_dest_ref, x_ref, wt_ref, g_ref, yb_ref, o_ref, buf, sem, *, tm, final_norm):
    i = pl.program_id(0)
    slot = i & 1

    def gather(idx_ref, s):
        def issue(r, carry):
            _row_copy(yb_ref, idx_ref[2 * r], buf.at[s, 0], r, sem.at[s]).start(priority=0)
            _row_copy(yb_ref, idx_ref[2 * r + 1], buf.at[s, 1], r, sem.at[s]).start(priority=1)
            return carry

        lax.fori_loop(0, tm, issue, 0, unroll=ROW_DMA_UNROLL)

    @pl.when(i == 0)
    def _():
        gather(dest_ref, 0)

    @pl.when(i + 1 < pl.num_programs(0))
    def _():
        gather(next_dest_ref, 1 - slot)

    for k in range(2):
        pltpu.make_async_copy(yb_ref.at[pl.ds(0, tm)], buf.at[slot, k], sem.at[slot]).wait()
    wt = wt_ref[...]
    y = x_ref[...] + (_unpack_halves(buf[slot, 0]) * wt[:, 0:1] + _unpack_halves(buf[slot, 1]) * wt[:, 1:2])
    o_ref[...] = _rms(y, g_ref[...]) if final_norm else y


def moe_combine(x2d, yb, dest_flat, wt, g_final, final_norm):
    n, d = x2d.shape
    c = yb.shape[1]
    tm = ROW_DMA_TILE
    nt = n // tm
    return pl.pallas_call(
        functools.partial(_combine_kernel, tm=tm, final_norm=final_norm),
        grid=(nt,),
        in_specs=[pl.BlockSpec((2 * tm,), lambda i: (i,), memory_space=pltpu.SMEM),
                  pl.BlockSpec((2 * tm,), lambda i: (jnp.minimum(i + 1, nt - 1),), memory_space=pltpu.SMEM),
                  pl.BlockSpec((tm, d), lambda i: (i, 0)),
                  pl.BlockSpec((tm, 2), lambda i: (i, 0)),
                  pl.BlockSpec((1, d), lambda i: (0, 0)),
                  pl.BlockSpec(memory_space=pl.ANY)],
        out_specs=pl.BlockSpec((tm, d), lambda i: (i, 0)),
        out_shape=jax.ShapeDtypeStruct((n, d), F32),
        scratch_shapes=[pltpu.VMEM((2, 2, tm, c), U32), pltpu.SemaphoreType.DMA((2,))],
        compiler_params=_cparams(("arbitrary",)),
        name="moe_combine",
    )(dest_flat, dest_flat, x2d, wt, g_final.reshape(1, d), yb)


def hier_moe_block(x2d, routing, w_gate, w_up, w_down, layer, g_final, final_norm):
    n = x2d.shape[0]
    cap = 2 * n + N_EXPERTS * MOE_BLOCK
    hp, eid, wt, rank, cnt = routing
    counts = cnt[0, N_GROUPS:N_GROUPS + N_EXPERTS].astype(I32)
    padded = (counts + MOE_BLOCK - 1) // MOE_BLOCK * MOE_BLOCK
    p_ends = jnp.cumsum(padded)
    p_starts = p_ends - padded
    experts = jnp.arange(N_EXPERTS, dtype=I32)
    dest = (jnp.sum(jnp.where(eid[..., None] == experts, p_starts, 0), axis=-1) + rank).reshape(-1)
    xs = moe_dispatch(hp, dest, jnp.stack([p_ends, padded]).astype(I32), cap)
    yb = moe_experts(xs, p_starts // MOE_BLOCK, padded // MOE_BLOCK, w_gate, w_up, w_down, layer)
    return moe_combine(x2d, yb, dest, wt, g_final, final_norm)


def kernel(x, mem, e_norm, e_w_in, e_conv_w, e_conv_b, e_filt_w1, e_filt_b1, e_filt_w2, e_filt_b2, e_filt_w3, e_filt_freq, e_hy_bias, e_lam, e_subln, e_w_out, o_norm, o_w_in, o_sink, o_w_out, c_norm, c_wq, c_wkv, c_wo, f_norm, f_w_grp, f_b_grp, f_w_exp, f_b_exp, f_w_gate, f_w_up, f_w_down, g_mem, g_final):
    b, seq, d = x.shape
    n = b * seq
    m = mem.shape[1]
    tables = dft_tables(seq // 2)
    twiddles = dft_twiddles(seq)
    x2 = x.reshape(n, d)
    mem2 = mem.reshape(b * m, d)
    w_gate = f_w_gate.reshape(DEPTH * N_EXPERTS, d, EXP_HIDDEN)
    w_up = f_w_up.reshape(DEPTH * N_EXPERTS, d, EXP_HIDDEN)
    w_down = f_w_down.reshape(DEPTH * N_EXPERTS, EXP_HIDDEN, d)
    qperm = window_column_perm()
    for i in range(DEPTH):
        j = i // 2
        if i % 2 == 0:
            hyw, qw = 3 * HY_WIDTH, 2 * DIFF_HEADS * DIFF_HEAD_DIM
            col = jnp.arange(e_w_in.shape[2])
            col_scale = jnp.where((col >= hyw) & (col < hyw + qw), DIFF_HEAD_DIM ** -0.5 * LOG2E, 1.0).astype(F32)
            w_in = (e_w_in[j] * col_scale).astype(BF16)
            proj_h, proj_a = norm_matmul(x2, e_norm[j], w_in, 1024, [(hyw, BF16), (w_in.shape[1] - hyw, BF16)])
            y_hy = hyena_mixer(proj_h.reshape(b, seq, -1), tables, twiddles, e_conv_w[j], e_conv_b[j], e_filt_w1[j],
                               e_filt_b1[j], e_filt_w2[j], e_filt_b2[j], e_filt_w3[j], e_filt_freq[j], e_hy_bias[j])
            y_df = diff_attention(proj_a.reshape(b, seq, -1), e_lam[j], e_subln[j], i)
            w_out = e_w_out[j].astype(BF16)
            mixed, w_mix = [y_hy, y_df], [w_out[:HY_WIDTH], w_out[HY_WIDTH:]]
        else:
            oq = GQA_HEADS * GQA_HD
            col = jnp.arange(o_w_in.shape[2])
            col_perm = jnp.concatenate([qperm, col[oq:]])
            col_scale = jnp.where(col < oq, GQA_HD ** -0.5 * LOG2E, 1.0).astype(F32)
            w_in = (o_w_in[j][:, col_perm] * col_scale).astype(BF16)
            (proj,) = norm_matmul(x2, o_norm[j], w_in, 1024, [(w_in.shape[1], BF16)])
            att = window_gqa(proj.reshape(b, seq, -1), o_sink[j])
            mixed, w_mix = [att], [o_w_out[j][qperm].astype(BF16)]
        (kv,) = norm_matmul(mem2, g_mem, c_wkv[i].astype(BF16), 512, [(2 * X_W, BF16)])
        wq = (c_wq[i] * (X_HD ** -0.5 * LOG2E)).astype(BF16)
        x3, *routing = post_mixer(x2.reshape(b, seq, d), mixed, w_mix, c_norm[i], wq,
                                  kv.reshape(b, m, -1), c_wo[i].astype(BF16), f_norm[i], f_w_grp[i], f_b_grp[i],
                                  f_w_exp[i], f_b_exp[i])
        x2 = hier_moe_block(x3.reshape(n, d), routing, w_gate, w_up, w_down, i, g_final, i == DEPTH - 1)
    return x2.reshape(b, seq, d)
```

```python
import functools
import math

import jax
import jax.numpy as jnp
from jax import lax
from jax.experimental import pallas as pl
from jax.experimental.pallas import tpu as pltpu

F32 = jnp.float32
BF16 = jnp.bfloat16
I32 = jnp.int32
U32 = jnp.uint32

D_MODEL = 1024
DEPTH = 4
EPS = 1e-6
NEG = -1e30
HY_WIDTH = 512
HY_BANDS = 16
HY_FILT_HIDDEN = 64
HY_DECAY_TARGET = 1e-2
HY_FAST_PCT = 0.3
HY_SLOW_PCT = 1.5
DIFF_HEADS = 4
DIFF_HEAD_DIM = 64
WIN = 128
GQA_HEADS = 16
GQA_KV = 4
GQA_HD = 64
X_HEADS = 4
X_HD = 128
X_W = X_HEADS * X_HD
N_GROUPS = 4
EXP_PER_GROUP = 8
N_EXPERTS = N_GROUPS * EXP_PER_GROUP
EXP_HIDDEN = 512

LOG2E = 1.4426950408889634
LANES = 128
VMEM_LIMIT = 56 * 1024 * 1024
MOE_BLOCK = 256
DIFF_Q_BLOCK = 256
DIFF_KEY_CHUNK = 512
DIFF_BATCH_PER_STEP = 8
WIN_BATCH_PER_STEP = 4
EXPERT_SLOTS = 4
EXPERT_AHEAD = EXPERT_SLOTS - 1
ROW_DMA_TILE = 1024
ROW_DMA_UNROLL = 16


def _cparams(sem):
    return pltpu.CompilerParams(dimension_semantics=sem, vmem_limit_bytes=VMEM_LIMIT)


def _rms(x, g):
    ms = jnp.mean(x * x, axis=-1, keepdims=True)
    return x * lax.rsqrt(ms + EPS) * g


def _dot(a, b):
    return jnp.dot(a, b, preferred_element_type=F32)


def _dot_nt(a, b):
    return lax.dot_general(a, b, (((1,), (1,)), ((), ())), preferred_element_type=F32)


def _norm_matmul_kernel(x_ref, g_ref, w_ref, *o_refs):
    h = _rms(x_ref[...], g_ref[...]).astype(BF16)
    col = 0
    for o_ref in o_refs:
        width = o_ref.shape[1]
        o_ref[...] = _dot(h, w_ref[:, col:col + width]).astype(o_ref.dtype)
        col += width


def norm_matmul(x2d, gamma, w_bf16, tm, outs):
    n, d = x2d.shape
    f = w_bf16.shape[1]
    assert sum(width for width, _ in outs) == f
    res = pl.pallas_call(
        _norm_matmul_kernel,
        grid=(n // tm,),
        in_specs=[pl.BlockSpec((tm, d), lambda i: (i, 0)),
                  pl.BlockSpec((1, d), lambda i: (0, 0)),
                  pl.BlockSpec((d, f), lambda i: (0, 0))],
        out_specs=[pl.BlockSpec((tm, width), lambda i: (i, 0)) for width, _ in outs],
        out_shape=[jax.ShapeDtypeStruct((n, width), dtype) for width, dtype in outs],
        compiler_params=_cparams(("parallel",)),
        name="norm_matmul",
    )(x2d, gamma.reshape(1, d), w_bf16)
    return res


def _conv3_kernel(u_ref, w_ref, b_ref, oe_ref, oo_ref, slab_ref):
    half = u_ref.shape[1] // 2
    w = w_ref[...]
    bias = b_ref[...]
    row = lax.broadcasted_iota(I32, (half, LANES), 0)
    for k in range(u_ref.shape[2] // LANES):
        cols = slice(k * LANES, (k + 1) * LANES)
        slab_ref[...] = u_ref[0, :, cols].astype(F32)
        ue = slab_ref[pl.ds(0, half, stride=2), :]
        uo = slab_ref[pl.ds(1, half, stride=2), :]
        uo_prev = jnp.where(row == 0, 0.0, pltpu.roll(uo, 1, 0))
        ue_next = jnp.where(row == half - 1, 0.0, pltpu.roll(ue, half - 1, 0))
        even = uo_prev * w[0:1, cols] + ue * w[1:2, cols] + uo * w[2:3, cols] + bias[:, cols]
        odd = ue * w[0:1, cols] + uo * w[1:2, cols] + ue_next * w[2:3, cols] + bias[:, cols]
        oe_ref[0, :, cols] = even.astype(oe_ref.dtype)
        oo_ref[0, :, cols] = odd.astype(oo_ref.dtype)


def conv3(proj3d, conv_w, conv_b):
    b, seq, c = proj3d.shape
    tc = HY_WIDTH
    out = pl.BlockSpec((1, seq // 2, tc), lambda i, j: (i, 0, j))
    return pl.pallas_call(
        _conv3_kernel,
        grid=(b, c // tc),
        in_specs=[pl.BlockSpec((1, seq, tc), lambda i, j: (i, 0, j)),
                  pl.BlockSpec((3, tc), lambda i, j: (0, j)),
                  pl.BlockSpec((1, tc), lambda i, j: (0, j))],
        out_specs=[out, out],
        out_shape=[jax.ShapeDtypeStruct((b, seq // 2, c), BF16)] * 2,
        scratch_shapes=[pltpu.VMEM((seq, LANES), F32)],
        compiler_params=_cparams(("parallel", "parallel")),
        name="hyena_conv3",
    )(proj3d, conv_w, conv_b.reshape(1, c))


def _hy_filter_kernel(t_ref, bands_ref, w1t_ref, w1c_ref, w1s_ref, b1_ref, w2_ref, b2_ref, w3_ref,
                      freq_ref, delta_ref, hf_ref, hb_ref, *, seq, tl):
    hi = lax.Precision.HIGHEST
    parity = pl.program_id(0)
    i = pl.program_id(1)
    t = t_ref[0]
    pos = (2 * (i * tl + lax.broadcasted_iota(I32, (tl, 1), 0)) + parity).astype(F32)
    ang = bands_ref[...] * (2.0 * math.pi * pos / seq)
    f = freq_ref[...]
    pre = (t * w1t_ref[...]
           + jnp.dot(jnp.cos(ang), w1c_ref[...], precision=hi, preferred_element_type=F32)
           - jnp.dot(jnp.sin(ang), w1s_ref[...], precision=hi, preferred_element_type=F32)
           + b1_ref[...])
    a = jnp.sin(f * pre)
    a = jnp.sin(f * (jnp.dot(a, w2_ref[...], precision=hi, preferred_element_type=F32) + b2_ref[...]))
    h = jnp.dot(a, w3_ref[...], precision=hi, preferred_element_type=F32)
    decay = jnp.exp(-t * jnp.abs(delta_ref[...]))
    w = HY_WIDTH
    for o in range(2):
        hf_ref[0, :, o * w:(o + 1) * w] = h[:, o * 2 * w:o * 2 * w + w] * decay
        hb_ref[0, :, o * w:(o + 1) * w] = jnp.where(pos == 0.0, 0.0, h[:, o * 2 * w + w:(o + 1) * 2 * w] * decay)


def hyena_filters_time(seq, w1, b1, w2, b2, w3, freq):
    tl = 512
    half = seq // 2
    hid = LANES
    pad_h = hid - HY_FILT_HIDDEN
    t = jnp.linspace(0.0, 1.0, seq, dtype=F32).reshape(half, 2).T.reshape(2, half, 1)
    bands = jnp.pad(jnp.linspace(1e-4, HY_BANDS - 1, HY_BANDS, dtype=F32)[None], ((0, 0), (0, LANES - HY_BANDS)))
    w1p = jnp.pad(w1, ((0, 0), (0, pad_h)))
    w1t = w1p[0:1]
    w1c = jnp.pad(w1p[1:1 + HY_BANDS], ((0, LANES - HY_BANDS), (0, 0)))
    w1s = jnp.pad(w1p[1 + HY_BANDS:], ((0, LANES - HY_BANDS), (0, 0)))
    b1p = jnp.pad(b1, (0, pad_h)).reshape(1, hid)
    w2p = jnp.pad(w2, ((0, pad_h), (0, pad_h)))
    b2p = jnp.pad(b2, (0, pad_h)).reshape(1, hid)
    w3p = jnp.pad(w3, ((0, pad_h), (0, 0)))
    freqp = jnp.pad(freq, (0, pad_h)).reshape(1, hid)
    max_decay = math.log(HY_DECAY_TARGET) / HY_FAST_PCT
    min_decay = math.log(HY_DECAY_TARGET) / HY_SLOW_PCT
    deltas = jnp.linspace(min_decay, max_decay, HY_WIDTH, dtype=F32)[None]
    fw = w3.shape[1]
    full = lambda shape: pl.BlockSpec(shape, lambda p, i: (0, 0))
    out = pl.BlockSpec((1, tl, 2 * HY_WIDTH), lambda p, i: (p, i, 0))
    return pl.pallas_call(
        functools.partial(_hy_filter_kernel, seq=seq, tl=tl),
        grid=(2, half // tl),
        in_specs=[pl.BlockSpec((1, tl, 1), lambda p, i: (p, i, 0)), full((1, LANES)), full((1, hid)),
                  full((LANES, hid)), full((LANES, hid)), full((1, hid)), full((hid, hid)), full((1, hid)),
                  full((hid, fw)), full((1, hid)), full((1, HY_WIDTH))],
        out_specs=[out, out],
        out_shape=[jax.ShapeDtypeStruct((2, half, 2 * HY_WIDTH), F32)] * 2,
        compiler_params=_cparams(("parallel", "parallel")),
        name="hyena_filter_mlp",
    )(t, bands, w1t, w1c, w1s, b1p, w2p, b2p, w3p, freqp, deltas)


def dft_tables(seq):
    n2 = 4 * seq
    sub = 64
    f = jnp.arange(seq, dtype=I32)[:, None]
    odd = 2 * f + 1
    s1 = jnp.arange(seq // sub, dtype=I32)[None]
    s0 = jnp.arange(sub, dtype=I32)[None]
    ang_p = ((odd * s1) % (n2 // sub)).astype(F32) * (2.0 * math.pi * sub / n2)
    ang_q = ((odd * s0) % n2).astype(F32) * (2.0 * math.pi / n2)
    pc, ps, qc, qs = jnp.cos(ang_p), jnp.sin(ang_p), jnp.cos(ang_q), jnp.sin(ang_q)
    c = (pc[:, :, None] * qc[:, None, :] - ps[:, :, None] * qs[:, None, :]).reshape(seq, seq)
    s = (ps[:, :, None] * qc[:, None, :] + pc[:, :, None] * qs[:, None, :]).reshape(seq, seq)
    return c.astype(BF16), s.astype(BF16), c.T.astype(BF16), s.T.astype(BF16)


def dft_twiddles(seq):
    g = jnp.arange(seq // 2, dtype=I32)[:, None]
    ang = (2 * g + 1).astype(F32) * (2.0 * math.pi / (4 * seq))
    return jnp.cos(ang), jnp.sin(ang)


def _half_transform(c1, s1, cg, sg, x0, x1):
    pc0, ps0 = _dot(c1, x0), _dot(s1, x0)
    pc1, ps1 = _dot(c1, x1), _dot(s1, x1)
    tr = cg * pc1 - sg * ps1
    ti = -(cg * ps1 + sg * pc1)
    return pc0 + tr, ti - ps0, pc0 - tr, ps0 + ti


def _spectrum_kernel(c_ref, s_ref, cg_ref, sg_ref, f0_ref, f1_ref, b0_ref, b1_ref,
                     kgre_ref, kgim_ref, khre_ref, khim_ref):
    c1, s1, cg, sg = c_ref[...], s_ref[...], cg_ref[...], sg_ref[...]
    fre_g, fim_g, fre_h, fim_h = _half_transform(c1, s1, cg, sg, f0_ref[0].astype(BF16), f1_ref[0].astype(BF16))
    bre_g, bim_g, bre_h, bim_h = _half_transform(c1, s1, cg, sg, b0_ref[0].astype(BF16), b1_ref[0].astype(BF16))
    kgre_ref[...] = fre_g + bre_g
    kgim_ref[...] = fim_g - bim_g
    khre_ref[...] = fre_h + bre_h
    khim_ref[...] = fim_h - bim_h


def filter_spectrum(tables, twiddles, hf, hb):
    _, half, cols = hf.shape
    tf, tn = 512, 512
    tab = pl.BlockSpec((tf, half), lambda i, j: (i, 0))
    twd = pl.BlockSpec((tf, 1), lambda i, j: (i, 0))
    even = pl.BlockSpec((1, half, tn), lambda i, j: (0, 0, j))
    odd = pl.BlockSpec((1, half, tn), lambda i, j: (1, 0, j))
    return pl.pallas_call(
        _spectrum_kernel,
        grid=(half // tf, cols // tn),
        in_specs=[tab, tab, twd, twd, even, odd, even, odd],
        out_specs=[pl.BlockSpec((tf, tn), lambda i, j: (i, j))] * 4,
        out_shape=[jax.ShapeDtypeStruct((half, cols), F32)] * 4,
        compiler_params=_cparams(("parallel", "parallel")),
        name="hyena_filter_spectrum",
    )(tables[0], tables[1], twiddles[0], twiddles[1], hf, hf, hb, hb)


def _hy_fwd_kernel(z0_ref, z1_ref, c_ref, s_ref, cg_ref, sg_ref, kgre_ref, kgim_ref, khre_ref, khim_ref,
                   u0re_ref, u0im_ref, u1re_ref, u1im_ref):
    cg, sg = cg_ref[...], sg_ref[...]
    zre_g, zim_g, zre_h, zim_h = _half_transform(c_ref[...], s_ref[...], cg, sg,
                                                 z0_ref[0].astype(BF16), z1_ref[0].astype(BF16))
    kgre, kgim, khre, khim = kgre_ref[...], kgim_ref[...], khre_ref[...], khim_ref[...]
    yre_g = zre_g * kgre - zim_g * kgim
    yim_g = zre_g * kgim + zim_g * kgre
    yre_h = zre_h * khre - zim_h * khim
    yim_h = zre_h * khim + zim_h * khre
    u0re_ref[0] = (yre_g + yre_h).astype(BF16)
    u0im_ref[0] = (yim_g - yim_h).astype(BF16)
    a = yre_g - yre_h
    b = yim_g + yim_h
    u1re_ref[0] = (cg * a - sg * b).astype(BF16)
    u1im_ref[0] = (cg * b + sg * a).astype(BF16)


def hyena_fwd(zsrc, tables, twiddles, kspec, order):
    (ze, ce), (zo, co) = zsrc
    b, half, _ = ze.shape
    w = HY_WIDTH
    tf = half
    tab = pl.BlockSpec((tf, half), lambda i, j: (j, 0))
    twd = pl.BlockSpec((tf, 1), lambda i, j: (j, 0))
    kblk = pl.BlockSpec((tf, w), lambda i, j: (j, order))
    return pl.pallas_call(
        _hy_fwd_kernel,
        grid=(b, half // tf),
        in_specs=[pl.BlockSpec((1, half, w), lambda i, j: (i, 0, ce)),
                  pl.BlockSpec((1, half, w), lambda i, j: (i, 0, co)),
                  tab, tab, twd, twd, kblk, kblk, kblk, kblk],
        out_specs=[pl.BlockSpec((1, tf, w), lambda i, j: (i, j, 0))] * 4,
        out_shape=[jax.ShapeDtypeStruct((b, half, w), BF16)] * 4,
        compiler_params=_cparams(("parallel", "parallel")),
        name="hyena_dft_fwd",
    )(ze, zo, tables[0], tables[1], twiddles[0], twiddles[1], *kspec)


def _hy_inv_kernel(ct_ref, st_ref, u0re_ref, u0im_ref, u1re_ref, u1im_ref, z0_ref, z1_ref, g0_ref, g1_ref,
                   bias_ref, o_ref, *, scale):
    ct, st = ct_ref[...], st_ref[...]
    w = z0_ref.shape[2]
    y0 = (_dot(ct, u0re_ref[0]) - _dot(st, u0im_ref[0])) * scale
    y1 = (_dot(ct, u1re_ref[0]) - _dot(st, u1im_ref[0])) * scale
    bias = bias_ref[...]
    o_ref[0, :, 0:w] = (g0_ref[0].astype(F32) * (y0 + z0_ref[0].astype(F32) * bias)).astype(o_ref.dtype)
    o_ref[0, :, w:2 * w] = (g1_ref[0].astype(F32) * (y1 + z1_ref[0].astype(F32) * bias)).astype(o_ref.dtype)


def hyena_inv(tables, u, zsrc, gsrc, bias_row, out_dtype):
    b, half, w = u[0].shape
    tt = half
    tab = pl.BlockSpec((tt, half), lambda i, j: (j, 0))
    ublk = pl.BlockSpec((1, half, w), lambda i, j: (i, 0, 0))

    def rows(col):
        return pl.BlockSpec((1, tt, w), lambda i, j: (i, j, col))

    (ze, ce), (zo, co) = zsrc
    (ge, gce), (go, gco) = gsrc
    return pl.pallas_call(
        functools.partial(_hy_inv_kernel, scale=0.5 / half),
        grid=(b, half // tt),
        in_specs=[tab, tab, ublk, ublk, ublk, ublk, rows(ce), rows(co), rows(gce), rows(gco),
                  pl.BlockSpec((1, w), lambda i, j: (0, 0))],
        out_specs=pl.BlockSpec((1, tt, 2 * w), lambda i, j: (i, j, 0)),
        out_shape=jax.ShapeDtypeStruct((b, half, 2 * w), out_dtype),
        compiler_params=_cparams(("parallel", "parallel")),
        name="hyena_dft_inv",
    )(tables[2], tables[3], *u, ze, zo, ge, go, bias_row)


def hyena_mixer(proj3d, tables, twiddles, conv_w, conv_b, w1, b1, w2, b2, w3, freq, hy_bias):
    seq = proj3d.shape[1]
    ue, uo = conv3(proj3d, conv_w, conv_b)
    hf, hb = hyena_filters_time(seq, w1, b1, w2, b2, w3, freq)
    kspec = filter_spectrum(tables, twiddles, hf, hb)
    v = ((ue, 0), (uo, 0))
    uu = hyena_fwd(v, tables, twiddles, kspec, 0)
    z1 = hyena_inv(tables, uu, v, ((ue, 1), (uo, 1)), hy_bias[0:1], F32)
    z = ((z1, 0), (z1, 1))
    uu = hyena_fwd(z, tables, twiddles, kspec, 1)
    return hyena_inv(tables, uu, z, ((ue, 2), (uo, 2)), hy_bias[1:2], BF16)


def _diff_attn_kernel(slope_ref, q_ref, k_ref, v_ref, lam_ref, sub_ref, o_ref, bias_ref, *, tq, lam_init):
    h = pl.program_id(0)
    qi = pl.program_id(1)
    seq = k_ref.shape[1]

    @pl.when(pl.program_id(2) == 0)
    def _():
        qpos = qi * tq + lax.broadcasted_iota(I32, (tq, seq), 0)
        kpos = lax.broadcasted_iota(I32, (tq, seq), 1)
        bias_ref[...] = slope_ref[h] * jnp.abs(qpos - kpos).astype(F32)

    lane = lax.broadcasted_iota(I32, (1, 2 * DIFF_HEAD_DIM), 1)
    ck = DIFF_KEY_CHUNK
    l = lam_ref[...]
    lam_full = (jnp.exp(jnp.sum(l[0:1] * l[1:2], axis=-1, keepdims=True))
                - jnp.exp(jnp.sum(l[2:3] * l[3:4], axis=-1, keepdims=True)) + lam_init)

    def attend(bb, m):
        q = q_ref[bb]
        k = k_ref[bb]
        v = v_ref[bb]
        keep = (lane < DIFF_HEAD_DIM) if m == 0 else (lane >= DIFF_HEAD_DIM)
        qm = jnp.where(keep, q, jnp.zeros_like(q))
        s = [_dot_nt(qm, k[c:c + ck]) - bias_ref[:, c:c + ck] for c in range(0, seq, ck)]
        mx = functools.reduce(jnp.maximum, [jnp.max(sc, axis=-1, keepdims=True) for sc in s])
        acc = jnp.zeros((tq, 2 * DIFF_HEAD_DIM), F32)
        den = jnp.zeros((tq, 1), F32)
        for i, sc in enumerate(s):
            e = jnp.exp2(sc - mx)
            den = den + jnp.sum(e, axis=-1, keepdims=True)
            acc = acc + _dot(e.astype(BF16), v[i * ck:(i + 1) * ck])
        return acc / den

    for bb in range(q_ref.shape[0]):
        o = attend(bb, 0) - lam_full * attend(bb, 1)
        o_ref[bb] = (_rms(o, sub_ref[...]) * (1.0 - lam_init)).astype(o_ref.dtype)


def diff_attention(proj3d, lam, subln, layer_idx):
    b, seq, _ = proj3d.shape
    tq = DIFF_Q_BLOCK
    nb = DIFF_BATCH_PER_STEP
    hw = 2 * DIFF_HEAD_DIM
    qb, kb, vb = 0, DIFF_HEADS, 2 * DIFF_HEADS
    lam_init = 0.8 - 0.6 * math.exp(-0.3 * layer_idx)
    slopes = 2.0 ** (-8.0 * jnp.arange(1, DIFF_HEADS + 1, dtype=F32) / DIFF_HEADS) * LOG2E
    return pl.pallas_call(
        functools.partial(_diff_attn_kernel, tq=tq, lam_init=lam_init),
        grid=(DIFF_HEADS, seq // tq, b // nb),
        in_specs=[pl.BlockSpec(memory_space=pltpu.SMEM),
                  pl.BlockSpec((nb, tq, hw), lambda h, j, i: (i, j, qb + h)),
                  pl.BlockSpec((nb, seq, hw), lambda h, j, i: (i, 0, kb + h)),
                  pl.BlockSpec((nb, seq, hw), lambda h, j, i: (i, 0, vb + h)),
                  pl.BlockSpec((4, DIFF_HEAD_DIM), lambda h, j, i: (0, 0)),
                  pl.BlockSpec((1, hw), lambda h, j, i: (0, 0))],
        out_specs=pl.BlockSpec((nb, tq, hw), lambda h, j, i: (i, j, h)),
        out_shape=jax.ShapeDtypeStruct((b, seq, DIFF_HEADS * hw), BF16),
        scratch_shapes=[pltpu.VMEM((tq, seq), F32)],
        compiler_params=_cparams(("parallel", "parallel", "arbitrary")),
        name="diff_attention",
    )(slopes, proj3d, proj3d, proj3d, lam, subln.reshape(1, hw))


def _win_attn_kernel(slope_ref, sink_ref, q_ref, kp_ref, kc_ref, kn_ref, vp_ref, vc_ref, vn_ref, o_ref, bias_ref,
                     *, tq, seq):
    qi = pl.program_id(0)
    span = 3 * tq
    group = GQA_HEADS // GQA_KV

    @pl.when(pl.program_id(1) == 0)
    def _():
        qpos = qi * tq + lax.broadcasted_iota(I32, (tq, span), 0)
        kpos = (qi - 1) * tq + lax.broadcasted_iota(I32, (tq, span), 1)
        rel = jnp.abs(qpos - kpos)
        relf = rel.astype(F32)
        masked = jnp.where((rel <= WIN) & (kpos >= 0) & (kpos < seq), 0.0, -NEG)
        for head in range(GQA_HEADS):
            bias_ref[head] = slope_ref[head] * relf + masked

    low = lax.broadcasted_iota(I32, (1, LANES), 1) < GQA_HD
    for bb in range(q_ref.shape[0]):
        kwin = jnp.concatenate([kp_ref[bb], kc_ref[bb], kn_ref[bb]], axis=0)
        vwin = jnp.concatenate([vp_ref[bb], vc_ref[bb], vn_ref[bb]], axis=0)
        for p in range(GQA_KV // 2):
            kb = kwin[:, p * LANES:(p + 1) * LANES]
            vb = vwin[:, p * LANES:(p + 1) * LANES]
            outs = []
            for half in range(2):
                kv = 2 * p + half
                mine = low if half == 0 else jnp.logical_not(low)
                qblocks = [q_ref[bb, :, (group * p + r) * LANES:(group * p + r + 1) * LANES] for r in range(group)]
                qs = jnp.concatenate([jnp.where(mine, qb, jnp.zeros_like(qb)) for qb in qblocks], axis=0)
                s = _dot_nt(qs, kb)
                es, inv = [], []
                for r in range(group):
                    head = kv * group + r
                    sink = sink_ref[head]
                    sr = s[r * tq:(r + 1) * tq] - bias_ref[head]
                    m = jnp.maximum(jnp.max(sr, axis=-1, keepdims=True), sink)
                    e = jnp.exp2(sr - m)
                    inv.append(1.0 / (jnp.sum(e, axis=-1, keepdims=True) + jnp.exp2(sink - m)))
                    es.append(e.astype(BF16))
                o = _dot(jnp.concatenate(es, axis=0), vb)
                outs.append([o[r * tq:(r + 1) * tq] * inv[r] for r in range(group)])
            for r in range(group):
                col = (group * p + r) * LANES
                o_ref[bb, :, col:col + LANES] = jnp.where(low, outs[0][r], outs[1][r]).astype(o_ref.dtype)


def window_head_order():
    group = GQA_HEADS // GQA_KV
    order = []
    for p in range(GQA_KV // 2):
        for r in range(group):
            order += [(2 * p) * group + r, (2 * p + 1) * group + r]
    return order


def window_column_perm():
    cols = []
    for head in window_head_order():
        cols += list(range(head * GQA_HD, (head + 1) * GQA_HD))
    return jnp.asarray(cols, dtype=I32)


def window_gqa(proj3d, sink):
    b, seq, _ = proj3d.shape
    tq = WIN
    nq = seq // tq
    oq = GQA_HEADS * GQA_HD
    okv = GQA_KV * GQA_HD
    kcol, vcol = oq // okv, oq // okv + 1
    slopes = 2.0 ** (-8.0 * jnp.arange(1, GQA_HEADS + 1, dtype=F32) / GQA_HEADS) * LOG2E

    nb = WIN_BATCH_PER_STEP

    def neighbour(col, step):
        return pl.BlockSpec((nb, tq, okv), lambda j, i: (i, jnp.clip(j + step, 0, nq - 1), col))

    return pl.pallas_call(
        functools.partial(_win_attn_kernel, tq=tq, seq=seq),
        grid=(nq, b // nb),
        in_specs=[pl.BlockSpec(memory_space=pltpu.SMEM),
                  pl.BlockSpec(memory_space=pltpu.SMEM),
                  pl.BlockSpec((nb, tq, oq), lambda j, i: (i, j, 0)),
                  neighbour(kcol, -1), neighbour(kcol, 0), neighbour(kcol, 1),
                  neighbour(vcol, -1), neighbour(vcol, 0), neighbour(vcol, 1)],
        out_specs=pl.BlockSpec((nb, tq, oq), lambda j, i: (i, j, 0)),
        out_shape=jax.ShapeDtypeStruct((b, seq, oq), BF16),
        scratch_shapes=[pltpu.VMEM((GQA_HEADS, tq, 3 * tq), F32)],
        compiler_params=_cparams(("parallel", "arbitrary")),
        name="window_gqa",
    )(slopes, sink.astype(F32) * LOG2E, proj3d, proj3d, proj3d, proj3d, proj3d, proj3d, proj3d)


def _cross_attend(x, g, wq, kv, wo):
    h = _rms(x, g).astype(BF16)
    q = _dot(h, wq)
    outs = []
    for hd in range(X_HEADS):
        qh = q[:, hd * X_HD:(hd + 1) * X_HD].astype(BF16)
        kh = kv[:, hd * X_HD:(hd + 1) * X_HD]
        vh = kv[:, X_W + hd * X_HD:X_W + (hd + 1) * X_HD]
        s = _dot_nt(qh, kh)
        e = jnp.exp2(s - jnp.max(s, axis=-1, keepdims=True))
        outs.append(_dot(e.astype(BF16), vh) / jnp.sum(e, axis=-1, keepdims=True))
    o = jnp.concatenate(outs, axis=-1).astype(BF16)
    return x + _dot(o, wo)


def _split_bf16(x):
    hi = x.astype(BF16)
    lo = (x - hi.astype(F32)).astype(BF16)
    return hi, lo


def _pack_halves(h):
    c = h.shape[1] // 2
    left = lax.bitcast_convert_type(h[:, :c].astype(BF16).astype(F32), U32)
    right = lax.bitcast_convert_type(h[:, c:].astype(BF16).astype(F32), U32)
    return left | (right >> 16)


def _unpack_halves(p):
    left = lax.bitcast_convert_type(p & jnp.uint32(0xFFFF0000), F32)
    right = lax.bitcast_convert_type(p << 16, F32)
    return jnp.concatenate([left, right], axis=-1)


def _route(x, g, whi, wlo, bias, hp_ref, eid_ref, wt_ref, rank_ref, cnt_ref, base_ref):
    tm = x.shape[0]
    h = _rms(x, g)
    hp_ref[...] = _pack_halves(h)
    hhi, hlo = _split_bf16(h)
    logits = _dot(hhi, whi) + _dot(hlo, whi) + _dot(hhi, wlo) + bias
    lane = lax.broadcasted_iota(I32, logits.shape, 1)
    big = jnp.int32(LANES)
    ninf = -jnp.inf

    gl = jnp.where(lane < N_GROUPS, logits, ninf)
    gmax = jnp.max(gl, axis=-1, keepdims=True)
    gsel = jnp.min(jnp.where(gl == gmax, lane, big), axis=-1, keepdims=True)
    ggate = 1.0 / jnp.sum(jnp.exp(gl - gmax), axis=-1, keepdims=True)

    lo_lane = N_GROUPS + gsel * EXP_PER_GROUP
    el = jnp.where((lane >= lo_lane) & (lane < lo_lane + EXP_PER_GROUP), logits, ninf)
    v1 = jnp.max(el, axis=-1, keepdims=True)
    i1 = jnp.min(jnp.where(el == v1, lane, big), axis=-1, keepdims=True)
    el2 = jnp.where(lane == i1, ninf, el)
    v2 = jnp.max(el2, axis=-1, keepdims=True)
    i2 = jnp.min(jnp.where(el2 == v2, lane, big), axis=-1, keepdims=True)
    e2 = jnp.exp(v2 - v1)
    w1 = ggate / (1.0 + e2)
    w2 = ggate * e2 / (1.0 + e2)

    one1 = lane == i1
    one2 = lane == i2
    onehot = (one1 | one2).astype(F32)
    r = lax.broadcasted_iota(I32, (tm, tm), 0)
    c = lax.broadcasted_iota(I32, (tm, tm), 1)
    tri = (c < r).astype(BF16)
    before = _dot(tri, onehot.astype(BF16)) + base_ref[...]
    rank1 = jnp.sum(jnp.where(one1, before, 0.0), axis=-1, keepdims=True)
    rank2 = jnp.sum(jnp.where(one2, before, 0.0), axis=-1, keepdims=True)
    total = base_ref[...] + jnp.sum(onehot, axis=0, keepdims=True)
    base_ref[...] = total
    cnt_ref[...] = total

    col = lax.broadcasted_iota(I32, (tm, 2), 1)
    eid_ref[...] = jnp.where(col == 0, i1, i2) - N_GROUPS
    wt_ref[...] = jnp.where(col == 0, w1, w2)
    rank_ref[...] = jnp.where(col == 0, rank1, rank2).astype(I32)


def _interleave_rows(pair):
    t, w2 = pair.shape
    w = w2 // 2
    r = lax.broadcasted_iota(I32, (2 * t, t), 0)
    c = lax.broadcasted_iota(I32, (2 * t, t), 1)
    pick_even = (r == 2 * c).astype(BF16)
    pick_odd = (r == 2 * c + 1).astype(BF16)
    return (_dot(pick_even, pair[:, :w]) + _dot(pick_odd, pair[:, w:])).astype(BF16)


def _post_mixer_kernel(*refs, n_lhs, paired):
    x_ref = refs[0]
    a_refs = refs[1:1 + n_lhs]
    w_refs = refs[1 + n_lhs:1 + 2 * n_lhs]
    (cg_ref, wq_ref, kv_ref, wo_ref, fg_ref, whi_ref, wlo_ref, rb_ref,
     x_out_ref, hp_ref, eid_ref, wt_ref, rank_ref, cnt_ref, base_ref) = refs[1 + 2 * n_lhs:]

    @pl.when((pl.program_id(0) == 0) & (pl.program_id(1) == 0))
    def _():
        base_ref[...] = jnp.zeros_like(base_ref)

    x = x_ref[0]
    for a_ref, w_ref, is_paired in zip(a_refs, w_refs, paired):
        a = a_ref[0].astype(BF16)
        x = x + _dot(_interleave_rows(a) if is_paired else a, w_ref[...])
    x = _cross_attend(x, cg_ref[...], wq_ref[...], kv_ref[0], wo_ref[...])
    x_out_ref[0] = x
    _route(x, fg_ref[...], whi_ref[...], wlo_ref[...], rb_ref[...], hp_ref, eid_ref, wt_ref, rank_ref, cnt_ref,
           base_ref)


def post_mixer(x3d, lhs_list, w_list, c_gamma, wq, kv, wo, f_gamma, w_grp, b_grp, w_exp, b_exp):
    b, seq, d = x3d.shape
    paired = tuple(a.shape[1] != seq for a in lhs_list)
    n = b * seq
    m = kv.shape[1]
    tm = 512
    nt = seq // tm
    wcat = jnp.pad(jnp.concatenate([w_grp, w_exp], axis=1), ((0, 0), (0, LANES - N_GROUPS - N_EXPERTS)))
    bcat = jnp.pad(jnp.concatenate([b_grp, b_exp]), (0, LANES - N_GROUPS - N_EXPERTS)).reshape(1, LANES)
    whi = wcat.astype(BF16)
    wlo = (wcat - whi.astype(F32)).astype(BF16)
    const = lambda shape: pl.BlockSpec(shape, lambda i, j: (0,) * len(shape))
    tok = lambda width: pl.BlockSpec((tm, width), lambda i, j: (i * nt + j, 0))
    in_specs = [pl.BlockSpec((1, tm, d), lambda i, j: (i, j, 0))]
    in_specs += [pl.BlockSpec((1, tm // 2 if p else tm, a.shape[2]), lambda i, j: (i, j, 0))
                 for a, p in zip(lhs_list, paired)]
    in_specs += [const(w.shape) for w in w_list]
    in_specs += [const((1, d)), const((d, X_W)), pl.BlockSpec((1, m, 2 * X_W), lambda i, j: (i, 0, 0)),
                 const((X_W, d)), const((1, d)), const((d, LANES)), const((d, LANES)), const((1, LANES))]
    return pl.pallas_call(
        functools.partial(_post_mixer_kernel, n_lhs=len(lhs_list), paired=paired),
        grid=(b, nt),
        in_specs=in_specs,
        out_specs=[pl.BlockSpec((1, tm, d), lambda i, j: (i, j, 0)), tok(d // 2), tok(2), tok(2), tok(2),
                   const((1, LANES))],
        out_shape=[jax.ShapeDtypeStruct((b, seq, d), F32),
                   jax.ShapeDtypeStruct((n, d // 2), U32),
                   jax.ShapeDtypeStruct((n, 2), I32),
                   jax.ShapeDtypeStruct((n, 2), F32),
                   jax.ShapeDtypeStruct((n, 2), I32),
                   jax.ShapeDtypeStruct((1, LANES), F32)],
        scratch_shapes=[pltpu.VMEM((1, LANES), F32)],
        compiler_params=_cparams(("arbitrary", "arbitrary")),
        name="post_mixer",
    )(x3d, *lhs_list, *w_list, c_gamma.reshape(1, d), wq, kv, wo, f_gamma.reshape(1, d), whi, wlo, bcat)


def _row_copy(src_ref, src_row, dst_ref, dst_row, sem):
    return pltpu.make_async_copy(src_ref.at[pl.ds(src_row, 1)], dst_ref.at[pl.ds(dst_row, 1)], sem)


def _dispatch_kernel(dest_ref, seg_ref, hp_ref, xs_ref, zbuf, sem, zsem, *, tm):
    @pl.when(pl.program_id(0) == 0)
    def _():
        zbuf[...] = jnp.zeros_like(zbuf)

        def zero_tail(e):
            tail = pl.multiple_of(seg_ref[0, e] - MOE_BLOCK, MOE_BLOCK)
            return pltpu.make_async_copy(zbuf, xs_ref.at[pl.ds(tail, MOE_BLOCK)], zsem)

        def start(e, carry):
            @pl.when(seg_ref[1, e] > 0)
            def _():
                zero_tail(e).start()
            return carry

        def finish(e, carry):
            @pl.when(seg_ref[1, e] > 0)
            def _():
                zero_tail(e).wait()
            return carry

        lax.fori_loop(0, N_EXPERTS, start, 0)
        lax.fori_loop(0, N_EXPERTS, finish, 0)

        def unused(blk):
            return pltpu.make_async_copy(zbuf, xs_ref.at[pl.ds(blk * MOE_BLOCK, MOE_BLOCK)], zsem)

        def start_unused(blk, carry):
            unused(blk).start()
            return carry

        def finish_unused(blk, carry):
            unused(blk).wait()
            return carry

        first_unused = seg_ref[0, N_EXPERTS - 1] // MOE_BLOCK
        lax.fori_loop(first_unused, xs_ref.shape[0] // MOE_BLOCK, start_unused, 0)
        lax.fori_loop(first_unused, xs_ref.shape[0] // MOE_BLOCK, finish_unused, 0)

    def issue(r, carry):
        _row_copy(hp_ref, r, xs_ref, dest_ref[2 * r], sem).start(priority=0)
        _row_copy(hp_ref, r, xs_ref, dest_ref[2 * r + 1], sem).start(priority=1)
        return carry

    lax.fori_loop(0, tm, issue, 0, unroll=ROW_DMA_UNROLL)
    for _ in range(2):
        pltpu.make_async_copy(hp_ref, xs_ref.at[pl.ds(0, tm)], sem).wait()


def moe_dispatch(hp, dest_flat, segments, cap):
    n, c = hp.shape
    tm = ROW_DMA_TILE
    return pl.pallas_call(
        functools.partial(_dispatch_kernel, tm=tm),
        grid=(n // tm,),
        in_specs=[pl.BlockSpec((2 * tm,), lambda i: (i,), memory_space=pltpu.SMEM),
                  pl.BlockSpec(memory_space=pltpu.SMEM),
                  pl.BlockSpec((tm, c), lambda i: (i, 0))],
        out_specs=pl.BlockSpec(memory_space=pl.ANY),
        out_shape=jax.ShapeDtypeStruct((cap, c), U32),
        scratch_shapes=[pltpu.VMEM((MOE_BLOCK, c), U32), pltpu.SemaphoreType.DMA(()), pltpu.SemaphoreType.DMA(())],
        compiler_params=_cparams(("arbitrary",)),
        name="moe_dispatch",
    )(dest_flat, segments, hp)


def _expert_kernel(start_ref, cnt_ref, xs_ref, wg_ref, wu_ref, wd_ref, yb_ref, xbuf, ybuf, sem_in, sem_out,
                   wg_s, wu_s, wd_s, *, nblk):
    e = pl.program_id(0)
    last = pl.num_programs(0) - 1
    nb = cnt_ref[e]
    b0 = start_ref[e]
    total = start_ref[last] + cnt_ref[last]

    def fetch(g):
        slot = g % EXPERT_SLOTS
        return pltpu.make_async_copy(xs_ref.at[pl.ds(g * MOE_BLOCK, MOE_BLOCK)], xbuf.at[slot], sem_in.at[slot])

    def put(g):
        slot = g % EXPERT_SLOTS
        return pltpu.make_async_copy(ybuf.at[slot], yb_ref.at[pl.ds(g * MOE_BLOCK, MOE_BLOCK)], sem_out.at[slot])

    @pl.when(e == 0)
    def _():
        for g in range(EXPERT_AHEAD):
            @pl.when(g < total)
            def _():
                fetch(g).start(priority=1)

    @pl.when(nb > 0)
    def _():
        wg_s[...] = wg_ref[0].astype(BF16)
        wu_s[...] = wu_ref[0].astype(BF16)
        wd_s[...] = wd_ref[0].astype(BF16)

    def block(g, carry):
        slot = g % EXPERT_SLOTS
        fetch(g).wait()

        @pl.when(g + EXPERT_AHEAD < total)
        def _():
            fetch(g + EXPERT_AHEAD).start(priority=1)

        @pl.when(g >= EXPERT_SLOTS)
        def _():
            put(g - EXPERT_SLOTS).wait()

        x = _unpack_halves(xbuf[slot]).astype(BF16)
        a = _dot(x, wg_s[...])
        u = _dot(x, wu_s[...])
        hmid = (a / (1.0 + jnp.exp(-a)) * u).astype(BF16)
        ybuf[slot] = _pack_halves(_dot(hmid, wd_s[...]))
        put(g).start()
        return carry

    lax.fori_loop(b0, b0 + nb, block, 0)

    @pl.when(e == last)
    def _():
        for back in range(1, EXPERT_SLOTS + 1):
            @pl.when(total >= back)
            def _():
                put(total - back).wait()

        ybuf[0] = jnp.zeros(ybuf.shape[1:], ybuf.dtype)

        def fill(blk, carry):
            copy = pltpu.make_async_copy(ybuf.at[0], yb_ref.at[pl.ds(blk * MOE_BLOCK, MOE_BLOCK)], sem_out.at[0])
            copy.start()
            copy.wait()
            return carry

        lax.fori_loop(total, nblk, fill, 0)


def moe_experts(xs, blk_start, blk_cnt, w_gate, w_up, w_down, layer):
    cap, c = xs.shape
    d = 2 * c
    hid = w_gate.shape[2]
    nblk = cap // MOE_BLOCK
    wmap = lambda e, st, cn: (layer * N_EXPERTS + e, 0, 0)
    grid_spec = pltpu.PrefetchScalarGridSpec(
        num_scalar_prefetch=2,
        grid=(N_EXPERTS,),
        in_specs=[pl.BlockSpec(memory_space=pl.ANY),
                  pl.BlockSpec((1, d, hid), wmap),
                  pl.BlockSpec((1, d, hid), wmap),
                  pl.BlockSpec((1, hid, d), wmap)],
        out_specs=pl.BlockSpec(memory_space=pl.ANY),
        scratch_shapes=[pltpu.VMEM((EXPERT_SLOTS, MOE_BLOCK, c), U32), pltpu.VMEM((EXPERT_SLOTS, MOE_BLOCK, c), U32),
                        pltpu.SemaphoreType.DMA((EXPERT_SLOTS,)), pltpu.SemaphoreType.DMA((EXPERT_SLOTS,)),
                        pltpu.VMEM((d, hid), BF16), pltpu.VMEM((d, hid), BF16), pltpu.VMEM((hid, d), BF16)],
    )
    return pl.pallas_call(
        functools.partial(_expert_kernel, nblk=nblk),
        grid_spec=grid_spec,
        out_shape=jax.ShapeDtypeStruct((cap, c), U32),
        compiler_params=_cparams(("arbitrary",)),
        name="moe_experts",
    )(blk_start, blk_cnt, xs, w_gate, w_up, w_down)


def _combine_kernel(dest_ref, next_dest_ref, x_ref, wt_ref, g_ref, yb_ref, o_ref, buf, sem, *, tm, final_norm):
    i = pl.program_id(0)
    slot = i & 1

    def gather(idx_ref, s):
        def issue(r, carry):
            _row_copy(yb_ref, idx_ref[2 * r], buf.at[s, 0], r, sem.at[s]).start(priority=0)
            _row_copy(yb_ref, idx_ref[2 * r + 1], buf.at[s, 1], r, sem.at[s]).start(priority=1)
            return carry

        lax.fori_loop(0, tm, issue, 0, unroll=ROW_DMA_UNROLL)

    @pl.when(i == 0)
    def _():
        gather(dest_ref, 0)

    @pl.when(i + 1 < pl.num_programs(0))
    def _():
        gather(next_dest_ref, 1 - slot)

    for k in range(2):
        pltpu.make_async_copy(yb_ref.at[pl.ds(0, tm)], buf.at[slot, k], sem.at[slot]).wait()
    wt = wt_ref[...]
    y = x_ref[...] + (_unpack_halves(buf[slot, 0]) * wt[:, 0:1] + _unpack_halves(buf[slot, 1]) * wt[:, 1:2])
    o_ref[...] = _rms(y, g_ref[...]) if final_norm else y


def moe_combine(x2d, yb, dest_flat, wt, g_final, final_norm):
    n, d = x2d.shape
    c = yb.shape[1]
    tm = ROW_DMA_TILE
    nt = n // tm
    return pl.pallas_call(
        functools.partial(_combine_kernel, tm=tm, final_norm=final_norm),
        grid=(nt,),
        in_specs=[pl.BlockSpec((2 * tm,), lambda i: (i,), memory_space=pltpu.SMEM),
                  pl.BlockSpec((2 * tm,), lambda i: (jnp.minimum(i + 1, nt - 1),), memory_space=pltpu.SMEM),
                  pl.BlockSpec((tm, d), lambda i: (i, 0)),
                  pl.BlockSpec((tm, 2), lambda i: (i, 0)),
                  pl.BlockSpec((1, d), lambda i: (0, 0)),
                  pl.BlockSpec(memory_space=pl.ANY)],
        out_specs=pl.BlockSpec((tm, d), lambda i: (i, 0)),
        out_shape=jax.ShapeDtypeStruct((n, d), F32),
        scratch_shapes=[pltpu.VMEM((2, 2, tm, c), U32), pltpu.SemaphoreType.DMA((2,))],
        compiler_params=_cparams(("arbitrary",)),
        name="moe_combine",
    )(dest_flat, dest_flat, x2d, wt, g_final.reshape(1, d), yb)


def hier_moe_block(x2d, routing, w_gate, w_up, w_down, layer, g_final, final_norm):
    n = x2d.shape[0]
    cap = 2 * n + N_EXPERTS * MOE_BLOCK
    hp, eid, wt, rank, cnt = routing
    counts = cnt[0, N_GROUPS:N_GROUPS + N_EXPERTS].astype(I32)
    padded = (counts + MOE_BLOCK - 1) // MOE_BLOCK * MOE_BLOCK
    p_ends = jnp.cumsum(padded)
    p_starts = p_ends - padded
    experts = jnp.arange(N_EXPERTS, dtype=I32)
    dest = (jnp.sum(jnp.where(eid[..., None] == experts, p_starts, 0), axis=-1) + rank).reshape(-1)
    xs = moe_dispatch(hp, dest, jnp.stack([p_ends, padded]).astype(I32), cap)
    yb = moe_experts(xs, p_starts // MOE_BLOCK, padded // MOE_BLOCK, w_gate, w_up, w_down, layer)
    return moe_combine(x2d, yb, dest, wt, g_final, final_norm)


def kernel(x, mem, e_norm, e_w_in, e_conv_w, e_conv_b, e_filt_w1, e_filt_b1, e_filt_w2, e_filt_b2, e_filt_w3, e_filt_freq, e_hy_bias, e_lam, e_subln, e_w_out, o_norm, o_w_in, o_sink, o_w_out, c_norm, c_wq, c_wkv, c_wo, f_norm, f_w_grp, f_b_grp, f_w_exp, f_b_exp, f_w_gate, f_w_up, f_w_down, g_mem, g_final):
    b, seq, d = x.shape
    n = b * seq
    m = mem.shape[1]
    tables = dft_tables(seq // 2)
    twiddles = dft_twiddles(seq)
    x2 = x.reshape(n, d)
    mem2 = mem.reshape(b * m, d)
    w_gate = f_w_gate.reshape(DEPTH * N_EXPERTS, d, EXP_HIDDEN)
    w_up = f_w_up.reshape(DEPTH * N_EXPERTS, d, EXP_HIDDEN)
    w_down = f_w_down.reshape(DEPTH * N_EXPERTS, EXP_HIDDEN, d)
    qperm = window_column_perm()
    for i in range(DEPTH):
        j = i // 2
        if i % 2 == 0:
            hyw, qw = 3 * HY_WIDTH, 2 * DIFF_HEADS * DIFF_HEAD_DIM
            col = jnp.arange(e_w_in.shape[2])
            col_scale = jnp.where((col >= hyw) & (col < hyw + qw), DIFF_HEAD_DIM ** -0.5 * LOG2E, 1.0).astype(F32)
            w_in = (e_w_in[j] * col_scale).astype(BF16)
            proj_h, proj_a = norm_matmul(x2, e_norm[j], w_in, 1024, [(hyw, BF16), (w_in.shape[1] - hyw, BF16)])
            y_hy = hyena_mixer(proj_h.reshape(b, seq, -1), tables, twiddles, e_conv_w[j], e_conv_b[j], e_filt_w1[j],
                               e_filt_b1[j], e_filt_w2[j], e_filt_b2[j], e_filt_w3[j], e_filt_freq[j], e_hy_bias[j])
            y_df = diff_attention(proj_a.reshape(b, seq, -1), e_lam[j], e_subln[j], i)
            w_out = e_w_out[j].astype(BF16)
            mixed, w_mix = [y_hy, y_df], [w_out[:HY_WIDTH], w_out[HY_WIDTH:]]
        else:
            oq = GQA_HEADS * GQA_HD
            col = jnp.arange(o_w_in.shape[2])
            col_perm = jnp.concatenate([qperm, col[oq:]])
            col_scale = jnp.where(col < oq, GQA_HD ** -0.5 * LOG2E, 1.0).astype(F32)
            w_in = (o_w_in[j][:, col_perm] * col_scale).astype(BF16)
            (proj,) = norm_matmul(x2, o_norm[j], w_in, 1024, [(w_in.shape[1], BF16)])
            att = window_gqa(proj.reshape(b, seq, -1), o_sink[j])
            mixed, w_mix = [att], [o_w_out[j][qperm].astype(BF16)]
        (kv,) = norm_matmul(mem2, g_mem, c_wkv[i].astype(BF16), 512, [(2 * X_W, BF16)])
        wq = (c_wq[i] * (X_HD ** -0.5 * LOG2E)).astype(BF16)
        x3, *routing = post_mixer(x2.reshape(b, seq, d), mixed, w_mix, c_norm[i], wq,
                                  kv.reshape(b, m, -1), c_wo[i].astype(BF16), f_norm[i], f_w_grp[i], f_b_grp[i],
                                  f_w_exp[i], f_b_exp[i])
        x2 = hier_moe_block(x3.reshape(n, d), routing, w_gate, w_up, w_down, i, g_final, i == DEPTH - 1)
    return x2.reshape(b, seq, d)
```

```python
import functools
import math

import jax
import jax.numpy as jnp
from jax import lax
from jax.experimental import pallas as pl
from jax.experimental.pallas import tpu as pltpu

F32 = jnp.float32
BF16 = jnp.bfloat16
I32 = jnp.int32
U32 = jnp.uint32

D_MODEL = 1024
DEPTH = 4
EPS = 1e-6
NEG = -1e30
HY_WIDTH = 512
HY_BANDS = 16
HY_FILT_HIDDEN = 64
HY_DECAY_TARGET = 1e-2
HY_FAST_PCT = 0.3
HY_SLOW_PCT = 1.5
DIFF_HEADS = 4
DIFF_HEAD_DIM = 64
WIN = 128
GQA_HEADS = 16
GQA_KV = 4
GQA_HD = 64
X_HEADS = 4
X_HD = 128
X_W = X_HEADS * X_HD
N_GROUPS = 4
EXP_PER_GROUP = 8
N_EXPERTS = N_GROUPS * EXP_PER_GROUP
EXP_HIDDEN = 512

LOG2E = 1.4426950408889634
LANES = 128
VMEM_LIMIT = 56 * 1024 * 1024
MOE_BLOCK = 256
DIFF_Q_BLOCK = 256
DIFF_KEY_CHUNK = 512
DIFF_BATCH_PER_STEP = 8
WIN_BATCH_PER_STEP = 4
EXPERT_SLOTS = 4
EXPERT_AHEAD = EXPERT_SLOTS - 1
ROW_DMA_TILE = 1024
ROW_DMA_UNROLL = 16


def _cparams(sem):
    return pltpu.CompilerParams(dimension_semantics=sem, vmem_limit_bytes=VMEM_LIMIT)


def _rms(x, g):
    ms = jnp.mean(x * x, axis=-1, keepdims=True)
    return x * lax.rsqrt(ms + EPS) * g


def _dot(a, b):
    return jnp.dot(a, b, preferred_element_type=F32)


def _dot_nt(a, b):
    return lax.dot_general(a, b, (((1,), (1,)), ((), ())), preferred_element_type=F32)


def _norm_matmul_kernel(x_ref, g_ref, w_ref, *o_refs):
    h = _rms(x_ref[...], g_ref[...]).astype(BF16)
    col = 0
    for o_ref in o_refs:
        width = o_ref.shape[1]
        o_ref[...] = _dot(h, w_ref[:, col:col + width]).astype(o_ref.dtype)
        col += width


def norm_matmul(x2d, gamma, w_bf16, tm, outs):
    n, d = x2d.shape
    f = w_bf16.shape[1]
    assert sum(width for width, _ in outs) == f
    res = pl.pallas_call(
        _norm_matmul_kernel,
        grid=(n // tm,),
        in_specs=[pl.BlockSpec((tm, d), lambda i: (i, 0)),
                  pl.BlockSpec((1, d), lambda i: (0, 0)),
                  pl.BlockSpec((d, f), lambda i: (0, 0))],
        out_specs=[pl.BlockSpec((tm, width), lambda i: (i, 0)) for width, _ in outs],
        out_shape=[jax.ShapeDtypeStruct((n, width), dtype) for width, dtype in outs],
        compiler_params=_cparams(("parallel",)),
        name="norm_matmul",
    )(x2d, gamma.reshape(1, d), w_bf16)
    return res


def _conv3_kernel(u_ref, w_ref, b_ref, oe_ref, oo_ref, slab_ref):
    half = u_ref.shape[1] // 2
    w = w_ref[...]
    bias = b_ref[...]
    row = lax.broadcasted_iota(I32, (half, LANES), 0)
    for k in range(u_ref.shape[2] // LANES):
        cols = slice(k * LANES, (k + 1) * LANES)
        slab_ref[...] = u_ref[0, :, cols].astype(F32)
        ue = slab_ref[pl.ds(0, half, stride=2), :]
        uo = slab_ref[pl.ds(1, half, stride=2), :]
        uo_prev = jnp.where(row == 0, 0.0, pltpu.roll(uo, 1, 0))
        ue_next = jnp.where(row == half - 1, 0.0, pltpu.roll(ue, half - 1, 0))
        even = uo_prev * w[0:1, cols] + ue * w[1:2, cols] + uo * w[2:3, cols] + bias[:, cols]
        odd = ue * w[0:1, cols] + uo * w[1:2, cols] + ue_next * w[2:3, cols] + bias[:, cols]
        oe_ref[0, :, cols] = even.astype(oe_ref.dtype)
        oo_ref[0, :, cols] = odd.astype(oo_ref.dtype)


def conv3(proj3d, conv_w, conv_b):
    b, seq, c = proj3d.shape
    tc = HY_WIDTH
    out = pl.BlockSpec((1, seq // 2, tc), lambda i, j: (i, 0, j))
    return pl.pallas_call(
        _conv3_kernel,
        grid=(b, c // tc),
        in_specs=[pl.BlockSpec((1, seq, tc), lambda i, j: (i, 0, j)),
                  pl.BlockSpec((3, tc), lambda i, j: (0, j)),
                  pl.BlockSpec((1, tc), lambda i, j: (0, j))],
        out_specs=[out, out],
        out_shape=[jax.ShapeDtypeStruct((b, seq // 2, c), BF16)] * 2,
        scratch_shapes=[pltpu.VMEM((seq, LANES), F32)],
        compiler_params=_cparams(("parallel", "parallel")),
        name="hyena_conv3",
    )(proj3d, conv_w, conv_b.reshape(1, c))


def _hy_filter_kernel(t_ref, bands_ref, w1t_ref, w1c_ref, w1s_ref, b1_ref, w2_ref, b2_ref, w3_ref,
                      freq_ref, delta_ref, hf_ref, hb_ref, *, seq, tl):
    hi = lax.Precision.HIGHEST
    parity = pl.program_id(0)
    i = pl.program_id(1)
    t = t_ref[0]
    pos = (2 * (i * tl + lax.broadcasted_iota(I32, (tl, 1), 0)) + parity).astype(F32)
    ang = bands_ref[...] * (2.0 * math.pi * pos / seq)
    f = freq_ref[...]
    pre = (t * w1t_ref[...]
           + jnp.dot(jnp.cos(ang), w1c_ref[...], precision=hi, preferred_element_type=F32)
           - jnp.dot(jnp.sin(ang), w1s_ref[...], precision=hi, preferred_element_type=F32)
           + b1_ref[...])
    a = jnp.sin(f * pre)
    a = jnp.sin(f * (jnp.dot(a, w2_ref[...], precision=hi, preferred_element_type=F32) + b2_ref[...]))
    h = jnp.dot(a, w3_ref[...], precision=hi, preferred_element_type=F32)
    decay = jnp.exp(-t * jnp.abs(delta_ref[...]))
    w = HY_WIDTH
    for o in range(2):
        hf_ref[0, :, o * w:(o + 1) * w] = h[:, o * 2 * w:o * 2 * w + w] * decay
        hb_ref[0, :, o * w:(o + 1) * w] = jnp.where(pos == 0.0, 0.0, h[:, o * 2 * w + w:(o + 1) * 2 * w] * decay)


def hyena_filters_time(seq, w1, b1, w2, b2, w3, freq):
    tl = 512
    half = seq // 2
    hid = LANES
    pad_h = hid - HY_FILT_HIDDEN
    t = jnp.linspace(0.0, 1.0, seq, dtype=F32).reshape(half, 2).T.reshape(2, half, 1)
    bands = jnp.pad(jnp.linspace(1e-4, HY_BANDS - 1, HY_BANDS, dtype=F32)[None], ((0, 0), (0, LANES - HY_BANDS)))
    w1p = jnp.pad(w1, ((0, 0), (0, pad_h)))
    w1t = w1p[0:1]
    w1c = jnp.pad(w1p[1:1 + HY_BANDS], ((0, LANES - HY_BANDS), (0, 0)))
    w1s = jnp.pad(w1p[1 + HY_BANDS:], ((0, LANES - HY_BANDS), (0, 0)))
    b1p = jnp.pad(b1, (0, pad_h)).reshape(1, hid)
    w2p = jnp.pad(w2, ((0, pad_h), (0, pad_h)))
    b2p = jnp.pad(b2, (0, pad_h)).reshape(1, hid)
    w3p = jnp.pad(w3, ((0, pad_h), (0, 0)))
    freqp = jnp.pad(freq, (0, pad_h)).reshape(1, hid)
    max_decay = math.log(HY_DECAY_TARGET) / HY_FAST_PCT
    min_decay = math.log(HY_DECAY_TARGET) / HY_SLOW_PCT
    deltas = jnp.linspace(min_decay, max_decay, HY_WIDTH, dtype=F32)[None]
    fw = w3.shape[1]
    full = lambda shape: pl.BlockSpec(shape, lambda p, i: (0, 0))
    out = pl.BlockSpec((1, tl, 2 * HY_WIDTH), lambda p, i: (p, i, 0))
    return pl.pallas_call(
        functools.partial(_hy_filter_kernel, seq=seq, tl=tl),
        grid=(2, half // tl),
        in_specs=[pl.BlockSpec((1, tl, 1), lambda p, i: (p, i, 0)), full((1, LANES)), full((1, hid)),
                  full((LANES, hid)), full((LANES, hid)), full((1, hid)), full((hid, hid)), full((1, hid)),
                  full((hid, fw)), full((1, hid)), full((1, HY_WIDTH))],
        out_specs=[out, out],
        out_shape=[jax.ShapeDtypeStruct((2, half, 2 * HY_WIDTH), F32)] * 2,
        compiler_params=_cparams(("parallel", "parallel")),
        name="hyena_filter_mlp",
    )(t, bands, w1t, w1c, w1s, b1p, w2p, b2p, w3p, freqp, deltas)


def dft_tables(seq):
    n2 = 4 * seq
    sub = 64
    f = jnp.arange(seq, dtype=I32)[:, None]
    odd = 2 * f + 1
    s1 = jnp.arange(seq // sub, dtype=I32)[None]
    s0 = jnp.arange(sub, dtype=I32)[None]
    ang_p = ((odd * s1) % (n2 // sub)).astype(F32) * (2.0 * math.pi * sub / n2)
    ang_q = ((odd * s0) % n2).astype(F32) * (2.0 * math.pi / n2)
    pc, ps, qc, qs = jnp.cos(ang_p), jnp.sin(ang_p), jnp.cos(ang_q), jnp.sin(ang_q)
    c = (pc[:, :, None] * qc[:, None, :] - ps[:, :, None] * qs[:, None, :]).reshape(seq, seq)
    s = (ps[:, :, None] * qc[:, None, :] + pc[:, :, None] * qs[:, None, :]).reshape(seq, seq)
    return c.astype(BF16), s.astype(BF16), c.T.astype(BF16), s.T.astype(BF16)


def dft_twiddles(seq):
    g = jnp.arange(seq // 2, dtype=I32)[:, None]
    ang = (2 * g + 1).astype(F32) * (2.0 * math.pi / (4 * seq))
    return jnp.cos(ang), jnp.sin(ang)


def _half_transform(c1, s1, cg, sg, x0, x1):
    pc0, ps0 = _dot(c1, x0), _dot(s1, x0)
    pc1, ps1 = _dot(c1, x1), _dot(s1, x1)
    tr = cg * pc1 - sg * ps1
    ti = -(cg * ps1 + sg * pc1)
    return pc0 + tr, ti - ps0, pc0 - tr, ps0 + ti


def _spectrum_kernel(c_ref, s_ref, cg_ref, sg_ref, f0_ref, f1_ref, b0_ref, b1_ref,
                     kgre_ref, kgim_ref, khre_ref, khim_ref):
    c1, s1, cg, sg = c_ref[...], s_ref[...], cg_ref[...], sg_ref[...]
    fre_g, fim_g, fre_h, fim_h = _half_transform(c1, s1, cg, sg, f0_ref[0].astype(BF16), f1_ref[0].astype(BF16))
    bre_g, bim_g, bre_h, bim_h = _half_transform(c1, s1, cg, sg, b0_ref[0].astype(BF16), b1_ref[0].astype(BF16))
    kgre_ref[...] = fre_g + bre_g
    kgim_ref[...] = fim_g - bim_g
    khre_ref[...] = fre_h + bre_h
    khim_ref[...] = fim_h - bim_h


def filter_spectrum(tables, twiddles, hf, hb):
    _, half, cols = hf.shape
    tf, tn = 512, 512
    tab = pl.BlockSpec((tf, half), lambda i, j: (i, 0))
    twd = pl.BlockSpec((tf, 1), lambda i, j: (i, 0))
    even = pl.BlockSpec((1, half, tn), lambda i, j: (0, 0, j))
    odd = pl.BlockSpec((1, half, tn), lambda i, j: (1, 0, j))
    return pl.pallas_call(
        _spectrum_kernel,
        grid=(half // tf, cols // tn),
        in_specs=[tab, tab, twd, twd, even, odd, even, odd],
        out_specs=[pl.BlockSpec((tf, tn), lambda i, j: (i, j))] * 4,
        out_shape=[jax.ShapeDtypeStruct((half, cols), F32)] * 4,
        compiler_params=_cparams(("parallel", "parallel")),
        name="hyena_filter_spectrum",
    )(tables[0], tables[1], twiddles[0], twiddles[1], hf, hf, hb, hb)


def _hy_fwd_kernel(z0_ref, z1_ref, c_ref, s_ref, cg_ref, sg_ref, kgre_ref, kgim_ref, khre_ref, khim_ref,
                   u0re_ref, u0im_ref, u1re_ref, u1im_ref):
    cg, sg = cg_ref[...], sg_ref[...]
    zre_g, zim_g, zre_h, zim_h = _half_transform(c_ref[...], s_ref[...], cg, sg,
                                                 z0_ref[0].astype(BF16), z1_ref[0].astype(BF16))
    kgre, kgim, khre, khim = kgre_ref[...], kgim_ref[...], khre_ref[...], khim_ref[...]
    yre_g = zre_g * kgre - zim_g * kgim
    yim_g = zre_g * kgim + zim_g * kgre
    yre_h = zre_h * khre - zim_h * khim
    yim_h = zre_h * khim + zim_h * khre
    u0re_ref[0] = (yre_g + yre_h).astype(BF16)
    u0im_ref[0] = (yim_g - yim_h).astype(BF16)
    a = yre_g - yre_h
    b = yim_g + yim_h
    u1re_ref[0] = (cg * a - sg * b).astype(BF16)
    u1im_ref[0] = (cg * b + sg * a).astype(BF16)


def hyena_fwd(zsrc, tables, twiddles, kspec, order):
    (ze, ce), (zo, co) = zsrc
    b, half, _ = ze.shape
    w = HY_WIDTH
    tf = half
    tab = pl.BlockSpec((tf, half), lambda i, j: (j, 0))
    twd = pl.BlockSpec((tf, 1), lambda i, j: (j, 0))
    kblk = pl.BlockSpec((tf, w), lambda i, j: (j, order))
    return pl.pallas_call(
        _hy_fwd_kernel,
        grid=(b, half // tf),
        in_specs=[pl.BlockSpec((1, half, w), lambda i, j: (i, 0, ce)),
                  pl.BlockSpec((1, half, w), lambda i, j: (i, 0, co)),
                  tab, tab, twd, twd, kblk, kblk, kblk, kblk],
        out_specs=[pl.BlockSpec((1, tf, w), lambda i, j: (i, j, 0))] * 4,
        out_shape=[jax.ShapeDtypeStruct((b, half, w), BF16)] * 4,
        compiler_params=_cparams(("parallel", "parallel")),
        name="hyena_dft_fwd",
    )(ze, zo, tables[0], tables[1], twiddles[0], twiddles[1], *kspec)


def _hy_inv_kernel(ct_ref, st_ref, u0re_ref, u0im_ref, u1re_ref, u1im_ref, z0_ref, z1_ref, g0_ref, g1_ref,
                   bias_ref, o_ref, *, scale):
    ct, st = ct_ref[...], st_ref[...]
    w = z0_ref.shape[2]
    y0 = (_dot(ct, u0re_ref[0]) - _dot(st, u0im_ref[0])) * scale
    y1 = (_dot(ct, u1re_ref[0]) - _dot(st, u1im_ref[0])) * scale
    bias = bias_ref[...]
    o_ref[0, :, 0:w] = (g0_ref[0].astype(F32) * (y0 + z0_ref[0].astype(F32) * bias)).astype(o_ref.dtype)
    o_ref[0, :, w:2 * w] = (g1_ref[0].astype(F32) * (y1 + z1_ref[0].astype(F32) * bias)).astype(o_ref.dtype)


def hyena_inv(tables, u, zsrc, gsrc, bias_row, out_dtype):
    b, half, w = u[0].shape
    tt = half
    tab = pl.BlockSpec((tt, half), lambda i, j: (j, 0))
    ublk = pl.BlockSpec((1, half, w), lambda i, j: (i, 0, 0))

    def rows(col):
        return pl.BlockSpec((1, tt, w), lambda i, j: (i, j, col))

    (ze, ce), (zo, co) = zsrc
    (ge, gce), (go, gco) = gsrc
    return pl.pallas_call(
        functools.partial(_hy_inv_kernel, scale=0.5 / half),
        grid=(b, half // tt),
        in_specs=[tab, tab, ublk, ublk, ublk, ublk, rows(ce), rows(co), rows(gce), rows(gco),
                  pl.BlockSpec((1, w), lambda i, j: (0, 0))],
        out_specs=pl.BlockSpec((1, tt, 2 * w), lambda i, j: (i, j, 0)),
        out_shape=jax.ShapeDtypeStruct((b, half, 2 * w), out_dtype),
        compiler_params=_cparams(("parallel", "parallel")),
        name="hyena_dft_inv",
    )(tables[2], tables[3], *u, ze, zo, ge, go, bias_row)


def hyena_mixer(proj3d, tables, twiddles, conv_w, conv_b, w1, b1, w2, b2, w3, freq, hy_bias):
    seq = proj3d.shape[1]
    ue, uo = conv3(proj3d, conv_w, conv_b)
    hf, hb = hyena_filters_time(seq, w1, b1, w2, b2, w3, freq)
    kspec = filter_spectrum(tables, twiddles, hf, hb)
    v = ((ue, 0), (uo, 0))
    uu = hyena_fwd(v, tables, twiddles, kspec, 0)
    z1 = hyena_inv(tables, uu, v, ((ue, 1), (uo, 1)), hy_bias[0:1], F32)
    z = ((z1, 0), (z1, 1))
    uu = hyena_fwd(z, tables, twiddles, kspec, 1)
    return hyena_inv(tables, uu, z, ((ue, 2), (uo, 2)), hy_bias[1:2], BF16)


def _diff_attn_kernel(slope_ref, q_ref, k_ref, v_ref, lam_ref, sub_ref, o_ref, bias_ref, *, tq, lam_init):
    h = pl.program_id(0)
    qi = pl.program_id(1)
    seq = k_ref.shape[1]

    @pl.when(pl.program_id(2) == 0)
    def _():
        qpos = qi * tq + lax.broadcasted_iota(I32, (tq, seq), 0)
        kpos = lax.broadcasted_iota(I32, (tq, seq), 1)
        bias_ref[...] = slope_ref[h] * jnp.abs(qpos - kpos).astype(F32)

    lane = lax.broadcasted_iota(I32, (1, 2 * DIFF_HEAD_DIM), 1)
    ck = DIFF_KEY_CHUNK
    l = lam_ref[...]
    lam_full = (jnp.exp(jnp.sum(l[0:1] * l[1:2], axis=-1, keepdims=True))
                - jnp.exp(jnp.sum(l[2:3] * l[3:4], axis=-1, keepdims=True)) + lam_init)

    def attend(bb, m):
        q = q_ref[bb]
        k = k_ref[bb]
        v = v_ref[bb]
        keep = (lane < DIFF_HEAD_DIM) if m == 0 else (lane >= DIFF_HEAD_DIM)
        qm = jnp.where(keep, q, jnp.zeros_like(q))
        s = [_dot_nt(qm, k[c:c + ck]) - bias_ref[:, c:c + ck] for c in range(0, seq, ck)]
        mx = functools.reduce(jnp.maximum, [jnp.max(sc, axis=-1, keepdims=True) for sc in s])
        acc = jnp.zeros((tq, 2 * DIFF_HEAD_DIM), F32)
        den = jnp.zeros((tq, 1), F32)
        for i, sc in enumerate(s):
            e = jnp.exp2(sc - mx)
            den = den + jnp.sum(e, axis=-1, keepdims=True)
            acc = acc + _dot(e.astype(BF16), v[i * ck:(i + 1) * ck])
        return acc / den

    for bb in range(q_ref.shape[0]):
        o = attend(bb, 0) - lam_full * attend(bb, 1)
        o_ref[bb] = (_rms(o, sub_ref[...]) * (1.0 - lam_init)).astype(o_ref.dtype)


def diff_attention(proj3d, lam, subln, layer_idx):
    b, seq, _ = proj3d.shape
    tq = DIFF_Q_BLOCK
    nb = DIFF_BATCH_PER_STEP
    hw = 2 * DIFF_HEAD_DIM
    qb, kb, vb = 0, DIFF_HEADS, 2 * DIFF_HEADS
    lam_init = 0.8 - 0.6 * math.exp(-0.3 * layer_idx)
    slopes = 2.0 ** (-8.0 * jnp.arange(1, DIFF_HEADS + 1, dtype=F32) / DIFF_HEADS) * LOG2E
    return pl.pallas_call(
        functools.partial(_diff_attn_kernel, tq=tq, lam_init=lam_init),
        grid=(DIFF_HEADS, seq // tq, b // nb),
        in_specs=[pl.BlockSpec(memory_space=pltpu.SMEM),
                  pl.BlockSpec((nb, tq, hw), lambda h, j, i: (i, j, qb + h)),
                  pl.BlockSpec((nb, seq, hw), lambda h, j, i: (i, 0, kb + h)),
                  pl.BlockSpec((nb, seq, hw), lambda h, j, i: (i, 0, vb + h)),
                  pl.BlockSpec((4, DIFF_HEAD_DIM), lambda h, j, i: (0, 0)),
                  pl.BlockSpec((1, hw), lambda h, j, i: (0, 0))],
        out_specs=pl.BlockSpec((nb, tq, hw), lambda h, j, i: (i, j, h)),
        out_shape=jax.ShapeDtypeStruct((b, seq, DIFF_HEADS * hw), BF16),
        scratch_shapes=[pltpu.VMEM((tq, seq), F32)],
        compiler_params=_cparams(("parallel", "parallel", "arbitrary")),
        name="diff_attention",
    )(slopes, proj3d, proj3d, proj3d, lam, subln.reshape(1, hw))


def _win_attn_kernel(slope_ref, sink_ref, q_ref, kp_ref, kc_ref, kn_ref, vp_ref, vc_ref, vn_ref, o_ref, bias_ref,
                     *, tq, seq):
    qi = pl.program_id(0)
    span = 3 * tq
    group = GQA_HEADS // GQA_KV

    @pl.when(pl.program_id(1) == 0)
    def _():
        qpos = qi * tq + lax.broadcasted_iota(I32, (tq, span), 0)
        kpos = (qi - 1) * tq + lax.broadcasted_iota(I32, (tq, span), 1)
        rel = jnp.abs(qpos - kpos)
        relf = rel.astype(F32)
        masked = jnp.where((rel <= WIN) & (kpos >= 0) & (kpos < seq), 0.0, -NEG)
        for head in range(GQA_HEADS):
            bias_ref[head] = slope_ref[head] * relf + masked

    low = lax.broadcasted_iota(I32, (1, LANES), 1) < GQA_HD
    for bb in range(q_ref.shape[0]):
        kwin = jnp.concatenate([kp_ref[bb], kc_ref[bb], kn_ref[bb]], axis=0)
        vwin = jnp.concatenate([vp_ref[bb], vc_ref[bb], vn_ref[bb]], axis=0)
        for p in range(GQA_KV // 2):
            kb = kwin[:, p * LANES:(p + 1) * LANES]
            vb = vwin[:, p * LANES:(p + 1) * LANES]
            outs = []
            for half in range(2):
                kv = 2 * p + half
                mine = low if half == 0 else jnp.logical_not(low)
                qblocks = [q_ref[bb, :, (group * p + r) * LANES:(group * p + r + 1) * LANES] for r in range(group)]
                qs = jnp.concatenate([jnp.where(mine, qb, jnp.zeros_like(qb)) for qb in qblocks], axis=0)
                s = _dot_nt(qs, kb)
                es, inv = [], []
                for r in range(group):
                    head = kv * group + r
                    sink = sink_ref[head]
                    sr = s[r * tq:(r + 1) * tq] - bias_ref[head]
                    m = jnp.maximum(jnp.max(sr, axis=-1, keepdims=True), sink)
                    e = jnp.exp2(sr - m)
                    inv.append(1.0 / (jnp.sum(e, axis=-1, keepdims=True) + jnp.exp2(sink - m)))
                    es.append(e.astype(BF16))
                o = _dot(jnp.concatenate(es, axis=0), vb)
                outs.append([o[r * tq:(r + 1) * tq] * inv[r] for r in range(group)])
            for r in range(group):
                col = (group * p + r) * LANES
                o_ref[bb, :, col:col + LANES] = jnp.where(low, outs[0][r], outs[1][r]).astype(o_ref.dtype)


def window_head_order():
    group = GQA_HEADS // GQA_KV
    order = []
    for p in range(GQA_KV // 2):
        for r in range(group):
            order += [(2 * p) * group + r, (2 * p + 1) * group + r]
    return order


def window_column_perm():
    cols = []
    for head in window_head_order():
        cols += list(range(head * GQA_HD, (head + 1) * GQA_HD))
    return jnp.asarray(cols, dtype=I32)


def window_gqa(proj3d, sink):
    b, seq, _ = proj3d.shape
    tq = WIN
    nq = seq // tq
    oq = GQA_HEADS * GQA_HD
    okv = GQA_KV * GQA_HD
    kcol, vcol = oq // okv, oq // okv + 1
    slopes = 2.0 ** (-8.0 * jnp.arange(1, GQA_HEADS + 1, dtype=F32) / GQA_HEADS) * LOG2E

    nb = WIN_BATCH_PER_STEP

    def neighbour(col, step):
        return pl.BlockSpec((nb, tq, okv), lambda j, i: (i, jnp.clip(j + step, 0, nq - 1), col))

    return pl.pallas_call(
        functools.partial(_win_attn_kernel, tq=tq, seq=seq),
        grid=(nq, b // nb),
        in_specs=[pl.BlockSpec(memory_space=pltpu.SMEM),
                  pl.BlockSpec(memory_space=pltpu.SMEM),
                  pl.BlockSpec((nb, tq, oq), lambda j, i: (i, j, 0)),
                  neighbour(kcol, -1), neighbour(kcol, 0), neighbour(kcol, 1),
                  neighbour(vcol, -1), neighbour(vcol, 0), neighbour(vcol, 1)],
        out_specs=pl.BlockSpec((nb, tq, oq), lambda j, i: (i, j, 0)),
        out_shape=jax.ShapeDtypeStruct((b, seq, oq), BF16),
        scratch_shapes=[pltpu.VMEM((GQA_HEADS, tq, 3 * tq), F32)],
        compiler_params=_cparams(("parallel", "arbitrary")),
        name="window_gqa",
    )(slopes, sink.astype(F32) * LOG2E, proj3d, proj3d, proj3d, proj3d, proj3d, proj3d, proj3d)


def _cross_attend(x, g, wq, kv, wo):
    h = _rms(x, g).astype(BF16)
    q = _dot(h, wq)
    outs = []
    for hd in range(X_HEADS):
        qh = q[:, hd * X_HD:(hd + 1) * X_HD].astype(BF16)
        kh = kv[:, hd * X_HD:(hd + 1) * X_HD]
        vh = kv[:, X_W + hd * X_HD:X_W + (hd + 1) * X_HD]
        s = _dot_nt(qh, kh)
        e = jnp.exp2(s - jnp.max(s, axis=-1, keepdims=True))
        outs.append(_dot(e.astype(BF16), vh) / jnp.sum(e, axis=-1, keepdims=True))
    o = jnp.concatenate(outs, axis=-1).astype(BF16)
    return x + _dot(o, wo)


def _split_bf16(x):
    hi = x.astype(BF16)
    lo = (x - hi.astype(F32)).astype(BF16)
    return hi, lo


def _pack_halves(h):
    c = h.shape[1] // 2
    left = lax.bitcast_convert_type(h[:, :c].astype(BF16).astype(F32), U32)
    right = lax.bitcast_convert_type(h[:, c:].astype(BF16).astype(F32), U32)
    return left | (right >> 16)


def _unpack_halves(p):
    left = lax.bitcast_convert_type(p & jnp.uint32(0xFFFF0000), F32)
    right = lax.bitcast_convert_type(p << 16, F32)
    return jnp.concatenate([left, right], axis=-1)


def _route(x, g, whi, wlo, bias, hp_ref, eid_ref, wt_ref, rank_ref, cnt_ref, base_ref):
    tm = x.shape[0]
    h = _rms(x, g)
    hp_ref[...] = _pack_halves(h)
    hhi, hlo = _split_bf16(h)
    logits = _dot(hhi, whi) + _dot(hlo, whi) + _dot(hhi, wlo) + bias
    lane = lax.broadcasted_iota(I32, logits.shape, 1)
    big = jnp.int32(LANES)
    ninf = -jnp.inf

    gl = jnp.where(lane < N_GROUPS, logits, ninf)
    gmax = jnp.max(gl, axis=-1, keepdims=True)
    gsel = jnp.min(jnp.where(gl == gmax, lane, big), axis=-1, keepdims=True)
    ggate = 1.0 / jnp.sum(jnp.exp(gl - gmax), axis=-1, keepdims=True)

    lo_lane = N_GROUPS + gsel * EXP_PER_GROUP
    el = jnp.where((lane >= lo_lane) & (lane < lo_lane + EXP_PER_GROUP), logits, ninf)
    v1 = jnp.max(el, axis=-1, keepdims=True)
    i1 = jnp.min(jnp.where(el == v1, lane, big), axis=-1, keepdims=True)
    el2 = jnp.where(lane == i1, ninf, el)
    v2 = jnp.max(el2, axis=-1, keepdims=True)
    i2 = jnp.min(jnp.where(el2 == v2, lane, big), axis=-1, keepdims=True)
    e2 = jnp.exp(v2 - v1)
    w1 = ggate / (1.0 + e2)
    w2 = ggate * e2 / (1.0 + e2)

    one1 = lane == i1
    one2 = lane == i2
    onehot = (one1 | one2).astype(F32)
    r = lax.broadcasted_iota(I32, (tm, tm), 0)
    c = lax.broadcasted_iota(I32, (tm, tm), 1)
    tri = (c < r).astype(BF16)
    before = _dot(tri, onehot.astype(BF16)) + base_ref[...]
    rank1 = jnp.sum(jnp.where(one1, before, 0.0), axis=-1, keepdims=True)
    rank2 = jnp.sum(jnp.where(one2, before, 0.0), axis=-1, keepdims=True)
    total = base_ref[...] + jnp.sum(onehot, axis=0, keepdims=True)
    base_ref[...] = total
    cnt_ref[...] = total

    col = lax.broadcasted_iota(I32, (tm, 2), 1)
    eid_ref[...] = jnp.where(col == 0, i1, i2) - N_GROUPS
    wt_ref[...] = jnp.where(col == 0, w1, w2)
    rank_ref[...] = jnp.where(col == 0, rank1, rank2).astype(I32)


def _interleave_rows(pair):
    t, w2 = pair.shape
    w = w2 // 2
    r = lax.broadcasted_iota(I32, (2 * t, t), 0)
    c = lax.broadcasted_iota(I32, (2 * t, t), 1)
    pick_even = (r == 2 * c).astype(BF16)
    pick_odd = (r == 2 * c + 1).astype(BF16)
    return (_dot(pick_even, pair[:, :w]) + _dot(pick_odd, pair[:, w:])).astype(BF16)


def _post_mixer_kernel(*refs, n_lhs, paired):
    x_ref = refs[0]
    a_refs = refs[1:1 + n_lhs]
    w_refs = refs[1 + n_lhs:1 + 2 * n_lhs]
    (cg_ref, wq_ref, kv_ref, wo_ref, fg_ref, whi_ref, wlo_ref, rb_ref,
     x_out_ref, hp_ref, eid_ref, wt_ref, rank_ref, cnt_ref, base_ref) = refs[1 + 2 * n_lhs:]

    @pl.when((pl.program_id(0) == 0) & (pl.program_id(1) == 0))
    def _():
        base_ref[...] = jnp.zeros_like(base_ref)

    x = x_ref[0]
    for a_ref, w_ref, is_paired in zip(a_refs, w_refs, paired):
        a = a_ref[0].astype(BF16)
        x = x + _dot(_interleave_rows(a) if is_paired else a, w_ref[...])
    x = _cross_attend(x, cg_ref[...], wq_ref[...], kv_ref[0], wo_ref[...])
    x_out_ref[0] = x
    _route(x, fg_ref[...], whi_ref[...], wlo_ref[...], rb_ref[...], hp_ref, eid_ref, wt_ref, rank_ref, cnt_ref,
           base_ref)


def post_mixer(x3d, lhs_list, w_list, c_gamma, wq, kv, wo, f_gamma, w_grp, b_grp, w_exp, b_exp):
    b, seq, d = x3d.shape
    paired = tuple(a.shape[1] != seq for a in lhs_list)
    n = b * seq
    m = kv.shape[1]
    tm = 1024
    nt = seq // tm
    wcat = jnp.pad(jnp.concatenate([w_grp, w_exp], axis=1), ((0, 0), (0, LANES - N_GROUPS - N_EXPERTS)))
    bcat = jnp.pad(jnp.concatenate([b_grp, b_exp]), (0, LANES - N_GROUPS - N_EXPERTS)).reshape(1, LANES)
    whi = wcat.astype(BF16)
    wlo = (wcat - whi.astype(F32)).astype(BF16)
    const = lambda shape: pl.BlockSpec(shape, lambda i, j: (0,) * len(shape))
    tok = lambda width: pl.BlockSpec((tm, width), lambda i, j: (i * nt + j, 0))
    in_specs = [pl.BlockSpec((1, tm, d), lambda i, j: (i, j, 0))]
    in_specs += [pl.BlockSpec((1, tm // 2 if p else tm, a.shape[2]), lambda i, j: (i, j, 0))
                 for a, p in zip(lhs_list, paired)]
    in_specs += [const(w.shape) for w in w_list]
    in_specs += [const((1, d)), const((d, X_W)), pl.BlockSpec((1, m, 2 * X_W), lambda i, j: (i, 0, 0)),
                 const((X_W, d)), const((1, d)), const((d, LANES)), const((d, LANES)), const((1, LANES))]
    return pl.pallas_call(
        functools.partial(_post_mixer_kernel, n_lhs=len(lhs_list), paired=paired),
        grid=(b, nt),
        in_specs=in_specs,
        out_specs=[pl.BlockSpec((1, tm, d), lambda i, j: (i, j, 0)), tok(d // 2), tok(2), tok(2), tok(2),
                   const((1, LANES))],
        out_shape=[jax.ShapeDtypeStruct((b, seq, d), F32),
                   jax.ShapeDtypeStruct((n, d // 2), U32),
                   jax.ShapeDtypeStruct((n, 2), I32),
                   jax.ShapeDtypeStruct((n, 2), F32),
                   jax.ShapeDtypeStruct((n, 2), I32),
                   jax.ShapeDtypeStruct((1, LANES), F32)],
        scratch_shapes=[pltpu.VMEM((1, LANES), F32)],
        compiler_params=_cparams(("arbitrary", "arbitrary")),
        name="post_mixer",
    )(x3d, *lhs_list, *w_list, c_gamma.reshape(1, d), wq, kv, wo, f_gamma.reshape(1, d), whi, wlo, bcat)


def _row_copy(src_ref, src_row, dst_ref, dst_row, sem):
    return pltpu.make_async_copy(src_ref.at[pl.ds(src_row, 1)], dst_ref.at[pl.ds(dst_row, 1)], sem)


def _dispatch_kernel(dest_ref, seg_ref, hp_ref, xs_ref, zbuf, sem, zsem, *, tm):
    @pl.when(pl.program_id(0) == 0)
    def _():
        zbuf[...] = jnp.zeros_like(zbuf)

        def zero_tail(e):
            tail = pl.multiple_of(seg_ref[0, e] - MOE_BLOCK, MOE_BLOCK)
            return pltpu.make_async_copy(zbuf, xs_ref.at[pl.ds(tail, MOE_BLOCK)], zsem)

        def start(e, carry):
            @pl.when(seg_ref[1, e] > 0)
            def _():
                zero_tail(e).start()
            return carry

        def finish(e, carry):
            @pl.when(seg_ref[1, e] > 0)
            def _():
                zero_tail(e).wait()
            return carry

        lax.fori_loop(0, N_EXPERTS, start, 0)
        lax.fori_loop(0, N_EXPERTS, finish, 0)

        def unused(blk):
            return pltpu.make_async_copy(zbuf, xs_ref.at[pl.ds(blk * MOE_BLOCK, MOE_BLOCK)], zsem)

        def start_unused(blk, carry):
            unused(blk).start()
            return carry

        def finish_unused(blk, carry):
            unused(blk).wait()
            return carry

        first_unused = seg_ref[0, N_EXPERTS - 1] // MOE_BLOCK
        lax.fori_loop(first_unused, xs_ref.shape[0] // MOE_BLOCK, start_unused, 0)
        lax.fori_loop(first_unused, xs_ref.shape[0] // MOE_BLOCK, finish_unused, 0)

    def issue(r, carry):
        _row_copy(hp_ref, r, xs_ref, dest_ref[2 * r], sem).start(priority=0)
        _row_copy(hp_ref, r, xs_ref, dest_ref[2 * r + 1], sem).start(priority=1)
        return carry

    lax.fori_loop(0, tm, issue, 0, unroll=ROW_DMA_UNROLL)
    for _ in range(2):
        pltpu.make_async_copy(hp_ref, xs_ref.at[pl.ds(0, tm)], sem).wait()


def moe_dispatch(hp, dest_flat, segments, cap):
    n, c = hp.shape
    tm = ROW_DMA_TILE
    return pl.pallas_call(
        functools.partial(_dispatch_kernel, tm=tm),
        grid=(n // tm,),
        in_specs=[pl.BlockSpec((2 * tm,), lambda i: (i,), memory_space=pltpu.SMEM),
                  pl.BlockSpec(memory_space=pltpu.SMEM),
                  pl.BlockSpec((tm, c), lambda i: (i, 0))],
        out_specs=pl.BlockSpec(memory_space=pl.ANY),
        out_shape=jax.ShapeDtypeStruct((cap, c), U32),
        scratch_shapes=[pltpu.VMEM((MOE_BLOCK, c), U32), pltpu.SemaphoreType.DMA(()), pltpu.SemaphoreType.DMA(())],
        compiler_params=_cparams(("arbitrary",)),
        name="moe_dispatch",
    )(dest_flat, segments, hp)


def _expert_kernel(start_ref, cnt_ref, xs_ref, wg_ref, wu_ref, wd_ref, yb_ref, xbuf, ybuf, sem_in, sem_out,
                   wg_s, wu_s, wd_s, *, nblk):
    e = pl.program_id(0)
    last = pl.num_programs(0) - 1
    nb = cnt_ref[e]
    b0 = start_ref[e]
    total = start_ref[last] + cnt_ref[last]

    def fetch(g):
        slot = g % EXPERT_SLOTS
        return pltpu.make_async_copy(xs_ref.at[pl.ds(g * MOE_BLOCK, MOE_BLOCK)], xbuf.at[slot], sem_in.at[slot])

    def put(g):
        slot = g % EXPERT_SLOTS
        return pltpu.make_async_copy(ybuf.at[slot], yb_ref.at[pl.ds(g * MOE_BLOCK, MOE_BLOCK)], sem_out.at[slot])

    @pl.when(e == 0)
    def _():
        for g in range(EXPERT_AHEAD):
            @pl.when(g < total)
            def _():
                fetch(g).start(priority=1)

    @pl.when(nb > 0)
    def _():
        wg_s[...] = wg_ref[0].astype(BF16)
        wu_s[...] = wu_ref[0].astype(BF16)
        wd_s[...] = wd_ref[0].astype(BF16)

    def block(g, carry):
        slot = g % EXPERT_SLOTS
        fetch(g).wait()

        @pl.when(g + EXPERT_AHEAD < total)
        def _():
            fetch(g + EXPERT_AHEAD).start(priority=1)

        @pl.when(g >= EXPERT_SLOTS)
        def _():
            put(g - EXPERT_SLOTS).wait()

        x = _unpack_halves(xbuf[slot]).astype(BF16)
        a = _dot(x, wg_s[...])
        u = _dot(x, wu_s[...])
        hmid = (a / (1.0 + jnp.exp(-a)) * u).astype(BF16)
        ybuf[slot] = _pack_halves(_dot(hmid, wd_s[...]))
        put(g).start()
        return carry

    lax.fori_loop(b0, b0 + nb, block, 0)

    @pl.when(e == last)
    def _():
        for back in range(1, EXPERT_SLOTS + 1):
            @pl.when(total >= back)
            def _():
                put(total - back).wait()

        ybuf[0] = jnp.zeros(ybuf.shape[1:], ybuf.dtype)

        def fill(blk, carry):
            copy = pltpu.make_async_copy(ybuf.at[0], yb_ref.at[pl.ds(blk * MOE_BLOCK, MOE_BLOCK)], sem_out.at[0])
            copy.start()
            copy.wait()
            return carry

        lax.fori_loop(total, nblk, fill, 0)


def moe_experts(xs, blk_start, blk_cnt, w_gate, w_up, w_down, layer):
    cap, c = xs.shape
    d = 2 * c
    hid = w_gate.shape[2]
    nblk = cap // MOE_BLOCK
    wmap = lambda e, st, cn: (layer * N_EXPERTS + e, 0, 0)
    grid_spec = pltpu.PrefetchScalarGridSpec(
        num_scalar_prefetch=2,
        grid=(N_EXPERTS,),
        in_specs=[pl.BlockSpec(memory_space=pl.ANY),
                  pl.BlockSpec((1, d, hid), wmap),
                  pl.BlockSpec((1, d, hid), wmap),
                  pl.BlockSpec((1, hid, d), wmap)],
        out_specs=pl.BlockSpec(memory_space=pl.ANY),
        scratch_shapes=[pltpu.VMEM((EXPERT_SLOTS, MOE_BLOCK, c), U32), pltpu.VMEM((EXPERT_SLOTS, MOE_BLOCK, c), U32),
                        pltpu.SemaphoreType.DMA((EXPERT_SLOTS,)), pltpu.SemaphoreType.DMA((EXPERT_SLOTS,)),
                        pltpu.VMEM((d, hid), BF16), pltpu.VMEM((d, hid), BF16), pltpu.VMEM((hid, d), BF16)],
    )
    return pl.pallas_call(
        functools.partial(_expert_kernel, nblk=nblk),
        grid_spec=grid_spec,
        out_shape=jax.ShapeDtypeStruct((cap, c), U32),
        compiler_params=_cparams(("arbitrary",)),
        name="moe_experts",
    )(blk_start, blk_cnt, xs, w_gate, w_up, w_down)


def _combine_kernel(dest_ref, next_dest_ref, x_ref, wt_ref, g_ref, yb_ref, o_ref, buf, sem, *, tm, final_norm):
    i = pl.program_id(0)
    slot = i & 1

    def gather(idx_ref, s):
        def issue(r, carry):
            _row_copy(yb_ref, idx_ref[2 * r], buf.at[s, 0], r, sem.at[s]).start(priority=0)
            _row_copy(yb_ref, idx_ref[2 * r + 1], buf.at[s, 1], r, sem.at[s]).start(priority=1)
            return carry

        lax.fori_loop(0, tm, issue, 0, unroll=ROW_DMA_UNROLL)

    @pl.when(i == 0)
    def _():
        gather(dest_ref, 0)

    @pl.when(i + 1 < pl.num_programs(0))
    def _():
        gather(next_dest_ref, 1 - slot)

    for k in range(2):
        pltpu.make_async_copy(yb_ref.at[pl.ds(0, tm)], buf.at[slot, k], sem.at[slot]).wait()
    wt = wt_ref[...]
    y = x_ref[...] + (_unpack_halves(buf[slot, 0]) * wt[:, 0:1] + _unpack_halves(buf[slot, 1]) * wt[:, 1:2])
    o_ref[...] = _rms(y, g_ref[...]) if final_norm else y


def moe_combine(x2d, yb, dest_flat, wt, g_final, final_norm):
    n, d = x2d.shape
    c = yb.shape[1]
    tm = ROW_DMA_TILE
    nt = n // tm
    return pl.pallas_call(
        functools.partial(_combine_kernel, tm=tm, final_norm=final_norm),
        grid=(nt,),
        in_specs=[pl.BlockSpec((2 * tm,), lambda i: (i,), memory_space=pltpu.SMEM),
                  pl.BlockSpec((2 * tm,), lambda i: (jnp.minimum(i + 1, nt - 1),), memory_space=pltpu.SMEM),
                  pl.BlockSpec((tm, d), lambda i: (i, 0)),
                  pl.BlockSpec((tm, 2), lambda i: (i, 0)),
                  pl.BlockSpec((1, d), lambda i: (0, 0)),
                  pl.BlockSpec(memory_space=pl.ANY)],
        out_specs=pl.BlockSpec((tm, d), lambda i: (i, 0)),
        out_shape=jax.ShapeDtypeStruct((n, d), F32),
        scratch_shapes=[pltpu.VMEM((2, 2, tm, c), U32), pltpu.SemaphoreType.DMA((2,))],
        compiler_params=_cparams(("arbitrary",)),
        name="moe_combine",
    )(dest_flat, dest_flat, x2d, wt, g_final.reshape(1, d), yb)


def hier_moe_block(x2d, routing, w_gate, w_up, w_down, layer, g_final, final_norm):
    n = x2d.shape[0]
    cap = 2 * n + N_EXPERTS * MOE_BLOCK
    hp, eid, wt, rank, cnt = routing
    counts = cnt[0, N_GROUPS:N_GROUPS + N_EXPERTS].astype(I32)
    padded = (counts + MOE_BLOCK - 1) // MOE_BLOCK * MOE_BLOCK
    p_ends = jnp.cumsum(padded)
    p_starts = p_ends - padded
    experts = jnp.arange(N_EXPERTS, dtype=I32)
    dest = (jnp.sum(jnp.where(eid[..., None] == experts, p_starts, 0), axis=-1) + rank).reshape(-1)
    xs = moe_dispatch(hp, dest, jnp.stack([p_ends, padded]).astype(I32), cap)
    yb = moe_experts(xs, p_starts // MOE_BLOCK, padded // MOE_BLOCK, w_gate, w_up, w_down, layer)
    return moe_combine(x2d, yb, dest, wt, g_final, final_norm)


def kernel(x, mem, e_norm, e_w_in, e_conv_w, e_conv_b, e_filt_w1, e_filt_b1, e_filt_w2, e_filt_b2, e_filt_w3, e_filt_freq, e_hy_bias, e_lam, e_subln, e_w_out, o_norm, o_w_in, o_sink, o_w_out, c_norm, c_wq, c_wkv, c_wo, f_norm, f_w_grp, f_b_grp, f_w_exp, f_b_exp, f_w_gate, f_w_up, f_w_down, g_mem, g_final):
    b, seq, d = x.shape
    n = b * seq
    m = mem.shape[1]
    tables = dft_tables(seq // 2)
    twiddles = dft_twiddles(seq)
    x2 = x.reshape(n, d)
    mem2 = mem.reshape(b * m, d)
    w_gate = f_w_gate.reshape(DEPTH * N_EXPERTS, d, EXP_HIDDEN)
    w_up = f_w_up.reshape(DEPTH * N_EXPERTS, d, EXP_HIDDEN)
    w_down = f_w_down.reshape(DEPTH * N_EXPERTS, EXP_HIDDEN, d)
    qperm = window_column_perm()
    for i in range(DEPTH):
        j = i // 2
        if i % 2 == 0:
            hyw, qw = 3 * HY_WIDTH, 2 * DIFF_HEADS * DIFF_HEAD_DIM
            col = jnp.arange(e_w_in.shape[2])
            col_scale = jnp.where((col >= hyw) & (col < hyw + qw), DIFF_HEAD_DIM ** -0.5 * LOG2E, 1.0).astype(F32)
            w_in = (e_w_in[j] * col_scale).astype(BF16)
            proj_h, proj_a = norm_matmul(x2, e_norm[j], w_in, 1024, [(hyw, BF16), (w_in.shape[1] - hyw, BF16)])
            y_hy = hyena_mixer(proj_h.reshape(b, seq, -1), tables, twiddles, e_conv_w[j], e_conv_b[j], e_filt_w1[j],
                               e_filt_b1[j], e_filt_w2[j], e_filt_b2[j], e_filt_w3[j], e_filt_freq[j], e_hy_bias[j])
            y_df = diff_attention(proj_a.reshape(b, seq, -1), e_lam[j], e_subln[j], i)
            w_out = e_w_out[j].astype(BF16)
            mixed, w_mix = [y_hy, y_df], [w_out[:HY_WIDTH], w_out[HY_WIDTH:]]
        else:
            oq = GQA_HEADS * GQA_HD
            col = jnp.arange(o_w_in.shape[2])
            col_perm = jnp.concatenate([qperm, col[oq:]])
            col_scale = jnp.where(col < oq, GQA_HD ** -0.5 * LOG2E, 1.0).astype(F32)
            w_in = (o_w_in[j][:, col_perm] * col_scale).astype(BF16)
            (proj,) = norm_matmul(x2, o_norm[j], w_in, 1024, [(w_in.shape[1], BF16)])
            att = window_gqa(proj.reshape(b, seq, -1), o_sink[j])
            mixed, w_mix = [att], [o_w_out[j][qperm].astype(BF16)]
        (kv,) = norm_matmul(mem2, g_mem, c_wkv[i].astype(BF16), 512, [(2 * X_W, BF16)])
        wq = (c_wq[i] * (X_HD ** -0.5 * LOG2E)).astype(BF16)
        x3, *routing = post_mixer(x2.reshape(b, seq, d), mixed, w_mix, c_norm[i], wq,
                                  kv.reshape(b, m, -1), c_wo[i].astype(BF16), f_norm[i], f_w_grp[i], f_b_grp[i],
                                  f_w_exp[i], f_b_exp[i])
        x2 = hier_moe_block(x3.reshape(n, d), routing, w_gate, w_up, w_down, i, g_final, i == DEPTH - 1)
    return x2.reshape(b, seq, d)
```
